```python
import jax, jax.numpy as jnp
from jax import lax
import numpy as np

D_MODEL = 2048
BATCH = 1
SEQ = 8192
DEPTH = 1

POOL_WINDOWS = (2, 4, 8, 16)
N_POOL_GROUPS = len(POOL_WINDOWS)
POOL_WIDTH = D_MODEL // 2
POOL_GROUP = POOL_WIDTH // N_POOL_GROUPS
HEAD_DIM = 128
ATTN_WIDTH = D_MODEL // 2
ATTN_HEADS = ATTN_WIDTH // HEAD_DIM
MOBA_BLOCK = 256
MOBA_TOPK = 3
Q_CHUNK = 64
IN_PROJ_WIDTH = POOL_WIDTH + 3 * ATTN_WIDTH + 2 * D_MODEL
N_GROUPS = 4
EXPERTS_PER_GROUP = 4
N_EXPERTS = N_GROUPS * EXPERTS_PER_GROUP
EXPERT_TOPK = 2
D_EXPERT = D_MODEL // 2
RMS_EPS = 1e-6
NEG_INF = -1e30

kernel_name = "hybrid_pool_moba_hmoe_block"


def rms_norm(x, g):
    xf = x.astype(jnp.float32)
    y = xf * lax.rsqrt(jnp.mean(xf * xf, axis=-1, keepdims=True) + RMS_EPS)
    return (y * g.astype(jnp.float32)).astype(x.dtype)


def alibi_slopes(n_heads):
    return jnp.exp2(-8.0 * jnp.arange(1, n_heads + 1, dtype=jnp.float32) / n_heads)


def pool_mixer(u, w_pool, pool_scale):
    B, S, _ = u.shape
    ug = u.reshape(B, S, N_POOL_GROUPS, POOL_GROUP).astype(jnp.float32)
    cs = jnp.cumsum(ug, axis=1)
    t = jnp.arange(S)
    pooled = []
    for gi, w in enumerate(POOL_WINDOWS):
        c = cs[:, :, gi]
        c_prev = jnp.pad(c, ((0, 0), (w, 0), (0, 0)))[:, :S]
        cnt = jnp.minimum(t + 1, w).astype(jnp.float32)[None, :, None]
        pooled.append((c - c_prev) / cnt)
    pooled = jnp.stack(pooled, axis=2)
    mixed = (pooled - ug).astype(u.dtype)
    y = jnp.einsum('bsgc,gcd->bsgd', mixed, w_pool).reshape(B, S, POOL_WIDTH)
    return y * pool_scale


def moba_attention(q, k, v):
    B, S, H, Dh = q.shape
    s_pad = -(-S // MOBA_BLOCK) * MOBA_BLOCK
    pad = ((0, 0), (0, 0), (0, s_pad - S), (0, 0))
    qh = jnp.pad(q.transpose(0, 2, 1, 3), pad)
    kh = jnp.pad(k.transpose(0, 2, 1, 3), pad)
    vh = jnp.pad(v.transpose(0, 2, 1, 3), pad)
    nb = s_pad // MOBA_BLOCK
    topk = min(MOBA_TOPK, nb)
    kb = kh.reshape(B, H, nb, MOBA_BLOCK, Dh)
    vb = vh.reshape(B, H, nb, MOBA_BLOCK, Dh)
    kmean = jnp.mean(kb.astype(jnp.float32), axis=3)
    slopes = alibi_slopes(H)
    scale = HEAD_DIM ** -0.5
    nq = s_pad // Q_CHUNK
    qc = qh.reshape(B, H, nq, Q_CHUNK, Dh).transpose(2, 0, 1, 3, 4)
    b_ar = jnp.arange(B)[:, None, None, None]
    h_ar = jnp.arange(H)[None, :, None, None]
    blk_pos = jnp.arange(MOBA_BLOCK)

    def chunk(args):
        qi, ci = args
        t = ci * Q_CHUNK + jnp.arange(Q_CHUNK)
        own = (ci * Q_CHUNK) // MOBA_BLOCK
        gate = jnp.einsum('bhqd,bhnd->bhqn', qi.astype(jnp.float32), kmean)
        gate = jnp.where(jnp.arange(nb) < own, gate, NEG_INF)
        _, idx = lax.top_k(gate, topk)
        valid = idx < own
        kg = kb[b_ar, h_ar, idx]
        vg = vb[b_ar, h_ar, idx]
        s_sel = jnp.einsum('bhqd,bhqkcd->bhqkc', qi, kg).astype(jnp.float32) * scale
        pos_sel = idx[..., None] * MOBA_BLOCK + blk_pos
        dist_sel = (t[None, None, :, None, None] - pos_sel).astype(jnp.float32)
        s_sel = jnp.where(valid[..., None], s_sel - slopes[None, :, None, None, None] * dist_sel, NEG_INF)
        ko = lax.dynamic_index_in_dim(kb, own, axis=2, keepdims=False)
        vo = lax.dynamic_index_in_dim(vb, own, axis=2, keepdims=False)
        s_own = jnp.einsum('bhqd,bhcd->bhqc', qi, ko).astype(jnp.float32) * scale
        pos_own = own * MOBA_BLOCK + blk_pos
        dist_own = (t[:, None] - pos_own[None, :]).astype(jnp.float32)
        s_own = jnp.where((dist_own >= 0)[None, None], s_own - slopes[None, :, None, None] * dist_own[None, None], NEG_INF)
        scores = jnp.concatenate([s_sel.reshape(B, H, Q_CHUNK, topk * MOBA_BLOCK), s_own], axis=-1)
        p = jax.nn.softmax(scores, axis=-1).astype(v.dtype)
        p_sel = p[..., :topk * MOBA_BLOCK].reshape(B, H, Q_CHUNK, topk, MOBA_BLOCK)
        p_own = p[..., topk * MOBA_BLOCK:]
        return (jnp.einsum('bhqkc,bhqkcd->bhqd', p_sel, vg)
                + jnp.einsum('bhqc,bhcd->bhqd', p_own, vo))

    outs = lax.map(chunk, (qc, jnp.arange(nq, dtype=jnp.int32)))
    out = outs.transpose(1, 0, 3, 2, 4).reshape(B, s_pad, H * Dh)
    return out[:, :S]


def hier_moe(h, w_r_group, b_r_group, w_r_expert, b_r_expert, w_gate, w_up, w_down):
    hf = h.astype(jnp.float32)
    g_logits = hf @ w_r_group.astype(jnp.float32) + b_r_group.astype(jnp.float32)
    g_prob = jax.nn.softmax(g_logits, axis=-1)
    g_w, g_idx = lax.top_k(g_prob, 1)
    e_logits = jnp.einsum('bsd,gde->bsge', hf, w_r_expert.astype(jnp.float32)) + b_r_expert.astype(jnp.float32)
    e_logits = jnp.take_along_axis(e_logits, g_idx[..., None], axis=2)[:, :, 0]
    top_v, top_i = lax.top_k(e_logits, EXPERT_TOPK)
    top_w = jax.nn.softmax(top_v, axis=-1) * g_w
    flat_i = g_idx * EXPERTS_PER_GROUP + top_i
    combine = jnp.sum(jax.nn.one_hot(flat_i, N_EXPERTS, dtype=jnp.float32) * top_w[..., None], axis=-2)
    a = jnp.einsum('bsd,edf->bsef', h, w_gate)
    u = jnp.einsum('bsd,edf->bsef', h, w_up)
    hid = jax.nn.silu(a) * u * combine[..., None].astype(h.dtype)
    return jnp.einsum('bsef,efd->bsd', hid, w_down)


def setup_inputs(seed: int = 0) -> dict:
    key = jax.random.key(seed)
    ks = jax.random.split(key, 18)
    f32 = jnp.float32
    L = DEPTH

    def nrm(k, shape, scale):
        return jax.random.normal(k, shape, f32) * scale

    return {
        "x": nrm(ks[0], (BATCH, SEQ, D_MODEL), 1.0),
        "norm_mix": 1.0 + nrm(ks[1], (L, D_MODEL), 0.02),
        "w_in": nrm(ks[2], (L, D_MODEL, IN_PROJ_WIDTH), D_MODEL ** -0.5),
        "w_pool": nrm(ks[3], (L, N_POOL_GROUPS, POOL_GROUP, POOL_GROUP), POOL_GROUP ** -0.5),
        "pool_scale": 1.0 + nrm(ks[4], (L, POOL_WIDTH), 0.02),
        "w_branch_pool": nrm(ks[5], (L, POOL_WIDTH, D_MODEL), POOL_WIDTH ** -0.5),
        "w_branch_attn": nrm(ks[6], (L, ATTN_WIDTH, D_MODEL), ATTN_WIDTH ** -0.5),
        "w_out": nrm(ks[7], (L, D_MODEL, D_MODEL), D_MODEL ** -0.5),
        "norm_ffn": 1.0 + nrm(ks[8], (L, D_MODEL), 0.02),
        "w_r_group": nrm(ks[9], (L, D_MODEL, N_GROUPS), D_MODEL ** -0.5),
        "b_r_group": nrm(ks[10], (L, N_GROUPS), 0.01),
        "w_r_expert": nrm(ks[11], (L, N_GROUPS, D_MODEL, EXPERTS_PER_GROUP), D_MODEL ** -0.5),
        "b_r_expert": nrm(ks[12], (L, N_GROUPS, EXPERTS_PER_GROUP), 0.01),
        "w_gate": nrm(ks[13], (L, N_EXPERTS, D_MODEL, D_EXPERT), D_MODEL ** -0.5),
        "w_up": nrm(ks[14], (L, N_EXPERTS, D_MODEL, D_EXPERT), D_MODEL ** -0.5),
        "w_down": nrm(ks[15], (L, N_EXPERTS, D_EXPERT, D_MODEL), D_EXPERT ** -0.5),
        "norm_final": 1.0 + nrm(ks[16], (D_MODEL,), 0.02),
    }


def reference(x, norm_mix, w_in, w_pool, pool_scale, w_branch_pool, w_branch_attn, w_out,
              norm_ffn, w_r_group, b_r_group, w_r_expert, b_r_expert, w_gate, w_up, w_down,
              norm_final):
    B, S, _ = x.shape
    splits = [POOL_WIDTH, POOL_WIDTH + ATTN_WIDTH, POOL_WIDTH + 2 * ATTN_WIDTH,
              POOL_WIDTH + 3 * ATTN_WIDTH, POOL_WIDTH + 3 * ATTN_WIDTH + D_MODEL]
    for l in range(DEPTH):
        h = rms_norm(x, norm_mix[l])
        proj = h @ w_in[l]
        u_pool, q, k, v, gl_pool, gl_attn = jnp.split(proj, splits, axis=-1)
        y_pool = pool_mixer(u_pool, w_pool[l], pool_scale[l]) @ w_branch_pool[l]
        qh = q.reshape(B, S, ATTN_HEADS, HEAD_DIM)
        kh = k.reshape(B, S, ATTN_HEADS, HEAD_DIM)
        vh = v.reshape(B, S, ATTN_HEADS, HEAD_DIM)
        y_attn = moba_attention(qh, kh, vh) @ w_branch_attn[l]
        merged = jax.nn.sigmoid(gl_pool) * y_pool + jax.nn.sigmoid(gl_attn) * y_attn
        x = x + merged @ w_out[l]
        h = rms_norm(x, norm_ffn[l])
        x = x + hier_moe(h, w_r_group[l], b_r_group[l], w_r_expert[l], b_r_expert[l],
                         w_gate[l], w_up[l], w_down[l])
    return rms_norm(x, norm_final)
```

```python
import functools

import jax
import jax.numpy as jnp
from jax import lax
from jax.experimental import pallas as pl
from jax.experimental.pallas import tpu as pltpu

F32 = jnp.float32
BF16 = jnp.bfloat16

POOL_WINDOWS = (2, 4, 8, 16)
MAX_WINDOW = 16
POOL_WIDTH = 1024
POOL_GROUP = 256
HEAD_DIM = 128
ATTN_HEADS = 8
ATTN_WIDTH = 1024
MOBA_BLOCK = 256
MOBA_TOPK = 3
N_GROUPS = 4
EXPERTS_PER_GROUP = 4
N_EXPERTS = 16
ROUTE_LANES = 128
EXPERT_LANE0 = N_GROUPS
RMS_EPS = 1e-6
NEG_INF = -1e30

V7X_VMEM_LIMIT_BYTES = 56 * 1024 * 1024


def _params(semantics, vmem=V7X_VMEM_LIMIT_BYTES):
    return pltpu.CompilerParams(dimension_semantics=semantics, vmem_limit_bytes=vmem)


def _norm_inproj_kernel(x_ref, g_ref, w_ref, o_ref, h_ref):
    @pl.when(pl.program_id(1) == 0)
    def _():
        x = x_ref[...]
        ms = jnp.mean(x * x, axis=-1, keepdims=True)
        h_ref[...] = (x * lax.rsqrt(ms + RMS_EPS) * g_ref[...]).astype(BF16)

    o_ref[...] = jnp.dot(h_ref[...], w_ref[...], preferred_element_type=F32).astype(o_ref.dtype)


def _norm_inproj(x, g, w, tm=512, tn=1024):
    S, D = x.shape
    N = w.shape[1]
    return pl.pallas_call(
        _norm_inproj_kernel,
        grid=(S // tm, N // tn),
        in_specs=[
            pl.BlockSpec((tm, D), lambda i, j: (i, 0)),
            pl.BlockSpec((1, D), lambda i, j: (0, 0)),
            pl.BlockSpec((D, tn), lambda i, j: (0, j)),
        ],
        out_specs=pl.BlockSpec((tm, tn), lambda i, j: (i, j)),
        out_shape=jax.ShapeDtypeStruct((S, N), BF16),
        scratch_shapes=[pltpu.VMEM((tm, D), BF16)],
        compiler_params=_params(("parallel", "arbitrary")),
        name="norm_inproj",
    )(x, g, w)


def _pool_kernel(cur_ref, prev_ref, w_ref, scale_ref, o_ref, ext_ref):
    i = pl.program_id(0)
    tm = cur_ref.shape[0]
    u = cur_ref[...].astype(F32)
    halo = jnp.where(i > 0, prev_ref[...].astype(F32), 0.0)
    ext_ref[0:MAX_WINDOW, :] = halo
    ext_ref[MAX_WINDOW:MAX_WINDOW + tm, :] = u
    t = i * tm + lax.broadcasted_iota(jnp.int32, (tm, 1), 0)
    for g, w in enumerate(POOL_WINDOWS):
        cols = slice(g * POOL_GROUP, (g + 1) * POOL_GROUP)
        ug = u[:, cols]
        wsum = ug
        for s in range(1, w):
            wsum = wsum + ext_ref[MAX_WINDOW - s:MAX_WINDOW - s + tm, cols]
        cnt = jnp.minimum(t + 1, w).astype(F32)
        mixed = (wsum / cnt - ug).astype(BF16)
        y = jnp.dot(mixed, w_ref[g], preferred_element_type=F32)
        o_ref[:, cols] = (y * scale_ref[:, cols]).astype(o_ref.dtype)


def _pool_mixer(proj, w_pool, pool_scale, tm=512):
    S = proj.shape[0]
    halo_blocks = tm // MAX_WINDOW
    return pl.pallas_call(
        _pool_kernel,
        grid=(S // tm,),
        in_specs=[
            pl.BlockSpec((tm, POOL_WIDTH), lambda i: (i, 0)),
            pl.BlockSpec((MAX_WINDOW, POOL_WIDTH),
                         lambda i: (jnp.maximum(i * halo_blocks - 1, 0), 0)),
            pl.BlockSpec((len(POOL_WINDOWS), POOL_GROUP, POOL_GROUP), lambda i: (0, 0, 0)),
            pl.BlockSpec((1, POOL_WIDTH), lambda i: (0, 0)),
        ],
        out_specs=pl.BlockSpec((tm, POOL_WIDTH), lambda i: (i, 0)),
        out_shape=jax.ShapeDtypeStruct((S, POOL_WIDTH), BF16),
        scratch_shapes=[pltpu.VMEM((tm + MAX_WINDOW, POOL_WIDTH), F32)],
        compiler_params=_params(("parallel",)),
        name="pool_mixer",
    )(proj, proj, w_pool, pool_scale)


def _moba_gate_kernel(q_ref, k_ref, v_ref, qT_ref, vT_ref, sel_ref):
    S = q_ref.shape[0]
    nb = S // MOBA_BLOCK
    topk = min(MOBA_TOPK, nb)
    kf = k_ref[...].astype(F32).reshape(nb, MOBA_BLOCK, HEAD_DIM)
    kmean = jnp.sum(kf, axis=1) * (1.0 / MOBA_BLOCK)
    km_hi = kmean.astype(BF16)
    km_lo = (kmean - km_hi.astype(F32)).astype(BF16)
    blk = lax.broadcasted_iota(jnp.int32, (nb, MOBA_BLOCK), 0)

    def body(i, carry):
        rows = pl.ds(pl.multiple_of(i * MOBA_BLOCK, MOBA_BLOCK), MOBA_BLOCK)
        qT = q_ref[rows, :].astype(F32).T.astype(BF16)
        qT_ref[i] = qT
        vT_ref[i] = v_ref[rows, :].astype(F32).T.astype(BF16)
        gate = (jnp.dot(km_hi, qT, preferred_element_type=F32)
                + jnp.dot(km_lo, qT, preferred_element_type=F32))
        gate = jnp.where(blk < i, gate, NEG_INF)
        sel = jnp.zeros((nb, MOBA_BLOCK), F32)
        for _ in range(topk):
            best = jnp.max(gate, axis=0, keepdims=True)
            idx = jnp.min(jnp.where(gate == best, blk, nb), axis=0, keepdims=True)
            hit = blk == idx
            sel = jnp.where(hit & (blk < i), 1.0, sel)
            gate = jnp.where(hit, -jnp.inf, gate)
        sel_ref[i] = sel
        return carry

    lax.fori_loop(0, nb, body, 0)


def _moba_gate(proj):
    S = proj.shape[0]
    nb = S // MOBA_BLOCK
    H = ATTN_HEADS
    q0 = POOL_WIDTH // HEAD_DIM
    k0 = q0 + H
    v0 = k0 + H
    blocked = lambda h: (h, 0, 0, 0)
    return pl.pallas_call(
        _moba_gate_kernel,
        grid=(H,),
        in_specs=[
            pl.BlockSpec((S, HEAD_DIM), lambda h: (0, q0 + h)),
            pl.BlockSpec((S, HEAD_DIM), lambda h: (0, k0 + h)),
            pl.BlockSpec((S, HEAD_DIM), lambda h: (0, v0 + h)),
        ],
        out_specs=[
            pl.BlockSpec((None, nb, HEAD_DIM, MOBA_BLOCK), blocked),
            pl.BlockSpec((None, nb, HEAD_DIM, MOBA_BLOCK), blocked),
            pl.BlockSpec((None, nb, nb, MOBA_BLOCK), blocked),
        ],
        out_shape=[
            jax.ShapeDtypeStruct((H, nb, HEAD_DIM, MOBA_BLOCK), BF16),
            jax.ShapeDtypeStruct((H, nb, HEAD_DIM, MOBA_BLOCK), BF16),
            jax.ShapeDtypeStruct((H, nb, nb, MOBA_BLOCK), F32),
        ],
        compiler_params=_params(("parallel",)),
        name="moba_gate",
    )(proj, proj, proj)


def _moba_attn_kernel(slope_ref, qT_ref, sel_ref, k_ref, vT_ref, o_ref):
    i = pl.program_id(1)
    scale = HEAD_DIM ** -0.5
    slope = slope_ref[...][:, 0:1]
    qT = qT_ref[...]
    kpos = lax.broadcasted_iota(jnp.int32, (MOBA_BLOCK, MOBA_BLOCK), 0)
    qpos = lax.broadcasted_iota(jnp.int32, (MOBA_BLOCK, MOBA_BLOCK), 1)
    dist0 = (qpos - kpos).astype(F32)
    bias0 = slope * dist0

    def scores(j):
        rows = pl.ds(pl.multiple_of(j * MOBA_BLOCK, MOBA_BLOCK), MOBA_BLOCK)
        return jnp.dot(k_ref[rows, :], qT, preferred_element_type=F32) * scale

    s = jnp.where(dist0 >= 0, scores(i) - bias0, NEG_INF)
    m = jnp.max(s, axis=0, keepdims=True)
    p = jnp.exp(s - m)
    l = jnp.sum(p, axis=0, keepdims=True)
    acc = jnp.dot(vT_ref[i], p.astype(BF16), preferred_element_type=F32)

    def body(j, carry):
        m, l, acc = carry
        block_gap = ((i - j) * MOBA_BLOCK).astype(F32)
        valid = sel_ref[pl.ds(j, 1), :] > 0.0
        s = jnp.where(valid, scores(j) - (bias0 + slope * block_gap), NEG_INF)
        m_new = jnp.maximum(m, jnp.max(s, axis=0, keepdims=True))
        alpha = jnp.exp(m - m_new)
        p = jnp.exp(s - m_new)
        l = alpha * l + jnp.sum(p, axis=0, keepdims=True)
        acc = alpha * acc + jnp.dot(vT_ref[j], p.astype(BF16), preferred_element_type=F32)
        return m_new, l, acc

    m, l, acc = lax.fori_loop(0, i, body, (m, l, acc))
    o_ref[...] = (acc / l).T.astype(o_ref.dtype)


def _moba_attention(proj, qT, vT, sel, slopes):
    S = proj.shape[0]
    nb = S // MOBA_BLOCK
    H = ATTN_HEADS
    k0 = POOL_WIDTH // HEAD_DIM + H
    return pl.pallas_call(
        _moba_attn_kernel,
        grid=(H, nb),
        in_specs=[
            pl.BlockSpec((None, 1, 128), lambda h, i: (h, 0, 0)),
            pl.BlockSpec((None, None, HEAD_DIM, MOBA_BLOCK), lambda h, i: (h, i, 0, 0)),
            pl.BlockSpec((None, None, nb, MOBA_BLOCK), lambda h, i: (h, i, 0, 0)),
            pl.BlockSpec((S, HEAD_DIM), lambda h, i: (0, k0 + h)),
            pl.BlockSpec((None, nb, HEAD_DIM, MOBA_BLOCK), lambda h, i: (h, 0, 0, 0)),
        ],
        out_specs=pl.BlockSpec((MOBA_BLOCK, HEAD_DIM), lambda h, i: (i, h)),
        out_shape=jax.ShapeDtypeStruct((S, ATTN_WIDTH), BF16),
        compiler_params=_params(("parallel", "arbitrary")),
        name="moba_attention",
    )(slopes, qT, sel, proj, vT)


def _first_lane_of_max(vals, lane):
    best = jnp.max(vals, axis=1, keepdims=True)
    idx = jnp.min(jnp.where(vals == best, lane, ROUTE_LANES), axis=1, keepdims=True)
    return best, idx


def _merge_route_kernel(yp_ref, ya_ref, glp_ref, gla_ref, x_ref, wbp_ref, wba_ref, wout_ref,
                        g_ref, wr_hi_ref, wr_lo_ref, br_ref, x1_ref, h2_ref, comb_ref):
    bp = jnp.dot(yp_ref[...], wbp_ref[...], preferred_element_type=F32)
    ba = jnp.dot(ya_ref[...], wba_ref[...], preferred_element_type=F32)
    merged = (jax.nn.sigmoid(glp_ref[...].astype(F32)) * bp
              + jax.nn.sigmoid(gla_ref[...].astype(F32)) * ba)
    x1 = x_ref[...] + jnp.dot(merged.astype(BF16), wout_ref[...], preferred_element_type=F32)
    x1_ref[...] = x1
    ms = jnp.mean(x1 * x1, axis=-1, keepdims=True)
    h2 = x1 * lax.rsqrt(ms + RMS_EPS) * g_ref[...]
    h2_hi = h2.astype(BF16)
    h2_ref[...] = h2_hi
    h2_lo = (h2 - h2_hi.astype(F32)).astype(BF16)
    logits = (jnp.dot(h2_hi, wr_hi_ref[...], preferred_element_type=F32)
              + jnp.dot(h2_lo, wr_hi_ref[...], preferred_element_type=F32)
              + jnp.dot(h2_hi, wr_lo_ref[...], preferred_element_type=F32)
              + br_ref[...])

    lane = lax.broadcasted_iota(jnp.int32, logits.shape, 1)
    g_logits = jnp.where(lane < N_GROUPS, logits, -jnp.inf)
    g_best, g_idx = _first_lane_of_max(g_logits, lane)
    g_w = 1.0 / jnp.sum(jnp.exp(g_logits - g_best), axis=1, keepdims=True)
    e_lo = EXPERT_LANE0 + EXPERTS_PER_GROUP * g_idx
    e_logits = jnp.where((lane >= e_lo) & (lane < e_lo + EXPERTS_PER_GROUP), logits, -jnp.inf)
    v1, i1 = _first_lane_of_max(e_logits, lane)
    v2, i2 = _first_lane_of_max(jnp.where(lane == i1, -jnp.inf, e_logits), lane)
    e21 = jnp.exp(v2 - v1)
    w1 = g_w / (1.0 + e21)
    w2 = g_w * e21 / (1.0 + e21)
    comb_ref[...] = jnp.where(lane == i1, w1, 0.0) + jnp.where(lane == i2, w2, 0.0)


def _merge_route(yp, ya, proj, x, wbp, wba, wout, g, wr_hi, wr_lo, br, tm=256):
    S, D = x.shape
    full = lambda i: (0, 0)
    row = lambda i: (i, 0)
    glp_blk = (POOL_WIDTH + 3 * ATTN_WIDTH) // D
    return pl.pallas_call(
        _merge_route_kernel,
        grid=(S // tm,),
        in_specs=[
            pl.BlockSpec((tm, POOL_WIDTH), row),
            pl.BlockSpec((tm, ATTN_WIDTH), row),
            pl.BlockSpec((tm, D), lambda i: (i, glp_blk)),
            pl.BlockSpec((tm, D), lambda i: (i, glp_blk + 1)),
            pl.BlockSpec((tm, D), row),
            pl.BlockSpec((POOL_WIDTH, D), full),
            pl.BlockSpec((ATTN_WIDTH, D), full),
            pl.BlockSpec((D, D), full),
            pl.BlockSpec((1, D), full),
            pl.BlockSpec((D, ROUTE_LANES), full),
            pl.BlockSpec((D, ROUTE_LANES), full),
            pl.BlockSpec((1, ROUTE_LANES), full),
        ],
        out_specs=[
            pl.BlockSpec((tm, D), row),
            pl.BlockSpec((tm, D), row),
            pl.BlockSpec((tm, ROUTE_LANES), row),
        ],
        out_shape=[
            jax.ShapeDtypeStruct((S, D), F32),
            jax.ShapeDtypeStruct((S, D), BF16),
            jax.ShapeDtypeStruct((S, ROUTE_LANES), F32),
        ],
        compiler_params=_params(("parallel",)),
        name="merge_route",
    )(yp, ya, proj, proj, x, wbp, wba, wout, g, wr_hi, wr_lo, br)


def _moe_dense_kernel(h_ref, comb_ref, wg_ref, wu_ref, wd_ref, x1_ref, g_ref, o_ref, acc_ref):
    e = pl.program_id(1)
    f = pl.program_id(2)
    first = (e == 0) & (f == 0)
    last = (e == pl.num_programs(1) - 1) & (f == pl.num_programs(2) - 1)

    @pl.when(first)
    def _():
        acc_ref[...] = jnp.zeros_like(acc_ref)

    h = h_ref[...]
    a = jnp.dot(h, wg_ref[...], preferred_element_type=F32)
    u = jnp.dot(h, wu_ref[...], preferred_element_type=F32)
    lane = lax.broadcasted_iota(jnp.int32, comb_ref.shape, 1)
    c = jnp.sum(jnp.where(lane == e + EXPERT_LANE0, comb_ref[...], 0.0), axis=1, keepdims=True)
    hid = jax.nn.silu(a) * u * c
    acc_ref[...] += jnp.dot(hid.astype(BF16), wd_ref[...], preferred_element_type=F32)

    @pl.when(last)
    def _():
        y = x1_ref[...] + acc_ref[...]
        ms = jnp.mean(y * y, axis=-1, keepdims=True)
        o_ref[...] = y * lax.rsqrt(ms + RMS_EPS) * g_ref[...]


def _moe_dense(h2, comb, wg, wu, wd, x1, g, tm=512, tf=512):
    S, D = x1.shape
    E, _, F = wg.shape
    row = lambda i, e, f: (i, 0)
    return pl.pallas_call(
        _moe_dense_kernel,
        grid=(S // tm, E, F // tf),
        in_specs=[
            pl.BlockSpec((tm, D), row),
            pl.BlockSpec((tm, ROUTE_LANES), row),
            pl.BlockSpec((None, D, tf), lambda i, e, f: (e, 0, f)),
            pl.BlockSpec((None, D, tf), lambda i, e, f: (e, 0, f)),
            pl.BlockSpec((None, tf, D), lambda i, e, f: (e, f, 0)),
            pl.BlockSpec((tm, D), row),
            pl.BlockSpec((1, D), lambda i, e, f: (0, 0)),
        ],
        out_specs=pl.BlockSpec((tm, D), row),
        out_shape=jax.ShapeDtypeStruct((S, D), F32),
        scratch_shapes=[pltpu.VMEM((tm, D), F32)],
        compiler_params=_params(("parallel", "arbitrary", "arbitrary")),
        name="moe_dense",
    )(h2, comb, wg, wu, wd, x1, g)


def _router_weights(w_r_group, b_r_group, w_r_expert, b_r_expert):
    D = w_r_group.shape[0]
    w = jnp.concatenate(
        [w_r_group, jnp.transpose(w_r_expert, (1, 0, 2)).reshape(D, N_EXPERTS)], axis=1)
    b = jnp.concatenate([b_r_group, b_r_expert.reshape(N_EXPERTS)])
    pad = ROUTE_LANES - w.shape[1]
    w = jnp.pad(w, ((0, 0), (0, pad)))
    b = jnp.pad(b, (0, pad)).reshape(1, ROUTE_LANES)
    w_hi = w.astype(BF16)
    w_lo = (w - w_hi.astype(F32)).astype(BF16)
    return w_hi, w_lo, b


def kernel(x, norm_mix, w_in, w_pool, pool_scale, w_branch_pool, w_branch_attn, w_out, norm_ffn,
           w_r_group, b_r_group, w_r_expert, b_r_expert, w_gate, w_up, w_down, norm_final):
    B, S, D = x.shape
    depth = w_in.shape[0]
    assert depth == 1, "the final rms_norm is fused into the expert kernel of a single layer"
    slopes = jnp.exp2(-8.0 * jnp.arange(1, ATTN_HEADS + 1, dtype=F32) / ATTN_HEADS)
    slopes = jnp.broadcast_to(slopes[:, None, None], (ATTN_HEADS, 1, 128))
    outs = []
    for b in range(B):
        xb = x[b]
        for l in range(depth):
            proj = _norm_inproj(xb, norm_mix[l].reshape(1, D), w_in[l].astype(BF16))
            y_pool = _pool_mixer(proj, w_pool[l].astype(BF16), pool_scale[l].reshape(1, POOL_WIDTH))
            qT, vT, sel = _moba_gate(proj)
            y_attn = _moba_attention(proj, qT, vT, sel, slopes)
            wr_hi, wr_lo, br = _router_weights(w_r_group[l], b_r_group[l], w_r_expert[l], b_r_expert[l])
            x1, h2, comb = _merge_route(
                y_pool, y_attn, proj, xb,
                w_branch_pool[l].astype(BF16), w_branch_attn[l].astype(BF16), w_out[l].astype(BF16),
                norm_ffn[l].reshape(1, D), wr_hi, wr_lo, br)
            xb = _moe_dense(h2, comb, w_gate[l].astype(BF16), w_up[l].astype(BF16),
                            w_down[l].astype(BF16), x1, norm_final.reshape(1, D))
        outs.append(xb)
    return jnp.stack(outs, axis=0)
```

```python
import functools

import jax
import jax.numpy as jnp
from jax import lax
from jax.experimental import pallas as pl
from jax.experimental.pallas import tpu as pltpu

F32 = jnp.float32
BF16 = jnp.bfloat16

POOL_WINDOWS = (2, 4, 8, 16)
MAX_WINDOW = 16
POOL_WIDTH = 1024
POOL_GROUP = 256
HEAD_DIM = 128
ATTN_HEADS = 8
ATTN_WIDTH = 1024
MOBA_BLOCK = 256
MOBA_TOPK = 3
N_GROUPS = 4
EXPERTS_PER_GROUP = 4
N_EXPERTS = 16
ROUTE_LANES = 128
EXPERT_LANE0 = N_GROUPS
RMS_EPS = 1e-6
NEG_INF = -1e30
LOG2_E = 1.4426950408889634
ATTN_KV_UNROLL = 2

V7X_VMEM_LIMIT_BYTES = 56 * 1024 * 1024


def _params(semantics, vmem=V7X_VMEM_LIMIT_BYTES):
    return pltpu.CompilerParams(dimension_semantics=semantics, vmem_limit_bytes=vmem)


def _norm_inproj_kernel(x_ref, g_ref, w_ref, o_ref, h_ref):
    @pl.when(pl.program_id(1) == 0)
    def _():
        x = x_ref[...]
        ms = jnp.mean(x * x, axis=-1, keepdims=True)
        h_ref[...] = (x * lax.rsqrt(ms + RMS_EPS) * g_ref[...]).astype(BF16)

    o_ref[...] = jnp.dot(h_ref[...], w_ref[...], preferred_element_type=F32).astype(o_ref.dtype)


def _norm_inproj(x, g, w, tm=512, tn=1024):
    S, D = x.shape
    N = w.shape[1]
    return pl.pallas_call(
        _norm_inproj_kernel,
        grid=(S // tm, N // tn),
        in_specs=[
            pl.BlockSpec((tm, D), lambda i, j: (i, 0)),
            pl.BlockSpec((1, D), lambda i, j: (0, 0)),
            pl.BlockSpec((D, tn), lambda i, j: (0, j)),
        ],
        out_specs=pl.BlockSpec((tm, tn), lambda i, j: (i, j)),
        out_shape=jax.ShapeDtypeStruct((S, N), BF16),
        scratch_shapes=[pltpu.VMEM((tm, D), BF16)],
        compiler_params=_params(("parallel", "arbitrary")),
        name="norm_inproj",
    )(x, g, w)


def _pool_kernel(cur_ref, prev_ref, w_ref, scale_ref, o_ref, ext_ref):
    i = pl.program_id(0)
    tm = cur_ref.shape[0]
    u = cur_ref[...].astype(F32)
    halo = jnp.where(i > 0, prev_ref[...].astype(F32), 0.0)
    ext_ref[0:MAX_WINDOW, :] = halo
    ext_ref[MAX_WINDOW:MAX_WINDOW + tm, :] = u
    t = i * tm + lax.broadcasted_iota(jnp.int32, (tm, 1), 0)
    for g, w in enumerate(POOL_WINDOWS):
        cols = slice(g * POOL_GROUP, (g + 1) * POOL_GROUP)
        ug = u[:, cols]
        wsum = ug
        for s in range(1, w):
            wsum = wsum + ext_ref[MAX_WINDOW - s:MAX_WINDOW - s + tm, cols]
        cnt = jnp.minimum(t + 1, w).astype(F32)
        mixed = (wsum / cnt - ug).astype(BF16)
        y = jnp.dot(mixed, w_ref[g], preferred_element_type=F32)
        o_ref[:, cols] = (y * scale_ref[:, cols]).astype(o_ref.dtype)


def _pool_mixer(proj, w_pool, pool_scale, tm=512):
    S = proj.shape[0]
    halo_blocks = tm // MAX_WINDOW
    return pl.pallas_call(
        _pool_kernel,
        grid=(S // tm,),
        in_specs=[
            pl.BlockSpec((tm, POOL_WIDTH), lambda i: (i, 0)),
            pl.BlockSpec((MAX_WINDOW, POOL_WIDTH),
                         lambda i: (jnp.maximum(i * halo_blocks - 1, 0), 0)),
            pl.BlockSpec((len(POOL_WINDOWS), POOL_GROUP, POOL_GROUP), lambda i: (0, 0, 0)),
            pl.BlockSpec((1, POOL_WIDTH), lambda i: (0, 0)),
        ],
        out_specs=pl.BlockSpec((tm, POOL_WIDTH), lambda i: (i, 0)),
        out_shape=jax.ShapeDtypeStruct((S, POOL_WIDTH), BF16),
        scratch_shapes=[pltpu.VMEM((tm + MAX_WINDOW, POOL_WIDTH), F32)],
        compiler_params=_params(("parallel",)),
        name="pool_mixer",
    )(proj, proj, w_pool, pool_scale)


def _moba_gate_kernel(q_ref, k_ref, v_ref, qT_ref, vT_ref, sel_ref):
    S = q_ref.shape[0]
    nb = S // MOBA_BLOCK
    topk = min(MOBA_TOPK, nb)
    kf = k_ref[...].astype(F32).reshape(nb, MOBA_BLOCK, HEAD_DIM)
    kmean = jnp.sum(kf, axis=1) * (1.0 / MOBA_BLOCK)
    km_hi = kmean.astype(BF16)
    km_lo = (kmean - km_hi.astype(F32)).astype(BF16)
    blk = lax.broadcasted_iota(jnp.int32, (nb, MOBA_BLOCK), 0)

    def body(i, carry):
        rows = pl.ds(pl.multiple_of(i * MOBA_BLOCK, MOBA_BLOCK), MOBA_BLOCK)
        qT = q_ref[rows, :].astype(F32).T.astype(BF16)
        qT_ref[i] = qT
        vT_ref[i] = v_ref[rows, :].astype(F32).T.astype(BF16)
        gate = (jnp.dot(km_hi, qT, preferred_element_type=F32)
                + jnp.dot(km_lo, qT, preferred_element_type=F32))
        gate = jnp.where(blk < i, gate, NEG_INF)
        sel = jnp.zeros((nb, MOBA_BLOCK), F32)
        for _ in range(topk):
            best = jnp.max(gate, axis=0, keepdims=True)
            idx = jnp.min(jnp.where(gate == best, blk, nb), axis=0, keepdims=True)
            hit = blk == idx
            sel = jnp.where(hit & (blk < i), 1.0, sel)
            gate = jnp.where(hit, -jnp.inf, gate)
        sel_ref[i] = sel
        return carry

    lax.fori_loop(0, nb, body, 0)


def _moba_gate(proj):
    S = proj.shape[0]
    nb = S // MOBA_BLOCK
    H = ATTN_HEADS
    q0 = POOL_WIDTH // HEAD_DIM
    k0 = q0 + H
    v0 = k0 + H
    blocked = lambda h: (h, 0, 0, 0)
    return pl.pallas_call(
        _moba_gate_kernel,
        grid=(H,),
        in_specs=[
            pl.BlockSpec((S, HEAD_DIM), lambda h: (0, q0 + h)),
            pl.BlockSpec((S, HEAD_DIM), lambda h: (0, k0 + h)),
            pl.BlockSpec((S, HEAD_DIM), lambda h: (0, v0 + h)),
        ],
        out_specs=[
            pl.BlockSpec((None, nb, HEAD_DIM, MOBA_BLOCK), blocked),
            pl.BlockSpec((None, nb, HEAD_DIM, MOBA_BLOCK), blocked),
            pl.BlockSpec((None, nb, nb, MOBA_BLOCK), blocked),
        ],
        out_shape=[
            jax.ShapeDtypeStruct((H, nb, HEAD_DIM, MOBA_BLOCK), BF16),
            jax.ShapeDtypeStruct((H, nb, HEAD_DIM, MOBA_BLOCK), BF16),
            jax.ShapeDtypeStruct((H, nb, nb, MOBA_BLOCK), F32),
        ],
        compiler_params=_params(("parallel",)),
        name="moba_gate",
    )(proj, proj, proj)


def _moba_attn_kernel(slope_ref, qT_ref, sel_ref, k_ref, vT_ref, o_ref,
                      s0_ref, s1_ref, p0_ref, p1_ref):
    i = pl.program_id(1)
    nb = sel_ref.shape[0]
    scale2 = (HEAD_DIM ** -0.5) * LOG2_E
    slope2 = slope_ref[...][:, 0:1] * LOG2_E
    qT = qT_ref[...]
    kpos = lax.broadcasted_iota(jnp.int32, (MOBA_BLOCK, MOBA_BLOCK), 0)
    qpos = lax.broadcasted_iota(jnp.int32, (MOBA_BLOCK, MOBA_BLOCK), 1)
    dist0 = (qpos - kpos).astype(F32)
    bias0 = slope2 * dist0

    def scores(j):
        rows = pl.ds(pl.multiple_of(j * MOBA_BLOCK, MOBA_BLOCK), MOBA_BLOCK)
        return jnp.dot(k_ref[rows, :], qT, preferred_element_type=F32) * scale2 - bias0

    s = jnp.where(dist0 >= 0, scores(i), NEG_INF)
    m = jnp.max(s, axis=0, keepdims=True)
    p = jnp.exp2(s - m)
    l = jnp.sum(p, axis=0, keepdims=True)
    acc = jnp.dot(vT_ref[i], p.astype(BF16), preferred_element_type=F32)

    def block_of(t, u):
        return jnp.clip(t * ATTN_KV_UNROLL + u, 0, nb - 1)

    def issue_scores(t, s_ref):
        for u in range(ATTN_KV_UNROLL):
            rows = pl.ds(pl.multiple_of(block_of(t, u) * MOBA_BLOCK, MOBA_BLOCK), MOBA_BLOCK)
            s_ref[u] = jnp.dot(k_ref[rows, :], qT, preferred_element_type=F32)

    def apply_probs(t, p_ref, alpha, acc):
        acc = alpha * acc
        for u in range(ATTN_KV_UNROLL):
            acc = acc + jnp.dot(vT_ref[block_of(t, u)], p_ref[u], preferred_element_type=F32)
        return acc

    def softmax_group(t, s_ref, p_ref, m, l):
        m_new = m
        shifts = []
        for u in range(ATTN_KV_UNROLL):
            j = t * ATTN_KV_UNROLL + u
            s = s_ref[u] * scale2 - bias0
            s_ref[u] = s
            valid = jnp.where(j < i, sel_ref[pl.ds(block_of(t, u), 1), :], 0.0) > 0.0
            gap = slope2 * ((i - j) * MOBA_BLOCK).astype(F32)
            top = jnp.max(s, axis=0, keepdims=True) - gap
            m_new = jnp.maximum(m_new, jnp.where(valid, top, NEG_INF))
            shifts.append((valid, gap))
        alpha = jnp.exp2(m - m_new)
        l = alpha * l
        for u, (valid, gap) in enumerate(shifts):
            p = jnp.exp2(s_ref[u] - jnp.where(valid, m_new + gap, jnp.inf))
            l = l + jnp.sum(p, axis=0, keepdims=True)
            p_ref[u] = p.astype(BF16)
        return m_new, l, alpha

    def body(r, carry):
        m, l, acc, alpha = carry
        acc = apply_probs(2 * r - 1, p1_ref, alpha, acc)
        m, l, alpha = softmax_group(2 * r, s0_ref, p0_ref, m, l)
        issue_scores(2 * r + 1, s1_ref)
        acc = apply_probs(2 * r, p0_ref, alpha, acc)
        m, l, alpha = softmax_group(2 * r + 1, s1_ref, p1_ref, m, l)
        issue_scores(2 * r + 2, s0_ref)
        return m, l, acc, alpha

    n_groups = (i + ATTN_KV_UNROLL - 1) // ATTN_KV_UNROLL
    n_pairs = n_groups // 2
    p1_ref[...] = jnp.zeros(p1_ref.shape, BF16)
    issue_scores(0, s0_ref)
    m, l, acc, alpha = lax.fori_loop(0, n_pairs, body, (m, l, acc, jnp.ones_like(m)))
    acc = apply_probs(2 * n_pairs - 1, p1_ref, alpha, acc)
    m, l, alpha = softmax_group(2 * n_pairs, s0_ref, p0_ref, m, l)
    acc = apply_probs(2 * n_pairs, p0_ref, alpha, acc)
    o_ref[...] = (acc / l).T.astype(o_ref.dtype)


def _moba_attention(proj, qT, vT, sel, slopes):
    S = proj.shape[0]
    nb = S // MOBA_BLOCK
    H = ATTN_HEADS
    k0 = POOL_WIDTH // HEAD_DIM + H
    return pl.pallas_call(
        _moba_attn_kernel,
        grid=(H, nb),
        in_specs=[
            pl.BlockSpec((None, 1, 128), lambda h, i: (h, 0, 0)),
            pl.BlockSpec((None, None, HEAD_DIM, MOBA_BLOCK), lambda h, i: (h, i, 0, 0)),
            pl.BlockSpec((None, None, nb, MOBA_BLOCK), lambda h, i: (h, i, 0, 0)),
            pl.BlockSpec((S, HEAD_DIM), lambda h, i: (0, k0 + h)),
            pl.BlockSpec((None, nb, HEAD_DIM, MOBA_BLOCK), lambda h, i: (h, 0, 0, 0)),
        ],
        out_specs=pl.BlockSpec((MOBA_BLOCK, HEAD_DIM), lambda h, i: (i, h)),
        out_shape=jax.ShapeDtypeStruct((S, ATTN_WIDTH), BF16),
        scratch_shapes=[
            pltpu.VMEM((ATTN_KV_UNROLL, MOBA_BLOCK, MOBA_BLOCK), F32),
            pltpu.VMEM((ATTN_KV_UNROLL, MOBA_BLOCK, MOBA_BLOCK), F32),
            pltpu.VMEM((ATTN_KV_UNROLL, MOBA_BLOCK, MOBA_BLOCK), BF16),
            pltpu.VMEM((ATTN_KV_UNROLL, MOBA_BLOCK, MOBA_BLOCK), BF16),
        ],
        compiler_params=_params(("parallel", "arbitrary")),
        name="moba_attention",
    )(slopes, qT, sel, proj, vT)


def _first_lane_of_max(vals, lane):
    best = jnp.max(vals, axis=1, keepdims=True)
    idx = jnp.min(jnp.where(vals == best, lane, ROUTE_LANES), axis=1, keepdims=True)
    return best, idx


def _merge_route_kernel(yp_ref, ya_ref, glp_ref, gla_ref, x_ref, wbp_ref, wba_ref, wout_ref,
                        g_ref, wr_hi_ref, wr_lo_ref, br_ref, x1_ref, h2_ref, comb_ref):
    bp = jnp.dot(yp_ref[...], wbp_ref[...], preferred_element_type=F32)
    ba = jnp.dot(ya_ref[...], wba_ref[...], preferred_element_type=F32)
    merged = (jax.nn.sigmoid(glp_ref[...].astype(F32)) * bp
              + jax.nn.sigmoid(gla_ref[...].astype(F32)) * ba)
    x1 = x_ref[...] + jnp.dot(merged.astype(BF16), wout_ref[...], preferred_element_type=F32)
    x1_ref[...] = x1
    ms = jnp.mean(x1 * x1, axis=-1, keepdims=True)
    h2 = x1 * lax.rsqrt(ms + RMS_EPS) * g_ref[...]
    h2_hi = h2.astype(BF16)
    h2_ref[...] = h2_hi
    h2_lo = (h2 - h2_hi.astype(F32)).astype(BF16)
    logits = (jnp.dot(h2_hi, wr_hi_ref[...], preferred_element_type=F32)
              + jnp.dot(h2_lo, wr_hi_ref[...], preferred_element_type=F32)
              + jnp.dot(h2_hi, wr_lo_ref[...], preferred_element_type=F32)
              + br_ref[...])

    lane = lax.broadcasted_iota(jnp.int32, logits.shape, 1)
    g_logits = jnp.where(lane < N_GROUPS, logits, -jnp.inf)
    g_best, g_idx = _first_lane_of_max(g_logits, lane)
    g_w = 1.0 / jnp.sum(jnp.exp(g_logits - g_best), axis=1, keepdims=True)
    e_lo = EXPERT_LANE0 + EXPERTS_PER_GROUP * g_idx
    e_logits = jnp.where((lane >= e_lo) & (lane < e_lo + EXPERTS_PER_GROUP), logits, -jnp.inf)
    v1, i1 = _first_lane_of_max(e_logits, lane)
    v2, i2 = _first_lane_of_max(jnp.where(lane == i1, -jnp.inf, e_logits), lane)
    e21 = jnp.exp(v2 - v1)
    w1 = g_w / (1.0 + e21)
    w2 = g_w * e21 / (1.0 + e21)
    comb_ref[...] = jnp.where(lane == i1, w1, 0.0) + jnp.where(lane == i2, w2, 0.0)


def _merge_route(yp, ya, proj, x, wbp, wba, wout, g, wr_hi, wr_lo, br, tm=256):
    S, D = x.shape
    full = lambda i: (0, 0)
    row = lambda i: (i, 0)
    glp_blk = (POOL_WIDTH + 3 * ATTN_WIDTH) // D
    return pl.pallas_call(
        _merge_route_kernel,
        grid=(S // tm,),
        in_specs=[
            pl.BlockSpec((tm, POOL_WIDTH), row),
            pl.BlockSpec((tm, ATTN_WIDTH), row),
            pl.BlockSpec((tm, D), lambda i: (i, glp_blk)),
            pl.BlockSpec((tm, D), lambda i: (i, glp_blk + 1)),
            pl.BlockSpec((tm, D), row),
            pl.BlockSpec((POOL_WIDTH, D), full),
            pl.BlockSpec((ATTN_WIDTH, D), full),
            pl.BlockSpec((D, D), full),
            pl.BlockSpec((1, D), full),
            pl.BlockSpec((D, ROUTE_LANES), full),
            pl.BlockSpec((D, ROUTE_LANES), full),
            pl.BlockSpec((1, ROUTE_LANES), full),
        ],
        out_specs=[
            pl.BlockSpec((tm, D), row),
            pl.BlockSpec((tm, D), row),
            pl.BlockSpec((tm, ROUTE_LANES), row),
        ],
        out_shape=[
            jax.ShapeDtypeStruct((S, D), F32),
            jax.ShapeDtypeStruct((S, D), BF16),
            jax.ShapeDtypeStruct((S, ROUTE_LANES), F32),
        ],
        compiler_params=_params(("parallel",)),
        name="merge_route",
    )(yp, ya, proj, proj, x, wbp, wba, wout, g, wr_hi, wr_lo, br)


def _moe_dense_kernel(h_ref, comb_ref, wg_ref, wu_ref, wd_ref, x1_ref, g_ref, o_ref, acc_ref):
    e = pl.program_id(1)
    f = pl.program_id(2)
    first = (e == 0) & (f == 0)
    last = (e == pl.num_programs(1) - 1) & (f == pl.num_programs(2) - 1)

    @pl.when(first)
    def _():
        acc_ref[...] = jnp.zeros_like(acc_ref)

    h = h_ref[...]
    a = jnp.dot(h, wg_ref[...], preferred_element_type=F32)
    u = jnp.dot(h, wu_ref[...], preferred_element_type=F32)
    lane = lax.broadcasted_iota(jnp.int32, comb_ref.shape, 1)
    c = jnp.sum(jnp.where(lane == e + EXPERT_LANE0, comb_ref[...], 0.0), axis=1, keepdims=True)
    hid = jax.nn.silu(a) * u * c
    acc_ref[...] += jnp.dot(hid.astype(BF16), wd_ref[...], preferred_element_type=F32)

    @pl.when(last)
    def _():
        y = x1_ref[...] + acc_ref[...]
        ms = jnp.mean(y * y, axis=-1, keepdims=True)
        o_ref[...] = y * lax.rsqrt(ms + RMS_EPS) * g_ref[...]


def _moe_dense(h2, comb, wg, wu, wd, x1, g, tm=512, tf=512):
    S, D = x1.shape
    E, _, F = wg.shape
    row = lambda i, e, f: (i, 0)
    return pl.pallas_call(
        _moe_dense_kernel,
        grid=(S // tm, E, F // tf),
        in_specs=[
            pl.BlockSpec((tm, D), row),
            pl.BlockSpec((tm, ROUTE_LANES), row),
            pl.BlockSpec((None, D, tf), lambda i, e, f: (e, 0, f)),
            pl.BlockSpec((None, D, tf), lambda i, e, f: (e, 0, f)),
            pl.BlockSpec((None, tf, D), lambda i, e, f: (e, f, 0)),
            pl.BlockSpec((tm, D), row),
            pl.BlockSpec((1, D), lambda i, e, f: (0, 0)),
        ],
        out_specs=pl.BlockSpec((tm, D), row),
        out_shape=jax.ShapeDtypeStruct((S, D), F32),
        scratch_shapes=[pltpu.VMEM((tm, D), F32)],
        compiler_params=_params(("parallel", "arbitrary", "arbitrary")),
        name="moe_dense",
    )(h2, comb, wg, wu, wd, x1, g)


def _router_weights(w_r_group, b_r_group, w_r_expert, b_r_expert):
    D = w_r_group.shape[0]
    w = jnp.concatenate(
        [w_r_group, jnp.transpose(w_r_expert, (1, 0, 2)).reshape(D, N_EXPERTS)], axis=1)
    b = jnp.concatenate([b_r_group, b_r_expert.reshape(N_EXPERTS)])
    pad = ROUTE_LANES - w.shape[1]
    w = jnp.pad(w, ((0, 0), (0, pad)))
    b = jnp.pad(b, (0, pad)).reshape(1, ROUTE_LANES)
    w_hi = w.astype(BF16)
    w_lo = (w - w_hi.astype(F32)).astype(BF16)
    return w_hi, w_lo, b


def kernel(x, norm_mix, w_in, w_pool, pool_scale, w_branch_pool, w_branch_attn, w_out, norm_ffn,
           w_r_group, b_r_group, w_r_expert, b_r_expert, w_gate, w_up, w_down, norm_final):
    B, S, D = x.shape
    depth = w_in.shape[0]
    assert depth == 1, "the final rms_norm is fused into the expert kernel of a single layer"
    slopes = jnp.exp2(-8.0 * jnp.arange(1, ATTN_HEADS + 1, dtype=F32) / ATTN_HEADS)
    slopes = jnp.broadcast_to(slopes[:, None, None], (ATTN_HEADS, 1, 128))
    outs = []
    for b in range(B):
        xb = x[b]
        for l in range(depth):
            proj = _norm_inproj(xb, norm_mix[l].reshape(1, D), w_in[l].astype(BF16))
            y_pool = _pool_mixer(proj, w_pool[l].astype(BF16), pool_scale[l].reshape(1, POOL_WIDTH))
            qT, vT, sel = _moba_gate(proj)
            y_attn = _moba_attention(proj, qT, vT, sel, slopes)
            wr_hi, wr_lo, br = _router_weights(w_r_group[l], b_r_group[l], w_r_expert[l], b_r_expert[l])
            x1, h2, comb = _merge_route(
                y_pool, y_attn, proj, xb,
                w_branch_pool[l].astype(BF16), w_branch_attn[l].astype(BF16), w_out[l].astype(BF16),
                norm_ffn[l].reshape(1, D), wr_hi, wr_lo, br)
            xb = _moe_dense(h2, comb, w_gate[l].astype(BF16), w_up[l].astype(BF16),
                            w_down[l].astype(BF16), x1, norm_final.reshape(1, D))
        outs.append(xb)
    return jnp.stack(outs, axis=0)
```

```python
import functools

import jax
import jax.numpy as jnp
from jax import lax
from jax.experimental import pallas as pl
from jax.experimental.pallas import tpu as pltpu

F32 = jnp.float32
BF16 = jnp.bfloat16

POOL_WINDOWS = (2, 4, 8, 16)
MAX_WINDOW = 16
POOL_WIDTH = 1024
POOL_GROUP = 256
HEAD_DIM = 128
ATTN_HEADS = 8
ATTN_WIDTH = 1024
MOBA_BLOCK = 256
MOBA_TOPK = 3
N_GROUPS = 4
EXPERTS_PER_GROUP = 4
N_EXPERTS = 16
ROUTE_LANES = 128
EXPERT_LANE0 = N_GROUPS
RMS_EPS = 1e-6
NEG_INF = -1e30
LOG2_E = 1.4426950408889634
ATTN_KV_UNROLL = 2
MOE_CHUNK = 16
MOE_TOKEN_TILE = 256
MOE_ROW_TILE = 256
MOE_CHUNKS_PER_TILE = MOE_ROW_TILE // MOE_CHUNK

V7X_VMEM_LIMIT_BYTES = 56 * 1024 * 1024


def _params(semantics, vmem=V7X_VMEM_LIMIT_BYTES):
    return pltpu.CompilerParams(dimension_semantics=semantics, vmem_limit_bytes=vmem)


def _norm_inproj_kernel(x_ref, g_ref, w_ref, o_ref, h_ref):
    @pl.when(pl.program_id(1) == 0)
    def _():
        x = x_ref[...]
        ms = jnp.mean(x * x, axis=-1, keepdims=True)
        h_ref[...] = (x * lax.rsqrt(ms + RMS_EPS) * g_ref[...]).astype(BF16)

    o_ref[...] = jnp.dot(h_ref[...], w_ref[...], preferred_element_type=F32).astype(o_ref.dtype)


def _norm_inproj(x, g, w, tm=512, tn=1024):
    S, D = x.shape
    N = w.shape[1]
    return pl.pallas_call(
        _norm_inproj_kernel,
        grid=(S // tm, N // tn),
        in_specs=[
            pl.BlockSpec((tm, D), lambda i, j: (i, 0)),
            pl.BlockSpec((1, D), lambda i, j: (0, 0)),
            pl.BlockSpec((D, tn), lambda i, j: (0, j)),
        ],
        out_specs=pl.BlockSpec((tm, tn), lambda i, j: (i, j)),
        out_shape=jax.ShapeDtypeStruct((S, N), BF16),
        scratch_shapes=[pltpu.VMEM((tm, D), BF16)],
        compiler_params=_params(("parallel", "arbitrary")),
        name="norm_inproj",
    )(x, g, w)


def _pool_kernel(cur_ref, prev_ref, w_ref, scale_ref, o_ref, ext_ref):
    i = pl.program_id(0)
    tm = cur_ref.shape[0]
    u = cur_ref[...].astype(F32)
    halo = jnp.where(i > 0, prev_ref[...].astype(F32), 0.0)
    ext_ref[0:MAX_WINDOW, :] = halo
    ext_ref[MAX_WINDOW:MAX_WINDOW + tm, :] = u
    t = i * tm + lax.broadcasted_iota(jnp.int32, (tm, 1), 0)
    for g, w in enumerate(POOL_WINDOWS):
        cols = slice(g * POOL_GROUP, (g + 1) * POOL_GROUP)
        ug = u[:, cols]
        wsum = ug
        for s in range(1, w):
            wsum = wsum + ext_ref[MAX_WINDOW - s:MAX_WINDOW - s + tm, cols]
        cnt = jnp.minimum(t + 1, w).astype(F32)
        mixed = (wsum / cnt - ug).astype(BF16)
        y = jnp.dot(mixed, w_ref[g], preferred_element_type=F32)
        o_ref[:, cols] = (y * scale_ref[:, cols]).astype(o_ref.dtype)


def _pool_mixer(proj, w_pool, pool_scale, tm=512):
    S = proj.shape[0]
    halo_blocks = tm // MAX_WINDOW
    return pl.pallas_call(
        _pool_kernel,
        grid=(S // tm,),
        in_specs=[
            pl.BlockSpec((tm, POOL_WIDTH), lambda i: (i, 0)),
            pl.BlockSpec((MAX_WINDOW, POOL_WIDTH),
                         lambda i: (jnp.maximum(i * halo_blocks - 1, 0), 0)),
            pl.BlockSpec((len(POOL_WINDOWS), POOL_GROUP, POOL_GROUP), lambda i: (0, 0, 0)),
            pl.BlockSpec((1, POOL_WIDTH), lambda i: (0, 0)),
        ],
        out_specs=pl.BlockSpec((tm, POOL_WIDTH), lambda i: (i, 0)),
        out_shape=jax.ShapeDtypeStruct((S, POOL_WIDTH), BF16),
        scratch_shapes=[pltpu.VMEM((tm + MAX_WINDOW, POOL_WIDTH), F32)],
        compiler_params=_params(("parallel",)),
        name="pool_mixer",
    )(proj, proj, w_pool, pool_scale)


def _moba_gate_kernel(q_ref, k_ref, v_ref, qT_ref, vT_ref, sel_ref):
    S = q_ref.shape[0]
    nb = S // MOBA_BLOCK
    topk = min(MOBA_TOPK, nb)
    kf = k_ref[...].astype(F32).reshape(nb, MOBA_BLOCK, HEAD_DIM)
    kmean = jnp.sum(kf, axis=1) * (1.0 / MOBA_BLOCK)
    km_hi = kmean.astype(BF16)
    km_lo = (kmean - km_hi.astype(F32)).astype(BF16)
    blk = lax.broadcasted_iota(jnp.int32, (nb, MOBA_BLOCK), 0)

    def body(i, carry):
        rows = pl.ds(pl.multiple_of(i * MOBA_BLOCK, MOBA_BLOCK), MOBA_BLOCK)
        qT = q_ref[rows, :].astype(F32).T.astype(BF16)
        qT_ref[i] = qT
        vT_ref[i] = v_ref[rows, :].astype(F32).T.astype(BF16)
        gate = (jnp.dot(km_hi, qT, preferred_element_type=F32)
                + jnp.dot(km_lo, qT, preferred_element_type=F32))
        gate = jnp.where(blk < i, gate, NEG_INF)
        sel = jnp.zeros((nb, MOBA_BLOCK), F32)
        for _ in range(topk):
            best = jnp.max(gate, axis=0, keepdims=True)
            idx = jnp.min(jnp.where(gate == best, blk, nb), axis=0, keepdims=True)
            hit = blk == idx
            sel = jnp.where(hit & (blk < i), 1.0, sel)
            gate = jnp.where(hit, -jnp.inf, gate)
        sel_ref[i] = sel
        return carry

    lax.fori_loop(0, nb, body, 0)


def _moba_gate(proj):
    S = proj.shape[0]
    nb = S // MOBA_BLOCK
    H = ATTN_HEADS
    q0 = POOL_WIDTH // HEAD_DIM
    k0 = q0 + H
    v0 = k0 + H
    blocked = lambda h: (h, 0, 0, 0)
    return pl.pallas_call(
        _moba_gate_kernel,
        grid=(H,),
        in_specs=[
            pl.BlockSpec((S, HEAD_DIM), lambda h: (0, q0 + h)),
            pl.BlockSpec((S, HEAD_DIM), lambda h: (0, k0 + h)),
            pl.BlockSpec((S, HEAD_DIM), lambda h: (0, v0 + h)),
        ],
        out_specs=[
            pl.BlockSpec((None, nb, HEAD_DIM, MOBA_BLOCK), blocked),
            pl.BlockSpec((None, nb, HEAD_DIM, MOBA_BLOCK), blocked),
            pl.BlockSpec((None, nb, nb, MOBA_BLOCK), blocked),
        ],
        out_shape=[
            jax.ShapeDtypeStruct((H, nb, HEAD_DIM, MOBA_BLOCK), BF16),
            jax.ShapeDtypeStruct((H, nb, HEAD_DIM, MOBA_BLOCK), BF16),
            jax.ShapeDtypeStruct((H, nb, nb, MOBA_BLOCK), F32),
        ],
        compiler_params=_params(("parallel",)),
        name="moba_gate",
    )(proj, proj, proj)


def _moba_attn_kernel(slope_ref, qT_ref, sel_ref, k_ref, vT_ref, o_ref,
                      s0_ref, s1_ref, p0_ref, p1_ref):
    i = pl.program_id(1)
    nb = sel_ref.shape[0]
    scale2 = (HEAD_DIM ** -0.5) * LOG2_E
    slope2 = slope_ref[...][:, 0:1] * LOG2_E
    qT = qT_ref[...]
    kpos = lax.broadcasted_iota(jnp.int32, (MOBA_BLOCK, MOBA_BLOCK), 0)
    qpos = lax.broadcasted_iota(jnp.int32, (MOBA_BLOCK, MOBA_BLOCK), 1)
    dist0 = (qpos - kpos).astype(F32)
    bias0 = slope2 * dist0

    def scores(j):
        rows = pl.ds(pl.multiple_of(j * MOBA_BLOCK, MOBA_BLOCK), MOBA_BLOCK)
        return jnp.dot(k_ref[rows, :], qT, preferred_element_type=F32) * scale2 - bias0

    s = jnp.where(dist0 >= 0, scores(i), NEG_INF)
    m = jnp.max(s, axis=0, keepdims=True)
    p = jnp.exp2(s - m)
    l = jnp.sum(p, axis=0, keepdims=True)
    acc = jnp.dot(vT_ref[i], p.astype(BF16), preferred_element_type=F32)

    def block_of(t, u):
        return jnp.clip(t * ATTN_KV_UNROLL + u, 0, nb - 1)

    def issue_scores(t, s_ref):
        for u in range(ATTN_KV_UNROLL):
            rows = pl.ds(pl.multiple_of(block_of(t, u) * MOBA_BLOCK, MOBA_BLOCK), MOBA_BLOCK)
            s_ref[u] = jnp.dot(k_ref[rows, :], qT, preferred_element_type=F32)

    def apply_probs(t, p_ref, alpha, acc):
        acc = alpha * acc
        for u in range(ATTN_KV_UNROLL):
            acc = acc + jnp.dot(vT_ref[block_of(t, u)], p_ref[u], preferred_element_type=F32)
        return acc

    def softmax_group(t, s_ref, p_ref, m, l):
        m_new = m
        shifts = []
        for u in range(ATTN_KV_UNROLL):
            j = t * ATTN_KV_UNROLL + u
            s = s_ref[u] * scale2 - bias0
            s_ref[u] = s
            valid = jnp.where(j < i, sel_ref[pl.ds(block_of(t, u), 1), :], 0.0) > 0.0
            gap = slope2 * ((i - j) * MOBA_BLOCK).astype(F32)
            top = jnp.max(s, axis=0, keepdims=True) - gap
            m_new = jnp.maximum(m_new, jnp.where(valid, top, NEG_INF))
            shifts.append((valid, gap))
        alpha = jnp.exp2(m - m_new)
        l = alpha * l
        for u, (valid, gap) in enumerate(shifts):
            p = jnp.exp2(s_ref[u] - jnp.where(valid, m_new + gap, jnp.inf))
            l = l + jnp.sum(p, axis=0, keepdims=True)
            p_ref[u] = p.astype(BF16)
        return m_new, l, alpha

    def body(r, carry):
        m, l, acc, alpha = carry
        acc = apply_probs(2 * r - 1, p1_ref, alpha, acc)
        m, l, alpha = softmax_group(2 * r, s0_ref, p0_ref, m, l)
        issue_scores(2 * r + 1, s1_ref)
        acc = apply_probs(2 * r, p0_ref, alpha, acc)
        m, l, alpha = softmax_group(2 * r + 1, s1_ref, p1_ref, m, l)
        issue_scores(2 * r + 2, s0_ref)
        return m, l, acc, alpha

    n_groups = (i + ATTN_KV_UNROLL - 1) // ATTN_KV_UNROLL
    n_pairs = n_groups // 2
    p1_ref[...] = jnp.zeros(p1_ref.shape, BF16)
    issue_scores(0, s0_ref)
    m, l, acc, alpha = lax.fori_loop(0, n_pairs, body, (m, l, acc, jnp.ones_like(m)))
    acc = apply_probs(2 * n_pairs - 1, p1_ref, alpha, acc)
    m, l, alpha = softmax_group(2 * n_pairs, s0_ref, p0_ref, m, l)
    acc = apply_probs(2 * n_pairs, p0_ref, alpha, acc)
    o_ref[...] = (acc / l).T.astype(o_ref.dtype)


def _moba_attention(proj, qT, vT, sel, slopes):
    S = proj.shape[0]
    nb = S // MOBA_BLOCK
    H = ATTN_HEADS
    k0 = POOL_WIDTH // HEAD_DIM + H
    return pl.pallas_call(
        _moba_attn_kernel,
        grid=(H, nb),
        in_specs=[
            pl.BlockSpec((None, 1, 128), lambda h, i: (h, 0, 0)),
            pl.BlockSpec((None, None, HEAD_DIM, MOBA_BLOCK), lambda h, i: (h, i, 0, 0)),
            pl.BlockSpec((None, None, nb, MOBA_BLOCK), lambda h, i: (h, i, 0, 0)),
            pl.BlockSpec((S, HEAD_DIM), lambda h, i: (0, k0 + h)),
            pl.BlockSpec((None, nb, HEAD_DIM, MOBA_BLOCK), lambda h, i: (h, 0, 0, 0)),
        ],
        out_specs=pl.BlockSpec((MOBA_BLOCK, HEAD_DIM), lambda h, i: (i, h)),
        out_shape=jax.ShapeDtypeStruct((S, ATTN_WIDTH), BF16),
        scratch_shapes=[
            pltpu.VMEM((ATTN_KV_UNROLL, MOBA_BLOCK, MOBA_BLOCK), F32),
            pltpu.VMEM((ATTN_KV_UNROLL, MOBA_BLOCK, MOBA_BLOCK), F32),
            pltpu.VMEM((ATTN_KV_UNROLL, MOBA_BLOCK, MOBA_BLOCK), BF16),
            pltpu.VMEM((ATTN_KV_UNROLL, MOBA_BLOCK, MOBA_BLOCK), BF16),
        ],
        compiler_params=_params(("parallel", "arbitrary")),
        name="moba_attention",
    )(slopes, qT, sel, proj, vT)


def _first_lane_of_max(vals, lane):
    best = jnp.max(vals, axis=1, keepdims=True)
    idx = jnp.min(jnp.where(vals == best, lane, ROUTE_LANES), axis=1, keepdims=True)
    return best, idx


def _merge_route_kernel(yp_ref, ya_ref, glp_ref, gla_ref, x_ref, wbp_ref, wba_ref, wout_ref,
                        g_ref, wr_hi_ref, wr_lo_ref, br_ref, x1_ref, hs_ref, route_ref, cnt_ref):
    tm = x_ref.shape[0]
    cap = hs_ref.shape[0]
    bp = jnp.dot(yp_ref[...], wbp_ref[...], preferred_element_type=F32)
    ba = jnp.dot(ya_ref[...], wba_ref[...], preferred_element_type=F32)
    merged = (jax.nn.sigmoid(glp_ref[...].astype(F32)) * bp
              + jax.nn.sigmoid(gla_ref[...].astype(F32)) * ba)
    x1 = x_ref[...] + jnp.dot(merged.astype(BF16), wout_ref[...], preferred_element_type=F32)
    x1_ref[...] = x1
    ms = jnp.mean(x1 * x1, axis=-1, keepdims=True)
    h2 = x1 * lax.rsqrt(ms + RMS_EPS) * g_ref[...]
    h2_hi = h2.astype(BF16)
    h2_lo = (h2 - h2_hi.astype(F32)).astype(BF16)
    logits = (jnp.dot(h2_hi, wr_hi_ref[...], preferred_element_type=F32)
              + jnp.dot(h2_lo, wr_hi_ref[...], preferred_element_type=F32)
              + jnp.dot(h2_hi, wr_lo_ref[...], preferred_element_type=F32)
              + br_ref[...])

    lane = lax.broadcasted_iota(jnp.int32, logits.shape, 1)
    g_logits = jnp.where(lane < N_GROUPS, logits, -jnp.inf)
    g_best, g_idx = _first_lane_of_max(g_logits, lane)
    g_w = 1.0 / jnp.sum(jnp.exp(g_logits - g_best), axis=1, keepdims=True)
    e_lo = EXPERT_LANE0 + EXPERTS_PER_GROUP * g_idx
    e_logits = jnp.where((lane >= e_lo) & (lane < e_lo + EXPERTS_PER_GROUP), logits, -jnp.inf)
    v1, i1 = _first_lane_of_max(e_logits, lane)
    v2, i2 = _first_lane_of_max(jnp.where(lane == i1, -jnp.inf, e_logits), lane)
    e21 = jnp.exp(v2 - v1)
    w1 = g_w / (1.0 + e21)
    w2 = g_w * e21 / (1.0 + e21)

    hit1 = lane == i1
    hit2 = lane == i2
    member = jnp.where(hit1, 1.0, jnp.where(hit2, 1.0, 0.0))
    r_tok = lax.broadcasted_iota(jnp.int32, (tm, tm), 0)
    c_tok = lax.broadcasted_iota(jnp.int32, (tm, tm), 1)
    earlier = jnp.where(c_tok < r_tok, 1.0, 0.0).astype(BF16)
    rank = jnp.dot(earlier, member.astype(BF16), preferred_element_type=F32)
    count = jnp.sum(member, axis=0, keepdims=True)
    chunks = jnp.floor((count + (MOE_CHUNK - 1)) * (1.0 / MOE_CHUNK))
    r_l = lax.broadcasted_iota(jnp.int32, (ROUTE_LANES, ROUTE_LANES), 0)
    c_l = lax.broadcasted_iota(jnp.int32, (ROUTE_LANES, ROUTE_LANES), 1)
    lower_lanes = jnp.where(r_l < c_l, 1.0, 0.0).astype(BF16)
    start = jnp.dot(jnp.broadcast_to(chunks, (8, ROUTE_LANES)).astype(BF16), lower_lanes,
                    preferred_element_type=F32)[0:1] * MOE_CHUNK
    pos = start + rank
    pos1 = jnp.sum(jnp.where(hit1, pos, 0.0), axis=1, keepdims=True)
    pos2 = jnp.sum(jnp.where(hit2, pos, 0.0), axis=1, keepdims=True)
    route = jnp.where(lane == 0, pos1, jnp.where(lane == 1, pos2,
                      jnp.where(lane == 2, w1, jnp.where(lane == 3, w2, 0.0))))
    route_ref[...] = route
    cnt_ref[...] = chunks
    route_t = route.T
    slot = lax.broadcasted_iota(jnp.int32, (cap, tm), 0)
    p1_row = route_t[0:1, :].astype(jnp.int32)
    p2_row = route_t[1:2, :].astype(jnp.int32)
    onehot = jnp.where(slot == p1_row, 1.0, jnp.where(slot == p2_row, 1.0, 0.0)).astype(BF16)
    hs_ref[...] = jnp.dot(onehot, h2_hi, preferred_element_type=F32).astype(BF16)


def _moe_cap(tm):
    worst = 2 * tm + N_EXPERTS * (MOE_CHUNK - 1)
    return -(-worst // 128) * 128


def _merge_route(yp, ya, proj, x, wbp, wba, wout, g, wr_hi, wr_lo, br, tm=MOE_TOKEN_TILE):
    S, D = x.shape
    nT = S // tm
    cap = _moe_cap(tm)
    full = lambda i: (0, 0)
    row = lambda i: (i, 0)
    glp_blk = (POOL_WIDTH + 3 * ATTN_WIDTH) // D
    return pl.pallas_call(
        _merge_route_kernel,
        grid=(S // tm,),
        in_specs=[
            pl.BlockSpec((tm, POOL_WIDTH), row),
            pl.BlockSpec((tm, ATTN_WIDTH), row),
            pl.BlockSpec((tm, D), lambda i: (i, glp_blk)),
            pl.BlockSpec((tm, D), lambda i: (i, glp_blk + 1)),
            pl.BlockSpec((tm, D), row),
            pl.BlockSpec((POOL_WIDTH, D), full),
            pl.BlockSpec((ATTN_WIDTH, D), full),
            pl.BlockSpec((D, D), full),
            pl.BlockSpec((1, D), full),
            pl.BlockSpec((D, ROUTE_LANES), full),
            pl.BlockSpec((D, ROUTE_LANES), full),
            pl.BlockSpec((1, ROUTE_LANES), full),
        ],
        out_specs=[
            pl.BlockSpec((tm, D), row),
            pl.BlockSpec((None, cap, D), lambda i: (i, 0, 0)),
            pl.BlockSpec((tm, ROUTE_LANES), row),
            pl.BlockSpec((None, 1, ROUTE_LANES), lambda i: (i, 0, 0)),
        ],
        out_shape=[
            jax.ShapeDtypeStruct((S, D), F32),
            jax.ShapeDtypeStruct((nT, cap, D), BF16),
            jax.ShapeDtypeStruct((S, ROUTE_LANES), F32),
            jax.ShapeDtypeStruct((nT, 1, ROUTE_LANES), F32),
        ],
        compiler_params=_params(("parallel",)),
        name="merge_route",
    )(yp, ya, proj, proj, x, wbp, wba, wout, g, wr_hi, wr_lo, br)


def _moe_plan(chunk_counts, n_row_tiles, n_chunk_slots):
    nT, E = chunk_counts.shape
    seg_start = jnp.cumsum(chunk_counts, axis=1) - chunk_counts
    per_expert = chunk_counts.T
    n_chunks = per_expert.sum(axis=1)
    padded = -(-n_chunks // MOE_CHUNKS_PER_TILE) * MOE_CHUNKS_PER_TILE
    e_end = jnp.cumsum(padded)
    e_start = e_end - padded
    n_used = e_end[-1] // MOE_CHUNKS_PER_TILE
    c = jnp.arange(n_chunk_slots, dtype=jnp.int32)
    e_of_c = jnp.minimum(jnp.searchsorted(e_end, c, side="right"), E - 1).astype(jnp.int32)
    local = c - e_start[e_of_c]
    real = local < n_chunks[e_of_c]
    seg_end = jnp.cumsum(per_expert, axis=1)
    t_of_c = jnp.minimum((seg_end[e_of_c] <= local[:, None]).sum(axis=1), nT - 1).astype(jnp.int32)
    within = local - (seg_end[e_of_c, t_of_c] - per_expert[e_of_c, t_of_c])
    src_tile = jnp.where(real, t_of_c, 0).astype(jnp.int32)
    src_row = jnp.where(real, (seg_start[t_of_c, e_of_c] + within) * MOE_CHUNK, 0).astype(jnp.int32)
    tile = jnp.arange(n_row_tiles, dtype=jnp.int32)
    last_used_expert = e_of_c[(n_used - 1) * MOE_CHUNKS_PER_TILE]
    tile_expert = jnp.where(tile < n_used, e_of_c[tile * MOE_CHUNKS_PER_TILE], last_used_expert)
    tile_real = real.reshape(n_row_tiles, MOE_CHUNKS_PER_TILE).sum(axis=1)
    return (tile_expert.astype(jnp.int32), tile_real.astype(jnp.int32), src_tile, src_row,
            n_used.reshape(1).astype(jnp.int32))


def _moe_ffn_kernel(texp_ref, treal_ref, ctile_ref, crow_ref, nused_ref,
                    hs_hbm, wg_ref, wu_ref, wd_ref, ys_in_hbm, ys_hbm,
                    xbuf, ybuf, gather_sem, scatter_sem):
    del texp_ref, ys_in_hbm
    i = pl.program_id(0)
    n_used = nused_ref[0]
    slot = lax.rem(i, 2)

    def chunk_rows(c):
        return pl.ds(pl.multiple_of(c * MOE_CHUNK, MOE_CHUNK), MOE_CHUNK)

    def gather_copy(tile, c, buf):
        g = tile * MOE_CHUNKS_PER_TILE + c
        src = hs_hbm.at[ctile_ref[g], pl.ds(pl.multiple_of(crow_ref[g], MOE_CHUNK), MOE_CHUNK), :]
        return pltpu.make_async_copy(src, xbuf.at[buf, chunk_rows(c), :], gather_sem.at[buf])

    def scatter_copy(tile, c, buf):
        g = tile * MOE_CHUNKS_PER_TILE + c
        dst = ys_hbm.at[ctile_ref[g], pl.ds(pl.multiple_of(crow_ref[g], MOE_CHUNK), MOE_CHUNK), :]
        return pltpu.make_async_copy(ybuf.at[buf, chunk_rows(c), :], dst, scatter_sem.at[buf])

    def for_real_chunks(tile, fn):
        def body(c, carry):
            fn(c)
            return carry
        lax.fori_loop(0, treal_ref[tile], body, 0)

    @pl.when(i == 0)
    def _():
        xbuf[...] = jnp.zeros(xbuf.shape, xbuf.dtype)
        for_real_chunks(0, lambda c: gather_copy(0, c, 0).start())

    @pl.when(i + 1 < n_used)
    def _():
        for_real_chunks(i + 1, lambda c: gather_copy(i + 1, c, 1 - slot).start())

    @pl.when(i < n_used)
    def _():
        for_real_chunks(i, lambda c: gather_copy(i, c, slot).wait())

        @pl.when(i >= 2)
        def _():
            for_real_chunks(i - 2, lambda c: scatter_copy(i - 2, c, slot).wait())

        x = xbuf[slot]
        a = jnp.dot(x, wg_ref[...], preferred_element_type=F32)
        u = jnp.dot(x, wu_ref[...], preferred_element_type=F32)
        hid = (jax.nn.silu(a) * u).astype(BF16)
        ybuf[slot] = jnp.dot(hid, wd_ref[...], preferred_element_type=F32).astype(BF16)
        for_real_chunks(i, lambda c: scatter_copy(i, c, slot).start())

    @pl.when(i == n_used - 1)
    def _():
        @pl.when(i >= 1)
        def _():
            for_real_chunks(i - 1, lambda c: scatter_copy(i - 1, c, 1 - slot).wait())
        for_real_chunks(i, lambda c: scatter_copy(i, c, slot).wait())


def _moe_ffn(hs, plan, wg, wu, wd):
    nT, cap, D = hs.shape
    E, _, F = wg.shape
    tile_expert, tile_real, src_tile, src_row, n_used = plan
    n_row_tiles = tile_expert.shape[0]
    grid_spec = pltpu.PrefetchScalarGridSpec(
        num_scalar_prefetch=5,
        grid=(n_row_tiles,),
        in_specs=[
            pl.BlockSpec(memory_space=pl.ANY),
            pl.BlockSpec((None, D, F), lambda i, te, tr, ct, cr, nu: (te[i], 0, 0)),
            pl.BlockSpec((None, D, F), lambda i, te, tr, ct, cr, nu: (te[i], 0, 0)),
            pl.BlockSpec((None, F, D), lambda i, te, tr, ct, cr, nu: (te[i], 0, 0)),
            pl.BlockSpec(memory_space=pl.ANY),
        ],
        out_specs=pl.BlockSpec(memory_space=pl.ANY),
        scratch_shapes=[
            pltpu.VMEM((2, MOE_ROW_TILE, D), BF16),
            pltpu.VMEM((2, MOE_ROW_TILE, D), BF16),
            pltpu.SemaphoreType.DMA((2,)),
            pltpu.SemaphoreType.DMA((2,)),
        ],
    )
    ys_init = jnp.zeros((nT, cap, D), BF16)
    return pl.pallas_call(
        _moe_ffn_kernel,
        grid_spec=grid_spec,
        out_shape=jax.ShapeDtypeStruct((nT, cap, D), BF16),
        input_output_aliases={9: 0},
        compiler_params=_params(("arbitrary",)),
        name="moe_ffn",
    )(tile_expert, tile_real, src_tile, src_row, n_used, hs, wg, wu, wd, ys_init)


def _moe_combine_kernel(ys_ref, route_ref, x1_ref, g_ref, o_ref):
    tm = x1_ref.shape[0]
    cap = ys_ref.shape[0]
    route = route_ref[...]
    slot = lax.broadcasted_iota(jnp.int32, (tm, cap), 1)
    pick1 = jnp.where(slot == route[:, 0:1].astype(jnp.int32), 1.0, 0.0).astype(BF16)
    pick2 = jnp.where(slot == route[:, 1:2].astype(jnp.int32), 1.0, 0.0).astype(BF16)
    ys = ys_ref[...]
    y1 = jnp.dot(pick1, ys, preferred_element_type=F32)
    y2 = jnp.dot(pick2, ys, preferred_element_type=F32)
    y = x1_ref[...] + route[:, 2:3] * y1 + route[:, 3:4] * y2
    ms = jnp.mean(y * y, axis=-1, keepdims=True)
    o_ref[...] = y * lax.rsqrt(ms + RMS_EPS) * g_ref[...]


def _moe_combine(ys, route, x1, g, tm=MOE_TOKEN_TILE):
    S, D = x1.shape
    cap = ys.shape[1]
    row = lambda i: (i, 0)
    return pl.pallas_call(
        _moe_combine_kernel,
        grid=(S // tm,),
        in_specs=[
            pl.BlockSpec((None, cap, D), lambda i: (i, 0, 0)),
            pl.BlockSpec((tm, ROUTE_LANES), row),
            pl.BlockSpec((tm, D), row),
            pl.BlockSpec((1, D), lambda i: (0, 0)),
        ],
        out_specs=pl.BlockSpec((tm, D), row),
        out_shape=jax.ShapeDtypeStruct((S, D), F32),
        compiler_params=_params(("parallel",)),
        name="moe_combine",
    )(ys, route, x1, g)


def _router_weights(w_r_group, b_r_group, w_r_expert, b_r_expert):
    D = w_r_group.shape[0]
    w = jnp.concatenate(
        [w_r_group, jnp.transpose(w_r_expert, (1, 0, 2)).reshape(D, N_EXPERTS)], axis=1)
    b = jnp.concatenate([b_r_group, b_r_expert.reshape(N_EXPERTS)])
    pad = ROUTE_LANES - w.shape[1]
    w = jnp.pad(w, ((0, 0), (0, pad)))
    b = jnp.pad(b, (0, pad)).reshape(1, ROUTE_LANES)
    w_hi = w.astype(BF16)
    w_lo = (w - w_hi.astype(F32)).astype(BF16)
    return w_hi, w_lo, b


def kernel(x, norm_mix, w_in, w_pool, pool_scale, w_branch_pool, w_branch_attn, w_out, norm_ffn,
           w_r_group, b_r_group, w_r_expert, b_r_expert, w_gate, w_up, w_down, norm_final):
    B, S, D = x.shape
    depth = w_in.shape[0]
    assert depth == 1, "the final rms_norm is fused into the expert kernel of a single layer"
    slopes = jnp.exp2(-8.0 * jnp.arange(1, ATTN_HEADS + 1, dtype=F32) / ATTN_HEADS)
    slopes = jnp.broadcast_to(slopes[:, None, None], (ATTN_HEADS, 1, 128))
    outs = []
    for b in range(B):
        xb = x[b]
        for l in range(depth):
            proj = _norm_inproj(xb, norm_mix[l].reshape(1, D), w_in[l].astype(BF16))
            y_pool = _pool_mixer(proj, w_pool[l].astype(BF16), pool_scale[l].reshape(1, POOL_WIDTH))
            qT, vT, sel = _moba_gate(proj)
            y_attn = _moba_attention(proj, qT, vT, sel, slopes)
            wr_hi, wr_lo, br = _router_weights(w_r_group[l], b_r_group[l], w_r_expert[l], b_r_expert[l])
            x1, hs, route, cnt = _merge_route(
                y_pool, y_attn, proj, xb,
                w_branch_pool[l].astype(BF16), w_branch_attn[l].astype(BF16), w_out[l].astype(BF16),
                norm_ffn[l].reshape(1, D), wr_hi, wr_lo, br)
            nT = hs.shape[0]
            chunk_counts = cnt[:, 0, EXPERT_LANE0:EXPERT_LANE0 + N_EXPERTS].astype(jnp.int32)
            max_chunks = nT * ((2 * MOE_TOKEN_TILE + N_EXPERTS * (MOE_CHUNK - 1)) // MOE_CHUNK)
            n_row_tiles = -(-(max_chunks + N_EXPERTS * (MOE_CHUNKS_PER_TILE - 1))
                            // MOE_CHUNKS_PER_TILE)
            plan = _moe_plan(chunk_counts, n_row_tiles, n_row_tiles * MOE_CHUNKS_PER_TILE)
            ys = _moe_ffn(hs, plan, w_gate[l].astype(BF16), w_up[l].astype(BF16),
                          w_down[l].astype(BF16))
            xb = _moe_combine(ys, route, x1, norm_final.reshape(1, D))
        outs.append(xb)
    return jnp.stack(outs, axis=0)
```

```python
import functools

import jax
import jax.numpy as jnp
from jax import lax
from jax.experimental import pallas as pl
from jax.experimental.pallas import tpu as pltpu

F32 = jnp.float32
BF16 = jnp.bfloat16

POOL_WINDOWS = (2, 4, 8, 16)
MAX_WINDOW = 16
POOL_WIDTH = 1024
POOL_GROUP = 256
HEAD_DIM = 128
ATTN_HEADS = 8
ATTN_WIDTH = 1024
MOBA_BLOCK = 256
MOBA_TOPK = 3
N_GROUPS = 4
EXPERTS_PER_GROUP = 4
N_EXPERTS = 16
ROUTE_LANES = 128
EXPERT_LANE0 = N_GROUPS
RMS_EPS = 1e-6
NEG_INF = -1e30
LOG2_E = 1.4426950408889634
ATTN_KV_UNROLL = 2
MOE_CHUNK = 16
MOE_TOKEN_TILE = 256
MOE_ROW_TILE = 256
MOE_CHUNKS_PER_TILE = MOE_ROW_TILE // MOE_CHUNK

V7X_VMEM_LIMIT_BYTES = 56 * 1024 * 1024


def _params(semantics, vmem=V7X_VMEM_LIMIT_BYTES):
    return pltpu.CompilerParams(dimension_semantics=semantics, vmem_limit_bytes=vmem)


def _norm_inproj_kernel(x_ref, g_ref, w_ref, o_ref, wb_ref):
    @pl.when(pl.program_id(1) == 0)
    def _():
        wb_ref[...] = w_ref[...].astype(BF16)

    x = x_ref[...]
    ms = jnp.mean(x * x, axis=-1, keepdims=True)
    h = (x * lax.rsqrt(ms + RMS_EPS) * g_ref[...]).astype(BF16)
    o_ref[...] = jnp.dot(h, wb_ref[...], preferred_element_type=F32).astype(o_ref.dtype)


def _norm_inproj(x, g, w, tm=512, tn=1024):
    S, D = x.shape
    N = w.shape[1]
    return pl.pallas_call(
        _norm_inproj_kernel,
        grid=(N // tn, S // tm),
        in_specs=[
            pl.BlockSpec((tm, D), lambda j, i: (i, 0)),
            pl.BlockSpec((1, D), lambda j, i: (0, 0)),
            pl.BlockSpec((D, tn), lambda j, i: (0, j)),
        ],
        out_specs=pl.BlockSpec((tm, tn), lambda j, i: (i, j)),
        out_shape=jax.ShapeDtypeStruct((S, N), BF16),
        scratch_shapes=[pltpu.VMEM((D, tn), BF16)],
        compiler_params=_params(("parallel", "arbitrary")),
        name="norm_inproj",
    )(x, g, w)


def _pool_kernel(cur_ref, prev_ref, w_ref, scale_ref, o_ref, ext_ref):
    i = pl.program_id(0)
    tm = cur_ref.shape[0]
    u = cur_ref[...].astype(F32)
    halo = jnp.where(i > 0, prev_ref[...].astype(F32), 0.0)
    ext_ref[0:MAX_WINDOW, :] = halo
    ext_ref[MAX_WINDOW:MAX_WINDOW + tm, :] = u
    t = i * tm + lax.broadcasted_iota(jnp.int32, (tm, 1), 0)
    for g, w in enumerate(POOL_WINDOWS):
        cols = slice(g * POOL_GROUP, (g + 1) * POOL_GROUP)
        ug = u[:, cols]
        wsum = ug
        for s in range(1, w):
            wsum = wsum + ext_ref[MAX_WINDOW - s:MAX_WINDOW - s + tm, cols]
        cnt = jnp.minimum(t + 1, w).astype(F32)
        mixed = (wsum / cnt - ug).astype(BF16)
        y = jnp.dot(mixed, w_ref[g], preferred_element_type=F32)
        o_ref[:, cols] = (y * scale_ref[:, cols]).astype(o_ref.dtype)


def _pool_mixer(proj, w_pool, pool_scale, tm=512):
    S = proj.shape[0]
    halo_blocks = tm // MAX_WINDOW
    return pl.pallas_call(
        _pool_kernel,
        grid=(S // tm,),
        in_specs=[
            pl.BlockSpec((tm, POOL_WIDTH), lambda i: (i, 0)),
            pl.BlockSpec((MAX_WINDOW, POOL_WIDTH),
                         lambda i: (jnp.maximum(i * halo_blocks - 1, 0), 0)),
            pl.BlockSpec((len(POOL_WINDOWS), POOL_GROUP, POOL_GROUP), lambda i: (0, 0, 0)),
            pl.BlockSpec((1, POOL_WIDTH), lambda i: (0, 0)),
        ],
        out_specs=pl.BlockSpec((tm, POOL_WIDTH), lambda i: (i, 0)),
        out_shape=jax.ShapeDtypeStruct((S, POOL_WIDTH), BF16),
        scratch_shapes=[pltpu.VMEM((tm + MAX_WINDOW, POOL_WIDTH), F32)],
        compiler_params=_params(("parallel",)),
        name="pool_mixer",
    )(proj, proj, w_pool, pool_scale)


def _moba_gate_kernel(q_ref, k_ref, v_ref, qT_ref, vT_ref, sel_ref):
    S = q_ref.shape[0]
    nb = S // MOBA_BLOCK
    topk = min(MOBA_TOPK, nb)
    kf = k_ref[...].astype(F32).reshape(nb, MOBA_BLOCK, HEAD_DIM)
    kmean = jnp.sum(kf, axis=1) * (1.0 / MOBA_BLOCK)
    km_hi = kmean.astype(BF16)
    km_lo = (kmean - km_hi.astype(F32)).astype(BF16)
    blk = lax.broadcasted_iota(jnp.int32, (nb, MOBA_BLOCK), 0)

    def body(i, carry):
        rows = pl.ds(pl.multiple_of(i * MOBA_BLOCK, MOBA_BLOCK), MOBA_BLOCK)
        qT = q_ref[rows, :].astype(F32).T.astype(BF16)
        qT_ref[i] = qT
        vT_ref[i] = v_ref[rows, :].astype(F32).T.astype(BF16)
        gate = (jnp.dot(km_hi, qT, preferred_element_type=F32)
                + jnp.dot(km_lo, qT, preferred_element_type=F32))
        gate = jnp.where(blk < i, gate, NEG_INF)
        sel = jnp.zeros((nb, MOBA_BLOCK), F32)
        for _ in range(topk):
            best = jnp.max(gate, axis=0, keepdims=True)
            idx = jnp.min(jnp.where(gate == best, blk, nb), axis=0, keepdims=True)
            hit = blk == idx
            sel = jnp.where(hit & (blk < i), 1.0, sel)
            gate = jnp.where(hit, -jnp.inf, gate)
        sel_ref[i] = sel
        return carry

    lax.fori_loop(0, nb, body, 0, unroll=4 if nb % 4 == 0 else 1)


def _moba_gate(proj):
    S = proj.shape[0]
    nb = S // MOBA_BLOCK
    H = ATTN_HEADS
    q0 = POOL_WIDTH // HEAD_DIM
    k0 = q0 + H
    v0 = k0 + H
    blocked = lambda h: (h, 0, 0, 0)
    return pl.pallas_call(
        _moba_gate_kernel,
        grid=(H,),
        in_specs=[
            pl.BlockSpec((S, HEAD_DIM), lambda h: (0, q0 + h)),
            pl.BlockSpec((S, HEAD_DIM), lambda h: (0, k0 + h)),
            pl.BlockSpec((S, HEAD_DIM), lambda h: (0, v0 + h)),
        ],
        out_specs=[
            pl.BlockSpec((None, nb, HEAD_DIM, MOBA_BLOCK), blocked),
            pl.BlockSpec((None, nb, HEAD_DIM, MOBA_BLOCK), blocked),
            pl.BlockSpec((None, nb, nb, MOBA_BLOCK), blocked),
        ],
        out_shape=[
            jax.ShapeDtypeStruct((H, nb, HEAD_DIM, MOBA_BLOCK), BF16),
            jax.ShapeDtypeStruct((H, nb, HEAD_DIM, MOBA_BLOCK), BF16),
            jax.ShapeDtypeStruct((H, nb, nb, MOBA_BLOCK), F32),
        ],
        compiler_params=_params(("parallel",)),
        name="moba_gate",
    )(proj, proj, proj)


def _moba_attn_kernel(slope_ref, qT_ref, sel_ref, k_ref, vT_ref, o_ref,
                      s0_ref, s1_ref, p0_ref, p1_ref):
    i = pl.program_id(1)
    nb = sel_ref.shape[0]
    scale2 = (HEAD_DIM ** -0.5) * LOG2_E
    slope2 = slope_ref[...][:, 0:1] * LOG2_E
    qT = qT_ref[...]
    kpos = lax.broadcasted_iota(jnp.int32, (MOBA_BLOCK, MOBA_BLOCK), 0)
    qpos = lax.broadcasted_iota(jnp.int32, (MOBA_BLOCK, MOBA_BLOCK), 1)
    dist0 = (qpos - kpos).astype(F32)
    bias0 = slope2 * dist0

    def block_of(t, u):
        return jnp.clip(t * ATTN_KV_UNROLL + u, 0, nb - 1)

    def issue_scores(t, s_ref):
        for u in range(ATTN_KV_UNROLL):
            rows = pl.ds(pl.multiple_of(block_of(t, u) * MOBA_BLOCK, MOBA_BLOCK), MOBA_BLOCK)
            s_ref[u] = jnp.dot(k_ref[rows, :], qT, preferred_element_type=F32)

    def apply_probs(t, p_ref, alpha, acc):
        acc = alpha * acc
        for u in range(ATTN_KV_UNROLL):
            acc = acc + jnp.dot(vT_ref[block_of(t, u)], p_ref[u], preferred_element_type=F32)
        return acc

    def softmax_group(t, s_ref, p_ref, m, l):
        m_new = m
        shifts = []
        for u in range(ATTN_KV_UNROLL):
            j = t * ATTN_KV_UNROLL + u
            s = s_ref[u] * scale2 - bias0
            s_ref[u] = s
            valid = jnp.where(j < i, sel_ref[pl.ds(block_of(t, u), 1), :], 0.0) > 0.0
            gap = slope2 * ((i - j) * MOBA_BLOCK).astype(F32)
            top = jnp.max(s, axis=0, keepdims=True) - gap
            m_new = jnp.maximum(m_new, jnp.where(valid, top, NEG_INF))
            shifts.append((valid, gap))
        alpha = jnp.exp2(m - m_new)
        l = alpha * l
        for u, (valid, gap) in enumerate(shifts):
            p = jnp.exp2(s_ref[u] - jnp.where(valid, m_new + gap, jnp.inf))
            l = l + jnp.sum(p, axis=0, keepdims=True)
            p_ref[u] = p.astype(BF16)
        return m_new, l, alpha

    def body(r, carry):
        m, l, acc, alpha = carry
        acc = apply_probs(2 * r - 1, p1_ref, alpha, acc)
        m, l, alpha = softmax_group(2 * r, s0_ref, p0_ref, m, l)
        issue_scores(2 * r + 1, s1_ref)
        acc = apply_probs(2 * r, p0_ref, alpha, acc)
        m, l, alpha = softmax_group(2 * r + 1, s1_ref, p1_ref, m, l)
        issue_scores(2 * r + 2, s0_ref)
        return m, l, acc, alpha

    n_groups = (i + ATTN_KV_UNROLL - 1) // ATTN_KV_UNROLL
    n_pairs = n_groups // 2
    p1_ref[...] = jnp.zeros(p1_ref.shape, BF16)
    issue_scores(0, s0_ref)
    own_rows = pl.ds(pl.multiple_of(i * MOBA_BLOCK, MOBA_BLOCK), MOBA_BLOCK)
    s_own = jnp.dot(k_ref[own_rows, :], qT, preferred_element_type=F32) * scale2 - bias0
    s_own = jnp.where(dist0 >= 0, s_own, NEG_INF)
    top_own = jnp.max(s_own, axis=0, keepdims=True)
    row = jnp.zeros((1, MOBA_BLOCK), F32)
    init = (row + NEG_INF, row, jnp.zeros((HEAD_DIM, MOBA_BLOCK), F32), row + 1.0)
    m, l, acc, alpha = lax.fori_loop(0, n_pairs, body, init)
    acc = apply_probs(2 * n_pairs - 1, p1_ref, alpha, acc)
    m, l, alpha = softmax_group(2 * n_pairs, s0_ref, p0_ref, m, l)
    acc = apply_probs(2 * n_pairs, p0_ref, alpha, acc)
    m_new = jnp.maximum(m, top_own)
    alpha = jnp.exp2(m - m_new)
    p = jnp.exp2(s_own - m_new)
    l = alpha * l + jnp.sum(p, axis=0, keepdims=True)
    acc = alpha * acc + jnp.dot(vT_ref[i], p.astype(BF16), preferred_element_type=F32)
    o_ref[...] = (acc / l).T.astype(o_ref.dtype)


def _moba_attention(proj, qT, vT, sel, slopes):
    S = proj.shape[0]
    nb = S // MOBA_BLOCK
    H = ATTN_HEADS
    k0 = POOL_WIDTH // HEAD_DIM + H
    return pl.pallas_call(
        _moba_attn_kernel,
        grid=(H, nb),
        in_specs=[
            pl.BlockSpec((None, 1, 128), lambda h, i: (h, 0, 0)),
            pl.BlockSpec((None, None, HEAD_DIM, MOBA_BLOCK), lambda h, i: (h, i, 0, 0)),
            pl.BlockSpec((None, None, nb, MOBA_BLOCK), lambda h, i: (h, i, 0, 0)),
            pl.BlockSpec((S, HEAD_DIM), lambda h, i: (0, k0 + h)),
            pl.BlockSpec((None, nb, HEAD_DIM, MOBA_BLOCK), lambda h, i: (h, 0, 0, 0)),
        ],
        out_specs=pl.BlockSpec((MOBA_BLOCK, HEAD_DIM), lambda h, i: (i, h)),
        out_shape=jax.ShapeDtypeStruct((S, ATTN_WIDTH), BF16),
        scratch_shapes=[
            pltpu.VMEM((ATTN_KV_UNROLL, MOBA_BLOCK, MOBA_BLOCK), F32),
            pltpu.VMEM((ATTN_KV_UNROLL, MOBA_BLOCK, MOBA_BLOCK), F32),
            pltpu.VMEM((ATTN_KV_UNROLL, MOBA_BLOCK, MOBA_BLOCK), BF16),
            pltpu.VMEM((ATTN_KV_UNROLL, MOBA_BLOCK, MOBA_BLOCK), BF16),
        ],
        compiler_params=_params(("parallel", "arbitrary")),
        name="moba_attention",
    )(slopes, qT, sel, proj, vT)


def _first_lane_of_max(vals, lane):
    best = jnp.max(vals, axis=1, keepdims=True)
    idx = jnp.min(jnp.where(vals == best, lane, ROUTE_LANES), axis=1, keepdims=True)
    return best, idx


def _merge_route_kernel(yp_ref, ya_ref, glp_ref, gla_ref, x_ref, wbp_ref, wba_ref, wout_ref,
                        g_ref, wr_hi_ref, wr_lo_ref, br_ref, x1_ref, hs_ref, route_ref, cnt_ref):
    tm = x_ref.shape[0]
    cap = hs_ref.shape[0]
    bp = jnp.dot(yp_ref[...], wbp_ref[...], preferred_element_type=F32)
    ba = jnp.dot(ya_ref[...], wba_ref[...], preferred_element_type=F32)
    merged = (jax.nn.sigmoid(glp_ref[...].astype(F32)) * bp
              + jax.nn.sigmoid(gla_ref[...].astype(F32)) * ba)
    x1 = x_ref[...] + jnp.dot(merged.astype(BF16), wout_ref[...], preferred_element_type=F32)
    x1_ref[...] = x1
    ms = jnp.mean(x1 * x1, axis=-1, keepdims=True)
    h2 = x1 * lax.rsqrt(ms + RMS_EPS) * g_ref[...]
    h2_hi = h2.astype(BF16)
    h2_lo = (h2 - h2_hi.astype(F32)).astype(BF16)
    logits = (jnp.dot(h2_hi, wr_hi_ref[...], preferred_element_type=F32)
              + jnp.dot(h2_lo, wr_hi_ref[...], preferred_element_type=F32)
              + jnp.dot(h2_hi, wr_lo_ref[...], preferred_element_type=F32)
              + br_ref[...])

    lane = lax.broadcasted_iota(jnp.int32, logits.shape, 1)
    g_logits = jnp.where(lane < N_GROUPS, logits, -jnp.inf)
    g_best, g_idx = _first_lane_of_max(g_logits, lane)
    g_w = 1.0 / jnp.sum(jnp.exp(g_logits - g_best), axis=1, keepdims=True)
    e_lo = EXPERT_LANE0 + EXPERTS_PER_GROUP * g_idx
    e_logits = jnp.where((lane >= e_lo) & (lane < e_lo + EXPERTS_PER_GROUP), logits, -jnp.inf)
    v1, i1 = _first_lane_of_max(e_logits, lane)
    v2, i2 = _first_lane_of_max(jnp.where(lane == i1, -jnp.inf, e_logits), lane)
    e21 = jnp.exp(v2 - v1)
    w1 = g_w / (1.0 + e21)
    w2 = g_w * e21 / (1.0 + e21)

    hit1 = lane == i1
    hit2 = lane == i2
    member = jnp.where(hit1, 1.0, jnp.where(hit2, 1.0, 0.0))
    r_tok = lax.broadcasted_iota(jnp.int32, (tm, tm), 0)
    c_tok = lax.broadcasted_iota(jnp.int32, (tm, tm), 1)
    earlier = jnp.where(c_tok < r_tok, 1.0, 0.0).astype(BF16)
    rank = jnp.dot(earlier, member.astype(BF16), preferred_element_type=F32)
    count = jnp.sum(member, axis=0, keepdims=True)
    chunks = jnp.floor((count + (MOE_CHUNK - 1)) * (1.0 / MOE_CHUNK))
    r_l = lax.broadcasted_iota(jnp.int32, (ROUTE_LANES, ROUTE_LANES), 0)
    c_l = lax.broadcasted_iota(jnp.int32, (ROUTE_LANES, ROUTE_LANES), 1)
    lower_lanes = jnp.where(r_l < c_l, 1.0, 0.0).astype(BF16)
    start = jnp.dot(jnp.broadcast_to(chunks, (8, ROUTE_LANES)).astype(BF16), lower_lanes,
                    preferred_element_type=F32)[0:1] * MOE_CHUNK
    pos = start + rank
    pos1 = jnp.sum(jnp.where(hit1, pos, 0.0), axis=1, keepdims=True)
    pos2 = jnp.sum(jnp.where(hit2, pos, 0.0), axis=1, keepdims=True)
    route = jnp.where(lane == 0, pos1, jnp.where(lane == 1, pos2,
                      jnp.where(lane == 2, w1, jnp.where(lane == 3, w2, 0.0))))
    route_ref[...] = route
    cnt_ref[...] = chunks
    route_t = route.T
    slot = lax.broadcasted_iota(jnp.int32, (cap, tm), 0)
    p1_row = route_t[0:1, :].astype(jnp.int32)
    p2_row = route_t[1:2, :].astype(jnp.int32)
    onehot = jnp.where(slot == p1_row, 1.0, jnp.where(slot == p2_row, 1.0, 0.0)).astype(BF16)
    hs_ref[...] = jnp.dot(onehot, h2_hi, preferred_element_type=F32).astype(BF16)


def _moe_cap(tm):
    worst = 2 * tm + N_EXPERTS * (MOE_CHUNK - 1)
    return -(-worst // 128) * 128


def _merge_route(yp, ya, proj, x, wbp, wba, wout, g, wr_hi, wr_lo, br, tm=MOE_TOKEN_TILE):
    S, D = x.shape
    nT = S // tm
    cap = _moe_cap(tm)
    full = lambda i: (0, 0)
    row = lambda i: (i, 0)
    glp_blk = (POOL_WIDTH + 3 * ATTN_WIDTH) // D
    return pl.pallas_call(
        _merge_route_kernel,
        grid=(S // tm,),
        in_specs=[
            pl.BlockSpec((tm, POOL_WIDTH), row),
            pl.BlockSpec((tm, ATTN_WIDTH), row),
            pl.BlockSpec((tm, D), lambda i: (i, glp_blk)),
            pl.BlockSpec((tm, D), lambda i: (i, glp_blk + 1)),
            pl.BlockSpec((tm, D), row),
            pl.BlockSpec((POOL_WIDTH, D), full),
            pl.BlockSpec((ATTN_WIDTH, D), full),
            pl.BlockSpec((D, D), full),
            pl.BlockSpec((1, D), full),
            pl.BlockSpec((D, ROUTE_LANES), full),
            pl.BlockSpec((D, ROUTE_LANES), full),
            pl.BlockSpec((1, ROUTE_LANES), full),
        ],
        out_specs=[
            pl.BlockSpec((tm, D), row),
            pl.BlockSpec((None, cap, D), lambda i: (i, 0, 0)),
            pl.BlockSpec((tm, ROUTE_LANES), row),
            pl.BlockSpec((None, 1, ROUTE_LANES), lambda i: (i, 0, 0)),
        ],
        out_shape=[
            jax.ShapeDtypeStruct((S, D), F32),
            jax.ShapeDtypeStruct((nT, cap, D), BF16),
            jax.ShapeDtypeStruct((S, ROUTE_LANES), F32),
            jax.ShapeDtypeStruct((nT, 1, ROUTE_LANES), F32),
        ],
        compiler_params=_params(("parallel",)),
        name="merge_route",
    )(yp, ya, proj, proj, x, wbp, wba, wout, g, wr_hi, wr_lo, br)


def _moe_plan(chunk_counts, n_row_tiles, n_chunk_slots):
    nT, E = chunk_counts.shape
    per_expert = chunk_counts.T
    seg_start = (jnp.cumsum(chunk_counts, axis=1) - chunk_counts).T
    seg_end = jnp.cumsum(per_expert, axis=1)
    n_chunks = seg_end[:, -1]
    padded = -(-n_chunks // MOE_CHUNKS_PER_TILE) * MOE_CHUNKS_PER_TILE
    e_end = jnp.cumsum(padded)
    n_used = e_end[-1] // MOE_CHUNKS_PER_TILE
    c = jnp.arange(n_chunk_slots, dtype=jnp.int32)
    e_of_c = jnp.minimum((e_end[None, :] <= c[:, None]).sum(axis=1), E - 1)
    is_e = (e_of_c[:, None] == jnp.arange(E)[None, :]).astype(jnp.int32)
    local = c - (is_e * (e_end - padded)[None, :]).sum(axis=1)
    real = local < (is_e * n_chunks[None, :]).sum(axis=1)
    pick_e = lambda table: (is_e[:, :, None] * table[None, :, :]).sum(axis=1)
    seg_end_c = pick_e(seg_end)
    t_of_c = jnp.minimum((seg_end_c <= local[:, None]).sum(axis=1), nT - 1)
    is_t = (t_of_c[:, None] == jnp.arange(nT)[None, :]).astype(jnp.int32)
    pick_t = lambda rows: (rows * is_t).sum(axis=1)
    within = local - (pick_t(seg_end_c) - pick_t(pick_e(per_expert)))
    src_tile = jnp.where(real, t_of_c, 0).astype(jnp.int32)
    src_row = jnp.where(real, (pick_t(pick_e(seg_start)) + within) * MOE_CHUNK, 0).astype(jnp.int32)
    tile = jnp.arange(n_row_tiles, dtype=jnp.int32)
    first = jnp.minimum(tile, n_used - 1) * MOE_CHUNKS_PER_TILE
    tile_expert = jnp.minimum((e_end[None, :] <= first[:, None]).sum(axis=1), E - 1)
    tile_real = real.reshape(n_row_tiles, MOE_CHUNKS_PER_TILE).sum(axis=1)
    return (tile_expert.astype(jnp.int32), tile_real.astype(jnp.int32), src_tile, src_row,
            n_used.reshape(1).astype(jnp.int32))


def _moe_ffn_kernel(texp_ref, treal_ref, ctile_ref, crow_ref, nused_ref,
                    hs_hbm, wg_ref, wu_ref, wd_ref, ys_in_hbm, ys_hbm,
                    xbuf, ybuf, gather_sem, scatter_sem):
    del texp_ref, ys_in_hbm
    i = pl.program_id(0)
    n_used = nused_ref[0]
    slot = lax.rem(i, 2)

    def chunk_rows(c):
        return pl.ds(pl.multiple_of(c * MOE_CHUNK, MOE_CHUNK), MOE_CHUNK)

    def gather_copy(tile, c, buf):
        g = tile * MOE_CHUNKS_PER_TILE + c
        src = hs_hbm.at[ctile_ref[g], pl.ds(pl.multiple_of(crow_ref[g], MOE_CHUNK), MOE_CHUNK), :]
        return pltpu.make_async_copy(src, xbuf.at[buf, chunk_rows(c), :], gather_sem.at[buf])

    def scatter_copy(tile, c, buf):
        g = tile * MOE_CHUNKS_PER_TILE + c
        dst = ys_hbm.at[ctile_ref[g], pl.ds(pl.multiple_of(crow_ref[g], MOE_CHUNK), MOE_CHUNK), :]
        return pltpu.make_async_copy(ybuf.at[buf, chunk_rows(c), :], dst, scatter_sem.at[buf])

    def for_real_chunks(tile, fn):
        def body(c, carry):
            fn(c)
            return carry
        lax.fori_loop(0, treal_ref[tile], body, 0)

    @pl.when(i == 0)
    def _():
        xbuf[...] = jnp.zeros(xbuf.shape, xbuf.dtype)
        for_real_chunks(0, lambda c: gather_copy(0, c, 0).start())

    @pl.when(i + 1 < n_used)
    def _():
        for_real_chunks(i + 1, lambda c: gather_copy(i + 1, c, 1 - slot).start())

    @pl.when(i < n_used)
    def _():
        for_real_chunks(i, lambda c: gather_copy(i, c, slot).wait())

        @pl.when(i >= 2)
        def _():
            for_real_chunks(i - 2, lambda c: scatter_copy(i - 2, c, slot).wait())

        x = xbuf[slot]
        a = jnp.dot(x, wg_ref[...], preferred_element_type=F32)
        u = jnp.dot(x, wu_ref[...], preferred_element_type=F32)
        hid = (jax.nn.silu(a) * u).astype(BF16)
        ybuf[slot] = jnp.dot(hid, wd_ref[...], preferred_element_type=F32).astype(BF16)
        for_real_chunks(i, lambda c: scatter_copy(i, c, slot).start())

    @pl.when(i == n_used - 1)
    def _():
        @pl.when(i >= 1)
        def _():
            for_real_chunks(i - 1, lambda c: scatter_copy(i - 1, c, 1 - slot).wait())
        for_real_chunks(i, lambda c: scatter_copy(i, c, slot).wait())


def _moe_ffn(hs, plan, wg, wu, wd):
    nT, cap, D = hs.shape
    E, _, F = wg.shape
    tile_expert, tile_real, src_tile, src_row, n_used = plan
    n_row_tiles = tile_expert.shape[0]
    grid_spec = pltpu.PrefetchScalarGridSpec(
        num_scalar_prefetch=5,
        grid=(n_row_tiles,),
        in_specs=[
            pl.BlockSpec(memory_space=pl.ANY),
            pl.BlockSpec((None, D, F), lambda i, te, tr, ct, cr, nu: (te[i], 0, 0)),
            pl.BlockSpec((None, D, F), lambda i, te, tr, ct, cr, nu: (te[i], 0, 0)),
            pl.BlockSpec((None, F, D), lambda i, te, tr, ct, cr, nu: (te[i], 0, 0)),
            pl.BlockSpec(memory_space=pl.ANY),
        ],
        out_specs=pl.BlockSpec(memory_space=pl.ANY),
        scratch_shapes=[
            pltpu.VMEM((2, MOE_ROW_TILE, D), BF16),
            pltpu.VMEM((2, MOE_ROW_TILE, D), BF16),
            pltpu.SemaphoreType.DMA((2,)),
            pltpu.SemaphoreType.DMA((2,)),
        ],
    )
    ys_init = jnp.zeros((nT, cap, D), BF16)
    return pl.pallas_call(
        _moe_ffn_kernel,
        grid_spec=grid_spec,
        out_shape=jax.ShapeDtypeStruct((nT, cap, D), BF16),
        input_output_aliases={9: 0},
        compiler_params=_params(("arbitrary",)),
        name="moe_ffn",
    )(tile_expert, tile_real, src_tile, src_row, n_used, hs, wg, wu, wd, ys_init)


def _moe_combine_kernel(ys_ref, route_ref, x1_ref, g_ref, o_ref):
    tm = x1_ref.shape[0]
    cap = ys_ref.shape[0]
    route = route_ref[...]
    slot = lax.broadcasted_iota(jnp.int32, (tm, cap), 1)
    pick1 = jnp.where(slot == route[:, 0:1].astype(jnp.int32), 1.0, 0.0).astype(BF16)
    pick2 = jnp.where(slot == route[:, 1:2].astype(jnp.int32), 1.0, 0.0).astype(BF16)
    ys = ys_ref[...]
    y1 = jnp.dot(pick1, ys, preferred_element_type=F32)
    y2 = jnp.dot(pick2, ys, preferred_element_type=F32)
    y = x1_ref[...] + route[:, 2:3] * y1 + route[:, 3:4] * y2
    ms = jnp.mean(y * y, axis=-1, keepdims=True)
    o_ref[...] = y * lax.rsqrt(ms + RMS_EPS) * g_ref[...]


def _moe_combine(ys, route, x1, g, tm=MOE_TOKEN_TILE):
    S, D = x1.shape
    cap = ys.shape[1]
    row = lambda i: (i, 0)
    return pl.pallas_call(
        _moe_combine_kernel,
        grid=(S // tm,),
        in_specs=[
            pl.BlockSpec((None, cap, D), lambda i: (i, 0, 0)),
            pl.BlockSpec((tm, ROUTE_LANES), row),
            pl.BlockSpec((tm, D), row),
            pl.BlockSpec((1, D), lambda i: (0, 0)),
        ],
        out_specs=pl.BlockSpec((tm, D), row),
        out_shape=jax.ShapeDtypeStruct((S, D), F32),
        compiler_params=_params(("parallel",)),
        name="moe_combine",
    )(ys, route, x1, g)


def _router_weights(w_r_group, b_r_group, w_r_expert, b_r_expert):
    D = w_r_group.shape[0]
    w = jnp.concatenate(
        [w_r_group, jnp.transpose(w_r_expert, (1, 0, 2)).reshape(D, N_EXPERTS)], axis=1)
    b = jnp.concatenate([b_r_group, b_r_expert.reshape(N_EXPERTS)])
    pad = ROUTE_LANES - w.shape[1]
    w = jnp.pad(w, ((0, 0), (0, pad)))
    b = jnp.pad(b, (0, pad)).reshape(1, ROUTE_LANES)
    w_hi = w.astype(BF16)
    w_lo = (w - w_hi.astype(F32)).astype(BF16)
    return w_hi, w_lo, b


def kernel(x, norm_mix, w_in, w_pool, pool_scale, w_branch_pool, w_branch_attn, w_out, norm_ffn,
           w_r_group, b_r_group, w_r_expert, b_r_expert, w_gate, w_up, w_down, norm_final):
    B, S, D = x.shape
    depth = w_in.shape[0]
    assert depth == 1, "the final rms_norm is fused into the expert kernel of a single layer"
    slopes = jnp.exp2(-8.0 * jnp.arange(1, ATTN_HEADS + 1, dtype=F32) / ATTN_HEADS)
    slopes = jnp.broadcast_to(slopes[:, None, None], (ATTN_HEADS, 1, 128))
    outs = []
    for b in range(B):
        xb = x[b]
        for l in range(depth):
            proj = _norm_inproj(xb, norm_mix[l].reshape(1, D), w_in[l])
            y_pool = _pool_mixer(proj, w_pool[l].astype(BF16), pool_scale[l].reshape(1, POOL_WIDTH))
            qT, vT, sel = _moba_gate(proj)
            y_attn = _moba_attention(proj, qT, vT, sel, slopes)
            wr_hi, wr_lo, br = _router_weights(w_r_group[l], b_r_group[l], w_r_expert[l], b_r_expert[l])
            x1, hs, route, cnt = _merge_route(
                y_pool, y_attn, proj, xb,
                w_branch_pool[l].astype(BF16), w_branch_attn[l].astype(BF16), w_out[l].astype(BF16),
                norm_ffn[l].reshape(1, D), wr_hi, wr_lo, br)
            nT = hs.shape[0]
            chunk_counts = cnt[:, 0, EXPERT_LANE0:EXPERT_LANE0 + N_EXPERTS].astype(jnp.int32)
            max_chunks = nT * ((2 * MOE_TOKEN_TILE + N_EXPERTS * (MOE_CHUNK - 1)) // MOE_CHUNK)
            n_row_tiles = -(-(max_chunks + N_EXPERTS * (MOE_CHUNKS_PER_TILE - 1))
                            // MOE_CHUNKS_PER_TILE)
            plan = _moe_plan(chunk_counts, n_row_tiles, n_row_tiles * MOE_CHUNKS_PER_TILE)
            ys = _moe_ffn(hs, plan, w_gate[l].astype(BF16), w_up[l].astype(BF16),
                          w_down[l].astype(BF16))
            xb = _moe_combine(ys, route, x1, norm_final.reshape(1, D))
        outs.append(xb)
    return jnp.stack(outs, axis=0)
```

```python
import functools

import jax
import jax.numpy as jnp
from jax import lax
from jax.experimental import pallas as pl
from jax.experimental.pallas import tpu as pltpu

F32 = jnp.float32
BF16 = jnp.bfloat16

POOL_WINDOWS = (2, 4, 8, 16)
MAX_WINDOW = 16
POOL_WIDTH = 1024
POOL_GROUP = 256
HEAD_DIM = 128
ATTN_HEADS = 8
ATTN_WIDTH = 1024
MOBA_BLOCK = 256
MOBA_TOPK = 3
N_GROUPS = 4
EXPERTS_PER_GROUP = 4
N_EXPERTS = 16
ROUTE_LANES = 128
EXPERT_LANE0 = N_GROUPS
RMS_EPS = 1e-6
NEG_INF = -1e30
LOG2_E = 1.4426950408889634
QUERY_SCALE = (HEAD_DIM ** -0.5) * LOG2_E
KEY_AUG = 2 * HEAD_DIM
ALIBI_TERMS = 3
VALUE_AUG = HEAD_DIM + 16
ATTN_KV_UNROLL = 2
MOE_CHUNK = 16
MOE_TOKEN_TILE = 256
MOE_ROW_TILE = 256
MOE_CHUNKS_PER_TILE = MOE_ROW_TILE // MOE_CHUNK
MOE_WEIGHT_PIECES = 8

V7X_VMEM_LIMIT_BYTES = 56 * 1024 * 1024


def _params(semantics, vmem=V7X_VMEM_LIMIT_BYTES, flags=None):
    return pltpu.CompilerParams(dimension_semantics=semantics, vmem_limit_bytes=vmem, flags=flags)


def _norm_inproj_kernel(x_ref, g_ref, w_ref, o_ref, wb_ref):
    @pl.when(pl.program_id(1) == 0)
    def _():
        wb_ref[...] = w_ref[...].astype(BF16)

    x = x_ref[...]
    ms = jnp.mean(x * x, axis=-1, keepdims=True)
    h = (x * lax.rsqrt(ms + RMS_EPS) * g_ref[...]).astype(BF16)
    col0 = pl.program_id(0) * o_ref.shape[1]
    is_q = (col0 >= POOL_WIDTH) & (col0 < POOL_WIDTH + ATTN_WIDTH)
    factor = jnp.where(is_q, QUERY_SCALE, 1.0).astype(F32)
    o_ref[...] = (jnp.dot(h, wb_ref[...], preferred_element_type=F32) * factor).astype(o_ref.dtype)


def _norm_inproj(x, g, w, tm=512, tn=1024):
    S, D = x.shape
    N = w.shape[1]
    assert POOL_WIDTH % tn == 0 and ATTN_WIDTH % tn == 0
    return pl.pallas_call(
        _norm_inproj_kernel,
        grid=(N // tn, S // tm),
        in_specs=[
            pl.BlockSpec((tm, D), lambda j, i: (i, 0)),
            pl.BlockSpec((1, D), lambda j, i: (0, 0)),
            pl.BlockSpec((D, tn), lambda j, i: (0, j)),
        ],
        out_specs=pl.BlockSpec((tm, tn), lambda j, i: (i, j)),
        out_shape=jax.ShapeDtypeStruct((S, N), BF16),
        scratch_shapes=[pltpu.VMEM((D, tn), BF16)],
        compiler_params=_params(("parallel", "arbitrary")),
        name="norm_inproj",
    )(x, g, w)


def _pool_kernel(cur_ref, prev_ref, w_ref, scale_ref, o_ref, ext_ref):
    i = pl.program_id(0)
    tm = cur_ref.shape[0]
    u = cur_ref[...].astype(F32)
    halo = jnp.where(i > 0, prev_ref[...].astype(F32), 0.0)
    ext_ref[0:MAX_WINDOW, :] = halo
    ext_ref[MAX_WINDOW:MAX_WINDOW + tm, :] = u
    t = i * tm + lax.broadcasted_iota(jnp.int32, (tm, 1), 0)
    for g, w in enumerate(POOL_WINDOWS):
        cols = slice(g * POOL_GROUP, (g + 1) * POOL_GROUP)
        ug = u[:, cols]
        wsum = ug
        for s in range(1, w):
            wsum = wsum + ext_ref[MAX_WINDOW - s:MAX_WINDOW - s + tm, cols]
        cnt = jnp.minimum(t + 1, w).astype(F32)
        mixed = (wsum / cnt - ug).astype(BF16)
        y = jnp.dot(mixed, w_ref[g], preferred_element_type=F32)
        o_ref[:, cols] = (y * scale_ref[:, cols]).astype(o_ref.dtype)


def _pool_mixer(proj, w_pool, pool_scale, tm=512):
    S = proj.shape[0]
    halo_blocks = tm // MAX_WINDOW
    return pl.pallas_call(
        _pool_kernel,
        grid=(S // tm,),
        in_specs=[
            pl.BlockSpec((tm, POOL_WIDTH), lambda i: (i, 0)),
            pl.BlockSpec((MAX_WINDOW, POOL_WIDTH),
                         lambda i: (jnp.maximum(i * halo_blocks - 1, 0), 0)),
            pl.BlockSpec((len(POOL_WINDOWS), POOL_GROUP, POOL_GROUP), lambda i: (0, 0, 0)),
            pl.BlockSpec((1, POOL_WIDTH), lambda i: (0, 0)),
        ],
        out_specs=pl.BlockSpec((tm, POOL_WIDTH), lambda i: (i, 0)),
        out_shape=jax.ShapeDtypeStruct((S, POOL_WIDTH), BF16),
        scratch_shapes=[pltpu.VMEM((tm + MAX_WINDOW, POOL_WIDTH), F32)],
        compiler_params=_params(("parallel",)),
        name="pool_mixer",
    )(proj, proj, w_pool, pool_scale)


def _moba_gate_kernel(slope_ref, q_ref, k_ref, v_ref, qT_ref, ka_ref, vT_ref, sel_ref):
    S = q_ref.shape[0]
    nb = S // MOBA_BLOCK
    topk = min(MOBA_TOPK, nb)
    kf = k_ref[...].astype(F32).reshape(nb, MOBA_BLOCK, HEAD_DIM)
    kmean = jnp.sum(kf, axis=1) * (1.0 / MOBA_BLOCK)
    km_hi = kmean.astype(BF16)
    km_lo = (kmean - km_hi.astype(F32)).astype(BF16)
    blk = lax.broadcasted_iota(jnp.int32, (nb, MOBA_BLOCK), 0)

    ka_ref[:, 0:HEAD_DIM] = k_ref[...]
    t = lax.broadcasted_iota(jnp.int32, (S, KEY_AUG - HEAD_DIM), 0)
    col = lax.broadcasted_iota(jnp.int32, (S, KEY_AUG - HEAD_DIM), 1)
    pos = jnp.bitwise_and(t, MOBA_BLOCK - 1).astype(F32)
    ka_ref[:, HEAD_DIM:KEY_AUG] = jnp.where(col < ALIBI_TERMS, pos, 0.0).astype(BF16)

    slope2 = slope_ref[...][:, 0:1] * LOG2_E
    q_extra = jnp.zeros((KEY_AUG - HEAD_DIM, MOBA_BLOCK), F32)
    row = lax.broadcasted_iota(jnp.int32, q_extra.shape, 0)
    rest = slope2
    for n in range(ALIBI_TERMS):
        piece = rest.astype(BF16).astype(F32)
        q_extra = jnp.where(row == n, piece, q_extra)
        rest = rest - piece
    q_extra = q_extra.astype(BF16)
    v_row = lax.broadcasted_iota(jnp.int32, (VALUE_AUG - HEAD_DIM, MOBA_BLOCK), 0)
    v_extra = jnp.where(v_row == 0, 1.0, 0.0).astype(BF16)

    def body(i, carry):
        rows = pl.ds(pl.multiple_of(i * MOBA_BLOCK, MOBA_BLOCK), MOBA_BLOCK)
        qT = q_ref[rows, :].astype(F32).T.astype(BF16)
        qT_ref[i, 0:HEAD_DIM, :] = qT
        qT_ref[i, HEAD_DIM:KEY_AUG, :] = q_extra
        vT_ref[i, 0:HEAD_DIM, :] = v_ref[rows, :].astype(F32).T.astype(BF16)
        vT_ref[i, HEAD_DIM:VALUE_AUG, :] = v_extra
        gate = (jnp.dot(km_hi, qT, preferred_element_type=F32)
                + jnp.dot(km_lo, qT, preferred_element_type=F32)) * (1.0 / QUERY_SCALE)
        gate = jnp.where(blk < i, gate, NEG_INF)
        sel = jnp.zeros((nb, MOBA_BLOCK), F32)
        for _ in range(topk):
            best = jnp.max(gate, axis=0, keepdims=True)
            idx = jnp.min(jnp.where(gate == best, blk, nb), axis=0, keepdims=True)
            hit = blk == idx
            sel = jnp.where(hit & (blk < i), 1.0, sel)
            gate = jnp.where(hit, -jnp.inf, gate)
        sel_ref[i] = sel
        return carry

    lax.fori_loop(0, nb, body, 0, unroll=4 if nb % 4 == 0 else 1)


def _moba_gate(proj, slopes):
    S = proj.shape[0]
    nb = S // MOBA_BLOCK
    H = ATTN_HEADS
    q0 = POOL_WIDTH // HEAD_DIM
    k0 = q0 + H
    v0 = k0 + H
    blocked = lambda h: (h, 0, 0, 0)
    return pl.pallas_call(
        _moba_gate_kernel,
        grid=(H,),
        in_specs=[
            pl.BlockSpec((None, 1, 128), lambda h: (h, 0, 0)),
            pl.BlockSpec((S, HEAD_DIM), lambda h: (0, q0 + h)),
            pl.BlockSpec((S, HEAD_DIM), lambda h: (0, k0 + h)),
            pl.BlockSpec((S, HEAD_DIM), lambda h: (0, v0 + h)),
        ],
        out_specs=[
            pl.BlockSpec((None, nb, KEY_AUG, MOBA_BLOCK), blocked),
            pl.BlockSpec((None, S, KEY_AUG), lambda h: (h, 0, 0)),
            pl.BlockSpec((None, nb, VALUE_AUG, MOBA_BLOCK), blocked),
            pl.BlockSpec((None, nb, nb, MOBA_BLOCK), blocked),
        ],
        out_shape=[
            jax.ShapeDtypeStruct((H, nb, KEY_AUG, MOBA_BLOCK), BF16),
            jax.ShapeDtypeStruct((H, S, KEY_AUG), BF16),
            jax.ShapeDtypeStruct((H, nb, VALUE_AUG, MOBA_BLOCK), BF16),
            jax.ShapeDtypeStruct((H, nb, nb, MOBA_BLOCK), F32),
        ],
        compiler_params=_params(("parallel",)),
        name="moba_gate",
    )(slopes, proj, proj, proj)


def _moba_attn_kernel(slope_ref, qT_ref, sel_ref, k_ref, vT_ref, o_ref,
                      s0_ref, s1_ref, p0_ref, p1_ref):
    i = pl.program_id(1)
    nb = sel_ref.shape[0]
    slope2 = slope_ref[...][:, 0:1] * LOG2_E
    qT = qT_ref[...]

    def block_of(t, u):
        return jnp.clip(t * ATTN_KV_UNROLL + u, 0, nb - 1)

    def issue_scores(t, s_ref):
        for u in range(ATTN_KV_UNROLL):
            rows = pl.ds(pl.multiple_of(block_of(t, u) * MOBA_BLOCK, MOBA_BLOCK), MOBA_BLOCK)
            s_ref[u] = jnp.dot(k_ref[rows, :], qT, preferred_element_type=F32)

    def apply_probs(t, p_ref, alpha, acc):
        acc = alpha * acc
        for u in range(ATTN_KV_UNROLL):
            acc = acc + jnp.dot(vT_ref[block_of(t, u)], p_ref[u], preferred_element_type=F32)
        return acc

    def softmax_group(t, s_ref, p_ref, m):
        m_new = m
        shifts = []
        for u in range(ATTN_KV_UNROLL):
            j = t * ATTN_KV_UNROLL + u
            valid = jnp.where(j < i, sel_ref[pl.ds(block_of(t, u), 1), :], 0.0) > 0.0
            gap = slope2 * ((i - j) * MOBA_BLOCK).astype(F32)
            top = jnp.max(s_ref[u], axis=0, keepdims=True) - gap
            m_new = jnp.maximum(m_new, jnp.where(valid, top, NEG_INF))
            shifts.append((valid, gap))
        alpha = jnp.exp2(m - m_new)
        for u, (valid, gap) in enumerate(shifts):
            p = jnp.exp2(s_ref[u] - jnp.where(valid, m_new + gap, jnp.inf))
            p_ref[u] = p.astype(BF16)
        return m_new, alpha

    def body(r, carry):
        m, acc, alpha = carry
        acc = apply_probs(2 * r - 1, p1_ref, alpha, acc)
        m, alpha = softmax_group(2 * r, s0_ref, p0_ref, m)
        issue_scores(2 * r + 1, s1_ref)
        acc = apply_probs(2 * r, p0_ref, alpha, acc)
        m, alpha = softmax_group(2 * r + 1, s1_ref, p1_ref, m)
        issue_scores(2 * r + 2, s0_ref)
        return m, acc, alpha

    n_groups = (i + ATTN_KV_UNROLL - 1) // ATTN_KV_UNROLL
    n_pairs = n_groups // 2
    p1_ref[...] = jnp.zeros(p1_ref.shape, BF16)
    issue_scores(0, s0_ref)
    own_rows = pl.ds(pl.multiple_of(i * MOBA_BLOCK, MOBA_BLOCK), MOBA_BLOCK)
    s_own = jnp.dot(k_ref[own_rows, :], qT, preferred_element_type=F32)
    kpos = lax.broadcasted_iota(jnp.int32, (MOBA_BLOCK, MOBA_BLOCK), 0)
    qpos = lax.broadcasted_iota(jnp.int32, (MOBA_BLOCK, MOBA_BLOCK), 1)
    s_own = jnp.where(qpos >= kpos, s_own, NEG_INF)
    top_own = jnp.max(s_own, axis=0, keepdims=True)
    row = jnp.zeros((1, MOBA_BLOCK), F32)
    init = (row + NEG_INF, jnp.zeros((VALUE_AUG, MOBA_BLOCK), F32), row + 1.0)
    m, acc, alpha = lax.fori_loop(0, n_pairs, body, init)
    acc = apply_probs(2 * n_pairs - 1, p1_ref, alpha, acc)
    m, alpha = softmax_group(2 * n_pairs, s0_ref, p0_ref, m)
    acc = apply_probs(2 * n_pairs, p0_ref, alpha, acc)
    m_new = jnp.maximum(m, top_own)
    alpha = jnp.exp2(m - m_new)
    p = jnp.exp2(s_own - m_new)
    acc = alpha * acc + jnp.dot(vT_ref[i], p.astype(BF16), preferred_element_type=F32)
    out = acc[0:HEAD_DIM] / acc[HEAD_DIM:HEAD_DIM + 1]
    o_ref[...] = out.T.astype(o_ref.dtype)


def _moba_attention(k_aug, qT, vT, sel, slopes):
    H, S, _ = k_aug.shape
    nb = S // MOBA_BLOCK
    return pl.pallas_call(
        _moba_attn_kernel,
        grid=(H, nb),
        in_specs=[
            pl.BlockSpec((None, 1, 128), lambda h, i: (h, 0, 0)),
            pl.BlockSpec((None, None, KEY_AUG, MOBA_BLOCK), lambda h, i: (h, i, 0, 0)),
            pl.BlockSpec((None, None, nb, MOBA_BLOCK), lambda h, i: (h, i, 0, 0)),
            pl.BlockSpec((None, S, KEY_AUG), lambda h, i: (h, 0, 0)),
            pl.BlockSpec((None, nb, VALUE_AUG, MOBA_BLOCK), lambda h, i: (h, 0, 0, 0)),
        ],
        out_specs=pl.BlockSpec((MOBA_BLOCK, HEAD_DIM), lambda h, i: (i, h)),
        out_shape=jax.ShapeDtypeStruct((S, ATTN_WIDTH), BF16),
        scratch_shapes=[
            pltpu.VMEM((ATTN_KV_UNROLL, MOBA_BLOCK, MOBA_BLOCK), F32),
            pltpu.VMEM((ATTN_KV_UNROLL, MOBA_BLOCK, MOBA_BLOCK), F32),
            pltpu.VMEM((ATTN_KV_UNROLL, MOBA_BLOCK, MOBA_BLOCK), BF16),
            pltpu.VMEM((ATTN_KV_UNROLL, MOBA_BLOCK, MOBA_BLOCK), BF16),
        ],
        compiler_params=_params(("parallel", "arbitrary")),
        name="moba_attention",
    )(slopes, qT, sel, k_aug, vT)


def _first_lane_of_max(vals, lane):
    best = jnp.max(vals, axis=1, keepdims=True)
    idx = jnp.min(jnp.where(vals == best, lane, ROUTE_LANES), axis=1, keepdims=True)
    return best, idx


def _merge_route_kernel(yp_ref, ya_ref, glp_ref, gla_ref, x_ref, wbp_ref, wba_ref, wout_ref,
                        g_ref, wr_hi_ref, wr_lo_ref, br_ref, x1_ref, hs_ref, route_ref, cnt_ref):
    tm = x_ref.shape[0]
    cap = hs_ref.shape[0]
    bp = jnp.dot(yp_ref[...], wbp_ref[...], preferred_element_type=F32)
    ba = jnp.dot(ya_ref[...], wba_ref[...], preferred_element_type=F32)
    merged = (jax.nn.sigmoid(glp_ref[...].astype(F32)) * bp
              + jax.nn.sigmoid(gla_ref[...].astype(F32)) * ba)
    x1 = x_ref[...] + jnp.dot(merged.astype(BF16), wout_ref[...], preferred_element_type=F32)
    x1_ref[...] = x1
    ms = jnp.mean(x1 * x1, axis=-1, keepdims=True)
    h2 = x1 * lax.rsqrt(ms + RMS_EPS) * g_ref[...]
    h2_hi = h2.astype(BF16)
    h2_lo = (h2 - h2_hi.astype(F32)).astype(BF16)
    logits = (jnp.dot(h2_hi, wr_hi_ref[...], preferred_element_type=F32)
              + jnp.dot(h2_lo, wr_hi_ref[...], preferred_element_type=F32)
              + jnp.dot(h2_hi, wr_lo_ref[...], preferred_element_type=F32)
              + br_ref[...])

    lane = lax.broadcasted_iota(jnp.int32, logits.shape, 1)
    g_logits = jnp.where(lane < N_GROUPS, logits, -jnp.inf)
    g_best, g_idx = _first_lane_of_max(g_logits, lane)
    g_w = 1.0 / jnp.sum(jnp.exp(g_logits - g_best), axis=1, keepdims=True)
    e_lo = EXPERT_LANE0 + EXPERTS_PER_GROUP * g_idx
    e_logits = jnp.where((lane >= e_lo) & (lane < e_lo + EXPERTS_PER_GROUP), logits, -jnp.inf)
    v1, i1 = _first_lane_of_max(e_logits, lane)
    v2, i2 = _first_lane_of_max(jnp.where(lane == i1, -jnp.inf, e_logits), lane)
    e21 = jnp.exp(v2 - v1)
    w1 = g_w / (1.0 + e21)
    w2 = g_w * e21 / (1.0 + e21)

    hit1 = lane == i1
    hit2 = lane == i2
    member = jnp.where(hit1, 1.0, jnp.where(hit2, 1.0, 0.0))
    r_tok = lax.broadcasted_iota(jnp.int32, (tm, tm), 0)
    c_tok = lax.broadcasted_iota(jnp.int32, (tm, tm), 1)
    earlier = jnp.where(c_tok < r_tok, 1.0, 0.0).astype(BF16)
    rank = jnp.dot(earlier, member.astype(BF16), preferred_element_type=F32)
    count = jnp.sum(member, axis=0, keepdims=True)
    chunks = jnp.floor((count + (MOE_CHUNK - 1)) * (1.0 / MOE_CHUNK))
    r_l = lax.broadcasted_iota(jnp.int32, (ROUTE_LANES, ROUTE_LANES), 0)
    c_l = lax.broadcasted_iota(jnp.int32, (ROUTE_LANES, ROUTE_LANES), 1)
    lower_lanes = jnp.where(r_l < c_l, 1.0, 0.0).astype(BF16)
    start = jnp.dot(jnp.broadcast_to(chunks, (8, ROUTE_LANES)).astype(BF16), lower_lanes,
                    preferred_element_type=F32)[0:1] * MOE_CHUNK
    pos = start + rank
    pos1 = jnp.sum(jnp.where(hit1, pos, 0.0), axis=1, keepdims=True)
    pos2 = jnp.sum(jnp.where(hit2, pos, 0.0), axis=1, keepdims=True)
    route = jnp.where(lane == 0, pos1, jnp.where(lane == 1, pos2,
                      jnp.where(lane == 2, w1, jnp.where(lane == 3, w2, 0.0))))
    route_ref[...] = route
    cnt_ref[...] = chunks
    route_t = route.T
    slot = lax.broadcasted_iota(jnp.int32, (cap, tm), 0)
    p1_row = route_t[0:1, :].astype(jnp.int32)
    p2_row = route_t[1:2, :].astype(jnp.int32)
    onehot = jnp.where(slot == p1_row, 1.0, jnp.where(slot == p2_row, 1.0, 0.0)).astype(BF16)
    hs_ref[...] = jnp.dot(onehot, h2_hi, preferred_element_type=F32).astype(BF16)


def _moe_cap(tm):
    worst = 2 * tm + N_EXPERTS * (MOE_CHUNK - 1)
    return -(-worst // 128) * 128


def _merge_route(yp, ya, proj, x, wbp, wba, wout, g, wr_hi, wr_lo, br, tm=MOE_TOKEN_TILE):
    S, D = x.shape
    nT = S // tm
    cap = _moe_cap(tm)
    full = lambda i: (0, 0)
    row = lambda i: (i, 0)
    glp_blk = (POOL_WIDTH + 3 * ATTN_WIDTH) // D
    return pl.pallas_call(
        _merge_route_kernel,
        grid=(S // tm,),
        in_specs=[
            pl.BlockSpec((tm, POOL_WIDTH), row),
            pl.BlockSpec((tm, ATTN_WIDTH), row),
            pl.BlockSpec((tm, D), lambda i: (i, glp_blk)),
            pl.BlockSpec((tm, D), lambda i: (i, glp_blk + 1)),
            pl.BlockSpec((tm, D), row),
            pl.BlockSpec((POOL_WIDTH, D), full),
            pl.BlockSpec((ATTN_WIDTH, D), full),
            pl.BlockSpec((D, D), full),
            pl.BlockSpec((1, D), full),
            pl.BlockSpec((D, ROUTE_LANES), full),
            pl.BlockSpec((D, ROUTE_LANES), full),
            pl.BlockSpec((1, ROUTE_LANES), full),
        ],
        out_specs=[
            pl.BlockSpec((tm, D), row),
            pl.BlockSpec((None, cap, D), lambda i: (i, 0, 0)),
            pl.BlockSpec((tm, ROUTE_LANES), row),
            pl.BlockSpec((None, 1, ROUTE_LANES), lambda i: (i, 0, 0)),
        ],
        out_shape=[
            jax.ShapeDtypeStruct((S, D), F32),
            jax.ShapeDtypeStruct((nT, cap, D), BF16),
            jax.ShapeDtypeStruct((S, ROUTE_LANES), F32),
            jax.ShapeDtypeStruct((nT, 1, ROUTE_LANES), F32),
        ],
        compiler_params=_params(("parallel",)),
        name="merge_route",
    )(yp, ya, proj, proj, x, wbp, wba, wout, g, wr_hi, wr_lo, br)


def _moe_plan(chunk_counts, n_row_tiles, n_chunk_slots):
    nT, E = chunk_counts.shape
    per_expert = chunk_counts.T
    seg_start = (jnp.cumsum(chunk_counts, axis=1) - chunk_counts).T
    seg_end = jnp.cumsum(per_expert, axis=1)
    n_chunks = seg_end[:, -1]
    padded = -(-n_chunks // MOE_CHUNKS_PER_TILE) * MOE_CHUNKS_PER_TILE
    e_end = jnp.cumsum(padded)
    n_used = e_end[-1] // MOE_CHUNKS_PER_TILE
    c = jnp.arange(n_chunk_slots, dtype=jnp.int32)
    e_of_c = jnp.minimum((e_end[None, :] <= c[:, None]).sum(axis=1), E - 1)
    is_e = (e_of_c[:, None] == jnp.arange(E)[None, :]).astype(jnp.int32)
    local = c - (is_e * (e_end - padded)[None, :]).sum(axis=1)
    real = local < (is_e * n_chunks[None, :]).sum(axis=1)
    pick_e = lambda table: (is_e[:, :, None] * table[None, :, :]).sum(axis=1)
    seg_end_c = pick_e(seg_end)
    t_of_c = jnp.minimum((seg_end_c <= local[:, None]).sum(axis=1), nT - 1)
    is_t = (t_of_c[:, None] == jnp.arange(nT)[None, :]).astype(jnp.int32)
    pick_t = lambda rows: (rows * is_t).sum(axis=1)
    within = local - (pick_t(seg_end_c) - pick_t(pick_e(per_expert)))
    src_tile = jnp.where(real, t_of_c, 0).astype(jnp.int32)
    src_row = jnp.where(real, (pick_t(pick_e(seg_start)) + within) * MOE_CHUNK, 0).astype(jnp.int32)
    tile = jnp.arange(n_row_tiles, dtype=jnp.int32)
    first = jnp.minimum(tile, n_used - 1) * MOE_CHUNKS_PER_TILE
    tile_expert = jnp.minimum((e_end[None, :] <= first[:, None]).sum(axis=1), E - 1)
    tile_real = real.reshape(n_row_tiles, MOE_CHUNKS_PER_TILE).sum(axis=1)
    experts = jnp.arange(E)
    nonempty = padded > 0
    buffer_of_e = jnp.cumsum(nonempty) - nonempty
    later = (experts[None, :] > experts[:, None]) & nonempty[None, :]
    next_of_e = jnp.min(jnp.where(later, experts[None, :], E), axis=1)
    next_of_e = jnp.where(next_of_e < E, next_of_e, -1)
    is_te = (tile_expert[:, None] == experts[None, :]).astype(jnp.int32)
    at = lambda table: (is_te * table[None, :]).sum(axis=1)
    run_len = jnp.maximum(at(padded) // MOE_CHUNKS_PER_TILE, 1)
    run_pos = tile - at(e_end - padded) // MOE_CHUNKS_PER_TILE
    next_expert = at(next_of_e)
    streams = (tile < n_used) & (next_expert >= 0)
    piece_lo = jnp.where(streams, (MOE_WEIGHT_PIECES * run_pos) // run_len, 0)
    piece_hi = jnp.where(streams, (MOE_WEIGHT_PIECES * (run_pos + 1)) // run_len, 0)
    as_i32 = lambda a: a.astype(jnp.int32)
    return (as_i32(tile_expert), as_i32(tile_real), src_tile, src_row, as_i32(n_used.reshape(1)),
            as_i32(at(buffer_of_e) % 2), as_i32(jnp.maximum(next_expert, 0)),
            as_i32(piece_lo), as_i32(piece_hi))


def _moe_ffn_kernel(texp_ref, treal_ref, ctile_ref, crow_ref, nused_ref,
                    wbuf_ref, wnext_ref, plo_ref, phi_ref,
                    hs_hbm, wg_hbm, wu_hbm, wd_hbm, ys_in_hbm, ys_hbm,
                    xbuf, ybuf, wg_buf, wu_buf, wd_buf, stage_g, stage_u, stage_d,
                    gather_sem, scatter_sem, weight_sem):
    del ys_in_hbm
    i = pl.program_id(0)
    n_used = nused_ref[0]
    slot = lax.rem(i, 2)
    rows_gu = stage_g.shape[1]
    rows_d = stage_d.shape[1]

    def piece_copies(e, p, s):
        gu_rows = pl.ds(pl.multiple_of(p * rows_gu, rows_gu), rows_gu)
        d_rows = pl.ds(pl.multiple_of(p * rows_d, rows_d), rows_d)
        return (pltpu.make_async_copy(wg_hbm.at[e, gu_rows, :], stage_g.at[s], weight_sem.at[s]),
                pltpu.make_async_copy(wu_hbm.at[e, gu_rows, :], stage_u.at[s], weight_sem.at[s]),
                pltpu.make_async_copy(wd_hbm.at[e, d_rows, :], stage_d.at[s], weight_sem.at[s]))

    def start_piece(e, p):
        for copy in piece_copies(e, p, lax.rem(p, 2)):
            copy.start()

    def finish_piece(e, p, side):
        s = lax.rem(p, 2)
        for copy in piece_copies(e, p, s):
            copy.wait()
        gu_rows = pl.ds(pl.multiple_of(p * rows_gu, rows_gu), rows_gu)
        d_rows = pl.ds(pl.multiple_of(p * rows_d, rows_d), rows_d)
        wg_buf[side, gu_rows, :] = stage_g[s].astype(BF16)
        wu_buf[side, gu_rows, :] = stage_u[s].astype(BF16)
        wd_buf[side, d_rows, :] = stage_d[s].astype(BF16)

    def stream_pieces(e, lo, hi, side):
        @pl.when(hi > lo)
        def _():
            @pl.when(lo == 0)
            def _():
                start_piece(e, 0)

            def body(p, carry):
                @pl.when(p + 1 < MOE_WEIGHT_PIECES)
                def _():
                    start_piece(e, p + 1)
                finish_piece(e, p, side)
                return carry
            lax.fori_loop(lo, hi, body, 0)

    def chunk_rows(c):
        return pl.ds(pl.multiple_of(c * MOE_CHUNK, MOE_CHUNK), MOE_CHUNK)

    def gather_copy(tile, c, buf):
        g = tile * MOE_CHUNKS_PER_TILE + c
        src = hs_hbm.at[ctile_ref[g], pl.ds(pl.multiple_of(crow_ref[g], MOE_CHUNK), MOE_CHUNK), :]
        return pltpu.make_async_copy(src, xbuf.at[buf, chunk_rows(c), :], gather_sem.at[buf])

    def scatter_copy(tile, c, buf):
        g = tile * MOE_CHUNKS_PER_TILE + c
        dst = ys_hbm.at[ctile_ref[g], pl.ds(pl.multiple_of(crow_ref[g], MOE_CHUNK), MOE_CHUNK), :]
        return pltpu.make_async_copy(ybuf.at[buf, chunk_rows(c), :], dst, scatter_sem.at[buf])

    def for_real_chunks(tile, fn):
        def body(c, carry):
            fn(c)
            return carry
        lax.fori_loop(0, treal_ref[tile], body, 0)

    @pl.when(i == 0)
    def _():
        xbuf[...] = jnp.zeros(xbuf.shape, xbuf.dtype)
        for_real_chunks(0, lambda c: gather_copy(0, c, 0).start())
        stream_pieces(texp_ref[0], 0, MOE_WEIGHT_PIECES, wbuf_ref[0])

    @pl.when(i + 1 < n_used)
    def _():
        for_real_chunks(i + 1, lambda c: gather_copy(i + 1, c, 1 - slot).start())

    side = wbuf_ref[i]
    stream_pieces(wnext_ref[i], plo_ref[i], phi_ref[i], 1 - side)

    @pl.when(i < n_used)
    def _():
        for_real_chunks(i, lambda c: gather_copy(i, c, slot).wait())

        @pl.when(i >= 2)
        def _():
            for_real_chunks(i - 2, lambda c: scatter_copy(i - 2, c, slot).wait())

        x = xbuf[slot]
        a = jnp.dot(x, wg_buf[side], preferred_element_type=F32)
        u = jnp.dot(x, wu_buf[side], preferred_element_type=F32)
        hid = (jax.nn.silu(a) * u).astype(BF16)
        ybuf[slot] = jnp.dot(hid, wd_buf[side], preferred_element_type=F32).astype(BF16)
        for_real_chunks(i, lambda c: scatter_copy(i, c, slot).start())

    @pl.when(i == n_used - 1)
    def _():
        @pl.when(i >= 1)
        def _():
            for_real_chunks(i - 1, lambda c: scatter_copy(i - 1, c, 1 - slot).wait())
        for_real_chunks(i, lambda c: scatter_copy(i, c, slot).wait())


def _moe_ffn(hs, plan, wg, wu, wd):
    nT, cap, D = hs.shape
    E, _, F = wg.shape
    n_row_tiles = plan[0].shape[0]
    assert D % MOE_WEIGHT_PIECES == 0 and F % MOE_WEIGHT_PIECES == 0
    grid_spec = pltpu.PrefetchScalarGridSpec(
        num_scalar_prefetch=len(plan),
        grid=(n_row_tiles,),
        in_specs=[pl.BlockSpec(memory_space=pl.ANY)] * 5,
        out_specs=pl.BlockSpec(memory_space=pl.ANY),
        scratch_shapes=[
            pltpu.VMEM((2, MOE_ROW_TILE, D), BF16),
            pltpu.VMEM((2, MOE_ROW_TILE, D), BF16),
            pltpu.VMEM((2, D, F), BF16),
            pltpu.VMEM((2, D, F), BF16),
            pltpu.VMEM((2, F, D), BF16),
            pltpu.VMEM((2, D // MOE_WEIGHT_PIECES, F), F32),
            pltpu.VMEM((2, D // MOE_WEIGHT_PIECES, F), F32),
            pltpu.VMEM((2, F // MOE_WEIGHT_PIECES, D), F32),
            pltpu.SemaphoreType.DMA((2,)),
            pltpu.SemaphoreType.DMA((2,)),
            pltpu.SemaphoreType.DMA((2,)),
        ],
    )
    ys_init = jnp.zeros((nT, cap, D), BF16)
    return pl.pallas_call(
        _moe_ffn_kernel,
        grid_spec=grid_spec,
        out_shape=jax.ShapeDtypeStruct((nT, cap, D), BF16),
        input_output_aliases={len(plan) + 4: 0},
        compiler_params=_params(("arbitrary",)),
        name="moe_ffn",
    )(*plan, hs, wg, wu, wd, ys_init)


def _moe_combine_kernel(ys_ref, route_ref, x1_ref, g_ref, o_ref):
    tm = x1_ref.shape[0]
    cap = ys_ref.shape[0]
    route = route_ref[...]
    slot = lax.broadcasted_iota(jnp.int32, (tm, cap), 1)
    pick1 = jnp.where(slot == route[:, 0:1].astype(jnp.int32), 1.0, 0.0).astype(BF16)
    pick2 = jnp.where(slot == route[:, 1:2].astype(jnp.int32), 1.0, 0.0).astype(BF16)
    ys = ys_ref[...]
    y1 = jnp.dot(pick1, ys, preferred_element_type=F32)
    y2 = jnp.dot(pick2, ys, preferred_element_type=F32)
    y = x1_ref[...] + route[:, 2:3] * y1 + route[:, 3:4] * y2
    ms = jnp.mean(y * y, axis=-1, keepdims=True)
    o_ref[...] = y * lax.rsqrt(ms + RMS_EPS) * g_ref[...]


def _moe_combine(ys, route, x1, g, tm=MOE_TOKEN_TILE):
    S, D = x1.shape
    cap = ys.shape[1]
    row = lambda i: (i, 0)
    return pl.pallas_call(
        _moe_combine_kernel,
        grid=(S // tm,),
        in_specs=[
            pl.BlockSpec((None, cap, D), lambda i: (i, 0, 0)),
            pl.BlockSpec((tm, ROUTE_LANES), row),
            pl.BlockSpec((tm, D), row),
            pl.BlockSpec((1, D), lambda i: (0, 0)),
        ],
        out_specs=pl.BlockSpec((tm, D), row),
        out_shape=jax.ShapeDtypeStruct((S, D), F32),
        compiler_params=_params(("parallel",)),
        name="moe_combine",
    )(ys, route, x1, g)


def _router_weights(w_r_group, b_r_group, w_r_expert, b_r_expert):
    D = w_r_group.shape[0]
    w = jnp.concatenate(
        [w_r_group, jnp.transpose(w_r_expert, (1, 0, 2)).reshape(D, N_EXPERTS)], axis=1)
    b = jnp.concatenate([b_r_group, b_r_expert.reshape(N_EXPERTS)])
    pad = ROUTE_LANES - w.shape[1]
    w = jnp.pad(w, ((0, 0), (0, pad)))
    b = jnp.pad(b, (0, pad)).reshape(1, ROUTE_LANES)
    w_hi = w.astype(BF16)
    w_lo = (w - w_hi.astype(F32)).astype(BF16)
    return w_hi, w_lo, b


def kernel(x, norm_mix, w_in, w_pool, pool_scale, w_branch_pool, w_branch_attn, w_out, norm_ffn,
           w_r_group, b_r_group, w_r_expert, b_r_expert, w_gate, w_up, w_down, norm_final):
    B, S, D = x.shape
    depth = w_in.shape[0]
    assert depth == 1, "the final rms_norm is fused into the expert kernel of a single layer"
    slopes = jnp.exp2(-8.0 * jnp.arange(1, ATTN_HEADS + 1, dtype=F32) / ATTN_HEADS)
    slopes = jnp.broadcast_to(slopes[:, None, None], (ATTN_HEADS, 1, 128))
    outs = []
    for b in range(B):
        xb = x[b]
        for l in range(depth):
            proj = _norm_inproj(xb, norm_mix[l].reshape(1, D), w_in[l])
            y_pool = _pool_mixer(proj, w_pool[l].astype(BF16), pool_scale[l].reshape(1, POOL_WIDTH))
            qT, k_aug, vT, sel = _moba_gate(proj, slopes)
            y_attn = _moba_attention(k_aug, qT, vT, sel, slopes)
            wr_hi, wr_lo, br = _router_weights(w_r_group[l], b_r_group[l], w_r_expert[l], b_r_expert[l])
            x1, hs, route, cnt = _merge_route(
                y_pool, y_attn, proj, xb,
                w_branch_pool[l].astype(BF16), w_branch_attn[l].astype(BF16), w_out[l].astype(BF16),
                norm_ffn[l].reshape(1, D), wr_hi, wr_lo, br)
            nT = hs.shape[0]
            chunk_counts = cnt[:, 0, EXPERT_LANE0:EXPERT_LANE0 + N_EXPERTS].astype(jnp.int32)
            max_chunks = nT * ((2 * MOE_TOKEN_TILE + N_EXPERTS * (MOE_CHUNK - 1)) // MOE_CHUNK)
            n_row_tiles = -(-(max_chunks + N_EXPERTS * (MOE_CHUNKS_PER_TILE - 1))
                            // MOE_CHUNKS_PER_TILE)
            plan = _moe_plan(chunk_counts, n_row_tiles, n_row_tiles * MOE_CHUNKS_PER_TILE)
            ys = _moe_ffn(hs, plan, w_gate[l], w_up[l], w_down[l])
            xb = _moe_combine(ys, route, x1, norm_final.reshape(1, D))
        outs.append(xb)
    return jnp.stack(outs, axis=0)
```

```python
import functools

import jax
import jax.numpy as jnp
from jax import lax
from jax.experimental import pallas as pl
from jax.experimental.pallas import tpu as pltpu

F32 = jnp.float32
BF16 = jnp.bfloat16

POOL_WINDOWS = (2, 4, 8, 16)
MAX_WINDOW = 16
POOL_WIDTH = 1024
POOL_GROUP = 256
HEAD_DIM = 128
ATTN_HEADS = 8
ATTN_WIDTH = 1024
MOBA_BLOCK = 256
MOBA_TOPK = 3
N_GROUPS = 4
EXPERTS_PER_GROUP = 4
N_EXPERTS = 16
ROUTE_LANES = 128
EXPERT_LANE0 = N_GROUPS
RMS_EPS = 1e-6
NEG_INF = -1e30
LOG2_E = 1.4426950408889634
QUERY_SCALE = (HEAD_DIM ** -0.5) * LOG2_E
KEY_AUG = 2 * HEAD_DIM
ALIBI_TERMS = 3
VALUE_AUG = HEAD_DIM + 16
ATTN_KV_UNROLL = 2
MOE_CHUNK = 16
MOE_TOKEN_TILE = 256
MOE_ROW_TILE = 256
MOE_CHUNKS_PER_TILE = MOE_ROW_TILE // MOE_CHUNK
MOE_WEIGHT_PIECES = 8
MOE_WEIGHT_STAGES = 4
assert MOE_WEIGHT_STAGES - 1 <= MOE_WEIGHT_PIECES

V7X_VMEM_LIMIT_BYTES = 56 * 1024 * 1024


def _params(semantics, vmem=V7X_VMEM_LIMIT_BYTES, flags=None):
    return pltpu.CompilerParams(dimension_semantics=semantics, vmem_limit_bytes=vmem, flags=flags)


def _norm_inproj_kernel(x_ref, g_ref, w_ref, o_ref, wb_ref):
    @pl.when(pl.program_id(1) == 0)
    def _():
        wb_ref[...] = w_ref[...].astype(BF16)

    x = x_ref[...]
    ms = jnp.mean(x * x, axis=-1, keepdims=True)
    h = (x * lax.rsqrt(ms + RMS_EPS) * g_ref[...]).astype(BF16)
    col0 = pl.program_id(0) * o_ref.shape[1]
    is_q = (col0 >= POOL_WIDTH) & (col0 < POOL_WIDTH + ATTN_WIDTH)
    factor = jnp.where(is_q, QUERY_SCALE, 1.0).astype(F32)
    o_ref[...] = (jnp.dot(h, wb_ref[...], preferred_element_type=F32) * factor).astype(o_ref.dtype)


def _norm_inproj(x, g, w, tm=1024, tn=1024):
    S, D = x.shape
    N = w.shape[1]
    assert POOL_WIDTH % tn == 0 and ATTN_WIDTH % tn == 0
    return pl.pallas_call(
        _norm_inproj_kernel,
        grid=(N // tn, S // tm),
        in_specs=[
            pl.BlockSpec((tm, D), lambda j, i: (i, 0)),
            pl.BlockSpec((1, D), lambda j, i: (0, 0)),
            pl.BlockSpec((D, tn), lambda j, i: (0, j)),
        ],
        out_specs=pl.BlockSpec((tm, tn), lambda j, i: (i, j)),
        out_shape=jax.ShapeDtypeStruct((S, N), BF16),
        scratch_shapes=[pltpu.VMEM((D, tn), BF16)],
        compiler_params=_params(("parallel", "arbitrary")),
        name="norm_inproj",
    )(x, g, w)


def _pool_kernel(cur_ref, prev_ref, w_ref, scale_ref, o_ref, ext_ref):
    i = pl.program_id(0)
    tm = cur_ref.shape[0]
    u = cur_ref[...].astype(F32)
    halo = jnp.where(i > 0, prev_ref[...].astype(F32), 0.0)
    ext_ref[0:MAX_WINDOW, :] = halo
    ext_ref[MAX_WINDOW:MAX_WINDOW + tm, :] = u
    t = i * tm + lax.broadcasted_iota(jnp.int32, (tm, 1), 0)
    for g, w in enumerate(POOL_WINDOWS):
        cols = slice(g * POOL_GROUP, (g + 1) * POOL_GROUP)
        ug = u[:, cols]
        wsum = ug
        for s in range(1, w):
            wsum = wsum + ext_ref[MAX_WINDOW - s:MAX_WINDOW - s + tm, cols]
        cnt = jnp.minimum(t + 1, w).astype(F32)
        mixed = (wsum / cnt - ug).astype(BF16)
        y = jnp.dot(mixed, w_ref[g], preferred_element_type=F32)
        o_ref[:, cols] = (y * scale_ref[:, cols]).astype(o_ref.dtype)


def _pool_mixer(proj, w_pool, pool_scale, tm=512):
    S = proj.shape[0]
    halo_blocks = tm // MAX_WINDOW
    return pl.pallas_call(
        _pool_kernel,
        grid=(S // tm,),
        in_specs=[
            pl.BlockSpec((tm, POOL_WIDTH), lambda i: (i, 0)),
            pl.BlockSpec((MAX_WINDOW, POOL_WIDTH),
                         lambda i: (jnp.maximum(i * halo_blocks - 1, 0), 0)),
            pl.BlockSpec((len(POOL_WINDOWS), POOL_GROUP, POOL_GROUP), lambda i: (0, 0, 0)),
            pl.BlockSpec((1, POOL_WIDTH), lambda i: (0, 0)),
        ],
        out_specs=pl.BlockSpec((tm, POOL_WIDTH), lambda i: (i, 0)),
        out_shape=jax.ShapeDtypeStruct((S, POOL_WIDTH), BF16),
        scratch_shapes=[pltpu.VMEM((tm + MAX_WINDOW, POOL_WIDTH), F32)],
        compiler_params=_params(("parallel",)),
        name="pool_mixer",
    )(proj, proj, w_pool, pool_scale)


def _moba_gate_kernel(slope_ref, q_ref, k_ref, v_ref, qT_ref, ka_ref, vT_ref, sel_ref):
    S = q_ref.shape[0]
    nb = S // MOBA_BLOCK
    topk = min(MOBA_TOPK, nb)
    kf = k_ref[...].astype(F32).reshape(nb, MOBA_BLOCK, HEAD_DIM)
    kmean = jnp.sum(kf, axis=1) * (1.0 / MOBA_BLOCK)
    km_hi = kmean.astype(BF16)
    km_lo = (kmean - km_hi.astype(F32)).astype(BF16)
    blk = lax.broadcasted_iota(jnp.int32, (nb, MOBA_BLOCK), 0)

    pos = lax.broadcasted_iota(jnp.int32, (MOBA_BLOCK, KEY_AUG - HEAD_DIM), 0).astype(F32)
    col = lax.broadcasted_iota(jnp.int32, (MOBA_BLOCK, KEY_AUG - HEAD_DIM), 1)
    k_extra = jnp.where(col < ALIBI_TERMS, pos, 0.0).astype(BF16)

    slope2 = slope_ref[...][:, 0:1] * LOG2_E
    q_extra = jnp.zeros((KEY_AUG - HEAD_DIM, MOBA_BLOCK), F32)
    row = lax.broadcasted_iota(jnp.int32, q_extra.shape, 0)
    rest = slope2
    for n in range(ALIBI_TERMS):
        piece = rest.astype(BF16).astype(F32)
        q_extra = jnp.where(row == n, piece, q_extra)
        rest = rest - piece
    q_extra = q_extra.astype(BF16)
    v_row = lax.broadcasted_iota(jnp.int32, (VALUE_AUG - HEAD_DIM, MOBA_BLOCK), 0)
    v_extra = jnp.where(v_row == 0, 1.0, 0.0).astype(BF16)

    def body(i, carry):
        rows = pl.ds(pl.multiple_of(i * MOBA_BLOCK, MOBA_BLOCK), MOBA_BLOCK)
        qT = q_ref[rows, :].astype(F32).T.astype(BF16)
        qT_ref[i, 0:HEAD_DIM, :] = qT
        qT_ref[i, HEAD_DIM:KEY_AUG, :] = q_extra
        ka_ref[rows, 0:HEAD_DIM] = k_ref[rows, :]
        ka_ref[rows, HEAD_DIM:KEY_AUG] = k_extra
        vT_ref[i, 0:HEAD_DIM, :] = v_ref[rows, :].astype(F32).T.astype(BF16)
        vT_ref[i, HEAD_DIM:VALUE_AUG, :] = v_extra
        gate = (jnp.dot(km_hi, qT, preferred_element_type=F32)
                + jnp.dot(km_lo, qT, preferred_element_type=F32)) * (1.0 / QUERY_SCALE)
        gate = jnp.where(blk < i, gate, NEG_INF)
        sel = jnp.zeros((nb, MOBA_BLOCK), F32)
        for _ in range(topk):
            best = jnp.max(gate, axis=0, keepdims=True)
            idx = jnp.min(jnp.where(gate == best, blk, nb), axis=0, keepdims=True)
            hit = blk == idx
            sel = jnp.where(hit & (blk < i), 1.0, sel)
            gate = jnp.where(hit, -jnp.inf, gate)
        sel_ref[i] = sel
        return carry

    lax.fori_loop(0, nb, body, 0, unroll=4 if nb % 4 == 0 else 1)


def _moba_gate(proj, slopes):
    S = proj.shape[0]
    nb = S // MOBA_BLOCK
    H = ATTN_HEADS
    q0 = POOL_WIDTH // HEAD_DIM
    k0 = q0 + H
    v0 = k0 + H
    blocked = lambda h: (h, 0, 0, 0)
    return pl.pallas_call(
        _moba_gate_kernel,
        grid=(H,),
        in_specs=[
            pl.BlockSpec((None, 1, 128), lambda h: (h, 0, 0)),
            pl.BlockSpec((S, HEAD_DIM), lambda h: (0, q0 + h)),
            pl.BlockSpec((S, HEAD_DIM), lambda h: (0, k0 + h)),
            pl.BlockSpec((S, HEAD_DIM), lambda h: (0, v0 + h)),
        ],
        out_specs=[
            pl.BlockSpec((None, nb, KEY_AUG, MOBA_BLOCK), blocked),
            pl.BlockSpec((None, S, KEY_AUG), lambda h: (h, 0, 0)),
            pl.BlockSpec((None, nb, VALUE_AUG, MOBA_BLOCK), blocked),
            pl.BlockSpec((None, nb, nb, MOBA_BLOCK), blocked),
        ],
        out_shape=[
            jax.ShapeDtypeStruct((H, nb, KEY_AUG, MOBA_BLOCK), BF16),
            jax.ShapeDtypeStruct((H, S, KEY_AUG), BF16),
            jax.ShapeDtypeStruct((H, nb, VALUE_AUG, MOBA_BLOCK), BF16),
            jax.ShapeDtypeStruct((H, nb, nb, MOBA_BLOCK), F32),
        ],
        compiler_params=_params(("parallel",)),
        name="moba_gate",
    )(slopes, proj, proj, proj)


def _moba_attn_kernel(slope_ref, qT_ref, sel_ref, k_ref, vT_ref, o_ref,
                      s0_ref, s1_ref, p0_ref, p1_ref):
    i = pl.program_id(1)
    nb = sel_ref.shape[0]
    slope2 = slope_ref[...][:, 0:1] * LOG2_E
    qT = qT_ref[...]

    def block_of(t, u):
        return jnp.clip(t * ATTN_KV_UNROLL + u, 0, nb - 1)

    def issue_scores(t, s_ref):
        for u in range(ATTN_KV_UNROLL):
            rows = pl.ds(pl.multiple_of(block_of(t, u) * MOBA_BLOCK, MOBA_BLOCK), MOBA_BLOCK)
            s_ref[u] = jnp.dot(k_ref[rows, :], qT, preferred_element_type=F32)

    def apply_probs(t, p_ref, alpha, acc):
        acc = alpha * acc
        for u in range(ATTN_KV_UNROLL):
            acc = acc + jnp.dot(vT_ref[block_of(t, u)], p_ref[u], preferred_element_type=F32)
        return acc

    def softmax_group(t, s_ref, p_ref, m):
        m_new = m
        shifts = []
        for u in range(ATTN_KV_UNROLL):
            j = t * ATTN_KV_UNROLL + u
            valid = jnp.where(j < i, sel_ref[pl.ds(block_of(t, u), 1), :], 0.0) > 0.0
            gap = slope2 * ((i - j) * MOBA_BLOCK).astype(F32)
            top = jnp.max(s_ref[u], axis=0, keepdims=True) - gap
            m_new = jnp.maximum(m_new, jnp.where(valid, top, NEG_INF))
            shifts.append((valid, gap))
        alpha = jnp.exp2(m - m_new)
        for u, (valid, gap) in enumerate(shifts):
            p = jnp.exp2(s_ref[u] - jnp.where(valid, m_new + gap, jnp.inf))
            p_ref[u] = p.astype(BF16)
        return m_new, alpha

    def body(r, carry):
        m, acc, alpha = carry
        acc = apply_probs(2 * r - 1, p1_ref, alpha, acc)
        m, alpha = softmax_group(2 * r, s0_ref, p0_ref, m)
        issue_scores(2 * r + 1, s1_ref)
        acc = apply_probs(2 * r, p0_ref, alpha, acc)
        m, alpha = softmax_group(2 * r + 1, s1_ref, p1_ref, m)
        issue_scores(2 * r + 2, s0_ref)
        return m, acc, alpha

    n_groups = (i + ATTN_KV_UNROLL - 1) // ATTN_KV_UNROLL
    n_pairs = n_groups // 2
    p1_ref[...] = jnp.zeros(p1_ref.shape, BF16)
    issue_scores(0, s0_ref)
    own_rows = pl.ds(pl.multiple_of(i * MOBA_BLOCK, MOBA_BLOCK), MOBA_BLOCK)
    s_own = jnp.dot(k_ref[own_rows, :], qT, preferred_element_type=F32)
    kpos = lax.broadcasted_iota(jnp.int32, (MOBA_BLOCK, MOBA_BLOCK), 0)
    qpos = lax.broadcasted_iota(jnp.int32, (MOBA_BLOCK, MOBA_BLOCK), 1)
    s_own = jnp.where(qpos >= kpos, s_own, NEG_INF)
    top_own = jnp.max(s_own, axis=0, keepdims=True)
    row = jnp.zeros((1, MOBA_BLOCK), F32)
    init = (row + NEG_INF, jnp.zeros((VALUE_AUG, MOBA_BLOCK), F32), row + 1.0)
    m, acc, alpha = lax.fori_loop(0, n_pairs, body, init)
    acc = apply_probs(2 * n_pairs - 1, p1_ref, alpha, acc)
    m, alpha = softmax_group(2 * n_pairs, s0_ref, p0_ref, m)
    acc = apply_probs(2 * n_pairs, p0_ref, alpha, acc)
    m_new = jnp.maximum(m, top_own)
    alpha = jnp.exp2(m - m_new)
    p = jnp.exp2(s_own - m_new)
    acc = alpha * acc + jnp.dot(vT_ref[i], p.astype(BF16), preferred_element_type=F32)
    out = acc[0:HEAD_DIM] / acc[HEAD_DIM:HEAD_DIM + 1]
    o_ref[...] = out.T.astype(o_ref.dtype)


def _moba_attention(k_aug, qT, vT, sel, slopes):
    H, S, _ = k_aug.shape
    nb = S // MOBA_BLOCK
    return pl.pallas_call(
        _moba_attn_kernel,
        grid=(H, nb),
        in_specs=[
            pl.BlockSpec((None, 1, 128), lambda h, i: (h, 0, 0)),
            pl.BlockSpec((None, None, KEY_AUG, MOBA_BLOCK), lambda h, i: (h, i, 0, 0)),
            pl.BlockSpec((None, None, nb, MOBA_BLOCK), lambda h, i: (h, i, 0, 0)),
            pl.BlockSpec((None, S, KEY_AUG), lambda h, i: (h, 0, 0)),
            pl.BlockSpec((None, nb, VALUE_AUG, MOBA_BLOCK), lambda h, i: (h, 0, 0, 0)),
        ],
        out_specs=pl.BlockSpec((MOBA_BLOCK, HEAD_DIM), lambda h, i: (i, h)),
        out_shape=jax.ShapeDtypeStruct((S, ATTN_WIDTH), BF16),
        scratch_shapes=[
            pltpu.VMEM((ATTN_KV_UNROLL, MOBA_BLOCK, MOBA_BLOCK), F32),
            pltpu.VMEM((ATTN_KV_UNROLL, MOBA_BLOCK, MOBA_BLOCK), F32),
            pltpu.VMEM((ATTN_KV_UNROLL, MOBA_BLOCK, MOBA_BLOCK), BF16),
            pltpu.VMEM((ATTN_KV_UNROLL, MOBA_BLOCK, MOBA_BLOCK), BF16),
        ],
        compiler_params=_params(("parallel", "arbitrary")),
        name="moba_attention",
    )(slopes, qT, sel, k_aug, vT)


def _first_lane_of_max(vals, lane):
    best = jnp.max(vals, axis=1, keepdims=True)
    idx = jnp.min(jnp.where(vals == best, lane, ROUTE_LANES), axis=1, keepdims=True)
    return best, idx


def _merge_route_kernel(yp_ref, ya_ref, glp_ref, gla_ref, x_ref, wbp_ref, wba_ref, wout_ref,
                        g_ref, wr_hi_ref, wr_lo_ref, br_ref, x1_ref, hs_ref, route_ref, cnt_ref):
    tm = x_ref.shape[0]
    cap = hs_ref.shape[0]
    bp = jnp.dot(yp_ref[...], wbp_ref[...], preferred_element_type=F32)
    ba = jnp.dot(ya_ref[...], wba_ref[...], preferred_element_type=F32)
    merged = (jax.nn.sigmoid(glp_ref[...].astype(F32)) * bp
              + jax.nn.sigmoid(gla_ref[...].astype(F32)) * ba)
    x1 = x_ref[...] + jnp.dot(merged.astype(BF16), wout_ref[...], preferred_element_type=F32)
    x1_ref[...] = x1
    ms = jnp.mean(x1 * x1, axis=-1, keepdims=True)
    h2 = x1 * lax.rsqrt(ms + RMS_EPS) * g_ref[...]
    h2_hi = h2.astype(BF16)
    h2_lo = (h2 - h2_hi.astype(F32)).astype(BF16)
    logits = (jnp.dot(h2_hi, wr_hi_ref[...], preferred_element_type=F32)
              + jnp.dot(h2_lo, wr_hi_ref[...], preferred_element_type=F32)
              + jnp.dot(h2_hi, wr_lo_ref[...], preferred_element_type=F32)
              + br_ref[...])

    lane = lax.broadcasted_iota(jnp.int32, logits.shape, 1)
    g_logits = jnp.where(lane < N_GROUPS, logits, -jnp.inf)
    g_best, g_idx = _first_lane_of_max(g_logits, lane)
    g_w = 1.0 / jnp.sum(jnp.exp(g_logits - g_best), axis=1, keepdims=True)
    e_lo = EXPERT_LANE0 + EXPERTS_PER_GROUP * g_idx
    e_logits = jnp.where((lane >= e_lo) & (lane < e_lo + EXPERTS_PER_GROUP), logits, -jnp.inf)
    v1, i1 = _first_lane_of_max(e_logits, lane)
    v2, i2 = _first_lane_of_max(jnp.where(lane == i1, -jnp.inf, e_logits), lane)
    e21 = jnp.exp(v2 - v1)
    w1 = g_w / (1.0 + e21)
    w2 = g_w * e21 / (1.0 + e21)

    hit1 = lane == i1
    hit2 = lane == i2
    member = jnp.where(hit1, 1.0, jnp.where(hit2, 1.0, 0.0))
    r_tok = lax.broadcasted_iota(jnp.int32, (tm, tm), 0)
    c_tok = lax.broadcasted_iota(jnp.int32, (tm, tm), 1)
    earlier = jnp.where(c_tok < r_tok, 1.0, 0.0).astype(BF16)
    rank = jnp.dot(earlier, member.astype(BF16), preferred_element_type=F32)
    count = jnp.sum(member, axis=0, keepdims=True)
    chunks = jnp.floor((count + (MOE_CHUNK - 1)) * (1.0 / MOE_CHUNK))
    r_l = lax.broadcasted_iota(jnp.int32, (ROUTE_LANES, ROUTE_LANES), 0)
    c_l = lax.broadcasted_iota(jnp.int32, (ROUTE_LANES, ROUTE_LANES), 1)
    lower_lanes = jnp.where(r_l < c_l, 1.0, 0.0).astype(BF16)
    start = jnp.dot(jnp.broadcast_to(chunks, (8, ROUTE_LANES)).astype(BF16), lower_lanes,
                    preferred_element_type=F32)[0:1] * MOE_CHUNK
    pos = start + rank
    pos1 = jnp.sum(jnp.where(hit1, pos, 0.0), axis=1, keepdims=True)
    pos2 = jnp.sum(jnp.where(hit2, pos, 0.0), axis=1, keepdims=True)
    route = jnp.where(lane == 0, pos1, jnp.where(lane == 1, pos2,
                      jnp.where(lane == 2, w1, jnp.where(lane == 3, w2, 0.0))))
    route_ref[...] = route
    cnt_ref[...] = chunks
    route_t = route.T
    slot = lax.broadcasted_iota(jnp.int32, (cap, tm), 0)
    p1_row = route_t[0:1, :].astype(jnp.int32)
    p2_row = route_t[1:2, :].astype(jnp.int32)
    onehot = jnp.where(slot == p1_row, 1.0, jnp.where(slot == p2_row, 1.0, 0.0)).astype(BF16)
    hs_ref[...] = jnp.dot(onehot, h2_hi, preferred_element_type=F32).astype(BF16)


def _moe_cap(tm):
    worst = 2 * tm + N_EXPERTS * (MOE_CHUNK - 1)
    return -(-worst // 128) * 128


def _merge_route(yp, ya, proj, x, wbp, wba, wout, g, wr_hi, wr_lo, br, tm=MOE_TOKEN_TILE):
    S, D = x.shape
    nT = S // tm
    cap = _moe_cap(tm)
    full = lambda i: (0, 0)
    row = lambda i: (i, 0)
    glp_blk = (POOL_WIDTH + 3 * ATTN_WIDTH) // D
    return pl.pallas_call(
        _merge_route_kernel,
        grid=(S // tm,),
        in_specs=[
            pl.BlockSpec((tm, POOL_WIDTH), row),
            pl.BlockSpec((tm, ATTN_WIDTH), row),
            pl.BlockSpec((tm, D), lambda i: (i, glp_blk)),
            pl.BlockSpec((tm, D), lambda i: (i, glp_blk + 1)),
            pl.BlockSpec((tm, D), row),
            pl.BlockSpec((POOL_WIDTH, D), full),
            pl.BlockSpec((ATTN_WIDTH, D), full),
            pl.BlockSpec((D, D), full),
            pl.BlockSpec((1, D), full),
            pl.BlockSpec((D, ROUTE_LANES), full),
            pl.BlockSpec((D, ROUTE_LANES), full),
            pl.BlockSpec((1, ROUTE_LANES), full),
        ],
        out_specs=[
            pl.BlockSpec((tm, D), row),
            pl.BlockSpec((None, cap, D), lambda i: (i, 0, 0)),
            pl.BlockSpec((tm, ROUTE_LANES), row),
            pl.BlockSpec((None, 1, ROUTE_LANES), lambda i: (i, 0, 0)),
        ],
        out_shape=[
            jax.ShapeDtypeStruct((S, D), F32),
            jax.ShapeDtypeStruct((nT, cap, D), BF16),
            jax.ShapeDtypeStruct((S, ROUTE_LANES), F32),
            jax.ShapeDtypeStruct((nT, 1, ROUTE_LANES), F32),
        ],
        compiler_params=_params(("parallel",)),
        name="merge_route",
    )(yp, ya, proj, proj, x, wbp, wba, wout, g, wr_hi, wr_lo, br)


def _moe_plan(chunk_counts, n_row_tiles, n_chunk_slots):
    nT, E = chunk_counts.shape
    per_expert = chunk_counts.T
    seg_start = (jnp.cumsum(chunk_counts, axis=1) - chunk_counts).T
    seg_end = jnp.cumsum(per_expert, axis=1)
    n_chunks = seg_end[:, -1]
    padded = -(-n_chunks // MOE_CHUNKS_PER_TILE) * MOE_CHUNKS_PER_TILE
    e_end = jnp.cumsum(padded)
    n_used = e_end[-1] // MOE_CHUNKS_PER_TILE
    c = jnp.arange(n_chunk_slots, dtype=jnp.int32)
    e_of_c = jnp.minimum((e_end[None, :] <= c[:, None]).sum(axis=1), E - 1)
    is_e = (e_of_c[:, None] == jnp.arange(E)[None, :]).astype(jnp.int32)
    local = c - (is_e * (e_end - padded)[None, :]).sum(axis=1)
    real = local < (is_e * n_chunks[None, :]).sum(axis=1)
    pick_e = lambda table: (is_e[:, :, None] * table[None, :, :]).sum(axis=1)
    seg_end_c = pick_e(seg_end)
    t_of_c = jnp.minimum((seg_end_c <= local[:, None]).sum(axis=1), nT - 1)
    is_t = (t_of_c[:, None] == jnp.arange(nT)[None, :]).astype(jnp.int32)
    pick_t = lambda rows: (rows * is_t).sum(axis=1)
    within = local - (pick_t(seg_end_c) - pick_t(pick_e(per_expert)))
    src_tile = jnp.where(real, t_of_c, 0).astype(jnp.int32)
    src_row = jnp.where(real, (pick_t(pick_e(seg_start)) + within) * MOE_CHUNK, 0).astype(jnp.int32)
    tile = jnp.arange(n_row_tiles, dtype=jnp.int32)
    first = jnp.minimum(tile, n_used - 1) * MOE_CHUNKS_PER_TILE
    tile_expert = jnp.minimum((e_end[None, :] <= first[:, None]).sum(axis=1), E - 1)
    tile_real = real.reshape(n_row_tiles, MOE_CHUNKS_PER_TILE).sum(axis=1)
    experts = jnp.arange(E)
    nonempty = padded > 0
    buffer_of_e = jnp.cumsum(nonempty) - nonempty
    later = (experts[None, :] > experts[:, None]) & nonempty[None, :]
    next_of_e = jnp.min(jnp.where(later, experts[None, :], E), axis=1)
    next_of_e = jnp.where(next_of_e < E, next_of_e, -1)
    is_next = (next_of_e[:, None] == experts[None, :]).astype(jnp.int32)
    after_of_e = jnp.where(next_of_e >= 0, (is_next * next_of_e[None, :]).sum(axis=1), -1)
    is_te = (tile_expert[:, None] == experts[None, :]).astype(jnp.int32)
    at = lambda table: (is_te * table[None, :]).sum(axis=1)
    run_len = jnp.maximum(at(padded) // MOE_CHUNKS_PER_TILE, 1)
    run_pos = tile - at(e_end - padded) // MOE_CHUNKS_PER_TILE
    next_expert = at(next_of_e)
    streams = (tile < n_used) & (next_expert >= 0)
    piece_lo = jnp.where(streams, (MOE_WEIGHT_PIECES * run_pos) // run_len, 0)
    piece_hi = jnp.where(streams, (MOE_WEIGHT_PIECES * (run_pos + 1)) // run_len, 0)
    as_i32 = lambda a: a.astype(jnp.int32)
    return (as_i32(tile_expert), as_i32(tile_real), src_tile, src_row, as_i32(n_used.reshape(1)),
            as_i32(at(buffer_of_e) % 2), as_i32(next_expert), as_i32(at(after_of_e)),
            as_i32(piece_lo), as_i32(piece_hi))


def _moe_ffn_kernel(texp_ref, treal_ref, ctile_ref, crow_ref, nused_ref,
                    wbuf_ref, wnext_ref, wafter_ref, plo_ref, phi_ref,
                    hs_hbm, wg_hbm, wu_hbm, wd_hbm, ys_in_hbm, ys_hbm,
                    xbuf, ybuf, wg_buf, wu_buf, wd_buf, stage_g, stage_u, stage_d,
                    gather_sem, scatter_sem, weight_sem):
    del ys_in_hbm
    i = pl.program_id(0)
    n_used = nused_ref[0]
    slot = lax.rem(i, 2)
    rows_gu = stage_g.shape[1]
    rows_d = stage_d.shape[1]

    def piece_copies(e, p, s):
        gu_rows = pl.ds(pl.multiple_of(p * rows_gu, rows_gu), rows_gu)
        d_rows = pl.ds(pl.multiple_of(p * rows_d, rows_d), rows_d)
        return (pltpu.make_async_copy(wg_hbm.at[e, gu_rows, :], stage_g.at[s], weight_sem.at[s]),
                pltpu.make_async_copy(wu_hbm.at[e, gu_rows, :], stage_u.at[s], weight_sem.at[s]),
                pltpu.make_async_copy(wd_hbm.at[e, d_rows, :], stage_d.at[s], weight_sem.at[s]))

    def start_piece(e, p):
        for copy in piece_copies(e, p, lax.rem(p, MOE_WEIGHT_STAGES)):
            copy.start()

    def finish_piece(e, p, side):
        s = lax.rem(p, MOE_WEIGHT_STAGES)
        for copy in piece_copies(e, p, s):
            copy.wait()
        gu_rows = pl.ds(pl.multiple_of(p * rows_gu, rows_gu), rows_gu)
        d_rows = pl.ds(pl.multiple_of(p * rows_d, rows_d), rows_d)
        wg_buf[side, gu_rows, :] = stage_g[s].astype(BF16)
        wu_buf[side, gu_rows, :] = stage_u[s].astype(BF16)
        wd_buf[side, d_rows, :] = stage_d[s].astype(BF16)

    ahead = MOE_WEIGHT_STAGES - 1

    def start_first_pieces(e):
        for p in range(ahead):
            start_piece(e, p)

    def stream_pieces(e, lo, hi, side, following):
        @pl.when(hi > lo)
        def _():
            def body(p, carry):
                @pl.when(p + ahead < MOE_WEIGHT_PIECES)
                def _():
                    start_piece(e, p + ahead)
                finish_piece(e, p, side)
                return carry
            lax.fori_loop(lo, hi, body, 0)

            @pl.when((hi == MOE_WEIGHT_PIECES) & (following >= 0))
            def _():
                start_first_pieces(following)

    def chunk_rows(c):
        return pl.ds(pl.multiple_of(c * MOE_CHUNK, MOE_CHUNK), MOE_CHUNK)

    def gather_copy(tile, c, buf):
        g = tile * MOE_CHUNKS_PER_TILE + c
        src = hs_hbm.at[ctile_ref[g], pl.ds(pl.multiple_of(crow_ref[g], MOE_CHUNK), MOE_CHUNK), :]
        return pltpu.make_async_copy(src, xbuf.at[buf, chunk_rows(c), :], gather_sem.at[buf])

    def scatter_copy(tile, c, buf):
        g = tile * MOE_CHUNKS_PER_TILE + c
        dst = ys_hbm.at[ctile_ref[g], pl.ds(pl.multiple_of(crow_ref[g], MOE_CHUNK), MOE_CHUNK), :]
        return pltpu.make_async_copy(ybuf.at[buf, chunk_rows(c), :], dst, scatter_sem.at[buf])

    def for_real_chunks(tile, fn):
        def body(c, carry):
            fn(c)
            return carry
        lax.fori_loop(0, treal_ref[tile], body, 0)

    @pl.when(i == 0)
    def _():
        xbuf[...] = jnp.zeros(xbuf.shape, xbuf.dtype)
        for_real_chunks(0, lambda c: gather_copy(0, c, 0).start())
        start_first_pieces(texp_ref[0])
        stream_pieces(texp_ref[0], 0, MOE_WEIGHT_PIECES, wbuf_ref[0], wnext_ref[0])

    @pl.when(i + 1 < n_used)
    def _():
        for_real_chunks(i + 1, lambda c: gather_copy(i + 1, c, 1 - slot).start())

    side = wbuf_ref[i]
    stream_pieces(wnext_ref[i], plo_ref[i], phi_ref[i], 1 - side, wafter_ref[i])

    @pl.when(i < n_used)
    def _():
        for_real_chunks(i, lambda c: gather_copy(i, c, slot).wait())

        @pl.when(i >= 2)
        def _():
            for_real_chunks(i - 2, lambda c: scatter_copy(i - 2, c, slot).wait())

        x = xbuf[slot]
        a = jnp.dot(x, wg_buf[side], preferred_element_type=F32)
        u = jnp.dot(x, wu_buf[side], preferred_element_type=F32)
        hid = (jax.nn.silu(a) * u).astype(BF16)
        ybuf[slot] = jnp.dot(hid, wd_buf[side], preferred_element_type=F32).astype(BF16)
        for_real_chunks(i, lambda c: scatter_copy(i, c, slot).start())

    @pl.when(i == n_used - 1)
    def _():
        @pl.when(i >= 1)
        def _():
            for_real_chunks(i - 1, lambda c: scatter_copy(i - 1, c, 1 - slot).wait())
        for_real_chunks(i, lambda c: scatter_copy(i, c, slot).wait())


def _moe_ffn(hs, plan, wg, wu, wd):
    nT, cap, D = hs.shape
    E, _, F = wg.shape
    n_row_tiles = plan[0].shape[0]
    assert D % MOE_WEIGHT_PIECES == 0 and F % MOE_WEIGHT_PIECES == 0
    grid_spec = pltpu.PrefetchScalarGridSpec(
        num_scalar_prefetch=len(plan),
        grid=(n_row_tiles,),
        in_specs=[pl.BlockSpec(memory_space=pl.ANY)] * 5,
        out_specs=pl.BlockSpec(memory_space=pl.ANY),
        scratch_shapes=[
            pltpu.VMEM((2, MOE_ROW_TILE, D), BF16),
            pltpu.VMEM((2, MOE_ROW_TILE, D), BF16),
            pltpu.VMEM((2, D, F), BF16),
            pltpu.VMEM((2, D, F), BF16),
            pltpu.VMEM((2, F, D), BF16),
            pltpu.VMEM((MOE_WEIGHT_STAGES, D // MOE_WEIGHT_PIECES, F), F32),
            pltpu.VMEM((MOE_WEIGHT_STAGES, D // MOE_WEIGHT_PIECES, F), F32),
            pltpu.VMEM((MOE_WEIGHT_STAGES, F // MOE_WEIGHT_PIECES, D), F32),
            pltpu.SemaphoreType.DMA((2,)),
            pltpu.SemaphoreType.DMA((2,)),
            pltpu.SemaphoreType.DMA((MOE_WEIGHT_STAGES,)),
        ],
    )
    ys_init = jnp.zeros((nT, cap, D), BF16)
    return pl.pallas_call(
        _moe_ffn_kernel,
        grid_spec=grid_spec,
        out_shape=jax.ShapeDtypeStruct((nT, cap, D), BF16),
        input_output_aliases={len(plan) + 4: 0},
        compiler_params=_params(("arbitrary",)),
        name="moe_ffn",
    )(*plan, hs, wg, wu, wd, ys_init)


def _moe_combine_kernel(ys_ref, route_ref, x1_ref, g_ref, o_ref):
    tm = x1_ref.shape[0]
    cap = ys_ref.shape[0]
    route = route_ref[...]
    slot = lax.broadcasted_iota(jnp.int32, (tm, cap), 1)
    pick1 = jnp.where(slot == route[:, 0:1].astype(jnp.int32), 1.0, 0.0).astype(BF16)
    pick2 = jnp.where(slot == route[:, 1:2].astype(jnp.int32), 1.0, 0.0).astype(BF16)
    ys = ys_ref[...]
    y1 = jnp.dot(pick1, ys, preferred_element_type=F32)
    y2 = jnp.dot(pick2, ys, preferred_element_type=F32)
    y = x1_ref[...] + route[:, 2:3] * y1 + route[:, 3:4] * y2
    ms = jnp.mean(y * y, axis=-1, keepdims=True)
    o_ref[...] = y * lax.rsqrt(ms + RMS_EPS) * g_ref[...]


def _moe_combine(ys, route, x1, g, tm=MOE_TOKEN_TILE):
    S, D = x1.shape
    cap = ys.shape[1]
    row = lambda i: (i, 0)
    return pl.pallas_call(
        _moe_combine_kernel,
        grid=(S // tm,),
        in_specs=[
            pl.BlockSpec((None, cap, D), lambda i: (i, 0, 0)),
            pl.BlockSpec((tm, ROUTE_LANES), row),
            pl.BlockSpec((tm, D), row),
            pl.BlockSpec((1, D), lambda i: (0, 0)),
        ],
        out_specs=pl.BlockSpec((tm, D), row),
        out_shape=jax.ShapeDtypeStruct((S, D), F32),
        compiler_params=_params(("parallel",)),
        name="moe_combine",
    )(ys, route, x1, g)


def _router_weights(w_r_group, b_r_group, w_r_expert, b_r_expert):
    D = w_r_group.shape[0]
    w = jnp.concatenate(
        [w_r_group, jnp.transpose(w_r_expert, (1, 0, 2)).reshape(D, N_EXPERTS)], axis=1)
    b = jnp.concatenate([b_r_group, b_r_expert.reshape(N_EXPERTS)])
    pad = ROUTE_LANES - w.shape[1]
    w = jnp.pad(w, ((0, 0), (0, pad)))
    b = jnp.pad(b, (0, pad)).reshape(1, ROUTE_LANES)
    w_hi = w.astype(BF16)
    w_lo = (w - w_hi.astype(F32)).astype(BF16)
    return w_hi, w_lo, b


def kernel(x, norm_mix, w_in, w_pool, pool_scale, w_branch_pool, w_branch_attn, w_out, norm_ffn,
           w_r_group, b_r_group, w_r_expert, b_r_expert, w_gate, w_up, w_down, norm_final):
    B, S, D = x.shape
    depth = w_in.shape[0]
    assert depth == 1, "the final rms_norm is fused into the expert kernel of a single layer"
    slopes = jnp.exp2(-8.0 * jnp.arange(1, ATTN_HEADS + 1, dtype=F32) / ATTN_HEADS)
    slopes = jnp.broadcast_to(slopes[:, None, None], (ATTN_HEADS, 1, 128))
    outs = []
    for b in range(B):
        xb = x[b]
        for l in range(depth):
            proj = _norm_inproj(xb, norm_mix[l].reshape(1, D), w_in[l])
            y_pool = _pool_mixer(proj, w_pool[l].astype(BF16), pool_scale[l].reshape(1, POOL_WIDTH))
            qT, k_aug, vT, sel = _moba_gate(proj, slopes)
            y_attn = _moba_attention(k_aug, qT, vT, sel, slopes)
            wr_hi, wr_lo, br = _router_weights(w_r_group[l], b_r_group[l], w_r_expert[l], b_r_expert[l])
            x1, hs, route, cnt = _merge_route(
                y_pool, y_attn, proj, xb,
                w_branch_pool[l].astype(BF16), w_branch_attn[l].astype(BF16), w_out[l].astype(BF16),
                norm_ffn[l].reshape(1, D), wr_hi, wr_lo, br)
            nT = hs.shape[0]
            chunk_counts = cnt[:, 0, EXPERT_LANE0:EXPERT_LANE0 + N_EXPERTS].astype(jnp.int32)
            max_chunks = nT * ((2 * MOE_TOKEN_TILE + N_EXPERTS * (MOE_CHUNK - 1)) // MOE_CHUNK)
            n_row_tiles = -(-(max_chunks + N_EXPERTS * (MOE_CHUNKS_PER_TILE - 1))
                            // MOE_CHUNKS_PER_TILE)
            plan = _moe_plan(chunk_counts, n_row_tiles, n_row_tiles * MOE_CHUNKS_PER_TILE)
            ys = _moe_ffn(hs, plan, w_gate[l], w_up[l], w_down[l])
            xb = _moe_combine(ys, route, x1, norm_final.reshape(1, D))
        outs.append(xb)
    return jnp.stack(outs, axis=0)
```

```python
import functools

import jax
import jax.numpy as jnp
from jax import lax
from jax.experimental import pallas as pl
from jax.experimental.pallas import tpu as pltpu

F32 = jnp.float32
BF16 = jnp.bfloat16

POOL_WINDOWS = (2, 4, 8, 16)
MAX_WINDOW = 16
POOL_WIDTH = 1024
POOL_GROUP = 256
HEAD_DIM = 128
ATTN_HEADS = 8
ATTN_WIDTH = 1024
MOBA_BLOCK = 256
MOBA_TOPK = 3
N_GROUPS = 4
EXPERTS_PER_GROUP = 4
N_EXPERTS = 16
ROUTE_LANES = 128
EXPERT_LANE0 = N_GROUPS
RMS_EPS = 1e-6
NEG_INF = -1e30
LOG2_E = 1.4426950408889634
QUERY_SCALE = (HEAD_DIM ** -0.5) * LOG2_E
KEY_AUG = 2 * HEAD_DIM
ALIBI_TERMS = 3
VALUE_AUG = HEAD_DIM + 16
ATTN_KV_UNROLL = 3
MOE_CHUNK = 16
MOE_TOKEN_TILE = 256
MOE_ROW_TILE = 256
MOE_CHUNKS_PER_TILE = MOE_ROW_TILE // MOE_CHUNK
MOE_WEIGHT_PIECES = 8
MOE_WEIGHT_STAGES = 4
assert MOE_WEIGHT_STAGES - 1 <= MOE_WEIGHT_PIECES

V7X_VMEM_LIMIT_BYTES = 56 * 1024 * 1024


def _params(semantics, vmem=V7X_VMEM_LIMIT_BYTES, flags=None):
    return pltpu.CompilerParams(dimension_semantics=semantics, vmem_limit_bytes=vmem, flags=flags)


def _norm_inproj_kernel(x_ref, g_ref, w_ref, o_ref, wb_ref):
    @pl.when(pl.program_id(1) == 0)
    def _():
        wb_ref[...] = w_ref[...].astype(BF16)

    x = x_ref[...]
    ms = jnp.mean(x * x, axis=-1, keepdims=True)
    h = (x * lax.rsqrt(ms + RMS_EPS) * g_ref[...]).astype(BF16)
    col0 = pl.program_id(0) * o_ref.shape[1]
    is_q = (col0 >= POOL_WIDTH) & (col0 < POOL_WIDTH + ATTN_WIDTH)
    factor = jnp.where(is_q, QUERY_SCALE, 1.0).astype(F32)
    o_ref[...] = (jnp.dot(h, wb_ref[...], preferred_element_type=F32) * factor).astype(o_ref.dtype)


def _norm_inproj(x, g, w, tm=1024, tn=1024):
    S, D = x.shape
    N = w.shape[1]
    assert POOL_WIDTH % tn == 0 and ATTN_WIDTH % tn == 0
    return pl.pallas_call(
        _norm_inproj_kernel,
        grid=(N // tn, S // tm),
        in_specs=[
            pl.BlockSpec((tm, D), lambda j, i: (i, 0)),
            pl.BlockSpec((1, D), lambda j, i: (0, 0)),
            pl.BlockSpec((D, tn), lambda j, i: (0, j)),
        ],
        out_specs=pl.BlockSpec((tm, tn), lambda j, i: (i, j)),
        out_shape=jax.ShapeDtypeStruct((S, N), BF16),
        scratch_shapes=[pltpu.VMEM((D, tn), BF16)],
        compiler_params=_params(("parallel", "arbitrary")),
        name="norm_inproj",
    )(x, g, w)


def _pool_kernel(cur_ref, prev_ref, w_ref, scale_ref, o_ref, ext_ref):
    i = pl.program_id(0)
    tm = cur_ref.shape[0]
    u = cur_ref[...].astype(F32)
    halo = jnp.where(i > 0, prev_ref[...].astype(F32), 0.0)
    ext_ref[0:MAX_WINDOW, :] = halo
    ext_ref[MAX_WINDOW:MAX_WINDOW + tm, :] = u
    t = i * tm + lax.broadcasted_iota(jnp.int32, (tm, 1), 0)
    for g, w in enumerate(POOL_WINDOWS):
        cols = slice(g * POOL_GROUP, (g + 1) * POOL_GROUP)
        ug = u[:, cols]
        wsum = ug
        for s in range(1, w):
            wsum = wsum + ext_ref[MAX_WINDOW - s:MAX_WINDOW - s + tm, cols]
        cnt = jnp.minimum(t + 1, w).astype(F32)
        mixed = (wsum / cnt - ug).astype(BF16)
        y = jnp.dot(mixed, w_ref[g], preferred_element_type=F32)
        o_ref[:, cols] = (y * scale_ref[:, cols]).astype(o_ref.dtype)


def _pool_mixer(proj, w_pool, pool_scale, tm=512):
    S = proj.shape[0]
    halo_blocks = tm // MAX_WINDOW
    return pl.pallas_call(
        _pool_kernel,
        grid=(S // tm,),
        in_specs=[
            pl.BlockSpec((tm, POOL_WIDTH), lambda i: (i, 0)),
            pl.BlockSpec((MAX_WINDOW, POOL_WIDTH),
                         lambda i: (jnp.maximum(i * halo_blocks - 1, 0), 0)),
            pl.BlockSpec((len(POOL_WINDOWS), POOL_GROUP, POOL_GROUP), lambda i: (0, 0, 0)),
            pl.BlockSpec((1, POOL_WIDTH), lambda i: (0, 0)),
        ],
        out_specs=pl.BlockSpec((tm, POOL_WIDTH), lambda i: (i, 0)),
        out_shape=jax.ShapeDtypeStruct((S, POOL_WIDTH), BF16),
        scratch_shapes=[pltpu.VMEM((tm + MAX_WINDOW, POOL_WIDTH), F32)],
        compiler_params=_params(("parallel",)),
        name="pool_mixer",
    )(proj, proj, w_pool, pool_scale)


def _moba_gate_kernel(slope_ref, q_ref, k_ref, v_ref, qT_ref, ka_ref, vT_ref, sel_ref):
    S = q_ref.shape[0]
    nb = S // MOBA_BLOCK
    topk = min(MOBA_TOPK, nb)
    kf = k_ref[...].astype(F32).reshape(nb, MOBA_BLOCK, HEAD_DIM)
    kmean = jnp.sum(kf, axis=1) * (1.0 / MOBA_BLOCK)
    km_hi = kmean.astype(BF16)
    km_lo = (kmean - km_hi.astype(F32)).astype(BF16)
    blk = lax.broadcasted_iota(jnp.int32, (nb, MOBA_BLOCK), 0)

    pos = lax.broadcasted_iota(jnp.int32, (MOBA_BLOCK, KEY_AUG - HEAD_DIM), 0).astype(F32)
    col = lax.broadcasted_iota(jnp.int32, (MOBA_BLOCK, KEY_AUG - HEAD_DIM), 1)
    k_extra = jnp.where(col < ALIBI_TERMS, pos, 0.0).astype(BF16)

    slope2 = slope_ref[...][:, 0:1] * LOG2_E
    q_extra = jnp.zeros((KEY_AUG - HEAD_DIM, MOBA_BLOCK), F32)
    row = lax.broadcasted_iota(jnp.int32, q_extra.shape, 0)
    rest = slope2
    for n in range(ALIBI_TERMS):
        piece = rest.astype(BF16).astype(F32)
        q_extra = jnp.where(row == n, piece, q_extra)
        rest = rest - piece
    q_extra = q_extra.astype(BF16)
    v_row = lax.broadcasted_iota(jnp.int32, (VALUE_AUG - HEAD_DIM, MOBA_BLOCK), 0)
    v_extra = jnp.where(v_row == 0, 1.0, 0.0).astype(BF16)

    def body(i, carry):
        rows = pl.ds(pl.multiple_of(i * MOBA_BLOCK, MOBA_BLOCK), MOBA_BLOCK)
        qT = q_ref[rows, :].astype(F32).T.astype(BF16)
        qT_ref[i, 0:HEAD_DIM, :] = qT
        qT_ref[i, HEAD_DIM:KEY_AUG, :] = q_extra
        ka_ref[rows, 0:HEAD_DIM] = k_ref[rows, :]
        ka_ref[rows, HEAD_DIM:KEY_AUG] = k_extra
        vT_ref[i, 0:HEAD_DIM, :] = v_ref[rows, :].astype(F32).T.astype(BF16)
        vT_ref[i, HEAD_DIM:VALUE_AUG, :] = v_extra
        gate = (jnp.dot(km_hi, qT, preferred_element_type=F32)
                + jnp.dot(km_lo, qT, preferred_element_type=F32)) * (1.0 / QUERY_SCALE)
        gate = jnp.where(blk < i, gate, NEG_INF)
        sel = jnp.zeros((nb, MOBA_BLOCK), F32)
        for _ in range(topk):
            best = jnp.max(gate, axis=0, keepdims=True)
            idx = jnp.min(jnp.where(gate == best, blk, nb), axis=0, keepdims=True)
            hit = blk == idx
            sel = jnp.where(hit & (blk < i), 1.0, sel)
            gate = jnp.where(hit, -jnp.inf, gate)
        sel_ref[i] = sel
        return carry

    lax.fori_loop(0, nb, body, 0, unroll=4 if nb % 4 == 0 else 1)


def _moba_gate(proj, slopes):
    S = proj.shape[0]
    nb = S // MOBA_BLOCK
    H = ATTN_HEADS
    q0 = POOL_WIDTH // HEAD_DIM
    k0 = q0 + H
    v0 = k0 + H
    blocked = lambda h: (h, 0, 0, 0)
    return pl.pallas_call(
        _moba_gate_kernel,
        grid=(H,),
        in_specs=[
            pl.BlockSpec((None, 1, 128), lambda h: (h, 0, 0)),
            pl.BlockSpec((S, HEAD_DIM), lambda h: (0, q0 + h)),
            pl.BlockSpec((S, HEAD_DIM), lambda h: (0, k0 + h)),
            pl.BlockSpec((S, HEAD_DIM), lambda h: (0, v0 + h)),
        ],
        out_specs=[
            pl.BlockSpec((None, nb, KEY_AUG, MOBA_BLOCK), blocked),
            pl.BlockSpec((None, S, KEY_AUG), lambda h: (h, 0, 0)),
            pl.BlockSpec((None, nb, VALUE_AUG, MOBA_BLOCK), blocked),
            pl.BlockSpec((None, nb, nb, MOBA_BLOCK), blocked),
        ],
        out_shape=[
            jax.ShapeDtypeStruct((H, nb, KEY_AUG, MOBA_BLOCK), BF16),
            jax.ShapeDtypeStruct((H, S, KEY_AUG), BF16),
            jax.ShapeDtypeStruct((H, nb, VALUE_AUG, MOBA_BLOCK), BF16),
            jax.ShapeDtypeStruct((H, nb, nb, MOBA_BLOCK), F32),
        ],
        compiler_params=_params(("parallel",)),
        name="moba_gate",
    )(slopes, proj, proj, proj)


def _moba_attn_kernel(slope_ref, qT_ref, sel_ref, k_ref, vT_ref, o_ref,
                      s0_ref, s1_ref, p0_ref, p1_ref):
    i = pl.program_id(1)
    nb = sel_ref.shape[0]
    slope2 = slope_ref[...][:, 0:1] * LOG2_E
    qT = qT_ref[...]

    def block_of(t, u):
        return jnp.clip(t * ATTN_KV_UNROLL + u, 0, nb - 1)

    def issue_scores(t, s_ref):
        for u in range(ATTN_KV_UNROLL):
            rows = pl.ds(pl.multiple_of(block_of(t, u) * MOBA_BLOCK, MOBA_BLOCK), MOBA_BLOCK)
            s_ref[u] = jnp.dot(k_ref[rows, :], qT, preferred_element_type=F32)

    def apply_probs(t, p_ref, alpha, acc):
        acc = alpha * acc
        for u in range(ATTN_KV_UNROLL):
            acc = acc + jnp.dot(vT_ref[block_of(t, u)], p_ref[u], preferred_element_type=F32)
        return acc

    def softmax_group(t, s_ref, p_ref, m):
        m_new = m
        shifts = []
        for u in range(ATTN_KV_UNROLL):
            j = t * ATTN_KV_UNROLL + u
            valid = jnp.where(j < i, sel_ref[pl.ds(block_of(t, u), 1), :], 0.0) > 0.0
            gap = slope2 * ((i - j) * MOBA_BLOCK).astype(F32)
            top = jnp.max(s_ref[u], axis=0, keepdims=True) - gap
            m_new = jnp.maximum(m_new, jnp.where(valid, top, NEG_INF))
            shifts.append((valid, gap))
        alpha = jnp.exp2(m - m_new)
        for u, (valid, gap) in enumerate(shifts):
            p = jnp.exp2(s_ref[u] - jnp.where(valid, m_new + gap, jnp.inf))
            p_ref[u] = p.astype(BF16)
        return m_new, alpha

    def body(r, carry):
        m, acc, alpha = carry
        acc = apply_probs(2 * r - 1, p1_ref, alpha, acc)
        m, alpha = softmax_group(2 * r, s0_ref, p0_ref, m)
        issue_scores(2 * r + 1, s1_ref)
        acc = apply_probs(2 * r, p0_ref, alpha, acc)
        m, alpha = softmax_group(2 * r + 1, s1_ref, p1_ref, m)
        issue_scores(2 * r + 2, s0_ref)
        return m, acc, alpha

    n_groups = (i + ATTN_KV_UNROLL - 1) // ATTN_KV_UNROLL
    n_pairs = n_groups // 2
    p1_ref[...] = jnp.zeros(p1_ref.shape, BF16)
    issue_scores(0, s0_ref)
    own_rows = pl.ds(pl.multiple_of(i * MOBA_BLOCK, MOBA_BLOCK), MOBA_BLOCK)
    s_own = jnp.dot(k_ref[own_rows, :], qT, preferred_element_type=F32)
    kpos = lax.broadcasted_iota(jnp.int32, (MOBA_BLOCK, MOBA_BLOCK), 0)
    qpos = lax.broadcasted_iota(jnp.int32, (MOBA_BLOCK, MOBA_BLOCK), 1)
    s_own = jnp.where(qpos >= kpos, s_own, NEG_INF)
    top_own = jnp.max(s_own, axis=0, keepdims=True)
    row = jnp.zeros((1, MOBA_BLOCK), F32)
    init = (row + NEG_INF, jnp.zeros((VALUE_AUG, MOBA_BLOCK), F32), row + 1.0)
    m, acc, alpha = lax.fori_loop(0, n_pairs, body, init)
    acc = apply_probs(2 * n_pairs - 1, p1_ref, alpha, acc)
    m, alpha = softmax_group(2 * n_pairs, s0_ref, p0_ref, m)
    acc = apply_probs(2 * n_pairs, p0_ref, alpha, acc)
    m_new = jnp.maximum(m, top_own)
    alpha = jnp.exp2(m - m_new)
    p = jnp.exp2(s_own - m_new)
    acc = alpha * acc + jnp.dot(vT_ref[i], p.astype(BF16), preferred_element_type=F32)
    out = acc[0:HEAD_DIM] / acc[HEAD_DIM:HEAD_DIM + 1]
    o_ref[...] = out.T.astype(o_ref.dtype)


def _moba_attention(k_aug, qT, vT, sel, slopes):
    H, S, _ = k_aug.shape
    nb = S // MOBA_BLOCK
    return pl.pallas_call(
        _moba_attn_kernel,
        grid=(H, nb),
        in_specs=[
            pl.BlockSpec((None, 1, 128), lambda h, i: (h, 0, 0)),
            pl.BlockSpec((None, None, KEY_AUG, MOBA_BLOCK), lambda h, i: (h, i, 0, 0)),
            pl.BlockSpec((None, None, nb, MOBA_BLOCK), lambda h, i: (h, i, 0, 0)),
            pl.BlockSpec((None, S, KEY_AUG), lambda h, i: (h, 0, 0)),
            pl.BlockSpec((None, nb, VALUE_AUG, MOBA_BLOCK), lambda h, i: (h, 0, 0, 0)),
        ],
        out_specs=pl.BlockSpec((MOBA_BLOCK, HEAD_DIM), lambda h, i: (i, h)),
        out_shape=jax.ShapeDtypeStruct((S, ATTN_WIDTH), BF16),
        scratch_shapes=[
            pltpu.VMEM((ATTN_KV_UNROLL, MOBA_BLOCK, MOBA_BLOCK), F32),
            pltpu.VMEM((ATTN_KV_UNROLL, MOBA_BLOCK, MOBA_BLOCK), F32),
            pltpu.VMEM((ATTN_KV_UNROLL, MOBA_BLOCK, MOBA_BLOCK), BF16),
            pltpu.VMEM((ATTN_KV_UNROLL, MOBA_BLOCK, MOBA_BLOCK), BF16),
        ],
        compiler_params=_params(("parallel", "arbitrary")),
        name="moba_attention",
    )(slopes, qT, sel, k_aug, vT)


def _first_lane_of_max(vals, lane):
    best = jnp.max(vals, axis=1, keepdims=True)
    idx = jnp.min(jnp.where(vals == best, lane, ROUTE_LANES), axis=1, keepdims=True)
    return best, idx


def _merge_route_kernel(yp_ref, ya_ref, glp_ref, gla_ref, x_ref, wbp_ref, wba_ref, wout_ref,
                        g_ref, wr_hi_ref, wr_lo_ref, br_ref, x1_ref, hs_ref, route_ref, cnt_ref):
    tm = x_ref.shape[0]
    cap = hs_ref.shape[0]
    bp = jnp.dot(yp_ref[...], wbp_ref[...], preferred_element_type=F32)
    ba = jnp.dot(ya_ref[...], wba_ref[...], preferred_element_type=F32)
    merged = (jax.nn.sigmoid(glp_ref[...].astype(F32)) * bp
              + jax.nn.sigmoid(gla_ref[...].astype(F32)) * ba)
    x1 = x_ref[...] + jnp.dot(merged.astype(BF16), wout_ref[...], preferred_element_type=F32)
    x1_ref[...] = x1
    ms = jnp.mean(x1 * x1, axis=-1, keepdims=True)
    h2 = x1 * lax.rsqrt(ms + RMS_EPS) * g_ref[...]
    h2_hi = h2.astype(BF16)
    h2_lo = (h2 - h2_hi.astype(F32)).astype(BF16)
    logits = (jnp.dot(h2_hi, wr_hi_ref[...], preferred_element_type=F32)
              + jnp.dot(h2_lo, wr_hi_ref[...], preferred_element_type=F32)
              + jnp.dot(h2_hi, wr_lo_ref[...], preferred_element_type=F32)
              + br_ref[...])

    lane = lax.broadcasted_iota(jnp.int32, logits.shape, 1)
    g_logits = jnp.where(lane < N_GROUPS, logits, -jnp.inf)
    g_best, g_idx = _first_lane_of_max(g_logits, lane)
    g_w = 1.0 / jnp.sum(jnp.exp(g_logits - g_best), axis=1, keepdims=True)
    e_lo = EXPERT_LANE0 + EXPERTS_PER_GROUP * g_idx
    e_logits = jnp.where((lane >= e_lo) & (lane < e_lo + EXPERTS_PER_GROUP), logits, -jnp.inf)
    v1, i1 = _first_lane_of_max(e_logits, lane)
    v2, i2 = _first_lane_of_max(jnp.where(lane == i1, -jnp.inf, e_logits), lane)
    e21 = jnp.exp(v2 - v1)
    w1 = g_w / (1.0 + e21)
    w2 = g_w * e21 / (1.0 + e21)

    hit1 = lane == i1
    hit2 = lane == i2
    member = jnp.where(hit1, 1.0, jnp.where(hit2, 1.0, 0.0))
    r_tok = lax.broadcasted_iota(jnp.int32, (tm, tm), 0)
    c_tok = lax.broadcasted_iota(jnp.int32, (tm, tm), 1)
    earlier = jnp.where(c_tok < r_tok, 1.0, 0.0).astype(BF16)
    rank = jnp.dot(earlier, member.astype(BF16), preferred_element_type=F32)
    count = jnp.sum(member, axis=0, keepdims=True)
    chunks = jnp.floor((count + (MOE_CHUNK - 1)) * (1.0 / MOE_CHUNK))
    r_l = lax.broadcasted_iota(jnp.int32, (ROUTE_LANES, ROUTE_LANES), 0)
    c_l = lax.broadcasted_iota(jnp.int32, (ROUTE_LANES, ROUTE_LANES), 1)
    lower_lanes = jnp.where(r_l < c_l, 1.0, 0.0).astype(BF16)
    start = jnp.dot(jnp.broadcast_to(chunks, (8, ROUTE_LANES)).astype(BF16), lower_lanes,
                    preferred_element_type=F32)[0:1] * MOE_CHUNK
    pos = start + rank
    pos1 = jnp.sum(jnp.where(hit1, pos, 0.0), axis=1, keepdims=True)
    pos2 = jnp.sum(jnp.where(hit2, pos, 0.0), axis=1, keepdims=True)
    route = jnp.where(lane == 0, pos1, jnp.where(lane == 1, pos2,
                      jnp.where(lane == 2, w1, jnp.where(lane == 3, w2, 0.0))))
    route_ref[...] = route
    cnt_ref[...] = chunks
    route_t = route.T
    slot = lax.broadcasted_iota(jnp.int32, (cap, tm), 0)
    p1_row = route_t[0:1, :].astype(jnp.int32)
    p2_row = route_t[1:2, :].astype(jnp.int32)
    onehot = jnp.where(slot == p1_row, 1.0, jnp.where(slot == p2_row, 1.0, 0.0)).astype(BF16)
    hs_ref[...] = jnp.dot(onehot, h2_hi, preferred_element_type=F32).astype(BF16)


def _moe_cap(tm):
    worst = 2 * tm + N_EXPERTS * (MOE_CHUNK - 1)
    return -(-worst // 128) * 128


def _merge_route(yp, ya, proj, x, wbp, wba, wout, g, wr_hi, wr_lo, br, tm=MOE_TOKEN_TILE):
    S, D = x.shape
    nT = S // tm
    cap = _moe_cap(tm)
    full = lambda i: (0, 0)
    row = lambda i: (i, 0)
    glp_blk = (POOL_WIDTH + 3 * ATTN_WIDTH) // D
    return pl.pallas_call(
        _merge_route_kernel,
        grid=(S // tm,),
        in_specs=[
            pl.BlockSpec((tm, POOL_WIDTH), row),
            pl.BlockSpec((tm, ATTN_WIDTH), row),
            pl.BlockSpec((tm, D), lambda i: (i, glp_blk)),
            pl.BlockSpec((tm, D), lambda i: (i, glp_blk + 1)),
            pl.BlockSpec((tm, D), row),
            pl.BlockSpec((POOL_WIDTH, D), full),
            pl.BlockSpec((ATTN_WIDTH, D), full),
            pl.BlockSpec((D, D), full),
            pl.BlockSpec((1, D), full),
            pl.BlockSpec((D, ROUTE_LANES), full),
            pl.BlockSpec((D, ROUTE_LANES), full),
            pl.BlockSpec((1, ROUTE_LANES), full),
        ],
        out_specs=[
            pl.BlockSpec((tm, D), row),
            pl.BlockSpec((None, cap, D), lambda i: (i, 0, 0)),
            pl.BlockSpec((tm, ROUTE_LANES), row),
            pl.BlockSpec((None, 1, ROUTE_LANES), lambda i: (i, 0, 0)),
        ],
        out_shape=[
            jax.ShapeDtypeStruct((S, D), F32),
            jax.ShapeDtypeStruct((nT, cap, D), BF16),
            jax.ShapeDtypeStruct((S, ROUTE_LANES), F32),
            jax.ShapeDtypeStruct((nT, 1, ROUTE_LANES), F32),
        ],
        compiler_params=_params(("parallel",)),
        name="merge_route",
    )(yp, ya, proj, proj, x, wbp, wba, wout, g, wr_hi, wr_lo, br)


def _moe_plan(chunk_counts, n_row_tiles, n_chunk_slots):
    nT, E = chunk_counts.shape
    per_expert = chunk_counts.T
    seg_start = (jnp.cumsum(chunk_counts, axis=1) - chunk_counts).T
    seg_end = jnp.cumsum(per_expert, axis=1)
    n_chunks = seg_end[:, -1]
    padded = -(-n_chunks // MOE_CHUNKS_PER_TILE) * MOE_CHUNKS_PER_TILE
    e_end = jnp.cumsum(padded)
    n_used = e_end[-1] // MOE_CHUNKS_PER_TILE
    c = jnp.arange(n_chunk_slots, dtype=jnp.int32)
    e_of_c = jnp.minimum((e_end[None, :] <= c[:, None]).sum(axis=1), E - 1)
    is_e = (e_of_c[:, None] == jnp.arange(E)[None, :]).astype(jnp.int32)
    local = c - (is_e * (e_end - padded)[None, :]).sum(axis=1)
    real = local < (is_e * n_chunks[None, :]).sum(axis=1)
    pick_e = lambda table: (is_e[:, :, None] * table[None, :, :]).sum(axis=1)
    seg_end_c = pick_e(seg_end)
    t_of_c = jnp.minimum((seg_end_c <= local[:, None]).sum(axis=1), nT - 1)
    is_t = (t_of_c[:, None] == jnp.arange(nT)[None, :]).astype(jnp.int32)
    pick_t = lambda rows: (rows * is_t).sum(axis=1)
    within = local - (pick_t(seg_end_c) - pick_t(pick_e(per_expert)))
    src_tile = jnp.where(real, t_of_c, 0).astype(jnp.int32)
    src_row = jnp.where(real, (pick_t(pick_e(seg_start)) + within) * MOE_CHUNK, 0).astype(jnp.int32)
    tile = jnp.arange(n_row_tiles, dtype=jnp.int32)
    first = jnp.minimum(tile, n_used - 1) * MOE_CHUNKS_PER_TILE
    tile_expert = jnp.minimum((e_end[None, :] <= first[:, None]).sum(axis=1), E - 1)
    tile_real = real.reshape(n_row_tiles, MOE_CHUNKS_PER_TILE).sum(axis=1)
    experts = jnp.arange(E)
    nonempty = padded > 0
    buffer_of_e = jnp.cumsum(nonempty) - nonempty
    later = (experts[None, :] > experts[:, None]) & nonempty[None, :]
    next_of_e = jnp.min(jnp.where(later, experts[None, :], E), axis=1)
    next_of_e = jnp.where(next_of_e < E, next_of_e, -1)
    is_next = (next_of_e[:, None] == experts[None, :]).astype(jnp.int32)
    after_of_e = jnp.where(next_of_e >= 0, (is_next * next_of_e[None, :]).sum(axis=1), -1)
    is_te = (tile_expert[:, None] == experts[None, :]).astype(jnp.int32)
    at = lambda table: (is_te * table[None, :]).sum(axis=1)
    run_len = jnp.maximum(at(padded) // MOE_CHUNKS_PER_TILE, 1)
    run_pos = tile - at(e_end - padded) // MOE_CHUNKS_PER_TILE
    next_expert = at(next_of_e)
    streams = (tile < n_used) & (next_expert >= 0)
    piece_lo = jnp.where(streams, (MOE_WEIGHT_PIECES * run_pos) // run_len, 0)
    piece_hi = jnp.where(streams, (MOE_WEIGHT_PIECES * (run_pos + 1)) // run_len, 0)
    as_i32 = lambda a: a.astype(jnp.int32)
    return (as_i32(tile_expert), as_i32(tile_real), src_tile, src_row, as_i32(n_used.reshape(1)),
            as_i32(at(buffer_of_e) % 2), as_i32(next_expert), as_i32(at(after_of_e)),
            as_i32(piece_lo), as_i32(piece_hi), as_i32(chunk_counts.sum(axis=1)))


def _moe_ffn_kernel(texp_ref, treal_ref, ctile_ref, crow_ref, nused_ref,
                    wbuf_ref, wnext_ref, wafter_ref, plo_ref, phi_ref, tused_ref,
                    hs_hbm, wg_hbm, wu_hbm, wd_hbm, ys_hbm,
                    xbuf, ybuf, zbuf, wg_buf, wu_buf, wd_buf, stage_g, stage_u, stage_d,
                    gather_sem, scatter_sem, weight_sem, zero_sem):
    i = pl.program_id(0)
    n_used = nused_ref[0]
    slot = lax.rem(i, 2)
    rows_gu = stage_g.shape[1]
    rows_d = stage_d.shape[1]

    def piece_copies(e, p, s):
        gu_rows = pl.ds(pl.multiple_of(p * rows_gu, rows_gu), rows_gu)
        d_rows = pl.ds(pl.multiple_of(p * rows_d, rows_d), rows_d)
        return (pltpu.make_async_copy(wg_hbm.at[e, gu_rows, :], stage_g.at[s], weight_sem.at[s]),
                pltpu.make_async_copy(wu_hbm.at[e, gu_rows, :], stage_u.at[s], weight_sem.at[s]),
                pltpu.make_async_copy(wd_hbm.at[e, d_rows, :], stage_d.at[s], weight_sem.at[s]))

    def start_piece(e, p):
        for copy in piece_copies(e, p, lax.rem(p, MOE_WEIGHT_STAGES)):
            copy.start()

    def finish_piece(e, p, side):
        s = lax.rem(p, MOE_WEIGHT_STAGES)
        for copy in piece_copies(e, p, s):
            copy.wait()
        gu_rows = pl.ds(pl.multiple_of(p * rows_gu, rows_gu), rows_gu)
        d_rows = pl.ds(pl.multiple_of(p * rows_d, rows_d), rows_d)
        wg_buf[side, gu_rows, :] = stage_g[s].astype(BF16)
        wu_buf[side, gu_rows, :] = stage_u[s].astype(BF16)
        wd_buf[side, d_rows, :] = stage_d[s].astype(BF16)

    ahead = MOE_WEIGHT_STAGES - 1

    def start_first_pieces(e):
        for p in range(ahead):
            start_piece(e, p)

    def stream_pieces(e, lo, hi, side, following):
        @pl.when(hi > lo)
        def _():
            def body(p, carry):
                @pl.when(p + ahead < MOE_WEIGHT_PIECES)
                def _():
                    start_piece(e, p + ahead)
                finish_piece(e, p, side)
                return carry
            lax.fori_loop(lo, hi, body, 0)

            @pl.when((hi == MOE_WEIGHT_PIECES) & (following >= 0))
            def _():
                start_first_pieces(following)

    def chunk_rows(c):
        return pl.ds(pl.multiple_of(c * MOE_CHUNK, MOE_CHUNK), MOE_CHUNK)

    def gather_copy(tile, c, buf):
        g = tile * MOE_CHUNKS_PER_TILE + c
        src = hs_hbm.at[ctile_ref[g], pl.ds(pl.multiple_of(crow_ref[g], MOE_CHUNK), MOE_CHUNK), :]
        return pltpu.make_async_copy(src, xbuf.at[buf, chunk_rows(c), :], gather_sem.at[buf])

    def scatter_copy(tile, c, buf):
        g = tile * MOE_CHUNKS_PER_TILE + c
        dst = ys_hbm.at[ctile_ref[g], pl.ds(pl.multiple_of(crow_ref[g], MOE_CHUNK), MOE_CHUNK), :]
        return pltpu.make_async_copy(ybuf.at[buf, chunk_rows(c), :], dst, scatter_sem.at[buf])

    def for_real_chunks(tile, fn):
        def body(c, carry):
            fn(c)
            return carry
        lax.fori_loop(0, treal_ref[tile], body, 0)

    def for_tail_chunks(fn):
        n_token_tiles, cap = ys_hbm.shape[0], ys_hbm.shape[1]

        def per_tile(t, carry):
            def per_chunk(c, carry):
                fn(pltpu.make_async_copy(zbuf, ys_hbm.at[t, chunk_rows(c), :], zero_sem.at[0]))
                return carry
            return lax.fori_loop(tused_ref[t], cap // MOE_CHUNK, per_chunk, carry)
        lax.fori_loop(0, n_token_tiles, per_tile, 0)

    @pl.when(i == 0)
    def _():
        zbuf[...] = jnp.zeros(zbuf.shape, zbuf.dtype)
        for_tail_chunks(lambda copy: copy.start())
        xbuf[...] = jnp.zeros(xbuf.shape, xbuf.dtype)
        for_real_chunks(0, lambda c: gather_copy(0, c, 0).start())
        start_first_pieces(texp_ref[0])
        stream_pieces(texp_ref[0], 0, MOE_WEIGHT_PIECES, wbuf_ref[0], wnext_ref[0])

    @pl.when(i + 1 < n_used)
    def _():
        for_real_chunks(i + 1, lambda c: gather_copy(i + 1, c, 1 - slot).start())

    side = wbuf_ref[i]
    stream_pieces(wnext_ref[i], plo_ref[i], phi_ref[i], 1 - side, wafter_ref[i])

    @pl.when(i < n_used)
    def _():
        for_real_chunks(i, lambda c: gather_copy(i, c, slot).wait())

        @pl.when(i >= 2)
        def _():
            for_real_chunks(i - 2, lambda c: scatter_copy(i - 2, c, slot).wait())

        x = xbuf[slot]
        a = jnp.dot(x, wg_buf[side], preferred_element_type=F32)
        u = jnp.dot(x, wu_buf[side], preferred_element_type=F32)
        hid = (jax.nn.silu(a) * u).astype(BF16)
        ybuf[slot] = jnp.dot(hid, wd_buf[side], preferred_element_type=F32).astype(BF16)
        for_real_chunks(i, lambda c: scatter_copy(i, c, slot).start())

    @pl.when(i == n_used - 1)
    def _():
        @pl.when(i >= 1)
        def _():
            for_real_chunks(i - 1, lambda c: scatter_copy(i - 1, c, 1 - slot).wait())
        for_real_chunks(i, lambda c: scatter_copy(i, c, slot).wait())
        for_tail_chunks(lambda copy: copy.wait())


def _moe_ffn(hs, plan, wg, wu, wd):
    nT, cap, D = hs.shape
    E, _, F = wg.shape
    n_row_tiles = plan[0].shape[0]
    assert D % MOE_WEIGHT_PIECES == 0 and F % MOE_WEIGHT_PIECES == 0
    grid_spec = pltpu.PrefetchScalarGridSpec(
        num_scalar_prefetch=len(plan),
        grid=(n_row_tiles,),
        in_specs=[pl.BlockSpec(memory_space=pl.ANY)] * 4,
        out_specs=pl.BlockSpec(memory_space=pl.ANY),
        scratch_shapes=[
            pltpu.VMEM((2, MOE_ROW_TILE, D), BF16),
            pltpu.VMEM((2, MOE_ROW_TILE, D), BF16),
            pltpu.VMEM((MOE_CHUNK, D), BF16),
            pltpu.VMEM((2, D, F), BF16),
            pltpu.VMEM((2, D, F), BF16),
            pltpu.VMEM((2, F, D), BF16),
            pltpu.VMEM((MOE_WEIGHT_STAGES, D // MOE_WEIGHT_PIECES, F), F32),
            pltpu.VMEM((MOE_WEIGHT_STAGES, D // MOE_WEIGHT_PIECES, F), F32),
            pltpu.VMEM((MOE_WEIGHT_STAGES, F // MOE_WEIGHT_PIECES, D), F32),
            pltpu.SemaphoreType.DMA((2,)),
            pltpu.SemaphoreType.DMA((2,)),
            pltpu.SemaphoreType.DMA((MOE_WEIGHT_STAGES,)),
            pltpu.SemaphoreType.DMA((1,)),
        ],
    )
    return pl.pallas_call(
        _moe_ffn_kernel,
        grid_spec=grid_spec,
        out_shape=jax.ShapeDtypeStruct((nT, cap, D), BF16),
        compiler_params=_params(("arbitrary",)),
        name="moe_ffn",
    )(*plan, hs, wg, wu, wd)


def _moe_combine_kernel(ys_ref, route_ref, x1_ref, g_ref, o_ref):
    tm = x1_ref.shape[0]
    cap = ys_ref.shape[0]
    route = route_ref[...]
    slot = lax.broadcasted_iota(jnp.int32, (tm, cap), 1)
    pick1 = jnp.where(slot == route[:, 0:1].astype(jnp.int32), 1.0, 0.0).astype(BF16)
    pick2 = jnp.where(slot == route[:, 1:2].astype(jnp.int32), 1.0, 0.0).astype(BF16)
    ys = ys_ref[...]
    y1 = jnp.dot(pick1, ys, preferred_element_type=F32)
    y2 = jnp.dot(pick2, ys, preferred_element_type=F32)
    y = x1_ref[...] + route[:, 2:3] * y1 + route[:, 3:4] * y2
    ms = jnp.mean(y * y, axis=-1, keepdims=True)
    o_ref[...] = y * lax.rsqrt(ms + RMS_EPS) * g_ref[...]


def _moe_combine(ys, route, x1, g, tm=MOE_TOKEN_TILE):
    S, D = x1.shape
    cap = ys.shape[1]
    row = lambda i: (i, 0)
    return pl.pallas_call(
        _moe_combine_kernel,
        grid=(S // tm,),
        in_specs=[
            pl.BlockSpec((None, cap, D), lambda i: (i, 0, 0)),
            pl.BlockSpec((tm, ROUTE_LANES), row),
            pl.BlockSpec((tm, D), row),
            pl.BlockSpec((1, D), lambda i: (0, 0)),
        ],
        out_specs=pl.BlockSpec((tm, D), row),
        out_shape=jax.ShapeDtypeStruct((S, D), F32),
        compiler_params=_params(("parallel",)),
        name="moe_combine",
    )(ys, route, x1, g)


def _router_weights(w_r_group, b_r_group, w_r_expert, b_r_expert):
    D = w_r_group.shape[0]
    w = jnp.concatenate(
        [w_r_group, jnp.transpose(w_r_expert, (1, 0, 2)).reshape(D, N_EXPERTS)], axis=1)
    b = jnp.concatenate([b_r_group, b_r_expert.reshape(N_EXPERTS)])
    pad = ROUTE_LANES - w.shape[1]
    w = jnp.pad(w, ((0, 0), (0, pad)))
    b = jnp.pad(b, (0, pad)).reshape(1, ROUTE_LANES)
    w_hi = w.astype(BF16)
    w_lo = (w - w_hi.astype(F32)).astype(BF16)
    return w_hi, w_lo, b


def kernel(x, norm_mix, w_in, w_pool, pool_scale, w_branch_pool, w_branch_attn, w_out, norm_ffn,
           w_r_group, b_r_group, w_r_expert, b_r_expert, w_gate, w_up, w_down, norm_final):
    B, S, D = x.shape
    depth = w_in.shape[0]
    assert depth == 1, "the final rms_norm is fused into the expert kernel of a single layer"
    slopes = jnp.exp2(-8.0 * jnp.arange(1, ATTN_HEADS + 1, dtype=F32) / ATTN_HEADS)
    slopes = jnp.broadcast_to(slopes[:, None, None], (ATTN_HEADS, 1, 128))
    outs = []
    for b in range(B):
        xb = x[b]
        for l in range(depth):
            proj = _norm_inproj(xb, norm_mix[l].reshape(1, D), w_in[l])
            y_pool = _pool_mixer(proj, w_pool[l].astype(BF16), pool_scale[l].reshape(1, POOL_WIDTH))
            qT, k_aug, vT, sel = _moba_gate(proj, slopes)
            y_attn = _moba_attention(k_aug, qT, vT, sel, slopes)
            wr_hi, wr_lo, br = _router_weights(w_r_group[l], b_r_group[l], w_r_expert[l], b_r_expert[l])
            x1, hs, route, cnt = _merge_route(
                y_pool, y_attn, proj, xb,
                w_branch_pool[l].astype(BF16), w_branch_attn[l].astype(BF16), w_out[l].astype(BF16),
                norm_ffn[l].reshape(1, D), wr_hi, wr_lo, br)
            nT = hs.shape[0]
            chunk_counts = cnt[:, 0, EXPERT_LANE0:EXPERT_LANE0 + N_EXPERTS].astype(jnp.int32)
            max_chunks = nT * ((2 * MOE_TOKEN_TILE + N_EXPERTS * (MOE_CHUNK - 1)) // MOE_CHUNK)
            n_row_tiles = -(-(max_chunks + N_EXPERTS * (MOE_CHUNKS_PER_TILE - 1))
                            // MOE_CHUNKS_PER_TILE)
            plan = _moe_plan(chunk_counts, n_row_tiles, n_row_tiles * MOE_CHUNKS_PER_TILE)
            ys = _moe_ffn(hs, plan, w_gate[l], w_up[l], w_down[l])
            xb = _moe_combine(ys, route, x1, norm_final.reshape(1, D))
        outs.append(xb)
    return jnp.stack(outs, axis=0)
```

```python
import functools

import jax
import jax.numpy as jnp
from jax import lax
from jax.experimental import pallas as pl
from jax.experimental.pallas import tpu as pltpu

F32 = jnp.float32
BF16 = jnp.bfloat16

POOL_WINDOWS = (2, 4, 8, 16)
MAX_WINDOW = 16
POOL_WIDTH = 1024
POOL_GROUP = 256
HEAD_DIM = 128
ATTN_HEADS = 8
ATTN_WIDTH = 1024
MOBA_BLOCK = 256
MOBA_TOPK = 3
N_GROUPS = 4
EXPERTS_PER_GROUP = 4
N_EXPERTS = 16
ROUTE_LANES = 128
EXPERT_LANE0 = N_GROUPS
RMS_EPS = 1e-6
NEG_INF = -1e30
LOG2_E = 1.4426950408889634
QUERY_SCALE = (HEAD_DIM ** -0.5) * LOG2_E
KEY_AUG = 2 * HEAD_DIM
ALIBI_TERMS = 3
VALUE_AUG = HEAD_DIM + 16
ATTN_KV_UNROLL = 3
MOE_CHUNK = 16
MOE_TOKEN_TILE = 256
MOE_ROW_TILE = 256
MOE_CHUNKS_PER_TILE = MOE_ROW_TILE // MOE_CHUNK
MOE_WEIGHT_PIECES = 8
MOE_WEIGHT_STAGES = 4
assert MOE_WEIGHT_STAGES - 1 <= MOE_WEIGHT_PIECES

V7X_VMEM_LIMIT_BYTES = 56 * 1024 * 1024


def _params(semantics, vmem=V7X_VMEM_LIMIT_BYTES, flags=None):
    return pltpu.CompilerParams(dimension_semantics=semantics, vmem_limit_bytes=vmem, flags=flags)


def _norm_inproj_kernel(x_ref, g_ref, w_ref, o_ref, wb_ref):
    @pl.when(pl.program_id(1) == 0)
    def _():
        wb_ref[...] = w_ref[...].astype(BF16)

    x = x_ref[...]
    ms = jnp.mean(x * x, axis=-1, keepdims=True)
    h = (x * lax.rsqrt(ms + RMS_EPS) * g_ref[...]).astype(BF16)
    col0 = pl.program_id(0) * o_ref.shape[1]
    is_q = (col0 >= POOL_WIDTH) & (col0 < POOL_WIDTH + ATTN_WIDTH)
    factor = jnp.where(is_q, QUERY_SCALE, 1.0).astype(F32)
    o_ref[...] = (jnp.dot(h, wb_ref[...], preferred_element_type=F32) * factor).astype(o_ref.dtype)


def _norm_inproj(x, g, w, tm=1024, tn=1024):
    S, D = x.shape
    N = w.shape[1]
    assert POOL_WIDTH % tn == 0 and ATTN_WIDTH % tn == 0
    return pl.pallas_call(
        _norm_inproj_kernel,
        grid=(N // tn, S // tm),
        in_specs=[
            pl.BlockSpec((tm, D), lambda j, i: (i, 0)),
            pl.BlockSpec((1, D), lambda j, i: (0, 0)),
            pl.BlockSpec((D, tn), lambda j, i: (0, j)),
        ],
        out_specs=pl.BlockSpec((tm, tn), lambda j, i: (i, j)),
        out_shape=jax.ShapeDtypeStruct((S, N), BF16),
        scratch_shapes=[pltpu.VMEM((D, tn), BF16)],
        compiler_params=_params(("parallel", "arbitrary")),
        name="norm_inproj",
    )(x, g, w)


def _pool_kernel(cur_ref, prev_ref, w_ref, scale_ref, o_ref, ext_ref):
    i = pl.program_id(0)
    tm = cur_ref.shape[0]
    u = cur_ref[...].astype(F32)
    halo = jnp.where(i > 0, prev_ref[...].astype(F32), 0.0)
    ext_ref[0:MAX_WINDOW, :] = halo
    ext_ref[MAX_WINDOW:MAX_WINDOW + tm, :] = u
    t = i * tm + lax.broadcasted_iota(jnp.int32, (tm, 1), 0)
    for g, w in enumerate(POOL_WINDOWS):
        cols = slice(g * POOL_GROUP, (g + 1) * POOL_GROUP)
        ug = u[:, cols]
        wsum = ug
        for s in range(1, w):
            wsum = wsum + ext_ref[MAX_WINDOW - s:MAX_WINDOW - s + tm, cols]
        cnt = jnp.minimum(t + 1, w).astype(F32)
        mixed = (wsum / cnt - ug).astype(BF16)
        y = jnp.dot(mixed, w_ref[g], preferred_element_type=F32)
        o_ref[:, cols] = (y * scale_ref[:, cols]).astype(o_ref.dtype)


def _pool_mixer(proj, w_pool, pool_scale, tm=512):
    S = proj.shape[0]
    halo_blocks = tm // MAX_WINDOW
    return pl.pallas_call(
        _pool_kernel,
        grid=(S // tm,),
        in_specs=[
            pl.BlockSpec((tm, POOL_WIDTH), lambda i: (i, 0)),
            pl.BlockSpec((MAX_WINDOW, POOL_WIDTH),
                         lambda i: (jnp.maximum(i * halo_blocks - 1, 0), 0)),
            pl.BlockSpec((len(POOL_WINDOWS), POOL_GROUP, POOL_GROUP), lambda i: (0, 0, 0)),
            pl.BlockSpec((1, POOL_WIDTH), lambda i: (0, 0)),
        ],
        out_specs=pl.BlockSpec((tm, POOL_WIDTH), lambda i: (i, 0)),
        out_shape=jax.ShapeDtypeStruct((S, POOL_WIDTH), BF16),
        scratch_shapes=[pltpu.VMEM((tm + MAX_WINDOW, POOL_WIDTH), F32)],
        compiler_params=_params(("parallel",)),
        name="pool_mixer",
    )(proj, proj, w_pool, pool_scale)


def _moba_gate_kernel(slope_ref, q_ref, k_ref, v_ref, qT_ref, ka_ref, vT_ref, sel_ref):
    S = q_ref.shape[0]
    nb = S // MOBA_BLOCK
    topk = min(MOBA_TOPK, nb)
    kf = k_ref[...].astype(F32).reshape(nb, MOBA_BLOCK, HEAD_DIM)
    kmean = jnp.sum(kf, axis=1) * (1.0 / MOBA_BLOCK)
    km_hi = kmean.astype(BF16)
    km_lo = (kmean - km_hi.astype(F32)).astype(BF16)
    blk = lax.broadcasted_iota(jnp.int32, (nb, MOBA_BLOCK), 0)

    pos = lax.broadcasted_iota(jnp.int32, (MOBA_BLOCK, KEY_AUG - HEAD_DIM), 0).astype(F32)
    col = lax.broadcasted_iota(jnp.int32, (MOBA_BLOCK, KEY_AUG - HEAD_DIM), 1)
    k_extra = jnp.where(col < ALIBI_TERMS, pos, 0.0).astype(BF16)

    slope2 = slope_ref[...][:, 0:1] * LOG2_E
    q_extra = jnp.zeros((KEY_AUG - HEAD_DIM, MOBA_BLOCK), F32)
    row = lax.broadcasted_iota(jnp.int32, q_extra.shape, 0)
    rest = slope2
    for n in range(ALIBI_TERMS):
        piece = rest.astype(BF16).astype(F32)
        q_extra = jnp.where(row == n, piece, q_extra)
        rest = rest - piece
    q_extra = q_extra.astype(BF16)
    v_row = lax.broadcasted_iota(jnp.int32, (VALUE_AUG - HEAD_DIM, MOBA_BLOCK), 0)
    v_extra = jnp.where(v_row == 0, 1.0, 0.0).astype(BF16)

    def body(i, carry):
        rows = pl.ds(pl.multiple_of(i * MOBA_BLOCK, MOBA_BLOCK), MOBA_BLOCK)
        qT = q_ref[rows, :].astype(F32).T.astype(BF16)
        qT_ref[i, 0:HEAD_DIM, :] = qT
        qT_ref[i, HEAD_DIM:KEY_AUG, :] = q_extra
        ka_ref[rows, 0:HEAD_DIM] = k_ref[rows, :]
        ka_ref[rows, HEAD_DIM:KEY_AUG] = k_extra
        vT_ref[i, 0:HEAD_DIM, :] = v_ref[rows, :].astype(F32).T.astype(BF16)
        vT_ref[i, HEAD_DIM:VALUE_AUG, :] = v_extra
        gate = (jnp.dot(km_hi, qT, preferred_element_type=F32)
                + jnp.dot(km_lo, qT, preferred_element_type=F32)) * (1.0 / QUERY_SCALE)
        gate = jnp.where(blk < i, gate, NEG_INF)
        sel = jnp.zeros((nb, MOBA_BLOCK), F32)
        for _ in range(topk):
            best = jnp.max(gate, axis=0, keepdims=True)
            idx = jnp.min(jnp.where(gate == best, blk, nb), axis=0, keepdims=True)
            hit = blk == idx
            sel = jnp.where(hit & (blk < i), 1.0, sel)
            gate = jnp.where(hit, -jnp.inf, gate)
        sel_ref[i] = sel
        return carry

    lax.fori_loop(0, nb, body, 0, unroll=4 if nb % 4 == 0 else 1)


def _moba_gate(proj, slopes):
    S = proj.shape[0]
    nb = S // MOBA_BLOCK
    H = ATTN_HEADS
    q0 = POOL_WIDTH // HEAD_DIM
    k0 = q0 + H
    v0 = k0 + H
    blocked = lambda h: (h, 0, 0, 0)
    return pl.pallas_call(
        _moba_gate_kernel,
        grid=(H,),
        in_specs=[
            pl.BlockSpec((None, 1, 128), lambda h: (h, 0, 0)),
            pl.BlockSpec((S, HEAD_DIM), lambda h: (0, q0 + h)),
            pl.BlockSpec((S, HEAD_DIM), lambda h: (0, k0 + h)),
            pl.BlockSpec((S, HEAD_DIM), lambda h: (0, v0 + h)),
        ],
        out_specs=[
            pl.BlockSpec((None, nb, KEY_AUG, MOBA_BLOCK), blocked),
            pl.BlockSpec((None, S, KEY_AUG), lambda h: (h, 0, 0)),
            pl.BlockSpec((None, nb, VALUE_AUG, MOBA_BLOCK), blocked),
            pl.BlockSpec((None, nb, nb, MOBA_BLOCK), blocked),
        ],
        out_shape=[
            jax.ShapeDtypeStruct((H, nb, KEY_AUG, MOBA_BLOCK), BF16),
            jax.ShapeDtypeStruct((H, S, KEY_AUG), BF16),
            jax.ShapeDtypeStruct((H, nb, VALUE_AUG, MOBA_BLOCK), BF16),
            jax.ShapeDtypeStruct((H, nb, nb, MOBA_BLOCK), F32),
        ],
        compiler_params=_params(("parallel",)),
        name="moba_gate",
    )(slopes, proj, proj, proj)


def _moba_attn_kernel(slope_ref, qT_ref, qT_next_ref, sel_ref, k_ref, vT_ref, o_ref,
                      s0_ref, s1_ref, p0_ref, p1_ref, s_own_ref):
    i = pl.program_id(1)
    nb = sel_ref.shape[0]
    slope2 = slope_ref[...][:, 0:1] * LOG2_E
    qT = qT_ref[...]

    def block_of(t, u):
        return jnp.clip(t * ATTN_KV_UNROLL + u, 0, nb - 1)

    def key_rows(j):
        return pl.ds(pl.multiple_of(j * MOBA_BLOCK, MOBA_BLOCK), MOBA_BLOCK)

    def issue_scores(t, s_ref, queries=qT):
        for u in range(ATTN_KV_UNROLL):
            s_ref[u] = jnp.dot(k_ref[key_rows(block_of(t, u)), :], queries,
                               preferred_element_type=F32)

    def apply_probs(t, p_ref, alpha, acc):
        acc = alpha * acc
        for u in range(ATTN_KV_UNROLL):
            acc = acc + jnp.dot(vT_ref[block_of(t, u)], p_ref[u], preferred_element_type=F32)
        return acc

    def softmax_group(t, s_ref, p_ref, m):
        m_new = m
        shifts = []
        for u in range(ATTN_KV_UNROLL):
            j = t * ATTN_KV_UNROLL + u
            valid = jnp.where(j < i, sel_ref[pl.ds(block_of(t, u), 1), :], 0.0) > 0.0
            gap = slope2 * ((i - j) * MOBA_BLOCK).astype(F32)
            top = jnp.max(s_ref[u], axis=0, keepdims=True) - gap
            m_new = jnp.maximum(m_new, jnp.where(valid, top, NEG_INF))
            shifts.append((valid, gap))
        alpha = jnp.exp2(m - m_new)
        for u, (valid, gap) in enumerate(shifts):
            p = jnp.exp2(s_ref[u] - jnp.where(valid, m_new + gap, jnp.inf))
            p_ref[u] = p.astype(BF16)
        return m_new, alpha

    def body(r, carry):
        m, acc, alpha = carry
        acc = apply_probs(2 * r - 1, p1_ref, alpha, acc)
        m, alpha = softmax_group(2 * r, s0_ref, p0_ref, m)
        issue_scores(2 * r + 1, s1_ref)
        acc = apply_probs(2 * r, p0_ref, alpha, acc)
        m, alpha = softmax_group(2 * r + 1, s1_ref, p1_ref, m)
        issue_scores(2 * r + 2, s0_ref)
        return m, acc, alpha

    n_groups = (i + ATTN_KV_UNROLL - 1) // ATTN_KV_UNROLL
    n_pairs = n_groups // 2
    @pl.when(i == 0)
    def _():
        s0_ref[...] = jnp.zeros(s0_ref.shape, F32)
        s_own_ref[...] = jnp.dot(k_ref[key_rows(0), :], qT, preferred_element_type=F32)

    p1_ref[...] = jnp.zeros(p1_ref.shape, BF16)
    kpos = lax.broadcasted_iota(jnp.int32, (MOBA_BLOCK, MOBA_BLOCK), 0)
    qpos = lax.broadcasted_iota(jnp.int32, (MOBA_BLOCK, MOBA_BLOCK), 1)
    s_own = jnp.where(qpos >= kpos, s_own_ref[...], NEG_INF)
    top_own = jnp.max(s_own, axis=0, keepdims=True)
    row = jnp.zeros((1, MOBA_BLOCK), F32)
    init = (row + NEG_INF, jnp.zeros((VALUE_AUG, MOBA_BLOCK), F32), row + 1.0)
    m, acc, alpha = lax.fori_loop(0, n_pairs, body, init)
    acc = apply_probs(2 * n_pairs - 1, p1_ref, alpha, acc)
    m, alpha = softmax_group(2 * n_pairs, s0_ref, p0_ref, m)
    acc = apply_probs(2 * n_pairs, p0_ref, alpha, acc)
    m_new = jnp.maximum(m, top_own)
    alpha = jnp.exp2(m - m_new)
    p = jnp.exp2(s_own - m_new)
    acc = alpha * acc + jnp.dot(vT_ref[i], p.astype(BF16), preferred_element_type=F32)
    out = acc[0:HEAD_DIM] / acc[HEAD_DIM:HEAD_DIM + 1]
    o_ref[...] = out.T.astype(o_ref.dtype)
    q_next = qT_next_ref[...]
    issue_scores(0, s0_ref, q_next)
    s_own_ref[...] = jnp.dot(k_ref[key_rows(jnp.minimum(i + 1, nb - 1)), :], q_next,
                             preferred_element_type=F32)


def _moba_attention(k_aug, qT, vT, sel, slopes):
    H, S, _ = k_aug.shape
    nb = S // MOBA_BLOCK
    return pl.pallas_call(
        _moba_attn_kernel,
        grid=(H, nb),
        in_specs=[
            pl.BlockSpec((None, 1, 128), lambda h, i: (h, 0, 0)),
            pl.BlockSpec((None, None, KEY_AUG, MOBA_BLOCK), lambda h, i: (h, i, 0, 0)),
            pl.BlockSpec((None, None, KEY_AUG, MOBA_BLOCK),
                         lambda h, i: (h, jnp.minimum(i + 1, nb - 1), 0, 0)),
            pl.BlockSpec((None, None, nb, MOBA_BLOCK), lambda h, i: (h, i, 0, 0)),
            pl.BlockSpec((None, S, KEY_AUG), lambda h, i: (h, 0, 0)),
            pl.BlockSpec((None, nb, VALUE_AUG, MOBA_BLOCK), lambda h, i: (h, 0, 0, 0)),
        ],
        out_specs=pl.BlockSpec((MOBA_BLOCK, HEAD_DIM), lambda h, i: (i, h)),
        out_shape=jax.ShapeDtypeStruct((S, ATTN_WIDTH), BF16),
        scratch_shapes=[
            pltpu.VMEM((ATTN_KV_UNROLL, MOBA_BLOCK, MOBA_BLOCK), F32),
            pltpu.VMEM((ATTN_KV_UNROLL, MOBA_BLOCK, MOBA_BLOCK), F32),
            pltpu.VMEM((ATTN_KV_UNROLL, MOBA_BLOCK, MOBA_BLOCK), BF16),
            pltpu.VMEM((ATTN_KV_UNROLL, MOBA_BLOCK, MOBA_BLOCK), BF16),
            pltpu.VMEM((MOBA_BLOCK, MOBA_BLOCK), F32),
        ],
        compiler_params=_params(("arbitrary", "arbitrary")),
        name="moba_attention",
    )(slopes, qT, qT, sel, k_aug, vT)


def _first_lane_of_max(vals, lane):
    best = jnp.max(vals, axis=1, keepdims=True)
    idx = jnp.min(jnp.where(vals == best, lane, ROUTE_LANES), axis=1, keepdims=True)
    return best, idx


def _merge_route_kernel(yp_ref, ya_ref, glp_ref, gla_ref, x_ref, wbp_ref, wba_ref, wout_ref,
                        g_ref, wr_hi_ref, wr_lo_ref, br_ref, x1_ref, hs_ref, route_ref, cnt_ref):
    tm = x_ref.shape[0]
    cap = hs_ref.shape[0]
    bp = jnp.dot(yp_ref[...], wbp_ref[...], preferred_element_type=F32)
    ba = jnp.dot(ya_ref[...], wba_ref[...], preferred_element_type=F32)
    merged = (jax.nn.sigmoid(glp_ref[...].astype(F32)) * bp
              + jax.nn.sigmoid(gla_ref[...].astype(F32)) * ba)
    x1 = x_ref[...] + jnp.dot(merged.astype(BF16), wout_ref[...], preferred_element_type=F32)
    x1_ref[...] = x1
    ms = jnp.mean(x1 * x1, axis=-1, keepdims=True)
    h2 = x1 * lax.rsqrt(ms + RMS_EPS) * g_ref[...]
    h2_hi = h2.astype(BF16)
    h2_lo = (h2 - h2_hi.astype(F32)).astype(BF16)
    logits = (jnp.dot(h2_hi, wr_hi_ref[...], preferred_element_type=F32)
              + jnp.dot(h2_lo, wr_hi_ref[...], preferred_element_type=F32)
              + jnp.dot(h2_hi, wr_lo_ref[...], preferred_element_type=F32)
              + br_ref[...])

    lane = lax.broadcasted_iota(jnp.int32, logits.shape, 1)
    g_logits = jnp.where(lane < N_GROUPS, logits, -jnp.inf)
    g_best, g_idx = _first_lane_of_max(g_logits, lane)
    g_w = 1.0 / jnp.sum(jnp.exp(g_logits - g_best), axis=1, keepdims=True)
    e_lo = EXPERT_LANE0 + EXPERTS_PER_GROUP * g_idx
    e_logits = jnp.where((lane >= e_lo) & (lane < e_lo + EXPERTS_PER_GROUP), logits, -jnp.inf)
    v1, i1 = _first_lane_of_max(e_logits, lane)
    v2, i2 = _first_lane_of_max(jnp.where(lane == i1, -jnp.inf, e_logits), lane)
    e21 = jnp.exp(v2 - v1)
    w1 = g_w / (1.0 + e21)
    w2 = g_w * e21 / (1.0 + e21)

    hit1 = lane == i1
    hit2 = lane == i2
    member = jnp.where(hit1, 1.0, jnp.where(hit2, 1.0, 0.0))
    r_tok = lax.broadcasted_iota(jnp.int32, (tm, tm), 0)
    c_tok = lax.broadcasted_iota(jnp.int32, (tm, tm), 1)
    earlier = jnp.where(c_tok < r_tok, 1.0, 0.0).astype(BF16)
    rank = jnp.dot(earlier, member.astype(BF16), preferred_element_type=F32)
    count = jnp.sum(member, axis=0, keepdims=True)
    chunks = jnp.floor((count + (MOE_CHUNK - 1)) * (1.0 / MOE_CHUNK))
    r_l = lax.broadcasted_iota(jnp.int32, (ROUTE_LANES, ROUTE_LANES), 0)
    c_l = lax.broadcasted_iota(jnp.int32, (ROUTE_LANES, ROUTE_LANES), 1)
    lower_lanes = jnp.where(r_l < c_l, 1.0, 0.0).astype(BF16)
    start = jnp.dot(jnp.broadcast_to(chunks, (8, ROUTE_LANES)).astype(BF16), lower_lanes,
                    preferred_element_type=F32)[0:1] * MOE_CHUNK
    pos = start + rank
    pos1 = jnp.sum(jnp.where(hit1, pos, 0.0), axis=1, keepdims=True)
    pos2 = jnp.sum(jnp.where(hit2, pos, 0.0), axis=1, keepdims=True)
    route = jnp.where(lane == 0, pos1, jnp.where(lane == 1, pos2,
                      jnp.where(lane == 2, w1, jnp.where(lane == 3, w2, 0.0))))
    route_ref[...] = route
    cnt_ref[...] = chunks
    route_t = route.T
    slot = lax.broadcasted_iota(jnp.int32, (cap, tm), 0)
    p1_row = route_t[0:1, :].astype(jnp.int32)
    p2_row = route_t[1:2, :].astype(jnp.int32)
    onehot = jnp.where(slot == p1_row, 1.0, jnp.where(slot == p2_row, 1.0, 0.0)).astype(BF16)
    hs_ref[...] = jnp.dot(onehot, h2_hi, preferred_element_type=F32).astype(BF16)


def _moe_cap(tm):
    worst = 2 * tm + N_EXPERTS * (MOE_CHUNK - 1)
    return -(-worst // 128) * 128


def _merge_route(yp, ya, proj, x, wbp, wba, wout, g, wr_hi, wr_lo, br, tm=MOE_TOKEN_TILE):
    S, D = x.shape
    nT = S // tm
    cap = _moe_cap(tm)
    full = lambda i: (0, 0)
    row = lambda i: (i, 0)
    glp_blk = (POOL_WIDTH + 3 * ATTN_WIDTH) // D
    return pl.pallas_call(
        _merge_route_kernel,
        grid=(S // tm,),
        in_specs=[
            pl.BlockSpec((tm, POOL_WIDTH), row),
            pl.BlockSpec((tm, ATTN_WIDTH), row),
            pl.BlockSpec((tm, D), lambda i: (i, glp_blk)),
            pl.BlockSpec((tm, D), lambda i: (i, glp_blk + 1)),
            pl.BlockSpec((tm, D), row),
            pl.BlockSpec((POOL_WIDTH, D), full),
            pl.BlockSpec((ATTN_WIDTH, D), full),
            pl.BlockSpec((D, D), full),
            pl.BlockSpec((1, D), full),
            pl.BlockSpec((D, ROUTE_LANES), full),
            pl.BlockSpec((D, ROUTE_LANES), full),
            pl.BlockSpec((1, ROUTE_LANES), full),
        ],
        out_specs=[
            pl.BlockSpec((tm, D), row),
            pl.BlockSpec((None, cap, D), lambda i: (i, 0, 0)),
            pl.BlockSpec((tm, ROUTE_LANES), row),
            pl.BlockSpec((None, 1, ROUTE_LANES), lambda i: (i, 0, 0)),
        ],
        out_shape=[
            jax.ShapeDtypeStruct((S, D), F32),
            jax.ShapeDtypeStruct((nT, cap, D), BF16),
            jax.ShapeDtypeStruct((S, ROUTE_LANES), F32),
            jax.ShapeDtypeStruct((nT, 1, ROUTE_LANES), F32),
        ],
        compiler_params=_params(("parallel",)),
        name="merge_route",
    )(yp, ya, proj, proj, x, wbp, wba, wout, g, wr_hi, wr_lo, br)


def _moe_plan(chunk_counts, n_row_tiles, n_chunk_slots):
    nT, E = chunk_counts.shape
    per_expert = chunk_counts.T
    seg_start = (jnp.cumsum(chunk_counts, axis=1) - chunk_counts).T
    seg_end = jnp.cumsum(per_expert, axis=1)
    n_chunks = seg_end[:, -1]
    padded = -(-n_chunks // MOE_CHUNKS_PER_TILE) * MOE_CHUNKS_PER_TILE
    e_end = jnp.cumsum(padded)
    n_used = e_end[-1] // MOE_CHUNKS_PER_TILE
    c = jnp.arange(n_chunk_slots, dtype=jnp.int32)
    e_of_c = jnp.minimum((e_end[None, :] <= c[:, None]).sum(axis=1), E - 1)
    is_e = (e_of_c[:, None] == jnp.arange(E)[None, :]).astype(jnp.int32)
    local = c - (is_e * (e_end - padded)[None, :]).sum(axis=1)
    real = local < (is_e * n_chunks[None, :]).sum(axis=1)
    pick_e = lambda table: (is_e[:, :, None] * table[None, :, :]).sum(axis=1)
    seg_end_c = pick_e(seg_end)
    t_of_c = jnp.minimum((seg_end_c <= local[:, None]).sum(axis=1), nT - 1)
    is_t = (t_of_c[:, None] == jnp.arange(nT)[None, :]).astype(jnp.int32)
    pick_t = lambda rows: (rows * is_t).sum(axis=1)
    within = local - (pick_t(seg_end_c) - pick_t(pick_e(per_expert)))
    src_tile = jnp.where(real, t_of_c, 0).astype(jnp.int32)
    src_row = jnp.where(real, (pick_t(pick_e(seg_start)) + within) * MOE_CHUNK, 0).astype(jnp.int32)
    tile = jnp.arange(n_row_tiles, dtype=jnp.int32)
    first = jnp.minimum(tile, n_used - 1) * MOE_CHUNKS_PER_TILE
    tile_expert = jnp.minimum((e_end[None, :] <= first[:, None]).sum(axis=1), E - 1)
    tile_real = real.reshape(n_row_tiles, MOE_CHUNKS_PER_TILE).sum(axis=1)
    experts = jnp.arange(E)
    nonempty = padded > 0
    buffer_of_e = jnp.cumsum(nonempty) - nonempty
    later = (experts[None, :] > experts[:, None]) & nonempty[None, :]
    next_of_e = jnp.min(jnp.where(later, experts[None, :], E), axis=1)
    next_of_e = jnp.where(next_of_e < E, next_of_e, -1)
    is_next = (next_of_e[:, None] == experts[None, :]).astype(jnp.int32)
    after_of_e = jnp.where(next_of_e >= 0, (is_next * next_of_e[None, :]).sum(axis=1), -1)
    is_te = (tile_expert[:, None] == experts[None, :]).astype(jnp.int32)
    at = lambda table: (is_te * table[None, :]).sum(axis=1)
    run_len = jnp.maximum(at(padded) // MOE_CHUNKS_PER_TILE, 1)
    run_pos = tile - at(e_end - padded) // MOE_CHUNKS_PER_TILE
    next_expert = at(next_of_e)
    streams = (tile < n_used) & (next_expert >= 0)
    piece_lo = jnp.where(streams, (MOE_WEIGHT_PIECES * run_pos) // run_len, 0)
    piece_hi = jnp.where(streams, (MOE_WEIGHT_PIECES * (run_pos + 1)) // run_len, 0)
    as_i32 = lambda a: a.astype(jnp.int32)
    return (as_i32(tile_expert), as_i32(tile_real), src_tile, src_row, as_i32(n_used.reshape(1)),
            as_i32(at(buffer_of_e) % 2), as_i32(next_expert), as_i32(at(after_of_e)),
            as_i32(piece_lo), as_i32(piece_hi), as_i32(chunk_counts.sum(axis=1)))


def _moe_ffn_kernel(texp_ref, treal_ref, ctile_ref, crow_ref, nused_ref,
                    wbuf_ref, wnext_ref, wafter_ref, plo_ref, phi_ref, tused_ref,
                    hs_hbm, wg_hbm, wu_hbm, wd_hbm, ys_hbm,
                    xbuf, ybuf, zbuf, wg_buf, wu_buf, wd_buf, stage_g, stage_u, stage_d,
                    gather_sem, scatter_sem, weight_sem, zero_sem):
    i = pl.program_id(0)
    n_used = nused_ref[0]
    slot = lax.rem(i, 2)
    rows_gu = stage_g.shape[1]
    rows_d = stage_d.shape[1]

    def piece_copies(e, p, s):
        gu_rows = pl.ds(pl.multiple_of(p * rows_gu, rows_gu), rows_gu)
        d_rows = pl.ds(pl.multiple_of(p * rows_d, rows_d), rows_d)
        return (pltpu.make_async_copy(wg_hbm.at[e, gu_rows, :], stage_g.at[s], weight_sem.at[s]),
                pltpu.make_async_copy(wu_hbm.at[e, gu_rows, :], stage_u.at[s], weight_sem.at[s]),
                pltpu.make_async_copy(wd_hbm.at[e, d_rows, :], stage_d.at[s], weight_sem.at[s]))

    def start_piece(e, p):
        for copy in piece_copies(e, p, lax.rem(p, MOE_WEIGHT_STAGES)):
            copy.start()

    def finish_piece(e, p, side):
        s = lax.rem(p, MOE_WEIGHT_STAGES)
        for copy in piece_copies(e, p, s):
            copy.wait()
        gu_rows = pl.ds(pl.multiple_of(p * rows_gu, rows_gu), rows_gu)
        d_rows = pl.ds(pl.multiple_of(p * rows_d, rows_d), rows_d)
        wg_buf[side, gu_rows, :] = stage_g[s].astype(BF16)
        wu_buf[side, gu_rows, :] = stage_u[s].astype(BF16)
        wd_buf[side, d_rows, :] = stage_d[s].astype(BF16)

    ahead = MOE_WEIGHT_STAGES - 1

    def start_first_pieces(e):
        for p in range(ahead):
            start_piece(e, p)

    def stream_pieces(e, lo, hi, side, following):
        @pl.when(hi > lo)
        def _():
            def body(p, carry):
                @pl.when(p + ahead < MOE_WEIGHT_PIECES)
                def _():
                    start_piece(e, p + ahead)
                finish_piece(e, p, side)
                return carry
            lax.fori_loop(lo, hi, body, 0)

            @pl.when((hi == MOE_WEIGHT_PIECES) & (following >= 0))
            def _():
                start_first_pieces(following)

    def chunk_rows(c):
        return pl.ds(pl.multiple_of(c * MOE_CHUNK, MOE_CHUNK), MOE_CHUNK)

    def gather_copy(tile, c, buf):
        g = tile * MOE_CHUNKS_PER_TILE + c
        src = hs_hbm.at[ctile_ref[g], pl.ds(pl.multiple_of(crow_ref[g], MOE_CHUNK), MOE_CHUNK), :]
        return pltpu.make_async_copy(src, xbuf.at[buf, chunk_rows(c), :], gather_sem.at[buf])

    def scatter_copy(tile, c, buf):
        g = tile * MOE_CHUNKS_PER_TILE + c
        dst = ys_hbm.at[ctile_ref[g], pl.ds(pl.multiple_of(crow_ref[g], MOE_CHUNK), MOE_CHUNK), :]
        return pltpu.make_async_copy(ybuf.at[buf, chunk_rows(c), :], dst, scatter_sem.at[buf])

    def for_real_chunks(tile, fn):
        n_real = treal_ref[tile]
        for c in range(MOE_CHUNKS_PER_TILE):
            pl.when(c < n_real)(functools.partial(fn, c))

    def for_tail_chunks(fn):
        n_token_tiles, cap = ys_hbm.shape[0], ys_hbm.shape[1]

        def per_tile(t, carry):
            def per_chunk(c, carry):
                fn(pltpu.make_async_copy(zbuf, ys_hbm.at[t, chunk_rows(c), :], zero_sem.at[0]))
                return carry
            return lax.fori_loop(tused_ref[t], cap // MOE_CHUNK, per_chunk, carry)
        lax.fori_loop(0, n_token_tiles, per_tile, 0)

    @pl.when(i == 0)
    def _():
        zbuf[...] = jnp.zeros(zbuf.shape, zbuf.dtype)
        for_tail_chunks(lambda copy: copy.start())
        xbuf[...] = jnp.zeros(xbuf.shape, xbuf.dtype)
        for_real_chunks(0, lambda c: gather_copy(0, c, 0).start())
        start_first_pieces(texp_ref[0])
        stream_pieces(texp_ref[0], 0, MOE_WEIGHT_PIECES, wbuf_ref[0], wnext_ref[0])

    @pl.when(i + 1 < n_used)
    def _():
        for_real_chunks(i + 1, lambda c: gather_copy(i + 1, c, 1 - slot).start())

    side = wbuf_ref[i]
    stream_pieces(wnext_ref[i], plo_ref[i], phi_ref[i], 1 - side, wafter_ref[i])

    @pl.when(i < n_used)
    def _():
        for_real_chunks(i, lambda c: gather_copy(i, c, slot).wait())

        @pl.when(i >= 2)
        def _():
            for_real_chunks(i - 2, lambda c: scatter_copy(i - 2, c, slot).wait())

        x = xbuf[slot]
        a = jnp.dot(x, wg_buf[side], preferred_element_type=F32)
        u = jnp.dot(x, wu_buf[side], preferred_element_type=F32)
        hid = (jax.nn.silu(a) * u).astype(BF16)
        ybuf[slot] = jnp.dot(hid, wd_buf[side], preferred_element_type=F32).astype(BF16)
        for_real_chunks(i, lambda c: scatter_copy(i, c, slot).start())

    @pl.when(i == n_used - 1)
    def _():
        @pl.when(i >= 1)
        def _():
            for_real_chunks(i - 1, lambda c: scatter_copy(i - 1, c, 1 - slot).wait())
        for_real_chunks(i, lambda c: scatter_copy(i, c, slot).wait())
        for_tail_chunks(lambda copy: copy.wait())


def _moe_ffn(hs, plan, wg, wu, wd):
    nT, cap, D = hs.shape
    E, _, F = wg.shape
    n_row_tiles = plan[0].shape[0]
    assert D % MOE_WEIGHT_PIECES == 0 and F % MOE_WEIGHT_PIECES == 0
    grid_spec = pltpu.PrefetchScalarGridSpec(
        num_scalar_prefetch=len(plan),
        grid=(n_row_tiles,),
        in_specs=[pl.BlockSpec(memory_space=pl.ANY)] * 4,
        out_specs=pl.BlockSpec(memory_space=pl.ANY),
        scratch_shapes=[
            pltpu.VMEM((2, MOE_ROW_TILE, D), BF16),
            pltpu.VMEM((2, MOE_ROW_TILE, D), BF16),
            pltpu.VMEM((MOE_CHUNK, D), BF16),
            pltpu.VMEM((2, D, F), BF16),
            pltpu.VMEM((2, D, F), BF16),
            pltpu.VMEM((2, F, D), BF16),
            pltpu.VMEM((MOE_WEIGHT_STAGES, D // MOE_WEIGHT_PIECES, F), F32),
            pltpu.VMEM((MOE_WEIGHT_STAGES, D // MOE_WEIGHT_PIECES, F), F32),
            pltpu.VMEM((MOE_WEIGHT_STAGES, F // MOE_WEIGHT_PIECES, D), F32),
            pltpu.SemaphoreType.DMA((2,)),
            pltpu.SemaphoreType.DMA((2,)),
            pltpu.SemaphoreType.DMA((MOE_WEIGHT_STAGES,)),
            pltpu.SemaphoreType.DMA((1,)),
        ],
    )
    return pl.pallas_call(
        _moe_ffn_kernel,
        grid_spec=grid_spec,
        out_shape=jax.ShapeDtypeStruct((nT, cap, D), BF16),
        compiler_params=_params(("arbitrary",)),
        name="moe_ffn",
    )(*plan, hs, wg, wu, wd)


def _moe_combine_kernel(ys_ref, route_ref, x1_ref, g_ref, o_ref):
    tm = x1_ref.shape[0]
    cap = ys_ref.shape[0]
    route = route_ref[...]
    slot = lax.broadcasted_iota(jnp.int32, (tm, cap), 1)
    pick1 = jnp.where(slot == route[:, 0:1].astype(jnp.int32), 1.0, 0.0).astype(BF16)
    pick2 = jnp.where(slot == route[:, 1:2].astype(jnp.int32), 1.0, 0.0).astype(BF16)
    ys = ys_ref[...]
    y1 = jnp.dot(pick1, ys, preferred_element_type=F32)
    y2 = jnp.dot(pick2, ys, preferred_element_type=F32)
    y = x1_ref[...] + route[:, 2:3] * y1 + route[:, 3:4] * y2
    ms = jnp.mean(y * y, axis=-1, keepdims=True)
    o_ref[...] = y * lax.rsqrt(ms + RMS_EPS) * g_ref[...]


def _moe_combine(ys, route, x1, g, tm=MOE_TOKEN_TILE):
    S, D = x1.shape
    cap = ys.shape[1]
    row = lambda i: (i, 0)
    return pl.pallas_call(
        _moe_combine_kernel,
        grid=(S // tm,),
        in_specs=[
            pl.BlockSpec((None, cap, D), lambda i: (i, 0, 0)),
            pl.BlockSpec((tm, ROUTE_LANES), row),
            pl.BlockSpec((tm, D), row),
            pl.BlockSpec((1, D), lambda i: (0, 0)),
        ],
        out_specs=pl.BlockSpec((tm, D), row),
        out_shape=jax.ShapeDtypeStruct((S, D), F32),
        compiler_params=_params(("parallel",)),
        name="moe_combine",
    )(ys, route, x1, g)


def _router_weights(w_r_group, b_r_group, w_r_expert, b_r_expert):
    D = w_r_group.shape[0]
    w = jnp.concatenate(
        [w_r_group, jnp.transpose(w_r_expert, (1, 0, 2)).reshape(D, N_EXPERTS)], axis=1)
    b = jnp.concatenate([b_r_group, b_r_expert.reshape(N_EXPERTS)])
    pad = ROUTE_LANES - w.shape[1]
    w = jnp.pad(w, ((0, 0), (0, pad)))
    b = jnp.pad(b, (0, pad)).reshape(1, ROUTE_LANES)
    w_hi = w.astype(BF16)
    w_lo = (w - w_hi.astype(F32)).astype(BF16)
    return w_hi, w_lo, b


def kernel(x, norm_mix, w_in, w_pool, pool_scale, w_branch_pool, w_branch_attn, w_out, norm_ffn,
           w_r_group, b_r_group, w_r_expert, b_r_expert, w_gate, w_up, w_down, norm_final):
    B, S, D = x.shape
    depth = w_in.shape[0]
    assert depth == 1, "the final rms_norm is fused into the expert kernel of a single layer"
    slopes = jnp.exp2(-8.0 * jnp.arange(1, ATTN_HEADS + 1, dtype=F32) / ATTN_HEADS)
    slopes = jnp.broadcast_to(slopes[:, None, None], (ATTN_HEADS, 1, 128))
    outs = []
    for b in range(B):
        xb = x[b]
        for l in range(depth):
            proj = _norm_inproj(xb, norm_mix[l].reshape(1, D), w_in[l])
            y_pool = _pool_mixer(proj, w_pool[l].astype(BF16), pool_scale[l].reshape(1, POOL_WIDTH))
            qT, k_aug, vT, sel = _moba_gate(proj, slopes)
            y_attn = _moba_attention(k_aug, qT, vT, sel, slopes)
            wr_hi, wr_lo, br = _router_weights(w_r_group[l], b_r_group[l], w_r_expert[l], b_r_expert[l])
            x1, hs, route, cnt = _merge_route(
                y_pool, y_attn, proj, xb,
                w_branch_pool[l].astype(BF16), w_branch_attn[l].astype(BF16), w_out[l].astype(BF16),
                norm_ffn[l].reshape(1, D), wr_hi, wr_lo, br)
            nT = hs.shape[0]
            chunk_counts = cnt[:, 0, EXPERT_LANE0:EXPERT_LANE0 + N_EXPERTS].astype(jnp.int32)
            max_chunks = nT * ((2 * MOE_TOKEN_TILE + N_EXPERTS * (MOE_CHUNK - 1)) // MOE_CHUNK)
            n_row_tiles = -(-(max_chunks + N_EXPERTS * (MOE_CHUNKS_PER_TILE - 1))
                            // MOE_CHUNKS_PER_TILE)
            plan = _moe_plan(chunk_counts, n_row_tiles, n_row_tiles * MOE_CHUNKS_PER_TILE)
            ys = _moe_ffn(hs, plan, w_gate[l], w_up[l], w_down[l])
            xb = _moe_combine(ys, route, x1, norm_final.reshape(1, D))
        outs.append(xb)
    return jnp.stack(outs, axis=0)
```

```python
import functools

import jax
import jax.numpy as jnp
from jax import lax
from jax.experimental import pallas as pl
from jax.experimental.pallas import tpu as pltpu

F32 = jnp.float32
BF16 = jnp.bfloat16

POOL_WINDOWS = (2, 4, 8, 16)
MAX_WINDOW = 16
POOL_WIDTH = 1024
POOL_GROUP = 256
HEAD_DIM = 128
ATTN_HEADS = 8
ATTN_WIDTH = 1024
MOBA_BLOCK = 256
MOBA_TOPK = 3
N_GROUPS = 4
EXPERTS_PER_GROUP = 4
N_EXPERTS = 16
ROUTE_LANES = 128
EXPERT_LANE0 = N_GROUPS
RMS_EPS = 1e-6
NEG_INF = -1e30
LOG2_E = 1.4426950408889634
QUERY_SCALE = (HEAD_DIM ** -0.5) * LOG2_E
KEY_AUG = 2 * HEAD_DIM
ALIBI_TERMS = 3
VALUE_AUG = HEAD_DIM + 16
ATTN_KV_UNROLL = 3
MOE_CHUNK = 16
MOE_TOKEN_TILE = 256
MOE_ROW_TILE = 256
MOE_CHUNKS_PER_TILE = MOE_ROW_TILE // MOE_CHUNK
MOE_WEIGHT_TERMS = 3
MOE_WEIGHT_PIECES = 8
MOE_WEIGHT_STAGES = 4
assert MOE_WEIGHT_STAGES - 1 <= MOE_WEIGHT_PIECES

V7X_VMEM_LIMIT_BYTES = 56 * 1024 * 1024


def _params(semantics, vmem=V7X_VMEM_LIMIT_BYTES, flags=None):
    return pltpu.CompilerParams(dimension_semantics=semantics, vmem_limit_bytes=vmem, flags=flags)


def _norm_inproj_kernel(x_ref, g_ref, w_ref, o_ref, wb_ref):
    @pl.when(pl.program_id(1) == 0)
    def _():
        wb_ref[...] = w_ref[...].astype(BF16)

    x = x_ref[...]
    ms = jnp.mean(x * x, axis=-1, keepdims=True)
    h = (x * lax.rsqrt(ms + RMS_EPS) * g_ref[...]).astype(BF16)
    col0 = pl.program_id(0) * o_ref.shape[1]
    is_q = (col0 >= POOL_WIDTH) & (col0 < POOL_WIDTH + ATTN_WIDTH)
    factor = jnp.where(is_q, QUERY_SCALE, 1.0).astype(F32)
    o_ref[...] = (jnp.dot(h, wb_ref[...], preferred_element_type=F32) * factor).astype(o_ref.dtype)


def _norm_inproj(x, g, w, tm=1024, tn=1024):
    S, D = x.shape
    N = w.shape[1]
    assert POOL_WIDTH % tn == 0 and ATTN_WIDTH % tn == 0
    return pl.pallas_call(
        _norm_inproj_kernel,
        grid=(N // tn, S // tm),
        in_specs=[
            pl.BlockSpec((tm, D), lambda j, i: (i, 0)),
            pl.BlockSpec((1, D), lambda j, i: (0, 0)),
            pl.BlockSpec((D, tn), lambda j, i: (0, j)),
        ],
        out_specs=pl.BlockSpec((tm, tn), lambda j, i: (i, j)),
        out_shape=jax.ShapeDtypeStruct((S, N), BF16),
        scratch_shapes=[pltpu.VMEM((D, tn), BF16)],
        compiler_params=_params(("parallel", "arbitrary")),
        name="norm_inproj",
    )(x, g, w)


def _pool_kernel(cur_ref, prev_ref, w_ref, scale_ref, o_ref, ext_ref):
    i = pl.program_id(0)
    tm = cur_ref.shape[0]
    u = cur_ref[...].astype(F32)
    halo = jnp.where(i > 0, prev_ref[...].astype(F32), 0.0)
    ext_ref[0:MAX_WINDOW, :] = halo
    ext_ref[MAX_WINDOW:MAX_WINDOW + tm, :] = u
    t = i * tm + lax.broadcasted_iota(jnp.int32, (tm, 1), 0)
    for g, w in enumerate(POOL_WINDOWS):
        cols = slice(g * POOL_GROUP, (g + 1) * POOL_GROUP)
        ug = u[:, cols]
        wsum = ug
        for s in range(1, w):
            wsum = wsum + ext_ref[MAX_WINDOW - s:MAX_WINDOW - s + tm, cols]
        cnt = jnp.minimum(t + 1, w).astype(F32)
        mixed = (wsum / cnt - ug).astype(BF16)
        y = jnp.dot(mixed, w_ref[g], preferred_element_type=F32)
        o_ref[:, cols] = (y * scale_ref[:, cols]).astype(o_ref.dtype)


def _pool_mixer(proj, w_pool, pool_scale, tm=512):
    S = proj.shape[0]
    halo_blocks = tm // MAX_WINDOW
    return pl.pallas_call(
        _pool_kernel,
        grid=(S // tm,),
        in_specs=[
            pl.BlockSpec((tm, POOL_WIDTH), lambda i: (i, 0)),
            pl.BlockSpec((MAX_WINDOW, POOL_WIDTH),
                         lambda i: (jnp.maximum(i * halo_blocks - 1, 0), 0)),
            pl.BlockSpec((len(POOL_WINDOWS), POOL_GROUP, POOL_GROUP), lambda i: (0, 0, 0)),
            pl.BlockSpec((1, POOL_WIDTH), lambda i: (0, 0)),
        ],
        out_specs=pl.BlockSpec((tm, POOL_WIDTH), lambda i: (i, 0)),
        out_shape=jax.ShapeDtypeStruct((S, POOL_WIDTH), BF16),
        scratch_shapes=[pltpu.VMEM((tm + MAX_WINDOW, POOL_WIDTH), F32)],
        compiler_params=_params(("parallel",)),
        name="pool_mixer",
    )(proj, proj, w_pool, pool_scale)


def _moba_gate_kernel(slope_ref, q_ref, k_ref, v_ref, qT_ref, ka_ref, vT_ref, sel_ref):
    S = q_ref.shape[0]
    nb = S // MOBA_BLOCK
    topk = min(MOBA_TOPK, nb)
    kf = k_ref[...].astype(F32).reshape(nb, MOBA_BLOCK, HEAD_DIM)
    kmean = jnp.sum(kf, axis=1) * (1.0 / MOBA_BLOCK)
    km_hi = kmean.astype(BF16)
    km_lo = (kmean - km_hi.astype(F32)).astype(BF16)
    blk = lax.broadcasted_iota(jnp.int32, (nb, MOBA_BLOCK), 0)

    pos = lax.broadcasted_iota(jnp.int32, (MOBA_BLOCK, KEY_AUG - HEAD_DIM), 0).astype(F32)
    col = lax.broadcasted_iota(jnp.int32, (MOBA_BLOCK, KEY_AUG - HEAD_DIM), 1)
    k_extra = jnp.where(col < ALIBI_TERMS, pos, 0.0).astype(BF16)

    slope2 = slope_ref[...][:, 0:1] * LOG2_E
    q_extra = jnp.zeros((KEY_AUG - HEAD_DIM, MOBA_BLOCK), F32)
    row = lax.broadcasted_iota(jnp.int32, q_extra.shape, 0)
    rest = slope2
    for n in range(ALIBI_TERMS):
        piece = rest.astype(BF16).astype(F32)
        q_extra = jnp.where(row == n, piece, q_extra)
        rest = rest - piece
    q_extra = q_extra.astype(BF16)
    v_row = lax.broadcasted_iota(jnp.int32, (VALUE_AUG - HEAD_DIM, MOBA_BLOCK), 0)
    v_extra = jnp.where(v_row == 0, 1.0, 0.0).astype(BF16)

    def body(i, carry):
        rows = pl.ds(pl.multiple_of(i * MOBA_BLOCK, MOBA_BLOCK), MOBA_BLOCK)
        qT = q_ref[rows, :].astype(F32).T.astype(BF16)
        qT_ref[i, 0:HEAD_DIM, :] = qT
        qT_ref[i, HEAD_DIM:KEY_AUG, :] = q_extra
        ka_ref[rows, 0:HEAD_DIM] = k_ref[rows, :]
        ka_ref[rows, HEAD_DIM:KEY_AUG] = k_extra
        vT_ref[i, 0:HEAD_DIM, :] = v_ref[rows, :].astype(F32).T.astype(BF16)
        vT_ref[i, HEAD_DIM:VALUE_AUG, :] = v_extra
        gate = (jnp.dot(km_hi, qT, preferred_element_type=F32)
                + jnp.dot(km_lo, qT, preferred_element_type=F32)) * (1.0 / QUERY_SCALE)
        gate = jnp.where(blk < i, gate, NEG_INF)
        sel = jnp.zeros((nb, MOBA_BLOCK), F32)
        for _ in range(topk):
            best = jnp.max(gate, axis=0, keepdims=True)
            idx = jnp.min(jnp.where(gate == best, blk, nb), axis=0, keepdims=True)
            hit = blk == idx
            sel = jnp.where(hit & (blk < i), 1.0, sel)
            gate = jnp.where(hit, -jnp.inf, gate)
        sel_ref[i] = sel
        return carry

    lax.fori_loop(0, nb, body, 0, unroll=4 if nb % 4 == 0 else 1)


def _moba_gate(proj, slopes):
    S = proj.shape[0]
    nb = S // MOBA_BLOCK
    H = ATTN_HEADS
    q0 = POOL_WIDTH // HEAD_DIM
    k0 = q0 + H
    v0 = k0 + H
    blocked = lambda h: (h, 0, 0, 0)
    return pl.pallas_call(
        _moba_gate_kernel,
        grid=(H,),
        in_specs=[
            pl.BlockSpec((None, 1, 128), lambda h: (h, 0, 0)),
            pl.BlockSpec((S, HEAD_DIM), lambda h: (0, q0 + h)),
            pl.BlockSpec((S, HEAD_DIM), lambda h: (0, k0 + h)),
            pl.BlockSpec((S, HEAD_DIM), lambda h: (0, v0 + h)),
        ],
        out_specs=[
            pl.BlockSpec((None, nb, KEY_AUG, MOBA_BLOCK), blocked),
            pl.BlockSpec((None, S, KEY_AUG), lambda h: (h, 0, 0)),
            pl.BlockSpec((None, nb, VALUE_AUG, MOBA_BLOCK), blocked),
            pl.BlockSpec((None, nb, nb, MOBA_BLOCK), blocked),
        ],
        out_shape=[
            jax.ShapeDtypeStruct((H, nb, KEY_AUG, MOBA_BLOCK), BF16),
            jax.ShapeDtypeStruct((H, S, KEY_AUG), BF16),
            jax.ShapeDtypeStruct((H, nb, VALUE_AUG, MOBA_BLOCK), BF16),
            jax.ShapeDtypeStruct((H, nb, nb, MOBA_BLOCK), F32),
        ],
        compiler_params=_params(("parallel",)),
        name="moba_gate",
    )(slopes, proj, proj, proj)


def _moba_attn_kernel(slope_ref, qT_ref, qT_next_ref, sel_ref, k_ref, vT_ref, o_ref,
                      s0_ref, s1_ref, p0_ref, p1_ref, s_own_ref):
    i = pl.program_id(1)
    nb = sel_ref.shape[0]
    slope2 = slope_ref[...][:, 0:1] * LOG2_E
    qT = qT_ref[...]

    def block_of(t, u):
        return jnp.clip(t * ATTN_KV_UNROLL + u, 0, nb - 1)

    def key_rows(j):
        return pl.ds(pl.multiple_of(j * MOBA_BLOCK, MOBA_BLOCK), MOBA_BLOCK)

    def issue_scores(t, s_ref, queries=qT):
        for u in range(ATTN_KV_UNROLL):
            s_ref[u] = jnp.dot(k_ref[key_rows(block_of(t, u)), :], queries,
                               preferred_element_type=F32)

    def apply_probs(t, p_ref, alpha, acc):
        acc = alpha * acc
        for u in range(ATTN_KV_UNROLL):
            acc = acc + jnp.dot(vT_ref[block_of(t, u)], p_ref[u], preferred_element_type=F32)
        return acc

    def softmax_group(t, s_ref, p_ref, m):
        m_new = m
        shifts = []
        for u in range(ATTN_KV_UNROLL):
            j = t * ATTN_KV_UNROLL + u
            valid = jnp.where(j < i, sel_ref[pl.ds(block_of(t, u), 1), :], 0.0) > 0.0
            gap = slope2 * ((i - j) * MOBA_BLOCK).astype(F32)
            top = jnp.max(s_ref[u], axis=0, keepdims=True) - gap
            m_new = jnp.maximum(m_new, jnp.where(valid, top, NEG_INF))
            shifts.append((valid, gap))
        alpha = jnp.exp2(m - m_new)
        for u, (valid, gap) in enumerate(shifts):
            p = jnp.exp2(s_ref[u] - jnp.where(valid, m_new + gap, jnp.inf))
            p_ref[u] = p.astype(BF16)
        return m_new, alpha

    def body(r, carry):
        m, acc, alpha = carry
        acc = apply_probs(2 * r - 1, p1_ref, alpha, acc)
        m, alpha = softmax_group(2 * r, s0_ref, p0_ref, m)
        issue_scores(2 * r + 1, s1_ref)
        acc = apply_probs(2 * r, p0_ref, alpha, acc)
        m, alpha = softmax_group(2 * r + 1, s1_ref, p1_ref, m)
        issue_scores(2 * r + 2, s0_ref)
        return m, acc, alpha

    n_groups = (i + ATTN_KV_UNROLL - 1) // ATTN_KV_UNROLL
    n_pairs = n_groups // 2
    @pl.when(i == 0)
    def _():
        s0_ref[...] = jnp.zeros(s0_ref.shape, F32)
        s_own_ref[...] = jnp.dot(k_ref[key_rows(0), :], qT, preferred_element_type=F32)

    p1_ref[...] = jnp.zeros(p1_ref.shape, BF16)
    kpos = lax.broadcasted_iota(jnp.int32, (MOBA_BLOCK, MOBA_BLOCK), 0)
    qpos = lax.broadcasted_iota(jnp.int32, (MOBA_BLOCK, MOBA_BLOCK), 1)
    s_own = jnp.where(qpos >= kpos, s_own_ref[...], NEG_INF)
    top_own = jnp.max(s_own, axis=0, keepdims=True)
    row = jnp.zeros((1, MOBA_BLOCK), F32)
    init = (row + NEG_INF, jnp.zeros((VALUE_AUG, MOBA_BLOCK), F32), row + 1.0)
    m, acc, alpha = lax.fori_loop(0, n_pairs, body, init)
    acc = apply_probs(2 * n_pairs - 1, p1_ref, alpha, acc)
    m, alpha = softmax_group(2 * n_pairs, s0_ref, p0_ref, m)
    acc = apply_probs(2 * n_pairs, p0_ref, alpha, acc)
    m_new = jnp.maximum(m, top_own)
    alpha = jnp.exp2(m - m_new)
    p = jnp.exp2(s_own - m_new)
    acc = alpha * acc + jnp.dot(vT_ref[i], p.astype(BF16), preferred_element_type=F32)
    out = acc[0:HEAD_DIM] / acc[HEAD_DIM:HEAD_DIM + 1]
    o_ref[...] = out.T.astype(o_ref.dtype)
    q_next = qT_next_ref[...]
    issue_scores(0, s0_ref, q_next)
    s_own_ref[...] = jnp.dot(k_ref[key_rows(jnp.minimum(i + 1, nb - 1)), :], q_next,
                             preferred_element_type=F32)


def _moba_attention(k_aug, qT, vT, sel, slopes):
    H, S, _ = k_aug.shape
    nb = S // MOBA_BLOCK
    return pl.pallas_call(
        _moba_attn_kernel,
        grid=(H, nb),
        in_specs=[
            pl.BlockSpec((None, 1, 128), lambda h, i: (h, 0, 0)),
            pl.BlockSpec((None, None, KEY_AUG, MOBA_BLOCK), lambda h, i: (h, i, 0, 0)),
            pl.BlockSpec((None, None, KEY_AUG, MOBA_BLOCK),
                         lambda h, i: (h, jnp.minimum(i + 1, nb - 1), 0, 0)),
            pl.BlockSpec((None, None, nb, MOBA_BLOCK), lambda h, i: (h, i, 0, 0)),
            pl.BlockSpec((None, S, KEY_AUG), lambda h, i: (h, 0, 0)),
            pl.BlockSpec((None, nb, VALUE_AUG, MOBA_BLOCK), lambda h, i: (h, 0, 0, 0)),
        ],
        out_specs=pl.BlockSpec((MOBA_BLOCK, HEAD_DIM), lambda h, i: (i, h)),
        out_shape=jax.ShapeDtypeStruct((S, ATTN_WIDTH), BF16),
        scratch_shapes=[
            pltpu.VMEM((ATTN_KV_UNROLL, MOBA_BLOCK, MOBA_BLOCK), F32),
            pltpu.VMEM((ATTN_KV_UNROLL, MOBA_BLOCK, MOBA_BLOCK), F32),
            pltpu.VMEM((ATTN_KV_UNROLL, MOBA_BLOCK, MOBA_BLOCK), BF16),
            pltpu.VMEM((ATTN_KV_UNROLL, MOBA_BLOCK, MOBA_BLOCK), BF16),
            pltpu.VMEM((MOBA_BLOCK, MOBA_BLOCK), F32),
        ],
        compiler_params=_params(("arbitrary", "arbitrary")),
        name="moba_attention",
    )(slopes, qT, qT, sel, k_aug, vT)


def _first_lane_of_max(vals, lane):
    best = jnp.max(vals, axis=1, keepdims=True)
    idx = jnp.min(jnp.where(vals == best, lane, ROUTE_LANES), axis=1, keepdims=True)
    return best, idx


def _merge_route_kernel(yp_ref, ya_ref, glp_ref, gla_ref, x_ref, wbp_ref, wba_ref, wout_ref,
                        g_ref, wr_hi_ref, wr_lo_ref, br_ref, x1_ref, hs_ref, route_ref, cnt_ref):
    tm = x_ref.shape[0]
    cap = hs_ref.shape[0]
    d_model = x_ref.shape[1]
    bp = jnp.dot(yp_ref[...], wbp_ref[...], preferred_element_type=F32)
    ba = jnp.dot(ya_ref[...], wba_ref[...], preferred_element_type=F32)
    merged = (jax.nn.sigmoid(glp_ref[...].astype(F32)) * bp
              + jax.nn.sigmoid(gla_ref[...].astype(F32)) * ba)
    x1 = x_ref[...] + jnp.dot(merged.astype(BF16), wout_ref[...], preferred_element_type=F32)
    x1_ref[...] = x1
    ms = jnp.mean(x1 * x1, axis=-1, keepdims=True)
    h2 = x1 * lax.rsqrt(ms + RMS_EPS) * g_ref[...]
    h2_hi = h2.astype(BF16)
    h2_lo = (h2 - h2_hi.astype(F32)).astype(BF16)
    logits = (jnp.dot(h2_hi, wr_hi_ref[...], preferred_element_type=F32)
              + jnp.dot(h2_lo, wr_hi_ref[...], preferred_element_type=F32)
              + jnp.dot(h2_hi, wr_lo_ref[...], preferred_element_type=F32)
              + br_ref[...])

    lane = lax.broadcasted_iota(jnp.int32, logits.shape, 1)
    g_logits = jnp.where(lane < N_GROUPS, logits, -jnp.inf)
    g_best, g_idx = _first_lane_of_max(g_logits, lane)
    g_w = 1.0 / jnp.sum(jnp.exp(g_logits - g_best), axis=1, keepdims=True)
    e_lo = EXPERT_LANE0 + EXPERTS_PER_GROUP * g_idx
    e_logits = jnp.where((lane >= e_lo) & (lane < e_lo + EXPERTS_PER_GROUP), logits, -jnp.inf)
    v1, i1 = _first_lane_of_max(e_logits, lane)
    v2, i2 = _first_lane_of_max(jnp.where(lane == i1, -jnp.inf, e_logits), lane)
    e21 = jnp.exp(v2 - v1)
    w1 = g_w / (1.0 + e21)
    w2 = g_w * e21 / (1.0 + e21)

    hit1 = lane == i1
    hit2 = lane == i2
    member = jnp.where(hit1, 1.0, jnp.where(hit2, 1.0, 0.0))
    r_tok = lax.broadcasted_iota(jnp.int32, (tm, tm), 0)
    c_tok = lax.broadcasted_iota(jnp.int32, (tm, tm), 1)
    earlier = jnp.where(c_tok < r_tok, 1.0, 0.0).astype(BF16)
    rank = jnp.dot(earlier, member.astype(BF16), preferred_element_type=F32)
    count = jnp.sum(member, axis=0, keepdims=True)
    chunks = jnp.floor((count + (MOE_CHUNK - 1)) * (1.0 / MOE_CHUNK))
    r_l = lax.broadcasted_iota(jnp.int32, (ROUTE_LANES, ROUTE_LANES), 0)
    c_l = lax.broadcasted_iota(jnp.int32, (ROUTE_LANES, ROUTE_LANES), 1)
    lower_lanes = jnp.where(r_l < c_l, 1.0, 0.0).astype(BF16)
    start = jnp.dot(jnp.broadcast_to(chunks, (8, ROUTE_LANES)).astype(BF16), lower_lanes,
                    preferred_element_type=F32)[0:1] * MOE_CHUNK
    pos = start + rank
    pos1 = jnp.sum(jnp.where(hit1, pos, 0.0), axis=1, keepdims=True)
    pos2 = jnp.sum(jnp.where(hit2, pos, 0.0), axis=1, keepdims=True)
    route = jnp.where(lane == 0, pos1, jnp.where(lane == 1, pos2, 0.0))
    route_ref[...] = route
    cnt_ref[...] = chunks
    route_t = route.T
    slot = lax.broadcasted_iota(jnp.int32, (cap, tm), 0)
    first = jnp.where(slot == route_t[0:1, :].astype(jnp.int32), 1.0, 0.0).astype(BF16)
    second = jnp.where(slot == route_t[1:2, :].astype(jnp.int32), 1.0, 0.0).astype(BF16)
    hs_ref[:, 0:d_model] = jnp.dot(first + second, h2_hi,
                                   preferred_element_type=F32).astype(BF16)

    def weight_pieces(w):
        pieces = jnp.zeros(lane.shape, F32)
        rest = w
        for n in range(MOE_WEIGHT_TERMS):
            piece = rest.astype(BF16).astype(F32)
            pieces = jnp.where(lane == n, piece, pieces)
            rest = rest - piece
        return pieces.astype(BF16)

    slot_w = (jnp.dot(first, weight_pieces(w1), preferred_element_type=F32)
              + jnp.dot(second, weight_pieces(w2), preferred_element_type=F32))
    hs_ref[:, d_model:] = slot_w.astype(BF16)


def _moe_cap(tm):
    worst = 2 * tm + N_EXPERTS * (MOE_CHUNK - 1)
    return -(-worst // 128) * 128


def _merge_route(yp, ya, proj, x, wbp, wba, wout, g, wr_hi, wr_lo, br, tm=MOE_TOKEN_TILE):
    S, D = x.shape
    nT = S // tm
    cap = _moe_cap(tm)
    full = lambda i: (0, 0)
    row = lambda i: (i, 0)
    glp_blk = (POOL_WIDTH + 3 * ATTN_WIDTH) // D
    return pl.pallas_call(
        _merge_route_kernel,
        grid=(S // tm,),
        in_specs=[
            pl.BlockSpec((tm, POOL_WIDTH), row),
            pl.BlockSpec((tm, ATTN_WIDTH), row),
            pl.BlockSpec((tm, D), lambda i: (i, glp_blk)),
            pl.BlockSpec((tm, D), lambda i: (i, glp_blk + 1)),
            pl.BlockSpec((tm, D), row),
            pl.BlockSpec((POOL_WIDTH, D), full),
            pl.BlockSpec((ATTN_WIDTH, D), full),
            pl.BlockSpec((D, D), full),
            pl.BlockSpec((1, D), full),
            pl.BlockSpec((D, ROUTE_LANES), full),
            pl.BlockSpec((D, ROUTE_LANES), full),
            pl.BlockSpec((1, ROUTE_LANES), full),
        ],
        out_specs=[
            pl.BlockSpec((tm, D), row),
            pl.BlockSpec((None, cap, D + ROUTE_LANES), lambda i: (i, 0, 0)),
            pl.BlockSpec((tm, ROUTE_LANES), row),
            pl.BlockSpec((None, 1, ROUTE_LANES), lambda i: (i, 0, 0)),
        ],
        out_shape=[
            jax.ShapeDtypeStruct((S, D), F32),
            jax.ShapeDtypeStruct((nT, cap, D + ROUTE_LANES), BF16),
            jax.ShapeDtypeStruct((S, ROUTE_LANES), F32),
            jax.ShapeDtypeStruct((nT, 1, ROUTE_LANES), F32),
        ],
        compiler_params=_params(("parallel",)),
        name="merge_route",
    )(yp, ya, proj, proj, x, wbp, wba, wout, g, wr_hi, wr_lo, br)


def _moe_plan(chunk_counts, n_row_tiles, n_chunk_slots):
    nT, E = chunk_counts.shape
    per_expert = chunk_counts.T
    seg_start = (jnp.cumsum(chunk_counts, axis=1) - chunk_counts).T
    seg_end = jnp.cumsum(per_expert, axis=1)
    n_chunks = seg_end[:, -1]
    padded = -(-n_chunks // MOE_CHUNKS_PER_TILE) * MOE_CHUNKS_PER_TILE
    e_end = jnp.cumsum(padded)
    n_used = e_end[-1] // MOE_CHUNKS_PER_TILE
    c = jnp.arange(n_chunk_slots, dtype=jnp.int32)
    e_of_c = jnp.minimum((e_end[None, :] <= c[:, None]).sum(axis=1), E - 1)
    is_e = (e_of_c[:, None] == jnp.arange(E)[None, :]).astype(jnp.int32)
    local = c - (is_e * (e_end - padded)[None, :]).sum(axis=1)
    real = local < (is_e * n_chunks[None, :]).sum(axis=1)
    pick_e = lambda table: (is_e[:, :, None] * table[None, :, :]).sum(axis=1)
    seg_end_c = pick_e(seg_end)
    t_of_c = jnp.minimum((seg_end_c <= local[:, None]).sum(axis=1), nT - 1)
    is_t = (t_of_c[:, None] == jnp.arange(nT)[None, :]).astype(jnp.int32)
    pick_t = lambda rows: (rows * is_t).sum(axis=1)
    within = local - (pick_t(seg_end_c) - pick_t(pick_e(per_expert)))
    src_tile = jnp.where(real, t_of_c, 0).astype(jnp.int32)
    src_row = jnp.where(real, (pick_t(pick_e(seg_start)) + within) * MOE_CHUNK, 0).astype(jnp.int32)
    tile = jnp.arange(n_row_tiles, dtype=jnp.int32)
    first = jnp.minimum(tile, n_used - 1) * MOE_CHUNKS_PER_TILE
    tile_expert = jnp.minimum((e_end[None, :] <= first[:, None]).sum(axis=1), E - 1)
    tile_real = real.reshape(n_row_tiles, MOE_CHUNKS_PER_TILE).sum(axis=1)
    experts = jnp.arange(E)
    nonempty = padded > 0
    buffer_of_e = jnp.cumsum(nonempty) - nonempty
    later = (experts[None, :] > experts[:, None]) & nonempty[None, :]
    next_of_e = jnp.min(jnp.where(later, experts[None, :], E), axis=1)
    next_of_e = jnp.where(next_of_e < E, next_of_e, -1)
    is_next = (next_of_e[:, None] == experts[None, :]).astype(jnp.int32)
    after_of_e = jnp.where(next_of_e >= 0, (is_next * next_of_e[None, :]).sum(axis=1), -1)
    is_te = (tile_expert[:, None] == experts[None, :]).astype(jnp.int32)
    at = lambda table: (is_te * table[None, :]).sum(axis=1)
    run_len = jnp.maximum(at(padded) // MOE_CHUNKS_PER_TILE, 1)
    run_pos = tile - at(e_end - padded) // MOE_CHUNKS_PER_TILE
    next_expert = at(next_of_e)
    streams = (tile < n_used) & (next_expert >= 0)
    piece_lo = jnp.where(streams, (MOE_WEIGHT_PIECES * run_pos) // run_len, 0)
    piece_hi = jnp.where(streams, (MOE_WEIGHT_PIECES * (run_pos + 1)) // run_len, 0)
    as_i32 = lambda a: a.astype(jnp.int32)
    return (as_i32(tile_expert), as_i32(tile_real), src_tile, src_row, as_i32(n_used.reshape(1)),
            as_i32(at(buffer_of_e) % 2), as_i32(next_expert), as_i32(at(after_of_e)),
            as_i32(piece_lo), as_i32(piece_hi), as_i32(chunk_counts.sum(axis=1)))


def _moe_ffn_kernel(texp_ref, treal_ref, ctile_ref, crow_ref, nused_ref,
                    wbuf_ref, wnext_ref, wafter_ref, plo_ref, phi_ref, tused_ref,
                    hs_hbm, wg_hbm, wu_hbm, wd_hbm, ys_hbm,
                    xbuf, ybuf, zbuf, wg_buf, wu_buf, wd_buf, stage_g, stage_u, stage_d,
                    gather_sem, scatter_sem, weight_sem, zero_sem):
    i = pl.program_id(0)
    n_used = nused_ref[0]
    slot = lax.rem(i, 2)
    rows_gu = stage_g.shape[1]
    rows_d = stage_d.shape[1]

    def piece_copies(e, p, s):
        gu_rows = pl.ds(pl.multiple_of(p * rows_gu, rows_gu), rows_gu)
        d_rows = pl.ds(pl.multiple_of(p * rows_d, rows_d), rows_d)
        return (pltpu.make_async_copy(wg_hbm.at[e, gu_rows, :], stage_g.at[s], weight_sem.at[s]),
                pltpu.make_async_copy(wu_hbm.at[e, gu_rows, :], stage_u.at[s], weight_sem.at[s]),
                pltpu.make_async_copy(wd_hbm.at[e, d_rows, :], stage_d.at[s], weight_sem.at[s]))

    def start_piece(e, p):
        for copy in piece_copies(e, p, lax.rem(p, MOE_WEIGHT_STAGES)):
            copy.start()

    def finish_piece(e, p, side):
        s = lax.rem(p, MOE_WEIGHT_STAGES)
        for copy in piece_copies(e, p, s):
            copy.wait()
        gu_rows = pl.ds(pl.multiple_of(p * rows_gu, rows_gu), rows_gu)
        d_rows = pl.ds(pl.multiple_of(p * rows_d, rows_d), rows_d)
        wg_buf[side, gu_rows, :] = stage_g[s].astype(BF16)
        wu_buf[side, gu_rows, :] = stage_u[s].astype(BF16)
        wd_buf[side, d_rows, :] = stage_d[s].astype(BF16)

    ahead = MOE_WEIGHT_STAGES - 1

    def start_first_pieces(e):
        for p in range(ahead):
            start_piece(e, p)

    def stream_pieces(e, lo, hi, side, following):
        @pl.when(hi > lo)
        def _():
            def body(p, carry):
                @pl.when(p + ahead < MOE_WEIGHT_PIECES)
                def _():
                    start_piece(e, p + ahead)
                finish_piece(e, p, side)
                return carry
            lax.fori_loop(lo, hi, body, 0)

            @pl.when((hi == MOE_WEIGHT_PIECES) & (following >= 0))
            def _():
                start_first_pieces(following)

    def chunk_rows(c):
        return pl.ds(pl.multiple_of(c * MOE_CHUNK, MOE_CHUNK), MOE_CHUNK)

    def gather_copy(tile, c, buf):
        g = tile * MOE_CHUNKS_PER_TILE + c
        src = hs_hbm.at[ctile_ref[g], pl.ds(pl.multiple_of(crow_ref[g], MOE_CHUNK), MOE_CHUNK), :]
        return pltpu.make_async_copy(src, xbuf.at[buf, chunk_rows(c), :], gather_sem.at[buf])

    def scatter_copy(tile, c, buf):
        g = tile * MOE_CHUNKS_PER_TILE + c
        dst = ys_hbm.at[ctile_ref[g], pl.ds(pl.multiple_of(crow_ref[g], MOE_CHUNK), MOE_CHUNK), :]
        return pltpu.make_async_copy(ybuf.at[buf, chunk_rows(c), :], dst, scatter_sem.at[buf])

    def for_real_chunks(tile, fn):
        n_real = treal_ref[tile]
        for c in range(MOE_CHUNKS_PER_TILE):
            pl.when(c < n_real)(functools.partial(fn, c))

    def for_tail_chunks(fn):
        n_token_tiles, cap = ys_hbm.shape[0], ys_hbm.shape[1]

        def per_tile(t, carry):
            def per_chunk(c, carry):
                fn(pltpu.make_async_copy(zbuf, ys_hbm.at[t, chunk_rows(c), :], zero_sem.at[0]))
                return carry
            return lax.fori_loop(tused_ref[t], cap // MOE_CHUNK, per_chunk, carry)
        lax.fori_loop(0, n_token_tiles, per_tile, 0)

    @pl.when(i == 0)
    def _():
        zbuf[...] = jnp.zeros(zbuf.shape, zbuf.dtype)
        for_tail_chunks(lambda copy: copy.start())
        xbuf[...] = jnp.zeros(xbuf.shape, xbuf.dtype)
        for_real_chunks(0, lambda c: gather_copy(0, c, 0).start())
        start_first_pieces(texp_ref[0])
        stream_pieces(texp_ref[0], 0, MOE_WEIGHT_PIECES, wbuf_ref[0], wnext_ref[0])

    @pl.when(i + 1 < n_used)
    def _():
        for_real_chunks(i + 1, lambda c: gather_copy(i + 1, c, 1 - slot).start())

    side = wbuf_ref[i]
    stream_pieces(wnext_ref[i], plo_ref[i], phi_ref[i], 1 - side, wafter_ref[i])

    @pl.when(i < n_used)
    def _():
        for_real_chunks(i, lambda c: gather_copy(i, c, slot).wait())

        @pl.when(i >= 2)
        def _():
            for_real_chunks(i - 2, lambda c: scatter_copy(i - 2, c, slot).wait())

        d_model = wg_buf.shape[1]
        x = xbuf[slot, :, 0:d_model]
        w = jnp.sum(xbuf[slot, :, d_model:].astype(F32), axis=1, keepdims=True)
        a = jnp.dot(x, wg_buf[side], preferred_element_type=F32)
        u = jnp.dot(x, wu_buf[side], preferred_element_type=F32)
        hid = (jax.nn.silu(a) * u * w).astype(BF16)
        ybuf[slot] = jnp.dot(hid, wd_buf[side], preferred_element_type=F32).astype(BF16)
        for_real_chunks(i, lambda c: scatter_copy(i, c, slot).start())

    @pl.when(i == n_used - 1)
    def _():
        @pl.when(i >= 1)
        def _():
            for_real_chunks(i - 1, lambda c: scatter_copy(i - 1, c, 1 - slot).wait())
        for_real_chunks(i, lambda c: scatter_copy(i, c, slot).wait())
        for_tail_chunks(lambda copy: copy.wait())


def _moe_ffn(hs, plan, wg, wu, wd):
    nT, cap, row_width = hs.shape
    E, D, F = wg.shape
    n_row_tiles = plan[0].shape[0]
    assert D % MOE_WEIGHT_PIECES == 0 and F % MOE_WEIGHT_PIECES == 0
    grid_spec = pltpu.PrefetchScalarGridSpec(
        num_scalar_prefetch=len(plan),
        grid=(n_row_tiles,),
        in_specs=[pl.BlockSpec(memory_space=pl.ANY)] * 4,
        out_specs=pl.BlockSpec(memory_space=pl.ANY),
        scratch_shapes=[
            pltpu.VMEM((2, MOE_ROW_TILE, row_width), BF16),
            pltpu.VMEM((2, MOE_ROW_TILE, D), BF16),
            pltpu.VMEM((MOE_CHUNK, D), BF16),
            pltpu.VMEM((2, D, F), BF16),
            pltpu.VMEM((2, D, F), BF16),
            pltpu.VMEM((2, F, D), BF16),
            pltpu.VMEM((MOE_WEIGHT_STAGES, D // MOE_WEIGHT_PIECES, F), F32),
            pltpu.VMEM((MOE_WEIGHT_STAGES, D // MOE_WEIGHT_PIECES, F), F32),
            pltpu.VMEM((MOE_WEIGHT_STAGES, F // MOE_WEIGHT_PIECES, D), F32),
            pltpu.SemaphoreType.DMA((2,)),
            pltpu.SemaphoreType.DMA((2,)),
            pltpu.SemaphoreType.DMA((MOE_WEIGHT_STAGES,)),
            pltpu.SemaphoreType.DMA((1,)),
        ],
    )
    return pl.pallas_call(
        _moe_ffn_kernel,
        grid_spec=grid_spec,
        out_shape=jax.ShapeDtypeStruct((nT, cap, D), BF16),
        compiler_params=_params(("arbitrary",)),
        name="moe_ffn",
    )(*plan, hs, wg, wu, wd)


def _moe_combine_kernel(ys_ref, route_ref, x1_ref, g_ref, o_ref):
    tm = x1_ref.shape[0]
    cap = ys_ref.shape[0]
    route = route_ref[...]
    slot = lax.broadcasted_iota(jnp.int32, (tm, cap), 1)
    pick = jnp.where(slot == route[:, 0:1].astype(jnp.int32), 1.0,
                     jnp.where(slot == route[:, 1:2].astype(jnp.int32), 1.0, 0.0)).astype(BF16)
    y = x1_ref[...] + jnp.dot(pick, ys_ref[...], preferred_element_type=F32)
    ms = jnp.mean(y * y, axis=-1, keepdims=True)
    o_ref[...] = y * lax.rsqrt(ms + RMS_EPS) * g_ref[...]


def _moe_combine(ys, route, x1, g, tm=MOE_TOKEN_TILE):
    S, D = x1.shape
    cap = ys.shape[1]
    row = lambda i: (i, 0)
    return pl.pallas_call(
        _moe_combine_kernel,
        grid=(S // tm,),
        in_specs=[
            pl.BlockSpec((None, cap, D), lambda i: (i, 0, 0)),
            pl.BlockSpec((tm, ROUTE_LANES), row),
            pl.BlockSpec((tm, D), row),
            pl.BlockSpec((1, D), lambda i: (0, 0)),
        ],
        out_specs=pl.BlockSpec((tm, D), row),
        out_shape=jax.ShapeDtypeStruct((S, D), F32),
        compiler_params=_params(("parallel",)),
        name="moe_combine",
    )(ys, route, x1, g)


def _router_weights(w_r_group, b_r_group, w_r_expert, b_r_expert):
    D = w_r_group.shape[0]
    w = jnp.concatenate(
        [w_r_group, jnp.transpose(w_r_expert, (1, 0, 2)).reshape(D, N_EXPERTS)], axis=1)
    b = jnp.concatenate([b_r_group, b_r_expert.reshape(N_EXPERTS)])
    pad = ROUTE_LANES - w.shape[1]
    w = jnp.pad(w, ((0, 0), (0, pad)))
    b = jnp.pad(b, (0, pad)).reshape(1, ROUTE_LANES)
    w_hi = w.astype(BF16)
    w_lo = (w - w_hi.astype(F32)).astype(BF16)
    return w_hi, w_lo, b


def kernel(x, norm_mix, w_in, w_pool, pool_scale, w_branch_pool, w_branch_attn, w_out, norm_ffn,
           w_r_group, b_r_group, w_r_expert, b_r_expert, w_gate, w_up, w_down, norm_final):
    B, S, D = x.shape
    depth = w_in.shape[0]
    assert depth == 1, "the final rms_norm is fused into the expert kernel of a single layer"
    slopes = jnp.exp2(-8.0 * jnp.arange(1, ATTN_HEADS + 1, dtype=F32) / ATTN_HEADS)
    slopes = jnp.broadcast_to(slopes[:, None, None], (ATTN_HEADS, 1, 128))
    outs = []
    for b in range(B):
        xb = x[b]
        for l in range(depth):
            proj = _norm_inproj(xb, norm_mix[l].reshape(1, D), w_in[l])
            y_pool = _pool_mixer(proj, w_pool[l].astype(BF16), pool_scale[l].reshape(1, POOL_WIDTH))
            qT, k_aug, vT, sel = _moba_gate(proj, slopes)
            y_attn = _moba_attention(k_aug, qT, vT, sel, slopes)
            wr_hi, wr_lo, br = _router_weights(w_r_group[l], b_r_group[l], w_r_expert[l], b_r_expert[l])
            x1, hs, route, cnt = _merge_route(
                y_pool, y_attn, proj, xb,
                w_branch_pool[l].astype(BF16), w_branch_attn[l].astype(BF16), w_out[l].astype(BF16),
                norm_ffn[l].reshape(1, D), wr_hi, wr_lo, br)
            nT = hs.shape[0]
            chunk_counts = cnt[:, 0, EXPERT_LANE0:EXPERT_LANE0 + N_EXPERTS].astype(jnp.int32)
            max_chunks = nT * ((2 * MOE_TOKEN_TILE + N_EXPERTS * (MOE_CHUNK - 1)) // MOE_CHUNK)
            n_row_tiles = -(-(max_chunks + N_EXPERTS * (MOE_CHUNKS_PER_TILE - 1))
                            // MOE_CHUNKS_PER_TILE)
            plan = _moe_plan(chunk_counts, n_row_tiles, n_row_tiles * MOE_CHUNKS_PER_TILE)
            ys = _moe_ffn(hs, plan, w_gate[l], w_up[l], w_down[l])
            xb = _moe_combine(ys, route, x1, norm_final.reshape(1, D))
        outs.append(xb)
    return jnp.stack(outs, axis=0)
```

```python
import functools

import jax
import jax.numpy as jnp
from jax import lax
from jax.experimental import pallas as pl
from jax.experimental.pallas import tpu as pltpu

F32 = jnp.float32
BF16 = jnp.bfloat16

POOL_WINDOWS = (2, 4, 8, 16)
MAX_WINDOW = 16
POOL_WIDTH = 1024
POOL_GROUP = 256
HEAD_DIM = 128
ATTN_HEADS = 8
ATTN_WIDTH = 1024
MOBA_BLOCK = 256
MOBA_TOPK = 3
N_GROUPS = 4
EXPERTS_PER_GROUP = 4
N_EXPERTS = 16
ROUTE_LANES = 128
EXPERT_LANE0 = N_GROUPS
RMS_EPS = 1e-6
NEG_INF = -1e30
LOG2_E = 1.4426950408889634
QUERY_SCALE = (HEAD_DIM ** -0.5) * LOG2_E
KEY_AUG = 2 * HEAD_DIM
ALIBI_TERMS = 3
VALUE_AUG = HEAD_DIM + 16
ATTN_KV_UNROLL = 3
MOE_CHUNK = 16
MOE_TOKEN_TILE = 256
ROUTE_TILES_PER_STEP = 4
MOE_ROW_TILE = 256
MOE_CHUNKS_PER_TILE = MOE_ROW_TILE // MOE_CHUNK
MOE_WEIGHT_TERMS = 3
MOE_WEIGHT_PIECES = 8
MOE_WEIGHT_STAGES = 4
assert MOE_WEIGHT_STAGES - 1 <= MOE_WEIGHT_PIECES

V7X_VMEM_LIMIT_BYTES = 56 * 1024 * 1024


def _params(semantics, vmem=V7X_VMEM_LIMIT_BYTES, flags=None):
    return pltpu.CompilerParams(dimension_semantics=semantics, vmem_limit_bytes=vmem, flags=flags)


def _norm_inproj_kernel(x_ref, g_ref, w_ref, o_ref, wb_ref):
    @pl.when(pl.program_id(1) == 0)
    def _():
        wb_ref[...] = w_ref[...].astype(BF16)

    x = x_ref[...]
    ms = jnp.mean(x * x, axis=-1, keepdims=True)
    h = (x * lax.rsqrt(ms + RMS_EPS) * g_ref[...]).astype(BF16)
    col0 = pl.program_id(0) * o_ref.shape[1]
    is_q = (col0 >= POOL_WIDTH) & (col0 < POOL_WIDTH + ATTN_WIDTH)
    factor = jnp.where(is_q, QUERY_SCALE, 1.0).astype(F32)
    o_ref[...] = (jnp.dot(h, wb_ref[...], preferred_element_type=F32) * factor).astype(o_ref.dtype)


def _norm_inproj(x, g, w, tm=1024, tn=1024):
    S, D = x.shape
    N = w.shape[1]
    assert POOL_WIDTH % tn == 0 and ATTN_WIDTH % tn == 0
    return pl.pallas_call(
        _norm_inproj_kernel,
        grid=(N // tn, S // tm),
        in_specs=[
            pl.BlockSpec((tm, D), lambda j, i: (i, 0)),
            pl.BlockSpec((1, D), lambda j, i: (0, 0)),
            pl.BlockSpec((D, tn), lambda j, i: (0, j)),
        ],
        out_specs=pl.BlockSpec((tm, tn), lambda j, i: (i, j)),
        out_shape=jax.ShapeDtypeStruct((S, N), BF16),
        scratch_shapes=[pltpu.VMEM((D, tn), BF16)],
        compiler_params=_params(("parallel", "arbitrary")),
        name="norm_inproj",
    )(x, g, w)


def _pool_kernel(cur_ref, prev_ref, w_ref, scale_ref, o_ref, ext_ref):
    i = pl.program_id(0)
    tm = cur_ref.shape[0]
    u = cur_ref[...].astype(F32)
    halo = jnp.where(i > 0, prev_ref[...].astype(F32), 0.0)
    ext_ref[0:MAX_WINDOW, :] = halo
    ext_ref[MAX_WINDOW:MAX_WINDOW + tm, :] = u
    t = i * tm + lax.broadcasted_iota(jnp.int32, (tm, 1), 0)
    for g, w in enumerate(POOL_WINDOWS):
        cols = slice(g * POOL_GROUP, (g + 1) * POOL_GROUP)
        ug = u[:, cols]
        wsum = ug
        for s in range(1, w):
            wsum = wsum + ext_ref[MAX_WINDOW - s:MAX_WINDOW - s + tm, cols]
        cnt = jnp.minimum(t + 1, w).astype(F32)
        mixed = (wsum / cnt - ug).astype(BF16)
        y = jnp.dot(mixed, w_ref[g], preferred_element_type=F32)
        o_ref[:, cols] = (y * scale_ref[:, cols]).astype(o_ref.dtype)


def _pool_mixer(proj, w_pool, pool_scale, tm=512):
    S = proj.shape[0]
    halo_blocks = tm // MAX_WINDOW
    return pl.pallas_call(
        _pool_kernel,
        grid=(S // tm,),
        in_specs=[
            pl.BlockSpec((tm, POOL_WIDTH), lambda i: (i, 0)),
            pl.BlockSpec((MAX_WINDOW, POOL_WIDTH),
                         lambda i: (jnp.maximum(i * halo_blocks - 1, 0), 0)),
            pl.BlockSpec((len(POOL_WINDOWS), POOL_GROUP, POOL_GROUP), lambda i: (0, 0, 0)),
            pl.BlockSpec((1, POOL_WIDTH), lambda i: (0, 0)),
        ],
        out_specs=pl.BlockSpec((tm, POOL_WIDTH), lambda i: (i, 0)),
        out_shape=jax.ShapeDtypeStruct((S, POOL_WIDTH), BF16),
        scratch_shapes=[pltpu.VMEM((tm + MAX_WINDOW, POOL_WIDTH), F32)],
        compiler_params=_params(("parallel",)),
        name="pool_mixer",
    )(proj, proj, w_pool, pool_scale)


def _moba_gate_kernel(slope_ref, q_ref, k_ref, v_ref, qT_ref, ka_ref, vT_ref, sel_ref):
    S = q_ref.shape[0]
    nb = S // MOBA_BLOCK
    topk = min(MOBA_TOPK, nb)
    kf = k_ref[...].astype(F32).reshape(nb, MOBA_BLOCK, HEAD_DIM)
    kmean = jnp.sum(kf, axis=1) * (1.0 / MOBA_BLOCK)
    km_hi = kmean.astype(BF16)
    km_lo = (kmean - km_hi.astype(F32)).astype(BF16)
    blk = lax.broadcasted_iota(jnp.int32, (nb, MOBA_BLOCK), 0)

    pos = lax.broadcasted_iota(jnp.int32, (MOBA_BLOCK, KEY_AUG - HEAD_DIM), 0).astype(F32)
    col = lax.broadcasted_iota(jnp.int32, (MOBA_BLOCK, KEY_AUG - HEAD_DIM), 1)
    k_extra = jnp.where(col < ALIBI_TERMS, pos, 0.0).astype(BF16)

    slope2 = slope_ref[...][:, 0:1] * LOG2_E
    q_extra = jnp.zeros((KEY_AUG - HEAD_DIM, MOBA_BLOCK), F32)
    row = lax.broadcasted_iota(jnp.int32, q_extra.shape, 0)
    rest = slope2
    for n in range(ALIBI_TERMS):
        piece = rest.astype(BF16).astype(F32)
        q_extra = jnp.where(row == n, piece, q_extra)
        rest = rest - piece
    q_extra = q_extra.astype(BF16)
    v_row = lax.broadcasted_iota(jnp.int32, (VALUE_AUG - HEAD_DIM, MOBA_BLOCK), 0)
    v_extra = jnp.where(v_row == 0, 1.0, 0.0).astype(BF16)

    def body(i, carry):
        rows = pl.ds(pl.multiple_of(i * MOBA_BLOCK, MOBA_BLOCK), MOBA_BLOCK)
        qT = q_ref[rows, :].astype(F32).T.astype(BF16)
        qT_ref[i, 0:HEAD_DIM, :] = qT
        qT_ref[i, HEAD_DIM:KEY_AUG, :] = q_extra
        ka_ref[rows, 0:HEAD_DIM] = k_ref[rows, :]
        ka_ref[rows, HEAD_DIM:KEY_AUG] = k_extra
        vT_ref[i, 0:HEAD_DIM, :] = v_ref[rows, :].astype(F32).T.astype(BF16)
        vT_ref[i, HEAD_DIM:VALUE_AUG, :] = v_extra
        gate = (jnp.dot(km_hi, qT, preferred_element_type=F32)
                + jnp.dot(km_lo, qT, preferred_element_type=F32)) * (1.0 / QUERY_SCALE)
        gate = jnp.where(blk < i, gate, NEG_INF)
        sel = jnp.zeros((nb, MOBA_BLOCK), F32)
        for _ in range(topk):
            best = jnp.max(gate, axis=0, keepdims=True)
            idx = jnp.min(jnp.where(gate == best, blk, nb), axis=0, keepdims=True)
            hit = blk == idx
            sel = jnp.where(hit & (blk < i), 1.0, sel)
            gate = jnp.where(hit, -jnp.inf, gate)
        sel_ref[i] = sel
        return carry

    lax.fori_loop(0, nb, body, 0, unroll=4 if nb % 4 == 0 else 1)


def _moba_gate(proj, slopes):
    S = proj.shape[0]
    nb = S // MOBA_BLOCK
    H = ATTN_HEADS
    q0 = POOL_WIDTH // HEAD_DIM
    k0 = q0 + H
    v0 = k0 + H
    blocked = lambda h: (h, 0, 0, 0)
    return pl.pallas_call(
        _moba_gate_kernel,
        grid=(H,),
        in_specs=[
            pl.BlockSpec((None, 1, 128), lambda h: (h, 0, 0)),
            pl.BlockSpec((S, HEAD_DIM), lambda h: (0, q0 + h)),
            pl.BlockSpec((S, HEAD_DIM), lambda h: (0, k0 + h)),
            pl.BlockSpec((S, HEAD_DIM), lambda h: (0, v0 + h)),
        ],
        out_specs=[
            pl.BlockSpec((None, nb, KEY_AUG, MOBA_BLOCK), blocked),
            pl.BlockSpec((None, S, KEY_AUG), lambda h: (h, 0, 0)),
            pl.BlockSpec((None, nb, VALUE_AUG, MOBA_BLOCK), blocked),
            pl.BlockSpec((None, nb, nb, MOBA_BLOCK), blocked),
        ],
        out_shape=[
            jax.ShapeDtypeStruct((H, nb, KEY_AUG, MOBA_BLOCK), BF16),
            jax.ShapeDtypeStruct((H, S, KEY_AUG), BF16),
            jax.ShapeDtypeStruct((H, nb, VALUE_AUG, MOBA_BLOCK), BF16),
            jax.ShapeDtypeStruct((H, nb, nb, MOBA_BLOCK), F32),
        ],
        compiler_params=_params(("parallel",)),
        name="moba_gate",
    )(slopes, proj, proj, proj)


def _moba_attn_kernel(slope_ref, qT_ref, qT_next_ref, sel_ref, k_ref, vT_ref, o_ref,
                      s0_ref, s1_ref, p0_ref, p1_ref, s_own_ref):
    i = pl.program_id(1)
    nb = sel_ref.shape[0]
    slope2 = slope_ref[...][:, 0:1] * LOG2_E
    qT = qT_ref[...]

    def block_of(t, u):
        return jnp.clip(t * ATTN_KV_UNROLL + u, 0, nb - 1)

    def key_rows(j):
        return pl.ds(pl.multiple_of(j * MOBA_BLOCK, MOBA_BLOCK), MOBA_BLOCK)

    def issue_scores(t, s_ref, queries=qT):
        for u in range(ATTN_KV_UNROLL):
            s_ref[u] = jnp.dot(k_ref[key_rows(block_of(t, u)), :], queries,
                               preferred_element_type=F32)

    def apply_probs(t, p_ref, alpha, acc):
        acc = alpha * acc
        for u in range(ATTN_KV_UNROLL):
            acc = acc + jnp.dot(vT_ref[block_of(t, u)], p_ref[u], preferred_element_type=F32)
        return acc

    def softmax_group(t, s_ref, p_ref, m):
        m_new = m
        shifts = []
        for u in range(ATTN_KV_UNROLL):
            j = t * ATTN_KV_UNROLL + u
            valid = jnp.where(j < i, sel_ref[pl.ds(block_of(t, u), 1), :], 0.0) > 0.0
            gap = slope2 * ((i - j) * MOBA_BLOCK).astype(F32)
            top = jnp.max(s_ref[u], axis=0, keepdims=True) - gap
            m_new = jnp.maximum(m_new, jnp.where(valid, top, NEG_INF))
            shifts.append((valid, gap))
        alpha = jnp.exp2(m - m_new)
        for u, (valid, gap) in enumerate(shifts):
            p = jnp.exp2(s_ref[u] - jnp.where(valid, m_new + gap, jnp.inf))
            p_ref[u] = p.astype(BF16)
        return m_new, alpha

    def body(r, carry):
        m, acc, alpha = carry
        acc = apply_probs(2 * r - 1, p1_ref, alpha, acc)
        m, alpha = softmax_group(2 * r, s0_ref, p0_ref, m)
        issue_scores(2 * r + 1, s1_ref)
        acc = apply_probs(2 * r, p0_ref, alpha, acc)
        m, alpha = softmax_group(2 * r + 1, s1_ref, p1_ref, m)
        issue_scores(2 * r + 2, s0_ref)
        return m, acc, alpha

    n_groups = (i + ATTN_KV_UNROLL - 1) // ATTN_KV_UNROLL
    n_pairs = n_groups // 2
    @pl.when(i == 0)
    def _():
        s0_ref[...] = jnp.zeros(s0_ref.shape, F32)
        s_own_ref[...] = jnp.dot(k_ref[key_rows(0), :], qT, preferred_element_type=F32)

    p1_ref[...] = jnp.zeros(p1_ref.shape, BF16)
    kpos = lax.broadcasted_iota(jnp.int32, (MOBA_BLOCK, MOBA_BLOCK), 0)
    qpos = lax.broadcasted_iota(jnp.int32, (MOBA_BLOCK, MOBA_BLOCK), 1)
    s_own = jnp.where(qpos >= kpos, s_own_ref[...], NEG_INF)
    top_own = jnp.max(s_own, axis=0, keepdims=True)
    row = jnp.zeros((1, MOBA_BLOCK), F32)
    init = (row + NEG_INF, jnp.zeros((VALUE_AUG, MOBA_BLOCK), F32), row + 1.0)
    m, acc, alpha = lax.fori_loop(0, n_pairs, body, init)
    acc = apply_probs(2 * n_pairs - 1, p1_ref, alpha, acc)
    m, alpha = softmax_group(2 * n_pairs, s0_ref, p0_ref, m)
    acc = apply_probs(2 * n_pairs, p0_ref, alpha, acc)
    m_new = jnp.maximum(m, top_own)
    alpha = jnp.exp2(m - m_new)
    p = jnp.exp2(s_own - m_new)
    acc = alpha * acc + jnp.dot(vT_ref[i], p.astype(BF16), preferred_element_type=F32)
    out = acc[0:HEAD_DIM] / acc[HEAD_DIM:HEAD_DIM + 1]
    o_ref[...] = out.T.astype(o_ref.dtype)
    q_next = qT_next_ref[...]
    issue_scores(0, s0_ref, q_next)
    s_own_ref[...] = jnp.dot(k_ref[key_rows(jnp.minimum(i + 1, nb - 1)), :], q_next,
                             preferred_element_type=F32)


def _moba_attention(k_aug, qT, vT, sel, slopes):
    H, S, _ = k_aug.shape
    nb = S // MOBA_BLOCK
    return pl.pallas_call(
        _moba_attn_kernel,
        grid=(H, nb),
        in_specs=[
            pl.BlockSpec((None, 1, 128), lambda h, i: (h, 0, 0)),
            pl.BlockSpec((None, None, KEY_AUG, MOBA_BLOCK), lambda h, i: (h, i, 0, 0)),
            pl.BlockSpec((None, None, KEY_AUG, MOBA_BLOCK),
                         lambda h, i: (h, jnp.minimum(i + 1, nb - 1), 0, 0)),
            pl.BlockSpec((None, None, nb, MOBA_BLOCK), lambda h, i: (h, i, 0, 0)),
            pl.BlockSpec((None, S, KEY_AUG), lambda h, i: (h, 0, 0)),
            pl.BlockSpec((None, nb, VALUE_AUG, MOBA_BLOCK), lambda h, i: (h, 0, 0, 0)),
        ],
        out_specs=pl.BlockSpec((MOBA_BLOCK, HEAD_DIM), lambda h, i: (i, h)),
        out_shape=jax.ShapeDtypeStruct((S, ATTN_WIDTH), BF16),
        scratch_shapes=[
            pltpu.VMEM((ATTN_KV_UNROLL, MOBA_BLOCK, MOBA_BLOCK), F32),
            pltpu.VMEM((ATTN_KV_UNROLL, MOBA_BLOCK, MOBA_BLOCK), F32),
            pltpu.VMEM((ATTN_KV_UNROLL, MOBA_BLOCK, MOBA_BLOCK), BF16),
            pltpu.VMEM((ATTN_KV_UNROLL, MOBA_BLOCK, MOBA_BLOCK), BF16),
            pltpu.VMEM((MOBA_BLOCK, MOBA_BLOCK), F32),
        ],
        compiler_params=_params(("arbitrary", "arbitrary")),
        name="moba_attention",
    )(slopes, qT, qT, sel, k_aug, vT)


def _first_lane_of_max(vals, lane):
    best = jnp.max(vals, axis=1, keepdims=True)
    idx = jnp.min(jnp.where(vals == best, lane, ROUTE_LANES), axis=1, keepdims=True)
    return best, idx


def _merge_out_kernel(yp_ref, ya_ref, glp_ref, gla_ref, x_ref, wbp_ref, wba_ref, wout_ref, x1_ref):
    bp = jnp.dot(yp_ref[...], wbp_ref[...], preferred_element_type=F32)
    ba = jnp.dot(ya_ref[...], wba_ref[...], preferred_element_type=F32)
    merged = (jax.nn.sigmoid(glp_ref[...].astype(F32)) * bp
              + jax.nn.sigmoid(gla_ref[...].astype(F32)) * ba)
    x1_ref[...] = x_ref[...] + jnp.dot(merged.astype(BF16), wout_ref[...],
                                       preferred_element_type=F32)


def _merge_out(yp, ya, proj, x, wbp, wba, wout, tm=512):
    S, D = x.shape
    row = lambda i: (i, 0)
    glp_blk = (POOL_WIDTH + 3 * ATTN_WIDTH) // D
    resident = lambda shape: pl.BlockSpec(shape, lambda i: (0, 0), pipeline_mode=pl.Buffered(1))
    return pl.pallas_call(
        _merge_out_kernel,
        grid=(S // tm,),
        in_specs=[
            pl.BlockSpec((tm, POOL_WIDTH), row),
            pl.BlockSpec((tm, ATTN_WIDTH), row),
            pl.BlockSpec((tm, D), lambda i: (i, glp_blk)),
            pl.BlockSpec((tm, D), lambda i: (i, glp_blk + 1)),
            pl.BlockSpec((tm, D), row),
            resident((POOL_WIDTH, D)),
            resident((ATTN_WIDTH, D)),
            resident((D, D)),
        ],
        out_specs=pl.BlockSpec((tm, D), row),
        out_shape=jax.ShapeDtypeStruct((S, D), F32),
        compiler_params=_params(("parallel",)),
        name="merge_out",
    )(yp, ya, proj, proj, x, wbp, wba, wout)


def _route_sort_kernel(x1_ref, g_ref, wr_hi_ref, wr_lo_ref, br_ref, hs_ref, route_ref, cnt_ref):
    n_tiles = ROUTE_TILES_PER_STEP
    tm = x1_ref.shape[0] // n_tiles
    cap = hs_ref.shape[1]
    d_model = x1_ref.shape[1]
    lane = lax.broadcasted_iota(jnp.int32, (tm, ROUTE_LANES), 1)
    tiles = [{"index": t, "rows": slice(t * tm, (t + 1) * tm)} for t in range(n_tiles)]

    def normalise(s):
        x1 = x1_ref[s["rows"], :]
        ms = jnp.mean(x1 * x1, axis=-1, keepdims=True)
        h2 = x1 * lax.rsqrt(ms + RMS_EPS) * g_ref[...]
        s["h2_hi"] = h2.astype(BF16)
        s["h2_lo"] = (h2 - s["h2_hi"].astype(F32)).astype(BF16)

    def router_logits(s):
        s["logits"] = (jnp.dot(s["h2_hi"], wr_hi_ref[...], preferred_element_type=F32)
                       + jnp.dot(s["h2_lo"], wr_hi_ref[...], preferred_element_type=F32)
                       + jnp.dot(s["h2_hi"], wr_lo_ref[...], preferred_element_type=F32)
                       + br_ref[...])

    def choose_experts(s):
        logits = s["logits"]
        g_logits = jnp.where(lane < N_GROUPS, logits, -jnp.inf)
        g_best, g_idx = _first_lane_of_max(g_logits, lane)
        g_w = 1.0 / jnp.sum(jnp.exp(g_logits - g_best), axis=1, keepdims=True)
        e_lo = EXPERT_LANE0 + EXPERTS_PER_GROUP * g_idx
        e_logits = jnp.where((lane >= e_lo) & (lane < e_lo + EXPERTS_PER_GROUP), logits, -jnp.inf)
        v1, i1 = _first_lane_of_max(e_logits, lane)
        v2, i2 = _first_lane_of_max(jnp.where(lane == i1, -jnp.inf, e_logits), lane)
        e21 = jnp.exp(v2 - v1)
        s["w1"] = g_w / (1.0 + e21)
        s["w2"] = g_w * e21 / (1.0 + e21)
        s["hit1"] = lane == i1
        s["hit2"] = lane == i2

    def sort_by_expert(s):
        hit1, hit2 = s["hit1"], s["hit2"]
        member = jnp.where(hit1, 1.0, jnp.where(hit2, 1.0, 0.0))
        r_tok = lax.broadcasted_iota(jnp.int32, (tm, tm), 0)
        c_tok = lax.broadcasted_iota(jnp.int32, (tm, tm), 1)
        earlier = jnp.where(c_tok < r_tok, 1.0, 0.0).astype(BF16)
        rank = jnp.dot(earlier, member.astype(BF16), preferred_element_type=F32)
        count = jnp.sum(member, axis=0, keepdims=True)
        chunks = jnp.floor((count + (MOE_CHUNK - 1)) * (1.0 / MOE_CHUNK))
        r_l = lax.broadcasted_iota(jnp.int32, (ROUTE_LANES, ROUTE_LANES), 0)
        c_l = lax.broadcasted_iota(jnp.int32, (ROUTE_LANES, ROUTE_LANES), 1)
        lower_lanes = jnp.where(r_l < c_l, 1.0, 0.0).astype(BF16)
        start = jnp.dot(jnp.broadcast_to(chunks, (8, ROUTE_LANES)).astype(BF16), lower_lanes,
                        preferred_element_type=F32)[0:1] * MOE_CHUNK
        pos = start + rank
        pos1 = jnp.sum(jnp.where(hit1, pos, 0.0), axis=1, keepdims=True)
        pos2 = jnp.sum(jnp.where(hit2, pos, 0.0), axis=1, keepdims=True)
        s["route"] = jnp.where(lane == 0, pos1, jnp.where(lane == 1, pos2, 0.0))
        route_ref[s["rows"], :] = s["route"]
        cnt_ref[s["index"]] = chunks

    def weight_pieces(w):
        pieces = jnp.zeros(lane.shape, F32)
        rest = w
        for n in range(MOE_WEIGHT_TERMS):
            piece = rest.astype(BF16).astype(F32)
            pieces = jnp.where(lane == n, piece, pieces)
            rest = rest - piece
        return pieces.astype(BF16)

    def compact(s):
        route_t = s["route"].T
        slot = lax.broadcasted_iota(jnp.int32, (cap, tm), 0)
        first = jnp.where(slot == route_t[0:1, :].astype(jnp.int32), 1.0, 0.0).astype(BF16)
        second = jnp.where(slot == route_t[1:2, :].astype(jnp.int32), 1.0, 0.0).astype(BF16)
        hs_ref[s["index"], :, 0:d_model] = jnp.dot(
            first + second, s["h2_hi"], preferred_element_type=F32).astype(BF16)
        slot_w = (jnp.dot(first, weight_pieces(s["w1"]), preferred_element_type=F32)
                  + jnp.dot(second, weight_pieces(s["w2"]), preferred_element_type=F32))
        hs_ref[s["index"], :, d_model:] = slot_w.astype(BF16)

    for stage in (normalise, router_logits, choose_experts, sort_by_expert, compact):
        for s in tiles:
            stage(s)


def _moe_cap(tm):
    worst = 2 * tm + N_EXPERTS * (MOE_CHUNK - 1)
    return -(-worst // 128) * 128


def _route_sort(x1, g, wr_hi, wr_lo, br, tm=MOE_TOKEN_TILE):
    S, D = x1.shape
    nT = S // tm
    cap = _moe_cap(tm)
    full = lambda i: (0, 0)
    row = lambda i: (i, 0)
    per_step = ROUTE_TILES_PER_STEP
    return pl.pallas_call(
        _route_sort_kernel,
        grid=(nT // per_step,),
        in_specs=[
            pl.BlockSpec((per_step * tm, D), row),
            pl.BlockSpec((1, D), full),
            pl.BlockSpec((D, ROUTE_LANES), full),
            pl.BlockSpec((D, ROUTE_LANES), full),
            pl.BlockSpec((1, ROUTE_LANES), full),
        ],
        out_specs=[
            pl.BlockSpec((per_step, cap, D + ROUTE_LANES), lambda i: (i, 0, 0)),
            pl.BlockSpec((per_step * tm, ROUTE_LANES), row),
            pl.BlockSpec((per_step, 1, ROUTE_LANES), lambda i: (i, 0, 0)),
        ],
        out_shape=[
            jax.ShapeDtypeStruct((nT, cap, D + ROUTE_LANES), BF16),
            jax.ShapeDtypeStruct((S, ROUTE_LANES), F32),
            jax.ShapeDtypeStruct((nT, 1, ROUTE_LANES), F32),
        ],
        compiler_params=_params(("parallel",)),
        name="route_sort",
    )(x1, g, wr_hi, wr_lo, br)


def _moe_plan(chunk_counts, n_row_tiles, n_chunk_slots):
    nT, E = chunk_counts.shape
    per_expert = chunk_counts.T
    seg_start = (jnp.cumsum(chunk_counts, axis=1) - chunk_counts).T
    seg_end = jnp.cumsum(per_expert, axis=1)
    n_chunks = seg_end[:, -1]
    padded = -(-n_chunks // MOE_CHUNKS_PER_TILE) * MOE_CHUNKS_PER_TILE
    e_end = jnp.cumsum(padded)
    n_used = e_end[-1] // MOE_CHUNKS_PER_TILE
    c = jnp.arange(n_chunk_slots, dtype=jnp.int32)
    e_of_c = jnp.minimum((e_end[None, :] <= c[:, None]).sum(axis=1), E - 1)
    is_e = (e_of_c[:, None] == jnp.arange(E)[None, :]).astype(jnp.int32)
    local = c - (is_e * (e_end - padded)[None, :]).sum(axis=1)
    real = local < (is_e * n_chunks[None, :]).sum(axis=1)
    pick_e = lambda table: (is_e[:, :, None] * table[None, :, :]).sum(axis=1)
    seg_end_c = pick_e(seg_end)
    t_of_c = jnp.minimum((seg_end_c <= local[:, None]).sum(axis=1), nT - 1)
    is_t = (t_of_c[:, None] == jnp.arange(nT)[None, :]).astype(jnp.int32)
    pick_t = lambda rows: (rows * is_t).sum(axis=1)
    within = local - (pick_t(seg_end_c) - pick_t(pick_e(per_expert)))
    src_tile = jnp.where(real, t_of_c, 0).astype(jnp.int32)
    src_row = jnp.where(real, (pick_t(pick_e(seg_start)) + within) * MOE_CHUNK, 0).astype(jnp.int32)
    tile = jnp.arange(n_row_tiles, dtype=jnp.int32)
    first = jnp.minimum(tile, n_used - 1) * MOE_CHUNKS_PER_TILE
    tile_expert = jnp.minimum((e_end[None, :] <= first[:, None]).sum(axis=1), E - 1)
    tile_real = real.reshape(n_row_tiles, MOE_CHUNKS_PER_TILE).sum(axis=1)
    experts = jnp.arange(E)
    nonempty = padded > 0
    buffer_of_e = jnp.cumsum(nonempty) - nonempty
    later = (experts[None, :] > experts[:, None]) & nonempty[None, :]
    next_of_e = jnp.min(jnp.where(later, experts[None, :], E), axis=1)
    next_of_e = jnp.where(next_of_e < E, next_of_e, -1)
    is_next = (next_of_e[:, None] == experts[None, :]).astype(jnp.int32)
    after_of_e = jnp.where(next_of_e >= 0, (is_next * next_of_e[None, :]).sum(axis=1), -1)
    is_te = (tile_expert[:, None] == experts[None, :]).astype(jnp.int32)
    at = lambda table: (is_te * table[None, :]).sum(axis=1)
    run_len = jnp.maximum(at(padded) // MOE_CHUNKS_PER_TILE, 1)
    run_pos = tile - at(e_end - padded) // MOE_CHUNKS_PER_TILE
    next_expert = at(next_of_e)
    streams = (tile < n_used) & (next_expert >= 0)
    piece_lo = jnp.where(streams, (MOE_WEIGHT_PIECES * run_pos) // run_len, 0)
    piece_hi = jnp.where(streams, (MOE_WEIGHT_PIECES * (run_pos + 1)) // run_len, 0)
    as_i32 = lambda a: a.astype(jnp.int32)
    return (as_i32(tile_expert), as_i32(tile_real), src_tile, src_row, as_i32(n_used.reshape(1)),
            as_i32(at(buffer_of_e) % 2), as_i32(next_expert), as_i32(at(after_of_e)),
            as_i32(piece_lo), as_i32(piece_hi), as_i32(chunk_counts.sum(axis=1)))


def _moe_ffn_kernel(texp_ref, treal_ref, ctile_ref, crow_ref, nused_ref,
                    wbuf_ref, wnext_ref, wafter_ref, plo_ref, phi_ref, tused_ref,
                    hs_hbm, wg_hbm, wu_hbm, wd_hbm, ys_hbm,
                    xbuf, ybuf, zbuf, wg_buf, wu_buf, wd_buf, stage_g, stage_u, stage_d,
                    gather_sem, scatter_sem, weight_sem, zero_sem):
    i = pl.program_id(0)
    n_used = nused_ref[0]
    slot = lax.rem(i, 2)
    rows_gu = stage_g.shape[1]
    rows_d = stage_d.shape[1]

    def piece_copies(e, p, s):
        gu_rows = pl.ds(pl.multiple_of(p * rows_gu, rows_gu), rows_gu)
        d_rows = pl.ds(pl.multiple_of(p * rows_d, rows_d), rows_d)
        return (pltpu.make_async_copy(wg_hbm.at[e, gu_rows, :], stage_g.at[s], weight_sem.at[s]),
                pltpu.make_async_copy(wu_hbm.at[e, gu_rows, :], stage_u.at[s], weight_sem.at[s]),
                pltpu.make_async_copy(wd_hbm.at[e, d_rows, :], stage_d.at[s], weight_sem.at[s]))

    def start_piece(e, p):
        for copy in piece_copies(e, p, lax.rem(p, MOE_WEIGHT_STAGES)):
            copy.start()

    def finish_piece(e, p, side):
        s = lax.rem(p, MOE_WEIGHT_STAGES)
        for copy in piece_copies(e, p, s):
            copy.wait()
        gu_rows = pl.ds(pl.multiple_of(p * rows_gu, rows_gu), rows_gu)
        d_rows = pl.ds(pl.multiple_of(p * rows_d, rows_d), rows_d)
        wg_buf[side, gu_rows, :] = stage_g[s].astype(BF16)
        wu_buf[side, gu_rows, :] = stage_u[s].astype(BF16)
        wd_buf[side, d_rows, :] = stage_d[s].astype(BF16)

    ahead = MOE_WEIGHT_STAGES - 1

    def start_first_pieces(e):
        for p in range(ahead):
            start_piece(e, p)

    def stream_pieces(e, lo, hi, side, following):
        @pl.when(hi > lo)
        def _():
            def body(p, carry):
                @pl.when(p + ahead < MOE_WEIGHT_PIECES)
                def _():
                    start_piece(e, p + ahead)
                finish_piece(e, p, side)
                return carry
            lax.fori_loop(lo, hi, body, 0)

            @pl.when((hi == MOE_WEIGHT_PIECES) & (following >= 0))
            def _():
                start_first_pieces(following)

    def chunk_rows(c):
        return pl.ds(pl.multiple_of(c * MOE_CHUNK, MOE_CHUNK), MOE_CHUNK)

    def gather_copy(tile, c, buf):
        g = tile * MOE_CHUNKS_PER_TILE + c
        src = hs_hbm.at[ctile_ref[g], pl.ds(pl.multiple_of(crow_ref[g], MOE_CHUNK), MOE_CHUNK), :]
        return pltpu.make_async_copy(src, xbuf.at[buf, chunk_rows(c), :], gather_sem.at[buf])

    def scatter_copy(tile, c, buf):
        g = tile * MOE_CHUNKS_PER_TILE + c
        dst = ys_hbm.at[ctile_ref[g], pl.ds(pl.multiple_of(crow_ref[g], MOE_CHUNK), MOE_CHUNK), :]
        return pltpu.make_async_copy(ybuf.at[buf, chunk_rows(c), :], dst, scatter_sem.at[buf])

    def for_real_chunks(tile, fn):
        n_real = treal_ref[tile]
        for c in range(MOE_CHUNKS_PER_TILE):
            pl.when(c < n_real)(functools.partial(fn, c))

    def for_tail_chunks(fn):
        n_token_tiles, cap = ys_hbm.shape[0], ys_hbm.shape[1]

        def per_tile(t, carry):
            def per_chunk(c, carry):
                fn(pltpu.make_async_copy(zbuf, ys_hbm.at[t, chunk_rows(c), :], zero_sem.at[0]))
                return carry
            return lax.fori_loop(tused_ref[t], cap // MOE_CHUNK, per_chunk, carry)
        lax.fori_loop(0, n_token_tiles, per_tile, 0)

    @pl.when(i == 0)
    def _():
        zbuf[...] = jnp.zeros(zbuf.shape, zbuf.dtype)
        for_tail_chunks(lambda copy: copy.start())
        xbuf[...] = jnp.zeros(xbuf.shape, xbuf.dtype)
        for_real_chunks(0, lambda c: gather_copy(0, c, 0).start())
        start_first_pieces(texp_ref[0])
        stream_pieces(texp_ref[0], 0, MOE_WEIGHT_PIECES, wbuf_ref[0], wnext_ref[0])

    @pl.when(i + 1 < n_used)
    def _():
        for_real_chunks(i + 1, lambda c: gather_copy(i + 1, c, 1 - slot).start())

    side = wbuf_ref[i]
    stream_pieces(wnext_ref[i], plo_ref[i], phi_ref[i], 1 - side, wafter_ref[i])

    @pl.when(i < n_used)
    def _():
        for_real_chunks(i, lambda c: gather_copy(i, c, slot).wait())

        @pl.when(i >= 2)
        def _():
            for_real_chunks(i - 2, lambda c: scatter_copy(i - 2, c, slot).wait())

        d_model = wg_buf.shape[1]
        x = xbuf[slot, :, 0:d_model]
        w = jnp.sum(xbuf[slot, :, d_model:].astype(F32), axis=1, keepdims=True)
        a = jnp.dot(x, wg_buf[side], preferred_element_type=F32)
        u = jnp.dot(x, wu_buf[side], preferred_element_type=F32)
        hid = (jax.nn.silu(a) * u * w).astype(BF16)
        ybuf[slot] = jnp.dot(hid, wd_buf[side], preferred_element_type=F32).astype(BF16)
        for_real_chunks(i, lambda c: scatter_copy(i, c, slot).start())

    @pl.when(i == n_used - 1)
    def _():
        @pl.when(i >= 1)
        def _():
            for_real_chunks(i - 1, lambda c: scatter_copy(i - 1, c, 1 - slot).wait())
        for_real_chunks(i, lambda c: scatter_copy(i, c, slot).wait())
        for_tail_chunks(lambda copy: copy.wait())


def _moe_ffn(hs, plan, wg, wu, wd):
    nT, cap, row_width = hs.shape
    E, D, F = wg.shape
    n_row_tiles = plan[0].shape[0]
    assert D % MOE_WEIGHT_PIECES == 0 and F % MOE_WEIGHT_PIECES == 0
    grid_spec = pltpu.PrefetchScalarGridSpec(
        num_scalar_prefetch=len(plan),
        grid=(n_row_tiles,),
        in_specs=[pl.BlockSpec(memory_space=pl.ANY)] * 4,
        out_specs=pl.BlockSpec(memory_space=pl.ANY),
        scratch_shapes=[
            pltpu.VMEM((2, MOE_ROW_TILE, row_width), BF16),
            pltpu.VMEM((2, MOE_ROW_TILE, D), BF16),
            pltpu.VMEM((MOE_CHUNK, D), BF16),
            pltpu.VMEM((2, D, F), BF16),
            pltpu.VMEM((2, D, F), BF16),
            pltpu.VMEM((2, F, D), BF16),
            pltpu.VMEM((MOE_WEIGHT_STAGES, D // MOE_WEIGHT_PIECES, F), F32),
            pltpu.VMEM((MOE_WEIGHT_STAGES, D // MOE_WEIGHT_PIECES, F), F32),
            pltpu.VMEM((MOE_WEIGHT_STAGES, F // MOE_WEIGHT_PIECES, D), F32),
            pltpu.SemaphoreType.DMA((2,)),
            pltpu.SemaphoreType.DMA((2,)),
            pltpu.SemaphoreType.DMA((MOE_WEIGHT_STAGES,)),
            pltpu.SemaphoreType.DMA((1,)),
        ],
    )
    return pl.pallas_call(
        _moe_ffn_kernel,
        grid_spec=grid_spec,
        out_shape=jax.ShapeDtypeStruct((nT, cap, D), BF16),
        compiler_params=_params(("arbitrary",)),
        name="moe_ffn",
    )(*plan, hs, wg, wu, wd)


def _moe_combine_kernel(ys_ref, route_ref, x1_ref, g_ref, o_ref):
    tm = x1_ref.shape[0]
    cap = ys_ref.shape[0]
    route = route_ref[...]
    slot = lax.broadcasted_iota(jnp.int32, (tm, cap), 1)
    pick = jnp.where(slot == route[:, 0:1].astype(jnp.int32), 1.0,
                     jnp.where(slot == route[:, 1:2].astype(jnp.int32), 1.0, 0.0)).astype(BF16)
    y = x1_ref[...] + jnp.dot(pick, ys_ref[...], preferred_element_type=F32)
    ms = jnp.mean(y * y, axis=-1, keepdims=True)
    o_ref[...] = y * lax.rsqrt(ms + RMS_EPS) * g_ref[...]


def _moe_combine(ys, route, x1, g, tm=MOE_TOKEN_TILE):
    S, D = x1.shape
    cap = ys.shape[1]
    row = lambda i: (i, 0)
    return pl.pallas_call(
        _moe_combine_kernel,
        grid=(S // tm,),
        in_specs=[
            pl.BlockSpec((None, cap, D), lambda i: (i, 0, 0)),
            pl.BlockSpec((tm, ROUTE_LANES), row),
            pl.BlockSpec((tm, D), row),
            pl.BlockSpec((1, D), lambda i: (0, 0)),
        ],
        out_specs=pl.BlockSpec((tm, D), row),
        out_shape=jax.ShapeDtypeStruct((S, D), F32),
        compiler_params=_params(("parallel",)),
        name="moe_combine",
    )(ys, route, x1, g)


def _router_weights(w_r_group, b_r_group, w_r_expert, b_r_expert):
    D = w_r_group.shape[0]
    w = jnp.concatenate(
        [w_r_group, jnp.transpose(w_r_expert, (1, 0, 2)).reshape(D, N_EXPERTS)], axis=1)
    b = jnp.concatenate([b_r_group, b_r_expert.reshape(N_EXPERTS)])
    pad = ROUTE_LANES - w.shape[1]
    w = jnp.pad(w, ((0, 0), (0, pad)))
    b = jnp.pad(b, (0, pad)).reshape(1, ROUTE_LANES)
    w_hi = w.astype(BF16)
    w_lo = (w - w_hi.astype(F32)).astype(BF16)
    return w_hi, w_lo, b


def kernel(x, norm_mix, w_in, w_pool, pool_scale, w_branch_pool, w_branch_attn, w_out, norm_ffn,
           w_r_group, b_r_group, w_r_expert, b_r_expert, w_gate, w_up, w_down, norm_final):
    B, S, D = x.shape
    depth = w_in.shape[0]
    assert depth == 1, "the final rms_norm is fused into the expert kernel of a single layer"
    slopes = jnp.exp2(-8.0 * jnp.arange(1, ATTN_HEADS + 1, dtype=F32) / ATTN_HEADS)
    slopes = jnp.broadcast_to(slopes[:, None, None], (ATTN_HEADS, 1, 128))
    outs = []
    for b in range(B):
        xb = x[b]
        for l in range(depth):
            proj = _norm_inproj(xb, norm_mix[l].reshape(1, D), w_in[l])
            y_pool = _pool_mixer(proj, w_pool[l].astype(BF16), pool_scale[l].reshape(1, POOL_WIDTH))
            qT, k_aug, vT, sel = _moba_gate(proj, slopes)
            y_attn = _moba_attention(k_aug, qT, vT, sel, slopes)
            wr_hi, wr_lo, br = _router_weights(w_r_group[l], b_r_group[l], w_r_expert[l], b_r_expert[l])
            x1 = _merge_out(y_pool, y_attn, proj, xb, w_branch_pool[l].astype(BF16),
                            w_branch_attn[l].astype(BF16), w_out[l].astype(BF16))
            hs, route, cnt = _route_sort(x1, norm_ffn[l].reshape(1, D), wr_hi, wr_lo, br)
            nT = hs.shape[0]
            chunk_counts = cnt[:, 0, EXPERT_LANE0:EXPERT_LANE0 + N_EXPERTS].astype(jnp.int32)
            max_chunks = nT * ((2 * MOE_TOKEN_TILE + N_EXPERTS * (MOE_CHUNK - 1)) // MOE_CHUNK)
            n_row_tiles = -(-(max_chunks + N_EXPERTS * (MOE_CHUNKS_PER_TILE - 1))
                            // MOE_CHUNKS_PER_TILE)
            plan = _moe_plan(chunk_counts, n_row_tiles, n_row_tiles * MOE_CHUNKS_PER_TILE)
            ys = _moe_ffn(hs, plan, w_gate[l], w_up[l], w_down[l])
            xb = _moe_combine(ys, route, x1, norm_final.reshape(1, D))
        outs.append(xb)
    return jnp.stack(outs, axis=0)
```

```python
import functools

import jax
import jax.numpy as jnp
from jax import lax
from jax.experimental import pallas as pl
from jax.experimental.pallas import tpu as pltpu

F32 = jnp.float32
BF16 = jnp.bfloat16

POOL_WINDOWS = (2, 4, 8, 16)
MAX_WINDOW = 16
POOL_WIDTH = 1024
POOL_GROUP = 256
HEAD_DIM = 128
ATTN_HEADS = 8
ATTN_WIDTH = 1024
MOBA_BLOCK = 256
MOBA_TOPK = 3
N_GROUPS = 4
EXPERTS_PER_GROUP = 4
N_EXPERTS = 16
ROUTE_LANES = 128
EXPERT_LANE0 = N_GROUPS
RMS_EPS = 1e-6
NEG_INF = -1e30
LOG2_E = 1.4426950408889634
QUERY_SCALE = (HEAD_DIM ** -0.5) * LOG2_E
KEY_AUG = 2 * HEAD_DIM
ALIBI_TERMS = 3
VALUE_AUG = HEAD_DIM + 16
ATTN_KV_UNROLL = 3
MOE_CHUNK = 16
MOE_TOKEN_TILE = 256
ROUTE_TILES_PER_STEP = 4
MOE_ROW_TILE = 256
MOE_CHUNKS_PER_TILE = MOE_ROW_TILE // MOE_CHUNK
MOE_WEIGHT_TERMS = 3
MOE_WEIGHT_PIECES = 8
MOE_WEIGHT_STAGES = 4
assert MOE_WEIGHT_STAGES - 1 <= MOE_WEIGHT_PIECES

V7X_VMEM_LIMIT_BYTES = 56 * 1024 * 1024


def _params(semantics, vmem=V7X_VMEM_LIMIT_BYTES, flags=None):
    return pltpu.CompilerParams(dimension_semantics=semantics, vmem_limit_bytes=vmem, flags=flags)


def _rms_norm_kernel(x_ref, g_ref, o_ref):
    x = x_ref[...]
    ms = jnp.mean(x * x, axis=-1, keepdims=True)
    o_ref[...] = (x * lax.rsqrt(ms + RMS_EPS) * g_ref[...]).astype(o_ref.dtype)


def _rms_norm(x, g, tm=512):
    S, D = x.shape
    return pl.pallas_call(
        _rms_norm_kernel,
        grid=(S // tm,),
        in_specs=[pl.BlockSpec((tm, D), lambda i: (i, 0)), pl.BlockSpec((1, D), lambda i: (0, 0))],
        out_specs=pl.BlockSpec((tm, D), lambda i: (i, 0)),
        out_shape=jax.ShapeDtypeStruct((S, D), BF16),
        compiler_params=_params(("parallel",)),
        name="rms_norm",
    )(x, g)


def _inproj_kernel(h_ref, w_ref, o_ref, wb_ref):
    @pl.when(pl.program_id(1) == 0)
    def _():
        wb_ref[...] = w_ref[...].astype(BF16)

    col0 = pl.program_id(0) * o_ref.shape[1]
    is_q = (col0 >= POOL_WIDTH) & (col0 < POOL_WIDTH + ATTN_WIDTH)
    factor = jnp.where(is_q, QUERY_SCALE, 1.0).astype(F32)
    o_ref[...] = (jnp.dot(h_ref[...], wb_ref[...], preferred_element_type=F32)
                  * factor).astype(o_ref.dtype)


def _inproj(h, w, tm=2048, tn=1024):
    S, D = h.shape
    N = w.shape[1]
    tm = min(tm, S)
    assert POOL_WIDTH % tn == 0 and ATTN_WIDTH % tn == 0
    return pl.pallas_call(
        _inproj_kernel,
        grid=(N // tn, S // tm),
        in_specs=[
            pl.BlockSpec((tm, D), lambda j, i: (i, 0)),
            pl.BlockSpec((D, tn), lambda j, i: (0, j)),
        ],
        out_specs=pl.BlockSpec((tm, tn), lambda j, i: (i, j)),
        out_shape=jax.ShapeDtypeStruct((S, N), BF16),
        scratch_shapes=[pltpu.VMEM((D, tn), BF16)],
        compiler_params=_params(("parallel", "arbitrary")),
        name="inproj",
    )(h, w)


def _pool_kernel(cur_ref, prev_ref, w_ref, scale_ref, o_ref, ext_ref):
    i = pl.program_id(0)
    tm = cur_ref.shape[0]
    u = cur_ref[...].astype(F32)
    halo = jnp.where(i > 0, prev_ref[...].astype(F32), 0.0)
    ext_ref[0:MAX_WINDOW, :] = halo
    ext_ref[MAX_WINDOW:MAX_WINDOW + tm, :] = u
    t = i * tm + lax.broadcasted_iota(jnp.int32, (tm, 1), 0)
    for g, w in enumerate(POOL_WINDOWS):
        cols = slice(g * POOL_GROUP, (g + 1) * POOL_GROUP)
        ug = u[:, cols]
        wsum = ug
        for s in range(1, w):
            wsum = wsum + ext_ref[MAX_WINDOW - s:MAX_WINDOW - s + tm, cols]
        cnt = jnp.minimum(t + 1, w).astype(F32)
        mixed = (wsum / cnt - ug).astype(BF16)
        y = jnp.dot(mixed, w_ref[g], preferred_element_type=F32)
        o_ref[:, cols] = (y * scale_ref[:, cols]).astype(o_ref.dtype)


def _pool_mixer(proj, w_pool, pool_scale, tm=512):
    S = proj.shape[0]
    halo_blocks = tm // MAX_WINDOW
    return pl.pallas_call(
        _pool_kernel,
        grid=(S // tm,),
        in_specs=[
            pl.BlockSpec((tm, POOL_WIDTH), lambda i: (i, 0)),
            pl.BlockSpec((MAX_WINDOW, POOL_WIDTH),
                         lambda i: (jnp.maximum(i * halo_blocks - 1, 0), 0)),
            pl.BlockSpec((len(POOL_WINDOWS), POOL_GROUP, POOL_GROUP), lambda i: (0, 0, 0)),
            pl.BlockSpec((1, POOL_WIDTH), lambda i: (0, 0)),
        ],
        out_specs=pl.BlockSpec((tm, POOL_WIDTH), lambda i: (i, 0)),
        out_shape=jax.ShapeDtypeStruct((S, POOL_WIDTH), BF16),
        scratch_shapes=[pltpu.VMEM((tm + MAX_WINDOW, POOL_WIDTH), F32)],
        compiler_params=_params(("parallel",)),
        name="pool_mixer",
    )(proj, proj, w_pool, pool_scale)


def _moba_gate_kernel(slope_ref, q_ref, k_ref, v_ref, qT_ref, ka_ref, vT_ref, sel_ref):
    S = q_ref.shape[0]
    nb = S // MOBA_BLOCK
    topk = min(MOBA_TOPK, nb)
    kf = k_ref[...].astype(F32).reshape(nb, MOBA_BLOCK, HEAD_DIM)
    kmean = jnp.sum(kf, axis=1) * (1.0 / MOBA_BLOCK)
    km_hi = kmean.astype(BF16)
    km_lo = (kmean - km_hi.astype(F32)).astype(BF16)
    blk = lax.broadcasted_iota(jnp.int32, (nb, MOBA_BLOCK), 0)

    pos = lax.broadcasted_iota(jnp.int32, (MOBA_BLOCK, KEY_AUG - HEAD_DIM), 0).astype(F32)
    col = lax.broadcasted_iota(jnp.int32, (MOBA_BLOCK, KEY_AUG - HEAD_DIM), 1)
    k_extra = jnp.where(col < ALIBI_TERMS, pos, 0.0).astype(BF16)

    slope2 = slope_ref[...][:, 0:1] * LOG2_E
    q_extra = jnp.zeros((KEY_AUG - HEAD_DIM, MOBA_BLOCK), F32)
    row = lax.broadcasted_iota(jnp.int32, q_extra.shape, 0)
    rest = slope2
    for n in range(ALIBI_TERMS):
        piece = rest.astype(BF16).astype(F32)
        q_extra = jnp.where(row == n, piece, q_extra)
        rest = rest - piece
    q_extra = q_extra.astype(BF16)
    v_row = lax.broadcasted_iota(jnp.int32, (VALUE_AUG - HEAD_DIM, MOBA_BLOCK), 0)
    v_extra = jnp.where(v_row == 0, 1.0, 0.0).astype(BF16)

    def body(i, carry):
        rows = pl.ds(pl.multiple_of(i * MOBA_BLOCK, MOBA_BLOCK), MOBA_BLOCK)
        qT = q_ref[rows, :].astype(F32).T.astype(BF16)
        qT_ref[i, 0:HEAD_DIM, :] = qT
        qT_ref[i, HEAD_DIM:KEY_AUG, :] = q_extra
        ka_ref[rows, 0:HEAD_DIM] = k_ref[rows, :]
        ka_ref[rows, HEAD_DIM:KEY_AUG] = k_extra
        vT_ref[i, 0:HEAD_DIM, :] = v_ref[rows, :].astype(F32).T.astype(BF16)
        vT_ref[i, HEAD_DIM:VALUE_AUG, :] = v_extra
        gate = (jnp.dot(km_hi, qT, preferred_element_type=F32)
                + jnp.dot(km_lo, qT, preferred_element_type=F32)) * (1.0 / QUERY_SCALE)
        gate = jnp.where(blk < i, gate, NEG_INF)
        sel = jnp.zeros((nb, MOBA_BLOCK), F32)
        for _ in range(topk):
            best = jnp.max(gate, axis=0, keepdims=True)
            idx = jnp.min(jnp.where(gate == best, blk, nb), axis=0, keepdims=True)
            hit = blk == idx
            sel = jnp.where(hit & (blk < i), 1.0, sel)
            gate = jnp.where(hit, -jnp.inf, gate)
        sel_ref[i] = sel
        return carry

    lax.fori_loop(0, nb, body, 0, unroll=4 if nb % 4 == 0 else 1)


def _moba_gate(proj, slopes):
    S = proj.shape[0]
    nb = S // MOBA_BLOCK
    H = ATTN_HEADS
    q0 = POOL_WIDTH // HEAD_DIM
    k0 = q0 + H
    v0 = k0 + H
    blocked = lambda h: (h, 0, 0, 0)
    return pl.pallas_call(
        _moba_gate_kernel,
        grid=(H,),
        in_specs=[
            pl.BlockSpec((None, 1, 128), lambda h: (h, 0, 0)),
            pl.BlockSpec((S, HEAD_DIM), lambda h: (0, q0 + h)),
            pl.BlockSpec((S, HEAD_DIM), lambda h: (0, k0 + h)),
            pl.BlockSpec((S, HEAD_DIM), lambda h: (0, v0 + h)),
        ],
        out_specs=[
            pl.BlockSpec((None, nb, KEY_AUG, MOBA_BLOCK), blocked),
            pl.BlockSpec((None, S, KEY_AUG), lambda h: (h, 0, 0)),
            pl.BlockSpec((None, nb, VALUE_AUG, MOBA_BLOCK), blocked),
            pl.BlockSpec((None, nb, nb, MOBA_BLOCK), blocked),
        ],
        out_shape=[
            jax.ShapeDtypeStruct((H, nb, KEY_AUG, MOBA_BLOCK), BF16),
            jax.ShapeDtypeStruct((H, S, KEY_AUG), BF16),
            jax.ShapeDtypeStruct((H, nb, VALUE_AUG, MOBA_BLOCK), BF16),
            jax.ShapeDtypeStruct((H, nb, nb, MOBA_BLOCK), F32),
        ],
        compiler_params=_params(("parallel",)),
        name="moba_gate",
    )(slopes, proj, proj, proj)


def _moba_attn_kernel(slope_ref, qT_ref, qT_next_ref, sel_ref, k_ref, vT_ref, o_ref,
                      s0_ref, s1_ref, p0_ref, p1_ref, s_own_ref):
    i = pl.program_id(1)
    nb = sel_ref.shape[0]
    slope2 = slope_ref[...][:, 0:1] * LOG2_E
    qT = qT_ref[...]

    def block_of(t, u):
        return jnp.clip(t * ATTN_KV_UNROLL + u, 0, nb - 1)

    def key_rows(j):
        return pl.ds(pl.multiple_of(j * MOBA_BLOCK, MOBA_BLOCK), MOBA_BLOCK)

    def issue_scores(t, s_ref, queries=qT):
        for u in range(ATTN_KV_UNROLL):
            s_ref[u] = jnp.dot(k_ref[key_rows(block_of(t, u)), :], queries,
                               preferred_element_type=F32)

    def apply_probs(t, p_ref, alpha, acc):
        acc = alpha * acc
        for u in range(ATTN_KV_UNROLL):
            acc = acc + jnp.dot(vT_ref[block_of(t, u)], p_ref[u], preferred_element_type=F32)
        return acc

    def softmax_group(t, s_ref, p_ref, m):
        m_new = m
        shifts = []
        for u in range(ATTN_KV_UNROLL):
            j = t * ATTN_KV_UNROLL + u
            valid = jnp.where(j < i, sel_ref[pl.ds(block_of(t, u), 1), :], 0.0) > 0.0
            gap = slope2 * ((i - j) * MOBA_BLOCK).astype(F32)
            top = jnp.max(s_ref[u], axis=0, keepdims=True) - gap
            m_new = jnp.maximum(m_new, jnp.where(valid, top, NEG_INF))
            shifts.append((valid, gap))
        alpha = jnp.exp2(m - m_new)
        for u, (valid, gap) in enumerate(shifts):
            p = jnp.exp2(s_ref[u] - jnp.where(valid, m_new + gap, jnp.inf))
            p_ref[u] = p.astype(BF16)
        return m_new, alpha

    def body(r, carry):
        m, acc, alpha = carry
        acc = apply_probs(2 * r - 1, p1_ref, alpha, acc)
        m, alpha = softmax_group(2 * r, s0_ref, p0_ref, m)
        issue_scores(2 * r + 1, s1_ref)
        acc = apply_probs(2 * r, p0_ref, alpha, acc)
        m, alpha = softmax_group(2 * r + 1, s1_ref, p1_ref, m)
        issue_scores(2 * r + 2, s0_ref)
        return m, acc, alpha

    n_groups = (i + ATTN_KV_UNROLL - 1) // ATTN_KV_UNROLL
    n_pairs = n_groups // 2
    @pl.when(i == 0)
    def _():
        s0_ref[...] = jnp.zeros(s0_ref.shape, F32)
        s_own_ref[...] = jnp.dot(k_ref[key_rows(0), :], qT, preferred_element_type=F32)

    p1_ref[...] = jnp.zeros(p1_ref.shape, BF16)
    kpos = lax.broadcasted_iota(jnp.int32, (MOBA_BLOCK, MOBA_BLOCK), 0)
    qpos = lax.broadcasted_iota(jnp.int32, (MOBA_BLOCK, MOBA_BLOCK), 1)
    s_own = jnp.where(qpos >= kpos, s_own_ref[...], NEG_INF)
    top_own = jnp.max(s_own, axis=0, keepdims=True)
    row = jnp.zeros((1, MOBA_BLOCK), F32)
    init = (row + NEG_INF, jnp.zeros((VALUE_AUG, MOBA_BLOCK), F32), row + 1.0)
    m, acc, alpha = lax.fori_loop(0, n_pairs, body, init)
    acc = apply_probs(2 * n_pairs - 1, p1_ref, alpha, acc)
    m, alpha = softmax_group(2 * n_pairs, s0_ref, p0_ref, m)
    acc = apply_probs(2 * n_pairs, p0_ref, alpha, acc)
    m_new = jnp.maximum(m, top_own)
    alpha = jnp.exp2(m - m_new)
    p = jnp.exp2(s_own - m_new)
    acc = alpha * acc + jnp.dot(vT_ref[i], p.astype(BF16), preferred_element_type=F32)
    out = acc[0:HEAD_DIM] / acc[HEAD_DIM:HEAD_DIM + 1]
    o_ref[...] = out.T.astype(o_ref.dtype)
    q_next = qT_next_ref[...]
    issue_scores(0, s0_ref, q_next)
    s_own_ref[...] = jnp.dot(k_ref[key_rows(jnp.minimum(i + 1, nb - 1)), :], q_next,
                             preferred_element_type=F32)


def _moba_attention(k_aug, qT, vT, sel, slopes):
    H, S, _ = k_aug.shape
    nb = S // MOBA_BLOCK
    return pl.pallas_call(
        _moba_attn_kernel,
        grid=(H, nb),
        in_specs=[
            pl.BlockSpec((None, 1, 128), lambda h, i: (h, 0, 0)),
            pl.BlockSpec((None, None, KEY_AUG, MOBA_BLOCK), lambda h, i: (h, i, 0, 0)),
            pl.BlockSpec((None, None, KEY_AUG, MOBA_BLOCK),
                         lambda h, i: (h, jnp.minimum(i + 1, nb - 1), 0, 0)),
            pl.BlockSpec((None, None, nb, MOBA_BLOCK), lambda h, i: (h, i, 0, 0)),
            pl.BlockSpec((None, S, KEY_AUG), lambda h, i: (h, 0, 0)),
            pl.BlockSpec((None, nb, VALUE_AUG, MOBA_BLOCK), lambda h, i: (h, 0, 0, 0)),
        ],
        out_specs=pl.BlockSpec((MOBA_BLOCK, HEAD_DIM), lambda h, i: (i, h)),
        out_shape=jax.ShapeDtypeStruct((S, ATTN_WIDTH), BF16),
        scratch_shapes=[
            pltpu.VMEM((ATTN_KV_UNROLL, MOBA_BLOCK, MOBA_BLOCK), F32),
            pltpu.VMEM((ATTN_KV_UNROLL, MOBA_BLOCK, MOBA_BLOCK), F32),
            pltpu.VMEM((ATTN_KV_UNROLL, MOBA_BLOCK, MOBA_BLOCK), BF16),
            pltpu.VMEM((ATTN_KV_UNROLL, MOBA_BLOCK, MOBA_BLOCK), BF16),
            pltpu.VMEM((MOBA_BLOCK, MOBA_BLOCK), F32),
        ],
        compiler_params=_params(("arbitrary", "arbitrary")),
        name="moba_attention",
    )(slopes, qT, qT, sel, k_aug, vT)


def _first_lane_of_max(vals, lane):
    best = jnp.max(vals, axis=1, keepdims=True)
    idx = jnp.min(jnp.where(vals == best, lane, ROUTE_LANES), axis=1, keepdims=True)
    return best, idx


def _merge_out_kernel(yp_ref, ya_ref, glp_ref, gla_ref, x_ref, wbp_ref, wba_ref, wout_ref, x1_ref):
    bp = jnp.dot(yp_ref[...], wbp_ref[...], preferred_element_type=F32)
    ba = jnp.dot(ya_ref[...], wba_ref[...], preferred_element_type=F32)
    merged = (jax.nn.sigmoid(glp_ref[...].astype(F32)) * bp
              + jax.nn.sigmoid(gla_ref[...].astype(F32)) * ba)
    x1_ref[...] = x_ref[...] + jnp.dot(merged.astype(BF16), wout_ref[...],
                                       preferred_element_type=F32)


def _merge_out(yp, ya, proj, x, wbp, wba, wout, tm=512):
    S, D = x.shape
    row = lambda i: (i, 0)
    glp_blk = (POOL_WIDTH + 3 * ATTN_WIDTH) // D
    resident = lambda shape: pl.BlockSpec(shape, lambda i: (0, 0), pipeline_mode=pl.Buffered(1))
    return pl.pallas_call(
        _merge_out_kernel,
        grid=(S // tm,),
        in_specs=[
            pl.BlockSpec((tm, POOL_WIDTH), row),
            pl.BlockSpec((tm, ATTN_WIDTH), row),
            pl.BlockSpec((tm, D), lambda i: (i, glp_blk)),
            pl.BlockSpec((tm, D), lambda i: (i, glp_blk + 1)),
            pl.BlockSpec((tm, D), row),
            resident((POOL_WIDTH, D)),
            resident((ATTN_WIDTH, D)),
            resident((D, D)),
        ],
        out_specs=pl.BlockSpec((tm, D), row),
        out_shape=jax.ShapeDtypeStruct((S, D), F32),
        compiler_params=_params(("parallel",)),
        name="merge_out",
    )(yp, ya, proj, proj, x, wbp, wba, wout)


def _route_sort_kernel(x1_ref, g_ref, wr_hi_ref, wr_lo_ref, br_ref, hs_ref, route_ref, cnt_ref):
    n_tiles = ROUTE_TILES_PER_STEP
    tm = x1_ref.shape[0] // n_tiles
    cap = hs_ref.shape[1]
    d_model = x1_ref.shape[1]
    lane = lax.broadcasted_iota(jnp.int32, (tm, ROUTE_LANES), 1)
    tiles = [{"index": t, "rows": slice(t * tm, (t + 1) * tm)} for t in range(n_tiles)]

    def normalise(s):
        x1 = x1_ref[s["rows"], :]
        ms = jnp.mean(x1 * x1, axis=-1, keepdims=True)
        h2 = x1 * lax.rsqrt(ms + RMS_EPS) * g_ref[...]
        s["h2_hi"] = h2.astype(BF16)
        s["h2_lo"] = (h2 - s["h2_hi"].astype(F32)).astype(BF16)

    def router_logits(s):
        s["logits"] = (jnp.dot(s["h2_hi"], wr_hi_ref[...], preferred_element_type=F32)
                       + jnp.dot(s["h2_lo"], wr_hi_ref[...], preferred_element_type=F32)
                       + jnp.dot(s["h2_hi"], wr_lo_ref[...], preferred_element_type=F32)
                       + br_ref[...])

    def choose_experts(s):
        logits = s["logits"]
        g_logits = jnp.where(lane < N_GROUPS, logits, -jnp.inf)
        g_best, g_idx = _first_lane_of_max(g_logits, lane)
        g_w = 1.0 / jnp.sum(jnp.exp(g_logits - g_best), axis=1, keepdims=True)
        e_lo = EXPERT_LANE0 + EXPERTS_PER_GROUP * g_idx
        e_logits = jnp.where((lane >= e_lo) & (lane < e_lo + EXPERTS_PER_GROUP), logits, -jnp.inf)
        v1, i1 = _first_lane_of_max(e_logits, lane)
        v2, i2 = _first_lane_of_max(jnp.where(lane == i1, -jnp.inf, e_logits), lane)
        e21 = jnp.exp(v2 - v1)
        s["w1"] = g_w / (1.0 + e21)
        s["w2"] = g_w * e21 / (1.0 + e21)
        s["hit1"] = lane == i1
        s["hit2"] = lane == i2

    def sort_by_expert(s):
        hit1, hit2 = s["hit1"], s["hit2"]
        member = jnp.where(hit1, 1.0, jnp.where(hit2, 1.0, 0.0))
        r_tok = lax.broadcasted_iota(jnp.int32, (tm, tm), 0)
        c_tok = lax.broadcasted_iota(jnp.int32, (tm, tm), 1)
        earlier = jnp.where(c_tok < r_tok, 1.0, 0.0).astype(BF16)
        rank = jnp.dot(earlier, member.astype(BF16), preferred_element_type=F32)
        count = jnp.sum(member, axis=0, keepdims=True)
        chunks = jnp.floor((count + (MOE_CHUNK - 1)) * (1.0 / MOE_CHUNK))
        r_l = lax.broadcasted_iota(jnp.int32, (ROUTE_LANES, ROUTE_LANES), 0)
        c_l = lax.broadcasted_iota(jnp.int32, (ROUTE_LANES, ROUTE_LANES), 1)
        lower_lanes = jnp.where(r_l < c_l, 1.0, 0.0).astype(BF16)
        start = jnp.dot(jnp.broadcast_to(chunks, (8, ROUTE_LANES)).astype(BF16), lower_lanes,
                        preferred_element_type=F32)[0:1] * MOE_CHUNK
        pos = start + rank
        pos1 = jnp.sum(jnp.where(hit1, pos, 0.0), axis=1, keepdims=True)
        pos2 = jnp.sum(jnp.where(hit2, pos, 0.0), axis=1, keepdims=True)
        s["route"] = jnp.where(lane == 0, pos1, jnp.where(lane == 1, pos2, 0.0))
        route_ref[s["rows"], :] = s["route"]
        cnt_ref[s["index"]] = chunks

    def weight_pieces(w):
        pieces = jnp.zeros(lane.shape, F32)
        rest = w
        for n in range(MOE_WEIGHT_TERMS):
            piece = rest.astype(BF16).astype(F32)
            pieces = jnp.where(lane == n, piece, pieces)
            rest = rest - piece
        return pieces.astype(BF16)

    def compact(s):
        route_t = s["route"].T
        slot = lax.broadcasted_iota(jnp.int32, (cap, tm), 0)
        first = jnp.where(slot == route_t[0:1, :].astype(jnp.int32), 1.0, 0.0).astype(BF16)
        second = jnp.where(slot == route_t[1:2, :].astype(jnp.int32), 1.0, 0.0).astype(BF16)
        hs_ref[s["index"], :, 0:d_model] = jnp.dot(
            first + second, s["h2_hi"], preferred_element_type=F32).astype(BF16)
        slot_w = (jnp.dot(first, weight_pieces(s["w1"]), preferred_element_type=F32)
                  + jnp.dot(second, weight_pieces(s["w2"]), preferred_element_type=F32))
        hs_ref[s["index"], :, d_model:] = slot_w.astype(BF16)

    for stage in (normalise, router_logits, choose_experts, sort_by_expert, compact):
        for s in tiles:
            stage(s)


def _moe_cap(tm):
    worst = 2 * tm + N_EXPERTS * (MOE_CHUNK - 1)
    return -(-worst // 128) * 128


def _route_sort(x1, g, wr_hi, wr_lo, br, tm=MOE_TOKEN_TILE):
    S, D = x1.shape
    nT = S // tm
    cap = _moe_cap(tm)
    full = lambda i: (0, 0)
    row = lambda i: (i, 0)
    per_step = ROUTE_TILES_PER_STEP
    return pl.pallas_call(
        _route_sort_kernel,
        grid=(nT // per_step,),
        in_specs=[
            pl.BlockSpec((per_step * tm, D), row),
            pl.BlockSpec((1, D), full),
            pl.BlockSpec((D, ROUTE_LANES), full),
            pl.BlockSpec((D, ROUTE_LANES), full),
            pl.BlockSpec((1, ROUTE_LANES), full),
        ],
        out_specs=[
            pl.BlockSpec((per_step, cap, D + ROUTE_LANES), lambda i: (i, 0, 0)),
            pl.BlockSpec((per_step * tm, ROUTE_LANES), row),
            pl.BlockSpec((per_step, 1, ROUTE_LANES), lambda i: (i, 0, 0)),
        ],
        out_shape=[
            jax.ShapeDtypeStruct((nT, cap, D + ROUTE_LANES), BF16),
            jax.ShapeDtypeStruct((S, ROUTE_LANES), F32),
            jax.ShapeDtypeStruct((nT, 1, ROUTE_LANES), F32),
        ],
        compiler_params=_params(("parallel",)),
        name="route_sort",
    )(x1, g, wr_hi, wr_lo, br)


def _moe_plan(chunk_counts, n_row_tiles, n_chunk_slots):
    nT, E = chunk_counts.shape
    per_expert = chunk_counts.T
    seg_start = (jnp.cumsum(chunk_counts, axis=1) - chunk_counts).T
    seg_end = jnp.cumsum(per_expert, axis=1)
    n_chunks = seg_end[:, -1]
    padded = -(-n_chunks // MOE_CHUNKS_PER_TILE) * MOE_CHUNKS_PER_TILE
    e_end = jnp.cumsum(padded)
    n_used = e_end[-1] // MOE_CHUNKS_PER_TILE
    c = jnp.arange(n_chunk_slots, dtype=jnp.int32)
    e_of_c = jnp.minimum((e_end[None, :] <= c[:, None]).sum(axis=1), E - 1)
    is_e = (e_of_c[:, None] == jnp.arange(E)[None, :]).astype(jnp.int32)
    local = c - (is_e * (e_end - padded)[None, :]).sum(axis=1)
    real = local < (is_e * n_chunks[None, :]).sum(axis=1)
    pick_e = lambda table: (is_e[:, :, None] * table[None, :, :]).sum(axis=1)
    seg_end_c = pick_e(seg_end)
    t_of_c = jnp.minimum((seg_end_c <= local[:, None]).sum(axis=1), nT - 1)
    is_t = (t_of_c[:, None] == jnp.arange(nT)[None, :]).astype(jnp.int32)
    pick_t = lambda rows: (rows * is_t).sum(axis=1)
    within = local - (pick_t(seg_end_c) - pick_t(pick_e(per_expert)))
    src_tile = jnp.where(real, t_of_c, 0).astype(jnp.int32)
    src_row = jnp.where(real, (pick_t(pick_e(seg_start)) + within) * MOE_CHUNK, 0).astype(jnp.int32)
    tile = jnp.arange(n_row_tiles, dtype=jnp.int32)
    first = jnp.minimum(tile, n_used - 1) * MOE_CHUNKS_PER_TILE
    tile_expert = jnp.minimum((e_end[None, :] <= first[:, None]).sum(axis=1), E - 1)
    tile_real = real.reshape(n_row_tiles, MOE_CHUNKS_PER_TILE).sum(axis=1)
    experts = jnp.arange(E)
    nonempty = padded > 0
    buffer_of_e = jnp.cumsum(nonempty) - nonempty
    later = (experts[None, :] > experts[:, None]) & nonempty[None, :]
    next_of_e = jnp.min(jnp.where(later, experts[None, :], E), axis=1)
    next_of_e = jnp.where(next_of_e < E, next_of_e, -1)
    is_next = (next_of_e[:, None] == experts[None, :]).astype(jnp.int32)
    after_of_e = jnp.where(next_of_e >= 0, (is_next * next_of_e[None, :]).sum(axis=1), -1)
    is_te = (tile_expert[:, None] == experts[None, :]).astype(jnp.int32)
    at = lambda table: (is_te * table[None, :]).sum(axis=1)
    run_len = jnp.maximum(at(padded) // MOE_CHUNKS_PER_TILE, 1)
    run_pos = tile - at(e_end - padded) // MOE_CHUNKS_PER_TILE
    next_expert = at(next_of_e)
    streams = (tile < n_used) & (next_expert >= 0)
    piece_lo = jnp.where(streams, (MOE_WEIGHT_PIECES * run_pos) // run_len, 0)
    piece_hi = jnp.where(streams, (MOE_WEIGHT_PIECES * (run_pos + 1)) // run_len, 0)
    as_i32 = lambda a: a.astype(jnp.int32)
    return (as_i32(tile_expert), as_i32(tile_real), src_tile, src_row, as_i32(n_used.reshape(1)),
            as_i32(at(buffer_of_e) % 2), as_i32(next_expert), as_i32(at(after_of_e)),
            as_i32(piece_lo), as_i32(piece_hi), as_i32(chunk_counts.sum(axis=1)))


def _moe_ffn_kernel(texp_ref, treal_ref, ctile_ref, crow_ref, nused_ref,
                    wbuf_ref, wnext_ref, wafter_ref, plo_ref, phi_ref, tused_ref,
                    hs_hbm, wg_hbm, wu_hbm, wd_hbm, ys_hbm,
                    xbuf, ybuf, zbuf, wg_buf, wu_buf, wd_buf, stage_g, stage_u, stage_d,
                    gather_sem, scatter_sem, weight_sem, zero_sem):
    i = pl.program_id(0)
    n_used = nused_ref[0]
    slot = lax.rem(i, 2)
    rows_gu = stage_g.shape[1]
    rows_d = stage_d.shape[1]

    def piece_copies(e, p, s):
        gu_rows = pl.ds(pl.multiple_of(p * rows_gu, rows_gu), rows_gu)
        d_rows = pl.ds(pl.multiple_of(p * rows_d, rows_d), rows_d)
        return (pltpu.make_async_copy(wg_hbm.at[e, gu_rows, :], stage_g.at[s], weight_sem.at[s]),
                pltpu.make_async_copy(wu_hbm.at[e, gu_rows, :], stage_u.at[s], weight_sem.at[s]),
                pltpu.make_async_copy(wd_hbm.at[e, d_rows, :], stage_d.at[s], weight_sem.at[s]))

    def start_piece(e, p):
        for copy in piece_copies(e, p, lax.rem(p, MOE_WEIGHT_STAGES)):
            copy.start()

    def finish_piece(e, p, side):
        s = lax.rem(p, MOE_WEIGHT_STAGES)
        for copy in piece_copies(e, p, s):
            copy.wait()
        gu_rows = pl.ds(pl.multiple_of(p * rows_gu, rows_gu), rows_gu)
        d_rows = pl.ds(pl.multiple_of(p * rows_d, rows_d), rows_d)
        wg_buf[side, gu_rows, :] = stage_g[s].astype(BF16)
        wu_buf[side, gu_rows, :] = stage_u[s].astype(BF16)
        wd_buf[side, d_rows, :] = stage_d[s].astype(BF16)

    ahead = MOE_WEIGHT_STAGES - 1

    def start_first_pieces(e):
        for p in range(ahead):
            start_piece(e, p)

    def stream_pieces(e, lo, hi, side, following):
        @pl.when(hi > lo)
        def _():
            def body(p, carry):
                @pl.when(p + ahead < MOE_WEIGHT_PIECES)
                def _():
                    start_piece(e, p + ahead)
                finish_piece(e, p, side)
                return carry
            lax.fori_loop(lo, hi, body, 0)

            @pl.when((hi == MOE_WEIGHT_PIECES) & (following >= 0))
            def _():
                start_first_pieces(following)

    def chunk_rows(c):
        return pl.ds(pl.multiple_of(c * MOE_CHUNK, MOE_CHUNK), MOE_CHUNK)

    def gather_copy(tile, c, buf):
        g = tile * MOE_CHUNKS_PER_TILE + c
        src = hs_hbm.at[ctile_ref[g], pl.ds(pl.multiple_of(crow_ref[g], MOE_CHUNK), MOE_CHUNK), :]
        return pltpu.make_async_copy(src, xbuf.at[buf, chunk_rows(c), :], gather_sem.at[buf])

    def scatter_copy(tile, c, buf):
        g = tile * MOE_CHUNKS_PER_TILE + c
        dst = ys_hbm.at[ctile_ref[g], pl.ds(pl.multiple_of(crow_ref[g], MOE_CHUNK), MOE_CHUNK), :]
        return pltpu.make_async_copy(ybuf.at[buf, chunk_rows(c), :], dst, scatter_sem.at[buf])

    def for_real_chunks(tile, fn):
        n_real = treal_ref[tile]
        for c in range(MOE_CHUNKS_PER_TILE):
            pl.when(c < n_real)(functools.partial(fn, c))

    def for_tail_chunks(fn):
        n_token_tiles, cap = ys_hbm.shape[0], ys_hbm.shape[1]

        def per_tile(t, carry):
            def per_chunk(c, carry):
                fn(pltpu.make_async_copy(zbuf, ys_hbm.at[t, chunk_rows(c), :], zero_sem.at[0]))
                return carry
            return lax.fori_loop(tused_ref[t], cap // MOE_CHUNK, per_chunk, carry)
        lax.fori_loop(0, n_token_tiles, per_tile, 0)

    @pl.when(i == 0)
    def _():
        zbuf[...] = jnp.zeros(zbuf.shape, zbuf.dtype)
        for_tail_chunks(lambda copy: copy.start())
        xbuf[...] = jnp.zeros(xbuf.shape, xbuf.dtype)
        for_real_chunks(0, lambda c: gather_copy(0, c, 0).start())
        start_first_pieces(texp_ref[0])
        stream_pieces(texp_ref[0], 0, MOE_WEIGHT_PIECES, wbuf_ref[0], wnext_ref[0])

    @pl.when(i + 1 < n_used)
    def _():
        for_real_chunks(i + 1, lambda c: gather_copy(i + 1, c, 1 - slot).start())

    side = wbuf_ref[i]
    stream_pieces(wnext_ref[i], plo_ref[i], phi_ref[i], 1 - side, wafter_ref[i])

    @pl.when(i < n_used)
    def _():
        for_real_chunks(i, lambda c: gather_copy(i, c, slot).wait())

        @pl.when(i >= 2)
        def _():
            for_real_chunks(i - 2, lambda c: scatter_copy(i - 2, c, slot).wait())

        d_model = wg_buf.shape[1]
        x = xbuf[slot, :, 0:d_model]
        w = jnp.sum(xbuf[slot, :, d_model:].astype(F32), axis=1, keepdims=True)
        a = jnp.dot(x, wg_buf[side], preferred_element_type=F32)
        u = jnp.dot(x, wu_buf[side], preferred_element_type=F32)
        hid = (jax.nn.silu(a) * u * w).astype(BF16)
        ybuf[slot] = jnp.dot(hid, wd_buf[side], preferred_element_type=F32).astype(BF16)
        for_real_chunks(i, lambda c: scatter_copy(i, c, slot).start())

    @pl.when(i == n_used - 1)
    def _():
        @pl.when(i >= 1)
        def _():
            for_real_chunks(i - 1, lambda c: scatter_copy(i - 1, c, 1 - slot).wait())
        for_real_chunks(i, lambda c: scatter_copy(i, c, slot).wait())
        for_tail_chunks(lambda copy: copy.wait())


def _moe_ffn(hs, plan, wg, wu, wd):
    nT, cap, row_width = hs.shape
    E, D, F = wg.shape
    n_row_tiles = plan[0].shape[0]
    assert D % MOE_WEIGHT_PIECES == 0 and F % MOE_WEIGHT_PIECES == 0
    grid_spec = pltpu.PrefetchScalarGridSpec(
        num_scalar_prefetch=len(plan),
        grid=(n_row_tiles,),
        in_specs=[pl.BlockSpec(memory_space=pl.ANY)] * 4,
        out_specs=pl.BlockSpec(memory_space=pl.ANY),
        scratch_shapes=[
            pltpu.VMEM((2, MOE_ROW_TILE, row_width), BF16),
            pltpu.VMEM((2, MOE_ROW_TILE, D), BF16),
            pltpu.VMEM((MOE_CHUNK, D), BF16),
            pltpu.VMEM((2, D, F), BF16),
            pltpu.VMEM((2, D, F), BF16),
            pltpu.VMEM((2, F, D), BF16),
            pltpu.VMEM((MOE_WEIGHT_STAGES, D // MOE_WEIGHT_PIECES, F), F32),
            pltpu.VMEM((MOE_WEIGHT_STAGES, D // MOE_WEIGHT_PIECES, F), F32),
            pltpu.VMEM((MOE_WEIGHT_STAGES, F // MOE_WEIGHT_PIECES, D), F32),
            pltpu.SemaphoreType.DMA((2,)),
            pltpu.SemaphoreType.DMA((2,)),
            pltpu.SemaphoreType.DMA((MOE_WEIGHT_STAGES,)),
            pltpu.SemaphoreType.DMA((1,)),
        ],
    )
    return pl.pallas_call(
        _moe_ffn_kernel,
        grid_spec=grid_spec,
        out_shape=jax.ShapeDtypeStruct((nT, cap, D), BF16),
        compiler_params=_params(("arbitrary",)),
        name="moe_ffn",
    )(*plan, hs, wg, wu, wd)


def _moe_combine_kernel(ys_ref, route_ref, x1_ref, g_ref, o_ref):
    tm = x1_ref.shape[0]
    cap = ys_ref.shape[0]
    route = route_ref[...]
    slot = lax.broadcasted_iota(jnp.int32, (tm, cap), 1)
    pick = jnp.where(slot == route[:, 0:1].astype(jnp.int32), 1.0,
                     jnp.where(slot == route[:, 1:2].astype(jnp.int32), 1.0, 0.0)).astype(BF16)
    y = x1_ref[...] + jnp.dot(pick, ys_ref[...], preferred_element_type=F32)
    ms = jnp.mean(y * y, axis=-1, keepdims=True)
    o_ref[...] = y * lax.rsqrt(ms + RMS_EPS) * g_ref[...]


def _moe_combine(ys, route, x1, g, tm=MOE_TOKEN_TILE):
    S, D = x1.shape
    cap = ys.shape[1]
    row = lambda i: (i, 0)
    return pl.pallas_call(
        _moe_combine_kernel,
        grid=(S // tm,),
        in_specs=[
            pl.BlockSpec((None, cap, D), lambda i: (i, 0, 0)),
            pl.BlockSpec((tm, ROUTE_LANES), row),
            pl.BlockSpec((tm, D), row),
            pl.BlockSpec((1, D), lambda i: (0, 0)),
        ],
        out_specs=pl.BlockSpec((tm, D), row),
        out_shape=jax.ShapeDtypeStruct((S, D), F32),
        compiler_params=_params(("parallel",)),
        name="moe_combine",
    )(ys, route, x1, g)


def _router_weights(w_r_group, b_r_group, w_r_expert, b_r_expert):
    D = w_r_group.shape[0]
    w = jnp.concatenate(
        [w_r_group, jnp.transpose(w_r_expert, (1, 0, 2)).reshape(D, N_EXPERTS)], axis=1)
    b = jnp.concatenate([b_r_group, b_r_expert.reshape(N_EXPERTS)])
    pad = ROUTE_LANES - w.shape[1]
    w = jnp.pad(w, ((0, 0), (0, pad)))
    b = jnp.pad(b, (0, pad)).reshape(1, ROUTE_LANES)
    w_hi = w.astype(BF16)
    w_lo = (w - w_hi.astype(F32)).astype(BF16)
    return w_hi, w_lo, b


def kernel(x, norm_mix, w_in, w_pool, pool_scale, w_branch_pool, w_branch_attn, w_out, norm_ffn,
           w_r_group, b_r_group, w_r_expert, b_r_expert, w_gate, w_up, w_down, norm_final):
    B, S, D = x.shape
    depth = w_in.shape[0]
    assert depth == 1, "the final rms_norm is fused into the expert kernel of a single layer"
    slopes = jnp.exp2(-8.0 * jnp.arange(1, ATTN_HEADS + 1, dtype=F32) / ATTN_HEADS)
    slopes = jnp.broadcast_to(slopes[:, None, None], (ATTN_HEADS, 1, 128))
    outs = []
    for b in range(B):
        xb = x[b]
        for l in range(depth):
            proj = _inproj(_rms_norm(xb, norm_mix[l].reshape(1, D)), w_in[l])
            y_pool = _pool_mixer(proj, w_pool[l].astype(BF16), pool_scale[l].reshape(1, POOL_WIDTH))
            qT, k_aug, vT, sel = _moba_gate(proj, slopes)
            y_attn = _moba_attention(k_aug, qT, vT, sel, slopes)
            wr_hi, wr_lo, br = _router_weights(w_r_group[l], b_r_group[l], w_r_expert[l], b_r_expert[l])
            x1 = _merge_out(y_pool, y_attn, proj, xb, w_branch_pool[l].astype(BF16),
                            w_branch_attn[l].astype(BF16), w_out[l].astype(BF16))
            hs, route, cnt = _route_sort(x1, norm_ffn[l].reshape(1, D), wr_hi, wr_lo, br)
            nT = hs.shape[0]
            chunk_counts = cnt[:, 0, EXPERT_LANE0:EXPERT_LANE0 + N_EXPERTS].astype(jnp.int32)
            max_chunks = nT * ((2 * MOE_TOKEN_TILE + N_EXPERTS * (MOE_CHUNK - 1)) // MOE_CHUNK)
            n_row_tiles = -(-(max_chunks + N_EXPERTS * (MOE_CHUNKS_PER_TILE - 1))
                            // MOE_CHUNKS_PER_TILE)
            plan = _moe_plan(chunk_counts, n_row_tiles, n_row_tiles * MOE_CHUNKS_PER_TILE)
            ys = _moe_ffn(hs, plan, w_gate[l], w_up[l], w_down[l])
            xb = _moe_combine(ys, route, x1, norm_final.reshape(1, D))
        outs.append(xb)
    return jnp.stack(outs, axis=0)
```

```python
import functools

import jax
import jax.numpy as jnp
from jax import lax
from jax.experimental import pallas as pl
from jax.experimental.pallas import tpu as pltpu

F32 = jnp.float32
BF16 = jnp.bfloat16

POOL_WINDOWS = (2, 4, 8, 16)
MAX_WINDOW = 16
POOL_WIDTH = 1024
POOL_GROUP = 256
HEAD_DIM = 128
ATTN_HEADS = 8
ATTN_WIDTH = 1024
MOBA_BLOCK = 256
MOBA_TOPK = 3
N_GROUPS = 4
EXPERTS_PER_GROUP = 4
N_EXPERTS = 16
ROUTE_LANES = 128
EXPERT_LANE0 = N_GROUPS
RMS_EPS = 1e-6
NEG_INF = -1e30
LOG2_E = 1.4426950408889634
QUERY_SCALE = (HEAD_DIM ** -0.5) * LOG2_E
KEY_AUG = 2 * HEAD_DIM
ALIBI_TERMS = 3
VALUE_AUG = HEAD_DIM + 16
ATTN_KV_UNROLL = 3
MOE_CHUNK = 16
MOE_TOKEN_TILE = 256
ROUTE_TILES_PER_STEP = 4
MOE_ROW_TILE = 256
MOE_CHUNKS_PER_TILE = MOE_ROW_TILE // MOE_CHUNK
MOE_WEIGHT_TERMS = 3
MOE_WEIGHT_PIECES = 8
MOE_WEIGHT_STAGES = 4
assert MOE_WEIGHT_STAGES - 1 <= MOE_WEIGHT_PIECES

V7X_VMEM_LIMIT_BYTES = 56 * 1024 * 1024


def _params(semantics, vmem=V7X_VMEM_LIMIT_BYTES, flags=None):
    return pltpu.CompilerParams(dimension_semantics=semantics, vmem_limit_bytes=vmem, flags=flags)


def _rms_norm_kernel(x_ref, g_ref, o_ref):
    x = x_ref[...]
    ms = jnp.mean(x * x, axis=-1, keepdims=True)
    o_ref[...] = (x * lax.rsqrt(ms + RMS_EPS) * g_ref[...]).astype(o_ref.dtype)


def _rms_norm(x, g, tm=512):
    S, D = x.shape
    return pl.pallas_call(
        _rms_norm_kernel,
        grid=(S // tm,),
        in_specs=[pl.BlockSpec((tm, D), lambda i: (i, 0)), pl.BlockSpec((1, D), lambda i: (0, 0))],
        out_specs=pl.BlockSpec((tm, D), lambda i: (i, 0)),
        out_shape=jax.ShapeDtypeStruct((S, D), BF16),
        compiler_params=_params(("parallel",)),
        name="rms_norm",
    )(x, g)


def _inproj_kernel(h_ref, w_ref, o_ref, wb_ref):
    @pl.when(pl.program_id(1) == 0)
    def _():
        wb_ref[...] = w_ref[...].astype(BF16)

    col0 = pl.program_id(0) * o_ref.shape[1]
    is_q = (col0 >= POOL_WIDTH) & (col0 < POOL_WIDTH + ATTN_WIDTH)
    factor = jnp.where(is_q, QUERY_SCALE, 1.0).astype(F32)
    o_ref[...] = (jnp.dot(h_ref[...], wb_ref[...], preferred_element_type=F32)
                  * factor).astype(o_ref.dtype)


def _inproj(h, w, tm=2048, tn=1024):
    S, D = h.shape
    N = w.shape[1]
    tm = min(tm, S)
    assert POOL_WIDTH % tn == 0 and ATTN_WIDTH % tn == 0
    return pl.pallas_call(
        _inproj_kernel,
        grid=(N // tn, S // tm),
        in_specs=[
            pl.BlockSpec((tm, D), lambda j, i: (i, 0)),
            pl.BlockSpec((D, tn), lambda j, i: (0, j)),
        ],
        out_specs=pl.BlockSpec((tm, tn), lambda j, i: (i, j)),
        out_shape=jax.ShapeDtypeStruct((S, N), BF16),
        scratch_shapes=[pltpu.VMEM((D, tn), BF16)],
        compiler_params=_params(("parallel", "arbitrary")),
        name="inproj",
    )(h, w)


def _pool_kernel(cur_ref, prev_ref, w_ref, scale_ref, o_ref, ext_ref):
    i = pl.program_id(0)
    tm = cur_ref.shape[0]
    u = cur_ref[...].astype(F32)
    halo = jnp.where(i > 0, prev_ref[...].astype(F32), 0.0)
    ext_ref[0:MAX_WINDOW, :] = halo
    ext_ref[MAX_WINDOW:MAX_WINDOW + tm, :] = u
    t = i * tm + lax.broadcasted_iota(jnp.int32, (tm, 1), 0)
    for g, w in enumerate(POOL_WINDOWS):
        cols = slice(g * POOL_GROUP, (g + 1) * POOL_GROUP)
        ug = u[:, cols]
        wsum = ug
        for s in range(1, w):
            wsum = wsum + ext_ref[MAX_WINDOW - s:MAX_WINDOW - s + tm, cols]
        cnt = jnp.minimum(t + 1, w).astype(F32)
        mixed = (wsum / cnt - ug).astype(BF16)
        y = jnp.dot(mixed, w_ref[g], preferred_element_type=F32)
        o_ref[:, cols] = (y * scale_ref[:, cols]).astype(o_ref.dtype)


def _pool_mixer(proj, w_pool, pool_scale, tm=512):
    S = proj.shape[0]
    halo_blocks = tm // MAX_WINDOW
    return pl.pallas_call(
        _pool_kernel,
        grid=(S // tm,),
        in_specs=[
            pl.BlockSpec((tm, POOL_WIDTH), lambda i: (i, 0)),
            pl.BlockSpec((MAX_WINDOW, POOL_WIDTH),
                         lambda i: (jnp.maximum(i * halo_blocks - 1, 0), 0)),
            pl.BlockSpec((len(POOL_WINDOWS), POOL_GROUP, POOL_GROUP), lambda i: (0, 0, 0)),
            pl.BlockSpec((1, POOL_WIDTH), lambda i: (0, 0)),
        ],
        out_specs=pl.BlockSpec((tm, POOL_WIDTH), lambda i: (i, 0)),
        out_shape=jax.ShapeDtypeStruct((S, POOL_WIDTH), BF16),
        scratch_shapes=[pltpu.VMEM((tm + MAX_WINDOW, POOL_WIDTH), F32)],
        compiler_params=_params(("parallel",)),
        name="pool_mixer",
    )(proj, proj, w_pool, pool_scale)


def _moba_gate_kernel(slope_ref, q_ref, k_ref, v_ref, qT_ref, ka_ref, vT_ref, sel_ref):
    S = q_ref.shape[0]
    nb = S // MOBA_BLOCK
    topk = min(MOBA_TOPK, nb)
    kf = k_ref[...].astype(F32).reshape(nb, MOBA_BLOCK, HEAD_DIM)
    kmean = jnp.sum(kf, axis=1) * (1.0 / MOBA_BLOCK)
    km_hi = kmean.astype(BF16)
    km_lo = (kmean - km_hi.astype(F32)).astype(BF16)
    blk = lax.broadcasted_iota(jnp.int32, (nb, MOBA_BLOCK), 0)

    pos = lax.broadcasted_iota(jnp.int32, (MOBA_BLOCK, KEY_AUG - HEAD_DIM), 0).astype(F32)
    col = lax.broadcasted_iota(jnp.int32, (MOBA_BLOCK, KEY_AUG - HEAD_DIM), 1)
    k_extra = jnp.where(col < ALIBI_TERMS, pos, 0.0).astype(BF16)

    slope2 = slope_ref[...][:, 0:1] * LOG2_E
    q_extra = jnp.zeros((KEY_AUG - HEAD_DIM, MOBA_BLOCK), F32)
    row = lax.broadcasted_iota(jnp.int32, q_extra.shape, 0)
    rest = slope2
    for n in range(ALIBI_TERMS):
        piece = rest.astype(BF16).astype(F32)
        q_extra = jnp.where(row == n, piece, q_extra)
        rest = rest - piece
    q_extra = q_extra.astype(BF16)
    v_row = lax.broadcasted_iota(jnp.int32, (VALUE_AUG - HEAD_DIM, MOBA_BLOCK), 0)
    v_extra = jnp.where(v_row == 0, 1.0, 0.0).astype(BF16)

    def body(i, carry):
        rows = pl.ds(pl.multiple_of(i * MOBA_BLOCK, MOBA_BLOCK), MOBA_BLOCK)
        qT = q_ref[rows, :].astype(F32).T.astype(BF16)
        qT_ref[i, 0:HEAD_DIM, :] = qT
        qT_ref[i, HEAD_DIM:KEY_AUG, :] = q_extra
        ka_ref[rows, 0:HEAD_DIM] = k_ref[rows, :]
        ka_ref[rows, HEAD_DIM:KEY_AUG] = k_extra
        vT_ref[i, 0:HEAD_DIM, :] = v_ref[rows, :].astype(F32).T.astype(BF16)
        vT_ref[i, HEAD_DIM:VALUE_AUG, :] = v_extra
        gate = (jnp.dot(km_hi, qT, preferred_element_type=F32)
                + jnp.dot(km_lo, qT, preferred_element_type=F32)) * (1.0 / QUERY_SCALE)
        gate = jnp.where(blk < i, gate, NEG_INF)
        sel = jnp.zeros((nb, MOBA_BLOCK), F32)
        for _ in range(topk):
            best = jnp.max(gate, axis=0, keepdims=True)
            idx = jnp.min(jnp.where(gate == best, blk, nb), axis=0, keepdims=True)
            hit = blk == idx
            sel = jnp.where(hit & (blk < i), 1.0, sel)
            gate = jnp.where(hit, -jnp.inf, gate)
        sel_ref[i] = sel
        return carry

    lax.fori_loop(0, nb, body, 0, unroll=4 if nb % 4 == 0 else 1)


def _moba_gate(proj, slopes):
    S = proj.shape[0]
    nb = S // MOBA_BLOCK
    H = ATTN_HEADS
    q0 = POOL_WIDTH // HEAD_DIM
    k0 = q0 + H
    v0 = k0 + H
    blocked = lambda h: (h, 0, 0, 0)
    return pl.pallas_call(
        _moba_gate_kernel,
        grid=(H,),
        in_specs=[
            pl.BlockSpec((None, 1, 128), lambda h: (h, 0, 0)),
            pl.BlockSpec((S, HEAD_DIM), lambda h: (0, q0 + h)),
            pl.BlockSpec((S, HEAD_DIM), lambda h: (0, k0 + h)),
            pl.BlockSpec((S, HEAD_DIM), lambda h: (0, v0 + h)),
        ],
        out_specs=[
            pl.BlockSpec((None, nb, KEY_AUG, MOBA_BLOCK), blocked),
            pl.BlockSpec((None, S, KEY_AUG), lambda h: (h, 0, 0)),
            pl.BlockSpec((None, nb, VALUE_AUG, MOBA_BLOCK), blocked),
            pl.BlockSpec((None, nb, nb, MOBA_BLOCK), blocked),
        ],
        out_shape=[
            jax.ShapeDtypeStruct((H, nb, KEY_AUG, MOBA_BLOCK), BF16),
            jax.ShapeDtypeStruct((H, S, KEY_AUG), BF16),
            jax.ShapeDtypeStruct((H, nb, VALUE_AUG, MOBA_BLOCK), BF16),
            jax.ShapeDtypeStruct((H, nb, nb, MOBA_BLOCK), F32),
        ],
        compiler_params=_params(("parallel",)),
        name="moba_gate",
    )(slopes, proj, proj, proj)


def _moba_attn_kernel(slope_ref, qT_ref, qT_next_ref, sel_ref, k_ref, vT_ref, o_ref,
                      s0_ref, s1_ref, p0_ref, p1_ref, s_own_ref):
    i = pl.program_id(1)
    nb = sel_ref.shape[0]
    slope2 = slope_ref[...][:, 0:1] * LOG2_E
    qT = qT_ref[...]

    def block_of(t, u):
        return jnp.clip(t * ATTN_KV_UNROLL + u, 0, nb - 1)

    def key_rows(j):
        return pl.ds(pl.multiple_of(j * MOBA_BLOCK, MOBA_BLOCK), MOBA_BLOCK)

    def issue_scores(t, s_ref, queries=qT):
        for u in range(ATTN_KV_UNROLL):
            s_ref[u] = jnp.dot(k_ref[key_rows(block_of(t, u)), :], queries,
                               preferred_element_type=F32)

    def apply_probs(t, p_ref, alpha, acc):
        acc = alpha * acc
        for u in range(ATTN_KV_UNROLL):
            acc = acc + jnp.dot(vT_ref[block_of(t, u)], p_ref[u], preferred_element_type=F32)
        return acc

    def softmax_group(t, s_ref, p_ref, m):
        m_new = m
        shifts = []
        for u in range(ATTN_KV_UNROLL):
            j = t * ATTN_KV_UNROLL + u
            valid = jnp.where(j < i, sel_ref[pl.ds(block_of(t, u), 1), :], 0.0) > 0.0
            gap = slope2 * ((i - j) * MOBA_BLOCK).astype(F32)
            top = jnp.max(s_ref[u], axis=0, keepdims=True) - gap
            m_new = jnp.maximum(m_new, jnp.where(valid, top, NEG_INF))
            shifts.append((valid, gap))
        alpha = jnp.exp2(m - m_new)
        for u, (valid, gap) in enumerate(shifts):
            p = jnp.exp2(s_ref[u] - jnp.where(valid, m_new + gap, jnp.inf))
            p_ref[u] = p.astype(BF16)
        return m_new, alpha

    def body(r, carry):
        m, acc, alpha = carry
        acc = apply_probs(2 * r - 1, p1_ref, alpha, acc)
        m, alpha = softmax_group(2 * r, s0_ref, p0_ref, m)
        issue_scores(2 * r + 1, s1_ref)
        acc = apply_probs(2 * r, p0_ref, alpha, acc)
        m, alpha = softmax_group(2 * r + 1, s1_ref, p1_ref, m)
        issue_scores(2 * r + 2, s0_ref)
        return m, acc, alpha

    n_groups = (i + ATTN_KV_UNROLL - 1) // ATTN_KV_UNROLL
    n_pairs = n_groups // 2
    @pl.when(i == 0)
    def _():
        s0_ref[...] = jnp.zeros(s0_ref.shape, F32)
        s_own_ref[...] = jnp.dot(k_ref[key_rows(0), :], qT, preferred_element_type=F32)

    p1_ref[...] = jnp.zeros(p1_ref.shape, BF16)
    kpos = lax.broadcasted_iota(jnp.int32, (MOBA_BLOCK, MOBA_BLOCK), 0)
    qpos = lax.broadcasted_iota(jnp.int32, (MOBA_BLOCK, MOBA_BLOCK), 1)
    s_own = jnp.where(qpos >= kpos, s_own_ref[...], NEG_INF)
    top_own = jnp.max(s_own, axis=0, keepdims=True)
    row = jnp.zeros((1, MOBA_BLOCK), F32)
    init = (row + NEG_INF, jnp.zeros((VALUE_AUG, MOBA_BLOCK), F32), row + 1.0)
    m, acc, alpha = lax.fori_loop(0, n_pairs, body, init)

    def finish(has_last_group):
        m_, acc_ = m, apply_probs(2 * n_pairs - 1, p1_ref, alpha, acc)
        if has_last_group:
            m_, alpha_ = softmax_group(2 * n_pairs, s0_ref, p0_ref, m_)
            acc_ = apply_probs(2 * n_pairs, p0_ref, alpha_, acc_)
        m_new = jnp.maximum(m_, top_own)
        p = jnp.exp2(s_own - m_new)
        acc_ = (jnp.exp2(m_ - m_new) * acc_
                + jnp.dot(vT_ref[i], p.astype(BF16), preferred_element_type=F32))
        out = acc_[0:HEAD_DIM] / acc_[HEAD_DIM:HEAD_DIM + 1]
        o_ref[...] = out.T.astype(o_ref.dtype)
        q_next = qT_next_ref[...]
        issue_scores(0, s0_ref, q_next)
        s_own_ref[...] = jnp.dot(k_ref[key_rows(jnp.minimum(i + 1, nb - 1)), :], q_next,
                                 preferred_element_type=F32)

    pl.when(n_groups > 2 * n_pairs)(functools.partial(finish, True))
    pl.when(n_groups == 2 * n_pairs)(functools.partial(finish, False))


def _moba_attention(k_aug, qT, vT, sel, slopes):
    H, S, _ = k_aug.shape
    nb = S // MOBA_BLOCK
    return pl.pallas_call(
        _moba_attn_kernel,
        grid=(H, nb),
        in_specs=[
            pl.BlockSpec((None, 1, 128), lambda h, i: (h, 0, 0)),
            pl.BlockSpec((None, None, KEY_AUG, MOBA_BLOCK), lambda h, i: (h, i, 0, 0)),
            pl.BlockSpec((None, None, KEY_AUG, MOBA_BLOCK),
                         lambda h, i: (h, jnp.minimum(i + 1, nb - 1), 0, 0)),
            pl.BlockSpec((None, None, nb, MOBA_BLOCK), lambda h, i: (h, i, 0, 0)),
            pl.BlockSpec((None, S, KEY_AUG), lambda h, i: (h, 0, 0)),
            pl.BlockSpec((None, nb, VALUE_AUG, MOBA_BLOCK), lambda h, i: (h, 0, 0, 0)),
        ],
        out_specs=pl.BlockSpec((MOBA_BLOCK, HEAD_DIM), lambda h, i: (i, h)),
        out_shape=jax.ShapeDtypeStruct((S, ATTN_WIDTH), BF16),
        scratch_shapes=[
            pltpu.VMEM((ATTN_KV_UNROLL, MOBA_BLOCK, MOBA_BLOCK), F32),
            pltpu.VMEM((ATTN_KV_UNROLL, MOBA_BLOCK, MOBA_BLOCK), F32),
            pltpu.VMEM((ATTN_KV_UNROLL, MOBA_BLOCK, MOBA_BLOCK), BF16),
            pltpu.VMEM((ATTN_KV_UNROLL, MOBA_BLOCK, MOBA_BLOCK), BF16),
            pltpu.VMEM((MOBA_BLOCK, MOBA_BLOCK), F32),
        ],
        compiler_params=_params(("arbitrary", "arbitrary")),
        name="moba_attention",
    )(slopes, qT, qT, sel, k_aug, vT)


def _first_lane_of_max(vals, lane):
    best = jnp.max(vals, axis=1, keepdims=True)
    idx = jnp.min(jnp.where(vals == best, lane, ROUTE_LANES), axis=1, keepdims=True)
    return best, idx


def _merge_out_kernel(yp_ref, ya_ref, glp_ref, gla_ref, x_ref, wbp_ref, wba_ref, wout_ref, x1_ref):
    bp = jnp.dot(yp_ref[...], wbp_ref[...], preferred_element_type=F32)
    ba = jnp.dot(ya_ref[...], wba_ref[...], preferred_element_type=F32)
    merged = (jax.nn.sigmoid(glp_ref[...].astype(F32)) * bp
              + jax.nn.sigmoid(gla_ref[...].astype(F32)) * ba)
    x1_ref[...] = x_ref[...] + jnp.dot(merged.astype(BF16), wout_ref[...],
                                       preferred_element_type=F32)


def _merge_out(yp, ya, proj, x, wbp, wba, wout, tm=512):
    S, D = x.shape
    row = lambda i: (i, 0)
    glp_blk = (POOL_WIDTH + 3 * ATTN_WIDTH) // D
    resident = lambda shape: pl.BlockSpec(shape, lambda i: (0, 0), pipeline_mode=pl.Buffered(1))
    return pl.pallas_call(
        _merge_out_kernel,
        grid=(S // tm,),
        in_specs=[
            pl.BlockSpec((tm, POOL_WIDTH), row),
            pl.BlockSpec((tm, ATTN_WIDTH), row),
            pl.BlockSpec((tm, D), lambda i: (i, glp_blk)),
            pl.BlockSpec((tm, D), lambda i: (i, glp_blk + 1)),
            pl.BlockSpec((tm, D), row),
            resident((POOL_WIDTH, D)),
            resident((ATTN_WIDTH, D)),
            resident((D, D)),
        ],
        out_specs=pl.BlockSpec((tm, D), row),
        out_shape=jax.ShapeDtypeStruct((S, D), F32),
        compiler_params=_params(("parallel",)),
        name="merge_out",
    )(yp, ya, proj, proj, x, wbp, wba, wout)


def _route_sort_kernel(x1_ref, g_ref, wr_hi_ref, wr_lo_ref, br_ref, hs_ref, route_ref, cnt_ref):
    n_tiles = ROUTE_TILES_PER_STEP
    tm = x1_ref.shape[0] // n_tiles
    cap = hs_ref.shape[1]
    d_model = x1_ref.shape[1]
    lane = lax.broadcasted_iota(jnp.int32, (tm, ROUTE_LANES), 1)
    tiles = [{"index": t, "rows": slice(t * tm, (t + 1) * tm)} for t in range(n_tiles)]

    def normalise(s):
        x1 = x1_ref[s["rows"], :]
        ms = jnp.mean(x1 * x1, axis=-1, keepdims=True)
        h2 = x1 * lax.rsqrt(ms + RMS_EPS) * g_ref[...]
        s["h2_hi"] = h2.astype(BF16)
        s["h2_lo"] = (h2 - s["h2_hi"].astype(F32)).astype(BF16)

    def router_logits(s):
        s["logits"] = (jnp.dot(s["h2_hi"], wr_hi_ref[...], preferred_element_type=F32)
                       + jnp.dot(s["h2_lo"], wr_hi_ref[...], preferred_element_type=F32)
                       + jnp.dot(s["h2_hi"], wr_lo_ref[...], preferred_element_type=F32)
                       + br_ref[...])

    def choose_experts(s):
        logits = s["logits"]
        g_logits = jnp.where(lane < N_GROUPS, logits, -jnp.inf)
        g_best, g_idx = _first_lane_of_max(g_logits, lane)
        g_w = 1.0 / jnp.sum(jnp.exp(g_logits - g_best), axis=1, keepdims=True)
        e_lo = EXPERT_LANE0 + EXPERTS_PER_GROUP * g_idx
        e_logits = jnp.where((lane >= e_lo) & (lane < e_lo + EXPERTS_PER_GROUP), logits, -jnp.inf)
        v1, i1 = _first_lane_of_max(e_logits, lane)
        v2, i2 = _first_lane_of_max(jnp.where(lane == i1, -jnp.inf, e_logits), lane)
        e21 = jnp.exp(v2 - v1)
        s["w1"] = g_w / (1.0 + e21)
        s["w2"] = g_w * e21 / (1.0 + e21)
        s["hit1"] = lane == i1
        s["hit2"] = lane == i2

    def sort_by_expert(s):
        hit1, hit2 = s["hit1"], s["hit2"]
        member = jnp.where(hit1, 1.0, jnp.where(hit2, 1.0, 0.0))
        r_tok = lax.broadcasted_iota(jnp.int32, (tm, tm), 0)
        c_tok = lax.broadcasted_iota(jnp.int32, (tm, tm), 1)
        earlier = jnp.where(c_tok < r_tok, 1.0, 0.0).astype(BF16)
        rank = jnp.dot(earlier, member.astype(BF16), preferred_element_type=F32)
        count = jnp.sum(member, axis=0, keepdims=True)
        chunks = jnp.floor((count + (MOE_CHUNK - 1)) * (1.0 / MOE_CHUNK))
        r_l = lax.broadcasted_iota(jnp.int32, (ROUTE_LANES, ROUTE_LANES), 0)
        c_l = lax.broadcasted_iota(jnp.int32, (ROUTE_LANES, ROUTE_LANES), 1)
        lower_lanes = jnp.where(r_l < c_l, 1.0, 0.0).astype(BF16)
        start = jnp.dot(jnp.broadcast_to(chunks, (8, ROUTE_LANES)).astype(BF16), lower_lanes,
                        preferred_element_type=F32)[0:1] * MOE_CHUNK
        pos = start + rank
        pos1 = jnp.sum(jnp.where(hit1, pos, 0.0), axis=1, keepdims=True)
        pos2 = jnp.sum(jnp.where(hit2, pos, 0.0), axis=1, keepdims=True)
        s["route"] = jnp.where(lane == 0, pos1, jnp.where(lane == 1, pos2, 0.0))
        route_ref[s["rows"], :] = s["route"]
        cnt_ref[s["index"]] = chunks

    def weight_pieces(w):
        pieces = jnp.zeros(lane.shape, F32)
        rest = w
        for n in range(MOE_WEIGHT_TERMS):
            piece = rest.astype(BF16).astype(F32)
            pieces = jnp.where(lane == n, piece, pieces)
            rest = rest - piece
        return pieces.astype(BF16)

    def compact(s):
        route_t = s["route"].T
        slot = lax.broadcasted_iota(jnp.int32, (cap, tm), 0)
        first = jnp.where(slot == route_t[0:1, :].astype(jnp.int32), 1.0, 0.0).astype(BF16)
        second = jnp.where(slot == route_t[1:2, :].astype(jnp.int32), 1.0, 0.0).astype(BF16)
        hs_ref[s["index"], :, 0:d_model] = jnp.dot(
            first + second, s["h2_hi"], preferred_element_type=F32).astype(BF16)
        slot_w = (jnp.dot(first, weight_pieces(s["w1"]), preferred_element_type=F32)
                  + jnp.dot(second, weight_pieces(s["w2"]), preferred_element_type=F32))
        hs_ref[s["index"], :, d_model:] = slot_w.astype(BF16)

    for stage in (normalise, router_logits, choose_experts, sort_by_expert, compact):
        for s in tiles:
            stage(s)


def _moe_cap(tm):
    worst = 2 * tm + N_EXPERTS * (MOE_CHUNK - 1)
    return -(-worst // 128) * 128


def _route_sort(x1, g, wr_hi, wr_lo, br, tm=MOE_TOKEN_TILE):
    S, D = x1.shape
    nT = S // tm
    cap = _moe_cap(tm)
    full = lambda i: (0, 0)
    row = lambda i: (i, 0)
    per_step = ROUTE_TILES_PER_STEP
    return pl.pallas_call(
        _route_sort_kernel,
        grid=(nT // per_step,),
        in_specs=[
            pl.BlockSpec((per_step * tm, D), row),
            pl.BlockSpec((1, D), full),
            pl.BlockSpec((D, ROUTE_LANES), full),
            pl.BlockSpec((D, ROUTE_LANES), full),
            pl.BlockSpec((1, ROUTE_LANES), full),
        ],
        out_specs=[
            pl.BlockSpec((per_step, cap, D + ROUTE_LANES), lambda i: (i, 0, 0)),
            pl.BlockSpec((per_step * tm, ROUTE_LANES), row),
            pl.BlockSpec((per_step, 1, ROUTE_LANES), lambda i: (i, 0, 0)),
        ],
        out_shape=[
            jax.ShapeDtypeStruct((nT, cap, D + ROUTE_LANES), BF16),
            jax.ShapeDtypeStruct((S, ROUTE_LANES), F32),
            jax.ShapeDtypeStruct((nT, 1, ROUTE_LANES), F32),
        ],
        compiler_params=_params(("parallel",)),
        name="route_sort",
    )(x1, g, wr_hi, wr_lo, br)


def _moe_plan(chunk_counts, n_row_tiles, n_chunk_slots):
    nT, E = chunk_counts.shape
    per_expert = chunk_counts.T
    seg_start = (jnp.cumsum(chunk_counts, axis=1) - chunk_counts).T
    seg_end = jnp.cumsum(per_expert, axis=1)
    n_chunks = seg_end[:, -1]
    padded = -(-n_chunks // MOE_CHUNKS_PER_TILE) * MOE_CHUNKS_PER_TILE
    e_end = jnp.cumsum(padded)
    n_used = e_end[-1] // MOE_CHUNKS_PER_TILE
    c = jnp.arange(n_chunk_slots, dtype=jnp.int32)
    e_of_c = jnp.minimum((e_end[None, :] <= c[:, None]).sum(axis=1), E - 1)
    is_e = (e_of_c[:, None] == jnp.arange(E)[None, :]).astype(jnp.int32)
    local = c - (is_e * (e_end - padded)[None, :]).sum(axis=1)
    real = local < (is_e * n_chunks[None, :]).sum(axis=1)
    pick_e = lambda table: (is_e[:, :, None] * table[None, :, :]).sum(axis=1)
    seg_end_c = pick_e(seg_end)
    t_of_c = jnp.minimum((seg_end_c <= local[:, None]).sum(axis=1), nT - 1)
    is_t = (t_of_c[:, None] == jnp.arange(nT)[None, :]).astype(jnp.int32)
    pick_t = lambda rows: (rows * is_t).sum(axis=1)
    within = local - (pick_t(seg_end_c) - pick_t(pick_e(per_expert)))
    src_tile = jnp.where(real, t_of_c, 0).astype(jnp.int32)
    src_row = jnp.where(real, (pick_t(pick_e(seg_start)) + within) * MOE_CHUNK, 0).astype(jnp.int32)
    tile = jnp.arange(n_row_tiles, dtype=jnp.int32)
    first = jnp.minimum(tile, n_used - 1) * MOE_CHUNKS_PER_TILE
    tile_expert = jnp.minimum((e_end[None, :] <= first[:, None]).sum(axis=1), E - 1)
    tile_real = real.reshape(n_row_tiles, MOE_CHUNKS_PER_TILE).sum(axis=1)
    experts = jnp.arange(E)
    nonempty = padded > 0
    buffer_of_e = jnp.cumsum(nonempty) - nonempty
    later = (experts[None, :] > experts[:, None]) & nonempty[None, :]
    next_of_e = jnp.min(jnp.where(later, experts[None, :], E), axis=1)
    next_of_e = jnp.where(next_of_e < E, next_of_e, -1)
    is_next = (next_of_e[:, None] == experts[None, :]).astype(jnp.int32)
    after_of_e = jnp.where(next_of_e >= 0, (is_next * next_of_e[None, :]).sum(axis=1), -1)
    is_te = (tile_expert[:, None] == experts[None, :]).astype(jnp.int32)
    at = lambda table: (is_te * table[None, :]).sum(axis=1)
    run_len = jnp.maximum(at(padded) // MOE_CHUNKS_PER_TILE, 1)
    run_pos = tile - at(e_end - padded) // MOE_CHUNKS_PER_TILE
    next_expert = at(next_of_e)
    streams = (tile < n_used) & (next_expert >= 0)
    piece_lo = jnp.where(streams, (MOE_WEIGHT_PIECES * run_pos) // run_len, 0)
    piece_hi = jnp.where(streams, (MOE_WEIGHT_PIECES * (run_pos + 1)) // run_len, 0)
    as_i32 = lambda a: a.astype(jnp.int32)
    return (as_i32(tile_expert), as_i32(tile_real), src_tile, src_row, as_i32(n_used.reshape(1)),
            as_i32(at(buffer_of_e) % 2), as_i32(next_expert), as_i32(at(after_of_e)),
            as_i32(piece_lo), as_i32(piece_hi), as_i32(chunk_counts.sum(axis=1)))


def _moe_ffn_kernel(texp_ref, treal_ref, ctile_ref, crow_ref, nused_ref,
                    wbuf_ref, wnext_ref, wafter_ref, plo_ref, phi_ref, tused_ref,
                    hs_hbm, wg_hbm, wu_hbm, wd_hbm, ys_hbm,
                    xbuf, ybuf, zbuf, wg_buf, wu_buf, wd_buf, stage_g, stage_u, stage_d,
                    gather_sem, scatter_sem, weight_sem, zero_sem):
    i = pl.program_id(0)
    n_used = nused_ref[0]
    slot = lax.rem(i, 2)
    rows_gu = stage_g.shape[1]
    rows_d = stage_d.shape[1]

    def piece_copies(e, p, s):
        gu_rows = pl.ds(pl.multiple_of(p * rows_gu, rows_gu), rows_gu)
        d_rows = pl.ds(pl.multiple_of(p * rows_d, rows_d), rows_d)
        return (pltpu.make_async_copy(wg_hbm.at[e, gu_rows, :], stage_g.at[s], weight_sem.at[s]),
                pltpu.make_async_copy(wu_hbm.at[e, gu_rows, :], stage_u.at[s], weight_sem.at[s]),
                pltpu.make_async_copy(wd_hbm.at[e, d_rows, :], stage_d.at[s], weight_sem.at[s]))

    def start_piece(e, p):
        for copy in piece_copies(e, p, lax.rem(p, MOE_WEIGHT_STAGES)):
            copy.start()

    def finish_piece(e, p, side):
        s = lax.rem(p, MOE_WEIGHT_STAGES)
        for copy in piece_copies(e, p, s):
            copy.wait()
        gu_rows = pl.ds(pl.multiple_of(p * rows_gu, rows_gu), rows_gu)
        d_rows = pl.ds(pl.multiple_of(p * rows_d, rows_d), rows_d)
        wg_buf[side, gu_rows, :] = stage_g[s].astype(BF16)
        wu_buf[side, gu_rows, :] = stage_u[s].astype(BF16)
        wd_buf[side, d_rows, :] = stage_d[s].astype(BF16)

    ahead = MOE_WEIGHT_STAGES - 1

    def start_first_pieces(e):
        for p in range(ahead):
            start_piece(e, p)

    def stream_pieces(e, lo, hi, side, following):
        @pl.when(hi > lo)
        def _():
            def body(p, carry):
                @pl.when(p + ahead < MOE_WEIGHT_PIECES)
                def _():
                    start_piece(e, p + ahead)
                finish_piece(e, p, side)
                return carry
            lax.fori_loop(lo, hi, body, 0)

            @pl.when((hi == MOE_WEIGHT_PIECES) & (following >= 0))
            def _():
                start_first_pieces(following)

    def chunk_rows(c):
        return pl.ds(pl.multiple_of(c * MOE_CHUNK, MOE_CHUNK), MOE_CHUNK)

    def gather_copy(tile, c, buf):
        g = tile * MOE_CHUNKS_PER_TILE + c
        src = hs_hbm.at[ctile_ref[g], pl.ds(pl.multiple_of(crow_ref[g], MOE_CHUNK), MOE_CHUNK), :]
        return pltpu.make_async_copy(src, xbuf.at[buf, chunk_rows(c), :], gather_sem.at[buf])

    def scatter_copy(tile, c, buf):
        g = tile * MOE_CHUNKS_PER_TILE + c
        dst = ys_hbm.at[ctile_ref[g], pl.ds(pl.multiple_of(crow_ref[g], MOE_CHUNK), MOE_CHUNK), :]
        return pltpu.make_async_copy(ybuf.at[buf, chunk_rows(c), :], dst, scatter_sem.at[buf])

    def for_real_chunks(tile, fn):
        n_real = treal_ref[tile]
        for c in range(MOE_CHUNKS_PER_TILE):
            pl.when(c < n_real)(functools.partial(fn, c))

    def for_tail_chunks(fn):
        n_token_tiles, cap = ys_hbm.shape[0], ys_hbm.shape[1]

        def per_tile(t, carry):
            def per_chunk(c, carry):
                fn(pltpu.make_async_copy(zbuf, ys_hbm.at[t, chunk_rows(c), :], zero_sem.at[0]))
                return carry
            return lax.fori_loop(tused_ref[t], cap // MOE_CHUNK, per_chunk, carry)
        lax.fori_loop(0, n_token_tiles, per_tile, 0)

    @pl.when(i == 0)
    def _():
        zbuf[...] = jnp.zeros(zbuf.shape, zbuf.dtype)
        for_tail_chunks(lambda copy: copy.start())
        xbuf[...] = jnp.zeros(xbuf.shape, xbuf.dtype)
        for_real_chunks(0, lambda c: gather_copy(0, c, 0).start())
        start_first_pieces(texp_ref[0])
        stream_pieces(texp_ref[0], 0, MOE_WEIGHT_PIECES, wbuf_ref[0], wnext_ref[0])

    @pl.when(i + 1 < n_used)
    def _():
        for_real_chunks(i + 1, lambda c: gather_copy(i + 1, c, 1 - slot).start())

    side = wbuf_ref[i]
    stream_pieces(wnext_ref[i], plo_ref[i], phi_ref[i], 1 - side, wafter_ref[i])

    @pl.when(i < n_used)
    def _():
        for_real_chunks(i, lambda c: gather_copy(i, c, slot).wait())

        @pl.when(i >= 2)
        def _():
            for_real_chunks(i - 2, lambda c: scatter_copy(i - 2, c, slot).wait())

        d_model = wg_buf.shape[1]
        x = xbuf[slot, :, 0:d_model]
        w = jnp.sum(xbuf[slot, :, d_model:].astype(F32), axis=1, keepdims=True)
        a = jnp.dot(x, wg_buf[side], preferred_element_type=F32)
        u = jnp.dot(x, wu_buf[side], preferred_element_type=F32)
        hid = (jax.nn.silu(a) * u * w).astype(BF16)
        ybuf[slot] = jnp.dot(hid, wd_buf[side], preferred_element_type=F32).astype(BF16)
        for_real_chunks(i, lambda c: scatter_copy(i, c, slot).start())

    @pl.when(i == n_used - 1)
    def _():
        @pl.when(i >= 1)
        def _():
            for_real_chunks(i - 1, lambda c: scatter_copy(i - 1, c, 1 - slot).wait())
        for_real_chunks(i, lambda c: scatter_copy(i, c, slot).wait())
        for_tail_chunks(lambda copy: copy.wait())


def _moe_ffn(hs, plan, wg, wu, wd):
    nT, cap, row_width = hs.shape
    E, D, F = wg.shape
    n_row_tiles = plan[0].shape[0]
    assert D % MOE_WEIGHT_PIECES == 0 and F % MOE_WEIGHT_PIECES == 0
    grid_spec = pltpu.PrefetchScalarGridSpec(
        num_scalar_prefetch=len(plan),
        grid=(n_row_tiles,),
        in_specs=[pl.BlockSpec(memory_space=pl.ANY)] * 4,
        out_specs=pl.BlockSpec(memory_space=pl.ANY),
        scratch_shapes=[
            pltpu.VMEM((2, MOE_ROW_TILE, row_width), BF16),
            pltpu.VMEM((2, MOE_ROW_TILE, D), BF16),
            pltpu.VMEM((MOE_CHUNK, D), BF16),
            pltpu.VMEM((2, D, F), BF16),
            pltpu.VMEM((2, D, F), BF16),
            pltpu.VMEM((2, F, D), BF16),
            pltpu.VMEM((MOE_WEIGHT_STAGES, D // MOE_WEIGHT_PIECES, F), F32),
            pltpu.VMEM((MOE_WEIGHT_STAGES, D // MOE_WEIGHT_PIECES, F), F32),
            pltpu.VMEM((MOE_WEIGHT_STAGES, F // MOE_WEIGHT_PIECES, D), F32),
            pltpu.SemaphoreType.DMA((2,)),
            pltpu.SemaphoreType.DMA((2,)),
            pltpu.SemaphoreType.DMA((MOE_WEIGHT_STAGES,)),
            pltpu.SemaphoreType.DMA((1,)),
        ],
    )
    return pl.pallas_call(
        _moe_ffn_kernel,
        grid_spec=grid_spec,
        out_shape=jax.ShapeDtypeStruct((nT, cap, D), BF16),
        compiler_params=_params(("arbitrary",)),
        name="moe_ffn",
    )(*plan, hs, wg, wu, wd)


def _moe_combine_kernel(ys_ref, route_ref, x1_ref, g_ref, o_ref):
    tm = x1_ref.shape[0]
    cap = ys_ref.shape[0]
    route = route_ref[...]
    slot = lax.broadcasted_iota(jnp.int32, (tm, cap), 1)
    pick = jnp.where(slot == route[:, 0:1].astype(jnp.int32), 1.0,
                     jnp.where(slot == route[:, 1:2].astype(jnp.int32), 1.0, 0.0)).astype(BF16)
    y = x1_ref[...] + jnp.dot(pick, ys_ref[...], preferred_element_type=F32)
    ms = jnp.mean(y * y, axis=-1, keepdims=True)
    o_ref[...] = y * lax.rsqrt(ms + RMS_EPS) * g_ref[...]


def _moe_combine(ys, route, x1, g, tm=MOE_TOKEN_TILE):
    S, D = x1.shape
    cap = ys.shape[1]
    row = lambda i: (i, 0)
    return pl.pallas_call(
        _moe_combine_kernel,
        grid=(S // tm,),
        in_specs=[
            pl.BlockSpec((None, cap, D), lambda i: (i, 0, 0)),
            pl.BlockSpec((tm, ROUTE_LANES), row),
            pl.BlockSpec((tm, D), row),
            pl.BlockSpec((1, D), lambda i: (0, 0)),
        ],
        out_specs=pl.BlockSpec((tm, D), row),
        out_shape=jax.ShapeDtypeStruct((S, D), F32),
        compiler_params=_params(("parallel",)),
        name="moe_combine",
    )(ys, route, x1, g)


def _router_weights(w_r_group, b_r_group, w_r_expert, b_r_expert):
    D = w_r_group.shape[0]
    w = jnp.concatenate(
        [w_r_group, jnp.transpose(w_r_expert, (1, 0, 2)).reshape(D, N_EXPERTS)], axis=1)
    b = jnp.concatenate([b_r_group, b_r_expert.reshape(N_EXPERTS)])
    pad = ROUTE_LANES - w.shape[1]
    w = jnp.pad(w, ((0, 0), (0, pad)))
    b = jnp.pad(b, (0, pad)).reshape(1, ROUTE_LANES)
    w_hi = w.astype(BF16)
    w_lo = (w - w_hi.astype(F32)).astype(BF16)
    return w_hi, w_lo, b


def kernel(x, norm_mix, w_in, w_pool, pool_scale, w_branch_pool, w_branch_attn, w_out, norm_ffn,
           w_r_group, b_r_group, w_r_expert, b_r_expert, w_gate, w_up, w_down, norm_final):
    B, S, D = x.shape
    depth = w_in.shape[0]
    assert depth == 1, "the final rms_norm is fused into the expert kernel of a single layer"
    slopes = jnp.exp2(-8.0 * jnp.arange(1, ATTN_HEADS + 1, dtype=F32) / ATTN_HEADS)
    slopes = jnp.broadcast_to(slopes[:, None, None], (ATTN_HEADS, 1, 128))
    outs = []
    for b in range(B):
        xb = x[b]
        for l in range(depth):
            proj = _inproj(_rms_norm(xb, norm_mix[l].reshape(1, D)), w_in[l])
            y_pool = _pool_mixer(proj, w_pool[l].astype(BF16), pool_scale[l].reshape(1, POOL_WIDTH))
            qT, k_aug, vT, sel = _moba_gate(proj, slopes)
            y_attn = _moba_attention(k_aug, qT, vT, sel, slopes)
            wr_hi, wr_lo, br = _router_weights(w_r_group[l], b_r_group[l], w_r_expert[l], b_r_expert[l])
            x1 = _merge_out(y_pool, y_attn, proj, xb, w_branch_pool[l].astype(BF16),
                            w_branch_attn[l].astype(BF16), w_out[l].astype(BF16))
            hs, route, cnt = _route_sort(x1, norm_ffn[l].reshape(1, D), wr_hi, wr_lo, br)
            nT = hs.shape[0]
            chunk_counts = cnt[:, 0, EXPERT_LANE0:EXPERT_LANE0 + N_EXPERTS].astype(jnp.int32)
            max_chunks = nT * ((2 * MOE_TOKEN_TILE + N_EXPERTS * (MOE_CHUNK - 1)) // MOE_CHUNK)
            n_row_tiles = -(-(max_chunks + N_EXPERTS * (MOE_CHUNKS_PER_TILE - 1))
                            // MOE_CHUNKS_PER_TILE)
            plan = _moe_plan(chunk_counts, n_row_tiles, n_row_tiles * MOE_CHUNKS_PER_TILE)
            ys = _moe_ffn(hs, plan, w_gate[l], w_up[l], w_down[l])
            xb = _moe_combine(ys, route, x1, norm_final.reshape(1, D))
        outs.append(xb)
    return jnp.stack(outs, axis=0)
```

```python
import functools

import jax
import jax.numpy as jnp
from jax import lax
from jax.experimental import pallas as pl
from jax.experimental.pallas import tpu as pltpu

F32 = jnp.float32
BF16 = jnp.bfloat16

POOL_WINDOWS = (2, 4, 8, 16)
MAX_WINDOW = 16
POOL_WIDTH = 1024
POOL_GROUP = 256
HEAD_DIM = 128
ATTN_HEADS = 8
ATTN_WIDTH = 1024
MOBA_BLOCK = 256
MOBA_TOPK = 3
N_GROUPS = 4
EXPERTS_PER_GROUP = 4
N_EXPERTS = 16
ROUTE_LANES = 128
EXPERT_LANE0 = N_GROUPS
RMS_EPS = 1e-6
NEG_INF = -1e30
LOG2_E = 1.4426950408889634
QUERY_SCALE = (HEAD_DIM ** -0.5) * LOG2_E
KEY_AUG = 2 * HEAD_DIM
ALIBI_TERMS = 3
VALUE_AUG = HEAD_DIM + 16
ATTN_KV_UNROLL = 3
ATTN_TILES_PER_STEP = 4
MOE_CHUNK = 16
MOE_TOKEN_TILE = 256
ROUTE_TILES_PER_STEP = 4
MOE_ROW_TILE = 256
MOE_CHUNKS_PER_TILE = MOE_ROW_TILE // MOE_CHUNK
MOE_WEIGHT_TERMS = 3
MOE_WEIGHT_PIECES = 8
MOE_WEIGHT_STAGES = 4
assert MOE_WEIGHT_STAGES - 1 <= MOE_WEIGHT_PIECES

V7X_VMEM_LIMIT_BYTES = 56 * 1024 * 1024


def _params(semantics, vmem=V7X_VMEM_LIMIT_BYTES, flags=None):
    return pltpu.CompilerParams(dimension_semantics=semantics, vmem_limit_bytes=vmem, flags=flags)


def _rms_norm_kernel(x_ref, g_ref, o_ref):
    x = x_ref[...]
    ms = jnp.mean(x * x, axis=-1, keepdims=True)
    o_ref[...] = (x * lax.rsqrt(ms + RMS_EPS) * g_ref[...]).astype(o_ref.dtype)


def _rms_norm(x, g, tm=512):
    S, D = x.shape
    return pl.pallas_call(
        _rms_norm_kernel,
        grid=(S // tm,),
        in_specs=[pl.BlockSpec((tm, D), lambda i: (i, 0)), pl.BlockSpec((1, D), lambda i: (0, 0))],
        out_specs=pl.BlockSpec((tm, D), lambda i: (i, 0)),
        out_shape=jax.ShapeDtypeStruct((S, D), BF16),
        compiler_params=_params(("parallel",)),
        name="rms_norm",
    )(x, g)


def _inproj_kernel(h_ref, w_ref, o_ref, wb_ref):
    @pl.when(pl.program_id(1) == 0)
    def _():
        wb_ref[...] = w_ref[...].astype(BF16)

    col0 = pl.program_id(0) * o_ref.shape[1]
    is_q = (col0 >= POOL_WIDTH) & (col0 < POOL_WIDTH + ATTN_WIDTH)
    factor = jnp.where(is_q, QUERY_SCALE, 1.0).astype(F32)
    o_ref[...] = (jnp.dot(h_ref[...], wb_ref[...], preferred_element_type=F32)
                  * factor).astype(o_ref.dtype)


def _inproj(h, w, tm=2048, tn=1024):
    S, D = h.shape
    N = w.shape[1]
    tm = min(tm, S)
    assert POOL_WIDTH % tn == 0 and ATTN_WIDTH % tn == 0
    return pl.pallas_call(
        _inproj_kernel,
        grid=(N // tn, S // tm),
        in_specs=[
            pl.BlockSpec((tm, D), lambda j, i: (i, 0)),
            pl.BlockSpec((D, tn), lambda j, i: (0, j)),
        ],
        out_specs=pl.BlockSpec((tm, tn), lambda j, i: (i, j)),
        out_shape=jax.ShapeDtypeStruct((S, N), BF16),
        scratch_shapes=[pltpu.VMEM((D, tn), BF16)],
        compiler_params=_params(("parallel", "arbitrary")),
        name="inproj",
    )(h, w)


def _pool_kernel(cur_ref, prev_ref, w_ref, scale_ref, o_ref, ext_ref):
    i = pl.program_id(0)
    tm = cur_ref.shape[0]
    u = cur_ref[...].astype(F32)
    halo = jnp.where(i > 0, prev_ref[...].astype(F32), 0.0)
    ext_ref[0:MAX_WINDOW, :] = halo
    ext_ref[MAX_WINDOW:MAX_WINDOW + tm, :] = u
    t = i * tm + lax.broadcasted_iota(jnp.int32, (tm, 1), 0)
    for g, w in enumerate(POOL_WINDOWS):
        cols = slice(g * POOL_GROUP, (g + 1) * POOL_GROUP)
        ug = u[:, cols]
        wsum = ug
        for s in range(1, w):
            wsum = wsum + ext_ref[MAX_WINDOW - s:MAX_WINDOW - s + tm, cols]
        cnt = jnp.minimum(t + 1, w).astype(F32)
        mixed = (wsum / cnt - ug).astype(BF16)
        y = jnp.dot(mixed, w_ref[g], preferred_element_type=F32)
        o_ref[:, cols] = (y * scale_ref[:, cols]).astype(o_ref.dtype)


def _pool_mixer(proj, w_pool, pool_scale, tm=512):
    S = proj.shape[0]
    halo_blocks = tm // MAX_WINDOW
    return pl.pallas_call(
        _pool_kernel,
        grid=(S // tm,),
        in_specs=[
            pl.BlockSpec((tm, POOL_WIDTH), lambda i: (i, 0)),
            pl.BlockSpec((MAX_WINDOW, POOL_WIDTH),
                         lambda i: (jnp.maximum(i * halo_blocks - 1, 0), 0)),
            pl.BlockSpec((len(POOL_WINDOWS), POOL_GROUP, POOL_GROUP), lambda i: (0, 0, 0)),
            pl.BlockSpec((1, POOL_WIDTH), lambda i: (0, 0)),
        ],
        out_specs=pl.BlockSpec((tm, POOL_WIDTH), lambda i: (i, 0)),
        out_shape=jax.ShapeDtypeStruct((S, POOL_WIDTH), BF16),
        scratch_shapes=[pltpu.VMEM((tm + MAX_WINDOW, POOL_WIDTH), F32)],
        compiler_params=_params(("parallel",)),
        name="pool_mixer",
    )(proj, proj, w_pool, pool_scale)


def _moba_gate_kernel(slope_ref, q_ref, k_ref, v_ref, qT_ref, ka_ref, vT_ref, sel_ref):
    S = q_ref.shape[0]
    nb = S // MOBA_BLOCK
    topk = min(MOBA_TOPK, nb)
    kf = k_ref[...].astype(F32).reshape(nb, MOBA_BLOCK, HEAD_DIM)
    kmean = jnp.sum(kf, axis=1) * (1.0 / MOBA_BLOCK)
    km_hi = kmean.astype(BF16)
    km_lo = (kmean - km_hi.astype(F32)).astype(BF16)
    blk = lax.broadcasted_iota(jnp.int32, (nb, MOBA_BLOCK), 0)

    pos = lax.broadcasted_iota(jnp.int32, (MOBA_BLOCK, KEY_AUG - HEAD_DIM), 0).astype(F32)
    col = lax.broadcasted_iota(jnp.int32, (MOBA_BLOCK, KEY_AUG - HEAD_DIM), 1)
    k_extra = jnp.where(col < ALIBI_TERMS, pos, 0.0).astype(BF16)

    slope2 = slope_ref[...][:, 0:1] * LOG2_E
    q_extra = jnp.zeros((KEY_AUG - HEAD_DIM, MOBA_BLOCK), F32)
    row = lax.broadcasted_iota(jnp.int32, q_extra.shape, 0)
    rest = slope2
    for n in range(ALIBI_TERMS):
        piece = rest.astype(BF16).astype(F32)
        q_extra = jnp.where(row == n, piece, q_extra)
        rest = rest - piece
    q_extra = q_extra.astype(BF16)
    v_row = lax.broadcasted_iota(jnp.int32, (VALUE_AUG - HEAD_DIM, MOBA_BLOCK), 0)
    v_extra = jnp.where(v_row == 0, 1.0, 0.0).astype(BF16)

    def body(i, carry):
        rows = pl.ds(pl.multiple_of(i * MOBA_BLOCK, MOBA_BLOCK), MOBA_BLOCK)
        qT = q_ref[rows, :].astype(F32).T.astype(BF16)
        qT_ref[i, 0:HEAD_DIM, :] = qT
        qT_ref[i, HEAD_DIM:KEY_AUG, :] = q_extra
        ka_ref[rows, 0:HEAD_DIM] = k_ref[rows, :]
        ka_ref[rows, HEAD_DIM:KEY_AUG] = k_extra
        vT_ref[i, 0:HEAD_DIM, :] = v_ref[rows, :].astype(F32).T.astype(BF16)
        vT_ref[i, HEAD_DIM:VALUE_AUG, :] = v_extra
        gate = (jnp.dot(km_hi, qT, preferred_element_type=F32)
                + jnp.dot(km_lo, qT, preferred_element_type=F32)) * (1.0 / QUERY_SCALE)
        gate = jnp.where(blk < i, gate, NEG_INF)
        sel = jnp.zeros((nb, MOBA_BLOCK), F32)
        for _ in range(topk):
            best = jnp.max(gate, axis=0, keepdims=True)
            idx = jnp.min(jnp.where(gate == best, blk, nb), axis=0, keepdims=True)
            hit = blk == idx
            sel = jnp.where(hit & (blk < i), 1.0, sel)
            gate = jnp.where(hit, -jnp.inf, gate)
        sel_ref[i] = sel
        return carry

    lax.fori_loop(0, nb, body, 0, unroll=4 if nb % 4 == 0 else 1)


def _moba_gate(proj, slopes):
    S = proj.shape[0]
    nb = S // MOBA_BLOCK
    H = ATTN_HEADS
    q0 = POOL_WIDTH // HEAD_DIM
    k0 = q0 + H
    v0 = k0 + H
    blocked = lambda h: (h, 0, 0, 0)
    return pl.pallas_call(
        _moba_gate_kernel,
        grid=(H,),
        in_specs=[
            pl.BlockSpec((None, 1, 128), lambda h: (h, 0, 0)),
            pl.BlockSpec((S, HEAD_DIM), lambda h: (0, q0 + h)),
            pl.BlockSpec((S, HEAD_DIM), lambda h: (0, k0 + h)),
            pl.BlockSpec((S, HEAD_DIM), lambda h: (0, v0 + h)),
        ],
        out_specs=[
            pl.BlockSpec((None, nb, KEY_AUG, MOBA_BLOCK), blocked),
            pl.BlockSpec((None, S, KEY_AUG), lambda h: (h, 0, 0)),
            pl.BlockSpec((None, nb, VALUE_AUG, MOBA_BLOCK), blocked),
            pl.BlockSpec((None, nb, nb, MOBA_BLOCK), blocked),
        ],
        out_shape=[
            jax.ShapeDtypeStruct((H, nb, KEY_AUG, MOBA_BLOCK), BF16),
            jax.ShapeDtypeStruct((H, S, KEY_AUG), BF16),
            jax.ShapeDtypeStruct((H, nb, VALUE_AUG, MOBA_BLOCK), BF16),
            jax.ShapeDtypeStruct((H, nb, nb, MOBA_BLOCK), F32),
        ],
        compiler_params=_params(("parallel",)),
        name="moba_gate",
    )(slopes, proj, proj, proj)


def _moba_attn_kernel(slope_ref, qT_ref, qT_next_ref, sel_ref, k_ref, vT_ref, o_ref, *scratch):
    first = pl.program_id(1) * ATTN_TILES_PER_STEP
    for t in range(ATTN_TILES_PER_STEP):
        last = t + 1 == ATTN_TILES_PER_STEP
        _moba_attn_tile(first + t, t == 0, slope_ref, qT_ref[t],
                        qT_next_ref[...] if last else qT_ref[t + 1], sel_ref.at[t], k_ref, vT_ref,
                        o_ref.at[t * MOBA_BLOCK:(t + 1) * MOBA_BLOCK, :], *scratch)


def _moba_attn_tile(i, may_start_head, slope_ref, qT, q_next, sel_ref, k_ref, vT_ref, o_ref,
                    s0_ref, s1_ref, p0_ref, p1_ref, s_own_ref):
    nb = sel_ref.shape[0]
    slope2 = slope_ref[...][:, 0:1] * LOG2_E

    def block_of(t, u):
        return jnp.clip(t * ATTN_KV_UNROLL + u, 0, nb - 1)

    def key_rows(j):
        return pl.ds(pl.multiple_of(j * MOBA_BLOCK, MOBA_BLOCK), MOBA_BLOCK)

    def issue_scores(t, s_ref, queries=qT):
        for u in range(ATTN_KV_UNROLL):
            s_ref[u] = jnp.dot(k_ref[key_rows(block_of(t, u)), :], queries,
                               preferred_element_type=F32)

    def apply_probs(t, p_ref, alpha, acc):
        acc = alpha * acc
        for u in range(ATTN_KV_UNROLL):
            acc = acc + jnp.dot(vT_ref[block_of(t, u)], p_ref[u], preferred_element_type=F32)
        return acc

    def softmax_group(t, s_ref, p_ref, m):
        m_new = m
        shifts = []
        for u in range(ATTN_KV_UNROLL):
            j = t * ATTN_KV_UNROLL + u
            valid = jnp.where(j < i, sel_ref[pl.ds(block_of(t, u), 1), :], 0.0) > 0.0
            gap = slope2 * ((i - j) * MOBA_BLOCK).astype(F32)
            top = jnp.max(s_ref[u], axis=0, keepdims=True) - gap
            m_new = jnp.maximum(m_new, jnp.where(valid, top, NEG_INF))
            shifts.append((valid, gap))
        alpha = jnp.exp2(m - m_new)
        for u, (valid, gap) in enumerate(shifts):
            p = jnp.exp2(s_ref[u] - jnp.where(valid, m_new + gap, jnp.inf))
            p_ref[u] = p.astype(BF16)
        return m_new, alpha

    def body(r, carry):
        m, acc, alpha = carry
        acc = apply_probs(2 * r - 1, p1_ref, alpha, acc)
        m, alpha = softmax_group(2 * r, s0_ref, p0_ref, m)
        issue_scores(2 * r + 1, s1_ref)
        acc = apply_probs(2 * r, p0_ref, alpha, acc)
        m, alpha = softmax_group(2 * r + 1, s1_ref, p1_ref, m)
        issue_scores(2 * r + 2, s0_ref)
        return m, acc, alpha

    n_groups = (i + ATTN_KV_UNROLL - 1) // ATTN_KV_UNROLL
    n_pairs = n_groups // 2
    if may_start_head:
        @pl.when(i == 0)
        def _():
            s0_ref[...] = jnp.zeros(s0_ref.shape, F32)
            s_own_ref[...] = jnp.dot(k_ref[key_rows(0), :], qT, preferred_element_type=F32)

    p1_ref[...] = jnp.zeros(p1_ref.shape, BF16)
    kpos = lax.broadcasted_iota(jnp.int32, (MOBA_BLOCK, MOBA_BLOCK), 0)
    qpos = lax.broadcasted_iota(jnp.int32, (MOBA_BLOCK, MOBA_BLOCK), 1)
    s_own = jnp.where(qpos >= kpos, s_own_ref[...], NEG_INF)
    top_own = jnp.max(s_own, axis=0, keepdims=True)
    row = jnp.zeros((1, MOBA_BLOCK), F32)
    init = (row + NEG_INF, jnp.zeros((VALUE_AUG, MOBA_BLOCK), F32), row + 1.0)
    m, acc, alpha = lax.fori_loop(0, n_pairs, body, init)

    def finish(has_last_group):
        m_, acc_ = m, apply_probs(2 * n_pairs - 1, p1_ref, alpha, acc)
        if has_last_group:
            m_, alpha_ = softmax_group(2 * n_pairs, s0_ref, p0_ref, m_)
            acc_ = apply_probs(2 * n_pairs, p0_ref, alpha_, acc_)
        m_new = jnp.maximum(m_, top_own)
        p = jnp.exp2(s_own - m_new)
        acc_ = (jnp.exp2(m_ - m_new) * acc_
                + jnp.dot(vT_ref[i], p.astype(BF16), preferred_element_type=F32))
        out = acc_[0:HEAD_DIM] / acc_[HEAD_DIM:HEAD_DIM + 1]
        o_ref[...] = out.T.astype(o_ref.dtype)
        issue_scores(0, s0_ref, q_next)
        s_own_ref[...] = jnp.dot(k_ref[key_rows(jnp.minimum(i + 1, nb - 1)), :], q_next,
                                 preferred_element_type=F32)

    pl.when(n_groups > 2 * n_pairs)(functools.partial(finish, True))
    pl.when(n_groups == 2 * n_pairs)(functools.partial(finish, False))


def _moba_attention(k_aug, qT, vT, sel, slopes):
    H, S, _ = k_aug.shape
    nb = S // MOBA_BLOCK
    per_step = ATTN_TILES_PER_STEP
    assert nb % per_step == 0
    return pl.pallas_call(
        _moba_attn_kernel,
        grid=(H, nb // per_step),
        in_specs=[
            pl.BlockSpec((None, 1, 128), lambda h, g: (h, 0, 0)),
            pl.BlockSpec((None, per_step, KEY_AUG, MOBA_BLOCK), lambda h, g: (h, g, 0, 0)),
            pl.BlockSpec((None, None, KEY_AUG, MOBA_BLOCK),
                         lambda h, g: (h, jnp.minimum((g + 1) * per_step, nb - 1), 0, 0)),
            pl.BlockSpec((None, per_step, nb, MOBA_BLOCK), lambda h, g: (h, g, 0, 0)),
            pl.BlockSpec((None, S, KEY_AUG), lambda h, g: (h, 0, 0)),
            pl.BlockSpec((None, nb, VALUE_AUG, MOBA_BLOCK), lambda h, g: (h, 0, 0, 0)),
        ],
        out_specs=pl.BlockSpec((per_step * MOBA_BLOCK, HEAD_DIM), lambda h, g: (g, h)),
        out_shape=jax.ShapeDtypeStruct((S, ATTN_WIDTH), BF16),
        scratch_shapes=[
            pltpu.VMEM((ATTN_KV_UNROLL, MOBA_BLOCK, MOBA_BLOCK), F32),
            pltpu.VMEM((ATTN_KV_UNROLL, MOBA_BLOCK, MOBA_BLOCK), F32),
            pltpu.VMEM((ATTN_KV_UNROLL, MOBA_BLOCK, MOBA_BLOCK), BF16),
            pltpu.VMEM((ATTN_KV_UNROLL, MOBA_BLOCK, MOBA_BLOCK), BF16),
            pltpu.VMEM((MOBA_BLOCK, MOBA_BLOCK), F32),
        ],
        compiler_params=_params(("arbitrary", "arbitrary")),
        name="moba_attention",
    )(slopes, qT, qT, sel, k_aug, vT)


def _first_lane_of_max(vals, lane):
    best = jnp.max(vals, axis=1, keepdims=True)
    idx = jnp.min(jnp.where(vals == best, lane, ROUTE_LANES), axis=1, keepdims=True)
    return best, idx


def _merge_out_kernel(yp_ref, ya_ref, glp_ref, gla_ref, x_ref, wbp_ref, wba_ref, wout_ref, x1_ref):
    bp = jnp.dot(yp_ref[...], wbp_ref[...], preferred_element_type=F32)
    ba = jnp.dot(ya_ref[...], wba_ref[...], preferred_element_type=F32)
    merged = (jax.nn.sigmoid(glp_ref[...].astype(F32)) * bp
              + jax.nn.sigmoid(gla_ref[...].astype(F32)) * ba)
    x1_ref[...] = x_ref[...] + jnp.dot(merged.astype(BF16), wout_ref[...],
                                       preferred_element_type=F32)


def _merge_out(yp, ya, proj, x, wbp, wba, wout, tm=512):
    S, D = x.shape
    row = lambda i: (i, 0)
    glp_blk = (POOL_WIDTH + 3 * ATTN_WIDTH) // D
    resident = lambda shape: pl.BlockSpec(shape, lambda i: (0, 0), pipeline_mode=pl.Buffered(1))
    return pl.pallas_call(
        _merge_out_kernel,
        grid=(S // tm,),
        in_specs=[
            pl.BlockSpec((tm, POOL_WIDTH), row),
            pl.BlockSpec((tm, ATTN_WIDTH), row),
            pl.BlockSpec((tm, D), lambda i: (i, glp_blk)),
            pl.BlockSpec((tm, D), lambda i: (i, glp_blk + 1)),
            pl.BlockSpec((tm, D), row),
            resident((POOL_WIDTH, D)),
            resident((ATTN_WIDTH, D)),
            resident((D, D)),
        ],
        out_specs=pl.BlockSpec((tm, D), row),
        out_shape=jax.ShapeDtypeStruct((S, D), F32),
        compiler_params=_params(("parallel",)),
        name="merge_out",
    )(yp, ya, proj, proj, x, wbp, wba, wout)


def _route_sort_kernel(x1_ref, g_ref, wr_hi_ref, wr_lo_ref, br_ref, hs_ref, route_ref, cnt_ref):
    n_tiles = ROUTE_TILES_PER_STEP
    tm = x1_ref.shape[0] // n_tiles
    cap = hs_ref.shape[1]
    d_model = x1_ref.shape[1]
    lane = lax.broadcasted_iota(jnp.int32, (tm, ROUTE_LANES), 1)
    tiles = [{"index": t, "rows": slice(t * tm, (t + 1) * tm)} for t in range(n_tiles)]

    def normalise(s):
        x1 = x1_ref[s["rows"], :]
        ms = jnp.mean(x1 * x1, axis=-1, keepdims=True)
        h2 = x1 * lax.rsqrt(ms + RMS_EPS) * g_ref[...]
        s["h2_hi"] = h2.astype(BF16)
        s["h2_lo"] = (h2 - s["h2_hi"].astype(F32)).astype(BF16)

    def router_logits(s):
        s["logits"] = (jnp.dot(s["h2_hi"], wr_hi_ref[...], preferred_element_type=F32)
                       + jnp.dot(s["h2_lo"], wr_hi_ref[...], preferred_element_type=F32)
                       + jnp.dot(s["h2_hi"], wr_lo_ref[...], preferred_element_type=F32)
                       + br_ref[...])

    def choose_experts(s):
        logits = s["logits"]
        g_logits = jnp.where(lane < N_GROUPS, logits, -jnp.inf)
        g_best, g_idx = _first_lane_of_max(g_logits, lane)
        g_w = 1.0 / jnp.sum(jnp.exp(g_logits - g_best), axis=1, keepdims=True)
        e_lo = EXPERT_LANE0 + EXPERTS_PER_GROUP * g_idx
        e_logits = jnp.where((lane >= e_lo) & (lane < e_lo + EXPERTS_PER_GROUP), logits, -jnp.inf)
        v1, i1 = _first_lane_of_max(e_logits, lane)
        v2, i2 = _first_lane_of_max(jnp.where(lane == i1, -jnp.inf, e_logits), lane)
        e21 = jnp.exp(v2 - v1)
        s["w1"] = g_w / (1.0 + e21)
        s["w2"] = g_w * e21 / (1.0 + e21)
        s["hit1"] = lane == i1
        s["hit2"] = lane == i2

    def sort_by_expert(s):
        hit1, hit2 = s["hit1"], s["hit2"]
        member = jnp.where(hit1, 1.0, jnp.where(hit2, 1.0, 0.0))
        r_tok = lax.broadcasted_iota(jnp.int32, (tm, tm), 0)
        c_tok = lax.broadcasted_iota(jnp.int32, (tm, tm), 1)
        earlier = jnp.where(c_tok < r_tok, 1.0, 0.0).astype(BF16)
        rank = jnp.dot(earlier, member.astype(BF16), preferred_element_type=F32)
        count = jnp.sum(member, axis=0, keepdims=True)
        chunks = jnp.floor((count + (MOE_CHUNK - 1)) * (1.0 / MOE_CHUNK))
        r_l = lax.broadcasted_iota(jnp.int32, (ROUTE_LANES, ROUTE_LANES), 0)
        c_l = lax.broadcasted_iota(jnp.int32, (ROUTE_LANES, ROUTE_LANES), 1)
        lower_lanes = jnp.where(r_l < c_l, 1.0, 0.0).astype(BF16)
        start = jnp.dot(jnp.broadcast_to(chunks, (8, ROUTE_LANES)).astype(BF16), lower_lanes,
                        preferred_element_type=F32)[0:1] * MOE_CHUNK
        pos = start + rank
        pos1 = jnp.sum(jnp.where(hit1, pos, 0.0), axis=1, keepdims=True)
        pos2 = jnp.sum(jnp.where(hit2, pos, 0.0), axis=1, keepdims=True)
        s["route"] = jnp.where(lane == 0, pos1, jnp.where(lane == 1, pos2, 0.0))
        route_ref[s["rows"], :] = s["route"]
        cnt_ref[s["index"]] = chunks

    def weight_pieces(w):
        pieces = jnp.zeros(lane.shape, F32)
        rest = w
        for n in range(MOE_WEIGHT_TERMS):
            piece = rest.astype(BF16).astype(F32)
            pieces = jnp.where(lane == n, piece, pieces)
            rest = rest - piece
        return pieces.astype(BF16)

    def compact(s):
        route_t = s["route"].T
        slot = lax.broadcasted_iota(jnp.int32, (cap, tm), 0)
        first = jnp.where(slot == route_t[0:1, :].astype(jnp.int32), 1.0, 0.0).astype(BF16)
        second = jnp.where(slot == route_t[1:2, :].astype(jnp.int32), 1.0, 0.0).astype(BF16)
        hs_ref[s["index"], :, 0:d_model] = jnp.dot(
            first + second, s["h2_hi"], preferred_element_type=F32).astype(BF16)
        slot_w = (jnp.dot(first, weight_pieces(s["w1"]), preferred_element_type=F32)
                  + jnp.dot(second, weight_pieces(s["w2"]), preferred_element_type=F32))
        hs_ref[s["index"], :, d_model:] = slot_w.astype(BF16)

    for stage in (normalise, router_logits, choose_experts, sort_by_expert, compact):
        for s in tiles:
            stage(s)


def _moe_cap(tm):
    worst = 2 * tm + N_EXPERTS * (MOE_CHUNK - 1)
    return -(-worst // 128) * 128


def _route_sort(x1, g, wr_hi, wr_lo, br, tm=MOE_TOKEN_TILE):
    S, D = x1.shape
    nT = S // tm
    cap = _moe_cap(tm)
    full = lambda i: (0, 0)
    row = lambda i: (i, 0)
    per_step = ROUTE_TILES_PER_STEP
    return pl.pallas_call(
        _route_sort_kernel,
        grid=(nT // per_step,),
        in_specs=[
            pl.BlockSpec((per_step * tm, D), row),
            pl.BlockSpec((1, D), full),
            pl.BlockSpec((D, ROUTE_LANES), full),
            pl.BlockSpec((D, ROUTE_LANES), full),
            pl.BlockSpec((1, ROUTE_LANES), full),
        ],
        out_specs=[
            pl.BlockSpec((per_step, cap, D + ROUTE_LANES), lambda i: (i, 0, 0)),
            pl.BlockSpec((per_step * tm, ROUTE_LANES), row),
            pl.BlockSpec((per_step, 1, ROUTE_LANES), lambda i: (i, 0, 0)),
        ],
        out_shape=[
            jax.ShapeDtypeStruct((nT, cap, D + ROUTE_LANES), BF16),
            jax.ShapeDtypeStruct((S, ROUTE_LANES), F32),
            jax.ShapeDtypeStruct((nT, 1, ROUTE_LANES), F32),
        ],
        compiler_params=_params(("parallel",)),
        name="route_sort",
    )(x1, g, wr_hi, wr_lo, br)


def _moe_plan(chunk_counts, n_row_tiles, n_chunk_slots):
    nT, E = chunk_counts.shape
    per_expert = chunk_counts.T
    seg_start = (jnp.cumsum(chunk_counts, axis=1) - chunk_counts).T
    seg_end = jnp.cumsum(per_expert, axis=1)
    n_chunks = seg_end[:, -1]
    padded = -(-n_chunks // MOE_CHUNKS_PER_TILE) * MOE_CHUNKS_PER_TILE
    e_end = jnp.cumsum(padded)
    n_used = e_end[-1] // MOE_CHUNKS_PER_TILE
    c = jnp.arange(n_chunk_slots, dtype=jnp.int32)
    e_of_c = jnp.minimum((e_end[None, :] <= c[:, None]).sum(axis=1), E - 1)
    is_e = (e_of_c[:, None] == jnp.arange(E)[None, :]).astype(jnp.int32)
    local = c - (is_e * (e_end - padded)[None, :]).sum(axis=1)
    real = local < (is_e * n_chunks[None, :]).sum(axis=1)
    pick_e = lambda table: (is_e[:, :, None] * table[None, :, :]).sum(axis=1)
    seg_end_c = pick_e(seg_end)
    t_of_c = jnp.minimum((seg_end_c <= local[:, None]).sum(axis=1), nT - 1)
    is_t = (t_of_c[:, None] == jnp.arange(nT)[None, :]).astype(jnp.int32)
    pick_t = lambda rows: (rows * is_t).sum(axis=1)
    within = local - (pick_t(seg_end_c) - pick_t(pick_e(per_expert)))
    src_tile = jnp.where(real, t_of_c, 0).astype(jnp.int32)
    src_row = jnp.where(real, (pick_t(pick_e(seg_start)) + within) * MOE_CHUNK, 0).astype(jnp.int32)
    tile = jnp.arange(n_row_tiles, dtype=jnp.int32)
    first = jnp.minimum(tile, n_used - 1) * MOE_CHUNKS_PER_TILE
    tile_expert = jnp.minimum((e_end[None, :] <= first[:, None]).sum(axis=1), E - 1)
    tile_real = real.reshape(n_row_tiles, MOE_CHUNKS_PER_TILE).sum(axis=1)
    experts = jnp.arange(E)
    nonempty = padded > 0
    buffer_of_e = jnp.cumsum(nonempty) - nonempty
    later = (experts[None, :] > experts[:, None]) & nonempty[None, :]
    next_of_e = jnp.min(jnp.where(later, experts[None, :], E), axis=1)
    next_of_e = jnp.where(next_of_e < E, next_of_e, -1)
    is_next = (next_of_e[:, None] == experts[None, :]).astype(jnp.int32)
    after_of_e = jnp.where(next_of_e >= 0, (is_next * next_of_e[None, :]).sum(axis=1), -1)
    is_te = (tile_expert[:, None] == experts[None, :]).astype(jnp.int32)
    at = lambda table: (is_te * table[None, :]).sum(axis=1)
    run_len = jnp.maximum(at(padded) // MOE_CHUNKS_PER_TILE, 1)
    run_pos = tile - at(e_end - padded) // MOE_CHUNKS_PER_TILE
    next_expert = at(next_of_e)
    streams = (tile < n_used) & (next_expert >= 0)
    piece_lo = jnp.where(streams, (MOE_WEIGHT_PIECES * run_pos) // run_len, 0)
    piece_hi = jnp.where(streams, (MOE_WEIGHT_PIECES * (run_pos + 1)) // run_len, 0)
    as_i32 = lambda a: a.astype(jnp.int32)
    return (as_i32(tile_expert), as_i32(tile_real), src_tile, src_row, as_i32(n_used.reshape(1)),
            as_i32(at(buffer_of_e) % 2), as_i32(next_expert), as_i32(at(after_of_e)),
            as_i32(piece_lo), as_i32(piece_hi), as_i32(chunk_counts.sum(axis=1)))


def _moe_ffn_kernel(texp_ref, treal_ref, ctile_ref, crow_ref, nused_ref,
                    wbuf_ref, wnext_ref, wafter_ref, plo_ref, phi_ref, tused_ref,
                    hs_hbm, wg_hbm, wu_hbm, wd_hbm, ys_hbm,
                    xbuf, ybuf, zbuf, wg_buf, wu_buf, wd_buf, stage_g, stage_u, stage_d,
                    gather_sem, scatter_sem, weight_sem, zero_sem):
    i = pl.program_id(0)
    n_used = nused_ref[0]
    slot = lax.rem(i, 2)
    rows_gu = stage_g.shape[1]
    rows_d = stage_d.shape[1]

    def piece_copies(e, p, s):
        gu_rows = pl.ds(pl.multiple_of(p * rows_gu, rows_gu), rows_gu)
        d_rows = pl.ds(pl.multiple_of(p * rows_d, rows_d), rows_d)
        return (pltpu.make_async_copy(wg_hbm.at[e, gu_rows, :], stage_g.at[s], weight_sem.at[s]),
                pltpu.make_async_copy(wu_hbm.at[e, gu_rows, :], stage_u.at[s], weight_sem.at[s]),
                pltpu.make_async_copy(wd_hbm.at[e, d_rows, :], stage_d.at[s], weight_sem.at[s]))

    def start_piece(e, p):
        for copy in piece_copies(e, p, lax.rem(p, MOE_WEIGHT_STAGES)):
            copy.start()

    def finish_piece(e, p, side):
        s = lax.rem(p, MOE_WEIGHT_STAGES)
        for copy in piece_copies(e, p, s):
            copy.wait()
        gu_rows = pl.ds(pl.multiple_of(p * rows_gu, rows_gu), rows_gu)
        d_rows = pl.ds(pl.multiple_of(p * rows_d, rows_d), rows_d)
        wg_buf[side, gu_rows, :] = stage_g[s].astype(BF16)
        wu_buf[side, gu_rows, :] = stage_u[s].astype(BF16)
        wd_buf[side, d_rows, :] = stage_d[s].astype(BF16)

    ahead = MOE_WEIGHT_STAGES - 1

    def start_first_pieces(e):
        for p in range(ahead):
            start_piece(e, p)

    def stream_pieces(e, lo, hi, side, following):
        @pl.when(hi > lo)
        def _():
            def body(p, carry):
                @pl.when(p + ahead < MOE_WEIGHT_PIECES)
                def _():
                    start_piece(e, p + ahead)
                finish_piece(e, p, side)
                return carry
            lax.fori_loop(lo, hi, body, 0)

            @pl.when((hi == MOE_WEIGHT_PIECES) & (following >= 0))
            def _():
                start_first_pieces(following)

    def chunk_rows(c):
        return pl.ds(pl.multiple_of(c * MOE_CHUNK, MOE_CHUNK), MOE_CHUNK)

    def gather_copy(tile, c, buf):
        g = tile * MOE_CHUNKS_PER_TILE + c
        src = hs_hbm.at[ctile_ref[g], pl.ds(pl.multiple_of(crow_ref[g], MOE_CHUNK), MOE_CHUNK), :]
        return pltpu.make_async_copy(src, xbuf.at[buf, chunk_rows(c), :], gather_sem.at[buf])

    def scatter_copy(tile, c, buf):
        g = tile * MOE_CHUNKS_PER_TILE + c
        dst = ys_hbm.at[ctile_ref[g], pl.ds(pl.multiple_of(crow_ref[g], MOE_CHUNK), MOE_CHUNK), :]
        return pltpu.make_async_copy(ybuf.at[buf, chunk_rows(c), :], dst, scatter_sem.at[buf])

    def for_real_chunks(tile, fn):
        n_real = treal_ref[tile]
        for c in range(MOE_CHUNKS_PER_TILE):
            pl.when(c < n_real)(functools.partial(fn, c))

    def for_tail_chunks(fn):
        n_token_tiles, cap = ys_hbm.shape[0], ys_hbm.shape[1]

        def per_tile(t, carry):
            def per_chunk(c, carry):
                fn(pltpu.make_async_copy(zbuf, ys_hbm.at[t, chunk_rows(c), :], zero_sem.at[0]))
                return carry
            return lax.fori_loop(tused_ref[t], cap // MOE_CHUNK, per_chunk, carry)
        lax.fori_loop(0, n_token_tiles, per_tile, 0)

    @pl.when(i == 0)
    def _():
        zbuf[...] = jnp.zeros(zbuf.shape, zbuf.dtype)
        for_tail_chunks(lambda copy: copy.start())
        xbuf[...] = jnp.zeros(xbuf.shape, xbuf.dtype)
        for_real_chunks(0, lambda c: gather_copy(0, c, 0).start())
        start_first_pieces(texp_ref[0])
        stream_pieces(texp_ref[0], 0, MOE_WEIGHT_PIECES, wbuf_ref[0], wnext_ref[0])

    @pl.when(i + 1 < n_used)
    def _():
        for_real_chunks(i + 1, lambda c: gather_copy(i + 1, c, 1 - slot).start())

    side = wbuf_ref[i]
    stream_pieces(wnext_ref[i], plo_ref[i], phi_ref[i], 1 - side, wafter_ref[i])

    @pl.when(i < n_used)
    def _():
        for_real_chunks(i, lambda c: gather_copy(i, c, slot).wait())

        @pl.when(i >= 2)
        def _():
            for_real_chunks(i - 2, lambda c: scatter_copy(i - 2, c, slot).wait())

        d_model = wg_buf.shape[1]
        x = xbuf[slot, :, 0:d_model]
        w = jnp.sum(xbuf[slot, :, d_model:].astype(F32), axis=1, keepdims=True)
        a = jnp.dot(x, wg_buf[side], preferred_element_type=F32)
        u = jnp.dot(x, wu_buf[side], preferred_element_type=F32)
        hid = (jax.nn.silu(a) * u * w).astype(BF16)
        ybuf[slot] = jnp.dot(hid, wd_buf[side], preferred_element_type=F32).astype(BF16)
        for_real_chunks(i, lambda c: scatter_copy(i, c, slot).start())

    @pl.when(i == n_used - 1)
    def _():
        @pl.when(i >= 1)
        def _():
            for_real_chunks(i - 1, lambda c: scatter_copy(i - 1, c, 1 - slot).wait())
        for_real_chunks(i, lambda c: scatter_copy(i, c, slot).wait())
        for_tail_chunks(lambda copy: copy.wait())


def _moe_ffn(hs, plan, wg, wu, wd):
    nT, cap, row_width = hs.shape
    E, D, F = wg.shape
    n_row_tiles = plan[0].shape[0]
    assert D % MOE_WEIGHT_PIECES == 0 and F % MOE_WEIGHT_PIECES == 0
    grid_spec = pltpu.PrefetchScalarGridSpec(
        num_scalar_prefetch=len(plan),
        grid=(n_row_tiles,),
        in_specs=[pl.BlockSpec(memory_space=pl.ANY)] * 4,
        out_specs=pl.BlockSpec(memory_space=pl.ANY),
        scratch_shapes=[
            pltpu.VMEM((2, MOE_ROW_TILE, row_width), BF16),
            pltpu.VMEM((2, MOE_ROW_TILE, D), BF16),
            pltpu.VMEM((MOE_CHUNK, D), BF16),
            pltpu.VMEM((2, D, F), BF16),
            pltpu.VMEM((2, D, F), BF16),
            pltpu.VMEM((2, F, D), BF16),
            pltpu.VMEM((MOE_WEIGHT_STAGES, D // MOE_WEIGHT_PIECES, F), F32),
            pltpu.VMEM((MOE_WEIGHT_STAGES, D // MOE_WEIGHT_PIECES, F), F32),
            pltpu.VMEM((MOE_WEIGHT_STAGES, F // MOE_WEIGHT_PIECES, D), F32),
            pltpu.SemaphoreType.DMA((2,)),
            pltpu.SemaphoreType.DMA((2,)),
            pltpu.SemaphoreType.DMA((MOE_WEIGHT_STAGES,)),
            pltpu.SemaphoreType.DMA((1,)),
        ],
    )
    return pl.pallas_call(
        _moe_ffn_kernel,
        grid_spec=grid_spec,
        out_shape=jax.ShapeDtypeStruct((nT, cap, D), BF16),
        compiler_params=_params(("arbitrary",)),
        name="moe_ffn",
    )(*plan, hs, wg, wu, wd)


def _moe_combine_kernel(ys_ref, route_ref, x1_ref, g_ref, o_ref):
    tm = x1_ref.shape[0]
    cap = ys_ref.shape[0]
    route = route_ref[...]
    slot = lax.broadcasted_iota(jnp.int32, (tm, cap), 1)
    pick = jnp.where(slot == route[:, 0:1].astype(jnp.int32), 1.0,
                     jnp.where(slot == route[:, 1:2].astype(jnp.int32), 1.0, 0.0)).astype(BF16)
    y = x1_ref[...] + jnp.dot(pick, ys_ref[...], preferred_element_type=F32)
    ms = jnp.mean(y * y, axis=-1, keepdims=True)
    o_ref[...] = y * lax.rsqrt(ms + RMS_EPS) * g_ref[...]


def _moe_combine(ys, route, x1, g, tm=MOE_TOKEN_TILE):
    S, D = x1.shape
    cap = ys.shape[1]
    row = lambda i: (i, 0)
    return pl.pallas_call(
        _moe_combine_kernel,
        grid=(S // tm,),
        in_specs=[
            pl.BlockSpec((None, cap, D), lambda i: (i, 0, 0)),
            pl.BlockSpec((tm, ROUTE_LANES), row),
            pl.BlockSpec((tm, D), row),
            pl.BlockSpec((1, D), lambda i: (0, 0)),
        ],
        out_specs=pl.BlockSpec((tm, D), row),
        out_shape=jax.ShapeDtypeStruct((S, D), F32),
        compiler_params=_params(("parallel",)),
        name="moe_combine",
    )(ys, route, x1, g)


def _router_weights(w_r_group, b_r_group, w_r_expert, b_r_expert):
    D = w_r_group.shape[0]
    w = jnp.concatenate(
        [w_r_group, jnp.transpose(w_r_expert, (1, 0, 2)).reshape(D, N_EXPERTS)], axis=1)
    b = jnp.concatenate([b_r_group, b_r_expert.reshape(N_EXPERTS)])
    pad = ROUTE_LANES - w.shape[1]
    w = jnp.pad(w, ((0, 0), (0, pad)))
    b = jnp.pad(b, (0, pad)).reshape(1, ROUTE_LANES)
    w_hi = w.astype(BF16)
    w_lo = (w - w_hi.astype(F32)).astype(BF16)
    return w_hi, w_lo, b


def kernel(x, norm_mix, w_in, w_pool, pool_scale, w_branch_pool, w_branch_attn, w_out, norm_ffn,
           w_r_group, b_r_group, w_r_expert, b_r_expert, w_gate, w_up, w_down, norm_final):
    B, S, D = x.shape
    depth = w_in.shape[0]
    assert depth == 1, "the final rms_norm is fused into the expert kernel of a single layer"
    slopes = jnp.exp2(-8.0 * jnp.arange(1, ATTN_HEADS + 1, dtype=F32) / ATTN_HEADS)
    slopes = jnp.broadcast_to(slopes[:, None, None], (ATTN_HEADS, 1, 128))
    outs = []
    for b in range(B):
        xb = x[b]
        for l in range(depth):
            proj = _inproj(_rms_norm(xb, norm_mix[l].reshape(1, D)), w_in[l])
            y_pool = _pool_mixer(proj, w_pool[l].astype(BF16), pool_scale[l].reshape(1, POOL_WIDTH))
            qT, k_aug, vT, sel = _moba_gate(proj, slopes)
            y_attn = _moba_attention(k_aug, qT, vT, sel, slopes)
            wr_hi, wr_lo, br = _router_weights(w_r_group[l], b_r_group[l], w_r_expert[l], b_r_expert[l])
            x1 = _merge_out(y_pool, y_attn, proj, xb, w_branch_pool[l].astype(BF16),
                            w_branch_attn[l].astype(BF16), w_out[l].astype(BF16))
            hs, route, cnt = _route_sort(x1, norm_ffn[l].reshape(1, D), wr_hi, wr_lo, br)
            nT = hs.shape[0]
            chunk_counts = cnt[:, 0, EXPERT_LANE0:EXPERT_LANE0 + N_EXPERTS].astype(jnp.int32)
            max_chunks = nT * ((2 * MOE_TOKEN_TILE + N_EXPERTS * (MOE_CHUNK - 1)) // MOE_CHUNK)
            n_row_tiles = -(-(max_chunks + N_EXPERTS * (MOE_CHUNKS_PER_TILE - 1))
                            // MOE_CHUNKS_PER_TILE)
            plan = _moe_plan(chunk_counts, n_row_tiles, n_row_tiles * MOE_CHUNKS_PER_TILE)
            ys = _moe_ffn(hs, plan, w_gate[l], w_up[l], w_down[l])
            xb = _moe_combine(ys, route, x1, norm_final.reshape(1, D))
        outs.append(xb)
    return jnp.stack(outs, axis=0)
```

```python
import functools

import jax
import jax.numpy as jnp
from jax import lax
from jax.experimental import pallas as pl
from jax.experimental.pallas import tpu as pltpu

F32 = jnp.float32
BF16 = jnp.bfloat16

POOL_WINDOWS = (2, 4, 8, 16)
MAX_WINDOW = 16
POOL_WIDTH = 1024
POOL_GROUP = 256
HEAD_DIM = 128
ATTN_HEADS = 8
ATTN_WIDTH = 1024
MOBA_BLOCK = 256
MOBA_TOPK = 3
N_GROUPS = 4
EXPERTS_PER_GROUP = 4
N_EXPERTS = 16
ROUTE_LANES = 128
EXPERT_LANE0 = N_GROUPS
RMS_EPS = 1e-6
NEG_INF = -1e30
LOG2_E = 1.4426950408889634
QUERY_SCALE = (HEAD_DIM ** -0.5) * LOG2_E
KEY_AUG = 2 * HEAD_DIM
ALIBI_TERMS = 3
VALUE_AUG = HEAD_DIM + 16
ATTN_KV_UNROLL = 3
ATTN_TILES_PER_STEP = 4
MOE_CHUNK = 16
MOE_TOKEN_TILE = 256
ROUTE_TILES_PER_STEP = 4
MOE_ROW_TILE = 256
MOE_CHUNKS_PER_TILE = MOE_ROW_TILE // MOE_CHUNK
MOE_WEIGHT_TERMS = 3
MOE_WEIGHT_PIECES = 8
MOE_WEIGHT_STAGES = 4
assert MOE_WEIGHT_STAGES - 1 <= MOE_WEIGHT_PIECES

V7X_VMEM_LIMIT_BYTES = 56 * 1024 * 1024


def _params(semantics, vmem=V7X_VMEM_LIMIT_BYTES, flags=None):
    return pltpu.CompilerParams(dimension_semantics=semantics, vmem_limit_bytes=vmem, flags=flags)


def _rms_norm_kernel(x_ref, g_ref, o_ref):
    x = x_ref[...]
    ms = jnp.mean(x * x, axis=-1, keepdims=True)
    o_ref[...] = (x * lax.rsqrt(ms + RMS_EPS) * g_ref[...]).astype(o_ref.dtype)


def _rms_norm(x, g, tm=512):
    S, D = x.shape
    return pl.pallas_call(
        _rms_norm_kernel,
        grid=(S // tm,),
        in_specs=[pl.BlockSpec((tm, D), lambda i: (i, 0)), pl.BlockSpec((1, D), lambda i: (0, 0))],
        out_specs=pl.BlockSpec((tm, D), lambda i: (i, 0)),
        out_shape=jax.ShapeDtypeStruct((S, D), BF16),
        compiler_params=_params(("parallel",)),
        name="rms_norm",
    )(x, g)


def _inproj_kernel(h_ref, w_ref, o_ref, wb_ref):
    @pl.when(pl.program_id(1) == 0)
    def _():
        wb_ref[...] = w_ref[...].astype(BF16)

    col0 = pl.program_id(0) * o_ref.shape[1]
    is_q = (col0 >= POOL_WIDTH) & (col0 < POOL_WIDTH + ATTN_WIDTH)
    factor = jnp.where(is_q, QUERY_SCALE, 1.0).astype(F32)
    o_ref[...] = (jnp.dot(h_ref[...], wb_ref[...], preferred_element_type=F32)
                  * factor).astype(o_ref.dtype)


def _inproj(h, w, tm=2048, tn=1024):
    S, D = h.shape
    N = w.shape[1]
    tm = min(tm, S)
    assert POOL_WIDTH % tn == 0 and ATTN_WIDTH % tn == 0
    return pl.pallas_call(
        _inproj_kernel,
        grid=(N // tn, S // tm),
        in_specs=[
            pl.BlockSpec((tm, D), lambda j, i: (i, 0)),
            pl.BlockSpec((D, tn), lambda j, i: (0, j)),
        ],
        out_specs=pl.BlockSpec((tm, tn), lambda j, i: (i, j)),
        out_shape=jax.ShapeDtypeStruct((S, N), BF16),
        scratch_shapes=[pltpu.VMEM((D, tn), BF16)],
        compiler_params=_params(("parallel", "arbitrary")),
        name="inproj",
    )(h, w)


def _pool_kernel(cur_ref, prev_ref, w_ref, scale_ref, o_ref, ext_ref):
    i = pl.program_id(0)
    tm = cur_ref.shape[0]
    u = cur_ref[...].astype(F32)
    halo = jnp.where(i > 0, prev_ref[...].astype(F32), 0.0)
    ext_ref[0:MAX_WINDOW, :] = halo
    ext_ref[MAX_WINDOW:MAX_WINDOW + tm, :] = u
    t = i * tm + lax.broadcasted_iota(jnp.int32, (tm, 1), 0)
    for g, w in enumerate(POOL_WINDOWS):
        cols = slice(g * POOL_GROUP, (g + 1) * POOL_GROUP)
        ug = u[:, cols]
        wsum = ug
        for s in range(1, w):
            wsum = wsum + ext_ref[MAX_WINDOW - s:MAX_WINDOW - s + tm, cols]
        cnt = jnp.minimum(t + 1, w).astype(F32)
        mixed = (wsum / cnt - ug).astype(BF16)
        y = jnp.dot(mixed, w_ref[g], preferred_element_type=F32)
        o_ref[:, cols] = (y * scale_ref[:, cols]).astype(o_ref.dtype)


def _pool_mixer(proj, w_pool, pool_scale, tm=512):
    S = proj.shape[0]
    halo_blocks = tm // MAX_WINDOW
    return pl.pallas_call(
        _pool_kernel,
        grid=(S // tm,),
        in_specs=[
            pl.BlockSpec((tm, POOL_WIDTH), lambda i: (i, 0)),
            pl.BlockSpec((MAX_WINDOW, POOL_WIDTH),
                         lambda i: (jnp.maximum(i * halo_blocks - 1, 0), 0)),
            pl.BlockSpec((len(POOL_WINDOWS), POOL_GROUP, POOL_GROUP), lambda i: (0, 0, 0)),
            pl.BlockSpec((1, POOL_WIDTH), lambda i: (0, 0)),
        ],
        out_specs=pl.BlockSpec((tm, POOL_WIDTH), lambda i: (i, 0)),
        out_shape=jax.ShapeDtypeStruct((S, POOL_WIDTH), BF16),
        scratch_shapes=[pltpu.VMEM((tm + MAX_WINDOW, POOL_WIDTH), F32)],
        compiler_params=_params(("parallel",)),
        name="pool_mixer",
    )(proj, proj, w_pool, pool_scale)


def _moba_gate_kernel(slope_ref, q_ref, k_ref, v_ref, qT_ref, ka_ref, vT_ref, sel_ref):
    S = q_ref.shape[0]
    nb = S // MOBA_BLOCK
    topk = min(MOBA_TOPK, nb)
    kf = k_ref[...].astype(F32).reshape(nb, MOBA_BLOCK, HEAD_DIM)
    kmean = jnp.sum(kf, axis=1) * (1.0 / MOBA_BLOCK)
    km_hi = kmean.astype(BF16)
    km_lo = (kmean - km_hi.astype(F32)).astype(BF16)
    blk = lax.broadcasted_iota(jnp.int32, (nb, MOBA_BLOCK), 0)

    pos = lax.broadcasted_iota(jnp.int32, (MOBA_BLOCK, KEY_AUG - HEAD_DIM), 0).astype(F32)
    col = lax.broadcasted_iota(jnp.int32, (MOBA_BLOCK, KEY_AUG - HEAD_DIM), 1)
    k_extra = jnp.where(col < ALIBI_TERMS, pos, 0.0).astype(BF16)

    slope2 = slope_ref[...][:, 0:1] * LOG2_E
    q_extra = jnp.zeros((KEY_AUG - HEAD_DIM, MOBA_BLOCK), F32)
    row = lax.broadcasted_iota(jnp.int32, q_extra.shape, 0)
    rest = slope2
    for n in range(ALIBI_TERMS):
        piece = rest.astype(BF16).astype(F32)
        q_extra = jnp.where(row == n, piece, q_extra)
        rest = rest - piece
    q_extra = q_extra.astype(BF16)
    v_row = lax.broadcasted_iota(jnp.int32, (VALUE_AUG - HEAD_DIM, MOBA_BLOCK), 0)
    v_extra = jnp.where(v_row == 0, 1.0, 0.0).astype(BF16)

    def body(i, carry):
        rows = pl.ds(pl.multiple_of(i * MOBA_BLOCK, MOBA_BLOCK), MOBA_BLOCK)
        qT = q_ref[rows, :].astype(F32).T.astype(BF16)
        qT_ref[i, 0:HEAD_DIM, :] = qT
        qT_ref[i, HEAD_DIM:KEY_AUG, :] = q_extra
        ka_ref[rows, 0:HEAD_DIM] = k_ref[rows, :]
        ka_ref[rows, HEAD_DIM:KEY_AUG] = k_extra
        vT_ref[i, 0:HEAD_DIM, :] = v_ref[rows, :].astype(F32).T.astype(BF16)
        vT_ref[i, HEAD_DIM:VALUE_AUG, :] = v_extra
        gate = (jnp.dot(km_hi, qT, preferred_element_type=F32)
                + jnp.dot(km_lo, qT, preferred_element_type=F32)) * (1.0 / QUERY_SCALE)
        gate = jnp.where(blk < i, gate, NEG_INF)
        sel = jnp.zeros((nb, MOBA_BLOCK), F32)
        for _ in range(topk):
            best = jnp.max(gate, axis=0, keepdims=True)
            idx = jnp.min(jnp.where(gate == best, blk, nb), axis=0, keepdims=True)
            hit = blk == idx
            sel = jnp.where(hit & (blk < i), 1.0, sel)
            gate = jnp.where(hit, -jnp.inf, gate)
        sel_ref[i] = sel
        return carry

    lax.fori_loop(0, nb, body, 0, unroll=4 if nb % 4 == 0 else 1)


def _moba_gate(proj, slopes):
    S = proj.shape[0]
    nb = S // MOBA_BLOCK
    H = ATTN_HEADS
    q0 = POOL_WIDTH // HEAD_DIM
    k0 = q0 + H
    v0 = k0 + H
    blocked = lambda h: (h, 0, 0, 0)
    return pl.pallas_call(
        _moba_gate_kernel,
        grid=(H,),
        in_specs=[
            pl.BlockSpec((None, 1, 128), lambda h: (h, 0, 0)),
            pl.BlockSpec((S, HEAD_DIM), lambda h: (0, q0 + h)),
            pl.BlockSpec((S, HEAD_DIM), lambda h: (0, k0 + h)),
            pl.BlockSpec((S, HEAD_DIM), lambda h: (0, v0 + h)),
        ],
        out_specs=[
            pl.BlockSpec((None, nb, KEY_AUG, MOBA_BLOCK), blocked),
            pl.BlockSpec((None, S, KEY_AUG), lambda h: (h, 0, 0)),
            pl.BlockSpec((None, nb, VALUE_AUG, MOBA_BLOCK), blocked),
            pl.BlockSpec((None, nb, nb, MOBA_BLOCK), blocked),
        ],
        out_shape=[
            jax.ShapeDtypeStruct((H, nb, KEY_AUG, MOBA_BLOCK), BF16),
            jax.ShapeDtypeStruct((H, S, KEY_AUG), BF16),
            jax.ShapeDtypeStruct((H, nb, VALUE_AUG, MOBA_BLOCK), BF16),
            jax.ShapeDtypeStruct((H, nb, nb, MOBA_BLOCK), F32),
        ],
        compiler_params=_params(("parallel",)),
        name="moba_gate",
    )(slopes, proj, proj, proj)


def _moba_attn_kernel(slope_ref, qT_ref, qT_next_ref, sel_ref, k_ref, vT_ref, o_ref, *scratch):
    first = pl.program_id(1) * ATTN_TILES_PER_STEP
    for t in range(ATTN_TILES_PER_STEP):
        last = t + 1 == ATTN_TILES_PER_STEP
        _moba_attn_tile(first + t, t == 0, slope_ref, qT_ref[t],
                        qT_next_ref[...] if last else qT_ref[t + 1], sel_ref.at[t], k_ref, vT_ref,
                        o_ref.at[t * MOBA_BLOCK:(t + 1) * MOBA_BLOCK, :], *scratch)


def _moba_attn_tile(i, may_start_head, slope_ref, qT, q_next, sel_ref, k_ref, vT_ref, o_ref,
                    s0_ref, s1_ref, p0_ref, p1_ref, s_own_ref):
    nb = sel_ref.shape[0]
    slope2 = slope_ref[...][:, 0:1] * LOG2_E

    def block_of(t, u):
        return jnp.clip(t * ATTN_KV_UNROLL + u, 0, nb - 1)

    def key_rows(j):
        return pl.ds(pl.multiple_of(j * MOBA_BLOCK, MOBA_BLOCK), MOBA_BLOCK)

    def issue_scores(t, s_ref, queries=qT):
        for u in range(ATTN_KV_UNROLL):
            s_ref[u] = jnp.dot(k_ref[key_rows(block_of(t, u)), :], queries,
                               preferred_element_type=F32)

    def apply_probs(t, p_ref, alpha, acc):
        acc = alpha * acc
        for u in range(ATTN_KV_UNROLL):
            acc = acc + jnp.dot(vT_ref[block_of(t, u)], p_ref[u], preferred_element_type=F32)
        return acc

    def softmax_group(t, s_ref, p_ref, m):
        m_new = m
        shifts = []
        for u in range(ATTN_KV_UNROLL):
            j = t * ATTN_KV_UNROLL + u
            valid = jnp.where(j < i, sel_ref[pl.ds(block_of(t, u), 1), :], 0.0) > 0.0
            gap = slope2 * ((i - j) * MOBA_BLOCK).astype(F32)
            top = jnp.max(s_ref[u], axis=0, keepdims=True) - gap
            m_new = jnp.maximum(m_new, jnp.where(valid, top, NEG_INF))
            shifts.append((valid, gap))
        alpha = jnp.exp2(m - m_new)
        for u, (valid, gap) in enumerate(shifts):
            p = jnp.exp2(s_ref[u] - jnp.where(valid, m_new + gap, jnp.inf))
            p_ref[u] = p.astype(BF16)
        return m_new, alpha

    def body(r, carry):
        m, acc, alpha = carry
        acc = apply_probs(2 * r - 1, p1_ref, alpha, acc)
        m, alpha = softmax_group(2 * r, s0_ref, p0_ref, m)
        issue_scores(2 * r + 1, s1_ref)
        acc = apply_probs(2 * r, p0_ref, alpha, acc)
        m, alpha = softmax_group(2 * r + 1, s1_ref, p1_ref, m)
        issue_scores(2 * r + 2, s0_ref)
        return m, acc, alpha

    n_groups = (i + ATTN_KV_UNROLL - 1) // ATTN_KV_UNROLL
    n_pairs = n_groups // 2
    if may_start_head:
        @pl.when(i == 0)
        def _():
            s0_ref[...] = jnp.zeros(s0_ref.shape, F32)
            s_own_ref[...] = jnp.dot(k_ref[key_rows(0), :], qT, preferred_element_type=F32)

    p1_ref[...] = jnp.zeros(p1_ref.shape, BF16)
    kpos = lax.broadcasted_iota(jnp.int32, (MOBA_BLOCK, MOBA_BLOCK), 0)
    qpos = lax.broadcasted_iota(jnp.int32, (MOBA_BLOCK, MOBA_BLOCK), 1)
    s_own = jnp.where(qpos >= kpos, s_own_ref[...], NEG_INF)
    top_own = jnp.max(s_own, axis=0, keepdims=True)
    row = jnp.zeros((1, MOBA_BLOCK), F32)
    init = (row + NEG_INF, jnp.zeros((VALUE_AUG, MOBA_BLOCK), F32), row + 1.0)
    m, acc, alpha = lax.fori_loop(0, n_pairs, body, init)

    def finish(has_last_group):
        m_, acc_ = m, apply_probs(2 * n_pairs - 1, p1_ref, alpha, acc)
        if has_last_group:
            m_, alpha_ = softmax_group(2 * n_pairs, s0_ref, p0_ref, m_)
            acc_ = apply_probs(2 * n_pairs, p0_ref, alpha_, acc_)
        m_new = jnp.maximum(m_, top_own)
        p = jnp.exp2(s_own - m_new)
        acc_ = (jnp.exp2(m_ - m_new) * acc_
                + jnp.dot(vT_ref[i], p.astype(BF16), preferred_element_type=F32))
        out = acc_[0:HEAD_DIM] / acc_[HEAD_DIM:HEAD_DIM + 1]
        o_ref[...] = out.T.astype(o_ref.dtype)
        issue_scores(0, s0_ref, q_next)
        s_own_ref[...] = jnp.dot(k_ref[key_rows(jnp.minimum(i + 1, nb - 1)), :], q_next,
                                 preferred_element_type=F32)

    pl.when(n_groups > 2 * n_pairs)(functools.partial(finish, True))
    pl.when(n_groups == 2 * n_pairs)(functools.partial(finish, False))


def _moba_attention(k_aug, qT, vT, sel, slopes):
    H, S, _ = k_aug.shape
    nb = S // MOBA_BLOCK
    per_step = ATTN_TILES_PER_STEP
    assert nb % per_step == 0
    return pl.pallas_call(
        _moba_attn_kernel,
        grid=(H, nb // per_step),
        in_specs=[
            pl.BlockSpec((None, 1, 128), lambda h, g: (h, 0, 0)),
            pl.BlockSpec((None, per_step, KEY_AUG, MOBA_BLOCK), lambda h, g: (h, g, 0, 0)),
            pl.BlockSpec((None, None, KEY_AUG, MOBA_BLOCK),
                         lambda h, g: (h, jnp.minimum((g + 1) * per_step, nb - 1), 0, 0)),
            pl.BlockSpec((None, per_step, nb, MOBA_BLOCK), lambda h, g: (h, g, 0, 0)),
            pl.BlockSpec((None, S, KEY_AUG), lambda h, g: (h, 0, 0)),
            pl.BlockSpec((None, nb, VALUE_AUG, MOBA_BLOCK), lambda h, g: (h, 0, 0, 0)),
        ],
        out_specs=pl.BlockSpec((per_step * MOBA_BLOCK, HEAD_DIM), lambda h, g: (g, h)),
        out_shape=jax.ShapeDtypeStruct((S, ATTN_WIDTH), BF16),
        scratch_shapes=[
            pltpu.VMEM((ATTN_KV_UNROLL, MOBA_BLOCK, MOBA_BLOCK), F32),
            pltpu.VMEM((ATTN_KV_UNROLL, MOBA_BLOCK, MOBA_BLOCK), F32),
            pltpu.VMEM((ATTN_KV_UNROLL, MOBA_BLOCK, MOBA_BLOCK), BF16),
            pltpu.VMEM((ATTN_KV_UNROLL, MOBA_BLOCK, MOBA_BLOCK), BF16),
            pltpu.VMEM((MOBA_BLOCK, MOBA_BLOCK), F32),
        ],
        compiler_params=_params(("arbitrary", "arbitrary")),
        name="moba_attention",
    )(slopes, qT, qT, sel, k_aug, vT)


def _first_lane_of_max(vals, lane):
    best = jnp.max(vals, axis=1, keepdims=True)
    idx = jnp.min(jnp.where(vals == best, lane, ROUTE_LANES), axis=1, keepdims=True)
    return best, idx


def _merge_out_kernel(yp_ref, ya_ref, glp_ref, gla_ref, x_ref, wbp_ref, wba_ref, wout_ref, x1_ref):
    bp = jnp.dot(yp_ref[...], wbp_ref[...], preferred_element_type=F32)
    ba = jnp.dot(ya_ref[...], wba_ref[...], preferred_element_type=F32)
    merged = (jax.nn.sigmoid(glp_ref[...].astype(F32)) * bp
              + jax.nn.sigmoid(gla_ref[...].astype(F32)) * ba)
    x1_ref[...] = x_ref[...] + jnp.dot(merged.astype(BF16), wout_ref[...],
                                       preferred_element_type=F32)


def _merge_out(yp, ya, proj, x, wbp, wba, wout, tm=512):
    S, D = x.shape
    row = lambda i: (i, 0)
    glp_blk = (POOL_WIDTH + 3 * ATTN_WIDTH) // D
    resident = lambda shape: pl.BlockSpec(shape, lambda i: (0, 0), pipeline_mode=pl.Buffered(1))
    return pl.pallas_call(
        _merge_out_kernel,
        grid=(S // tm,),
        in_specs=[
            pl.BlockSpec((tm, POOL_WIDTH), row),
            pl.BlockSpec((tm, ATTN_WIDTH), row),
            pl.BlockSpec((tm, D), lambda i: (i, glp_blk)),
            pl.BlockSpec((tm, D), lambda i: (i, glp_blk + 1)),
            pl.BlockSpec((tm, D), row),
            resident((POOL_WIDTH, D)),
            resident((ATTN_WIDTH, D)),
            resident((D, D)),
        ],
        out_specs=pl.BlockSpec((tm, D), row),
        out_shape=jax.ShapeDtypeStruct((S, D), F32),
        compiler_params=_params(("parallel",)),
        name="merge_out",
    )(yp, ya, proj, proj, x, wbp, wba, wout)


def _route_sort_kernel(x1_ref, g_ref, wr_hi_ref, wr_lo_ref, br_ref, hs_ref, route_ref, cnt_ref):
    n_tiles = ROUTE_TILES_PER_STEP
    tm = x1_ref.shape[0] // n_tiles
    cap = hs_ref.shape[1]
    d_model = x1_ref.shape[1]
    lane = lax.broadcasted_iota(jnp.int32, (tm, ROUTE_LANES), 1)
    tiles = [{"index": t, "rows": slice(t * tm, (t + 1) * tm)} for t in range(n_tiles)]

    def normalise(s):
        x1 = x1_ref[s["rows"], :]
        ms = jnp.mean(x1 * x1, axis=-1, keepdims=True)
        h2 = x1 * lax.rsqrt(ms + RMS_EPS) * g_ref[...]
        s["h2_hi"] = h2.astype(BF16)
        s["h2_lo"] = (h2 - s["h2_hi"].astype(F32)).astype(BF16)

    def router_logits(s):
        s["logits"] = (jnp.dot(s["h2_hi"], wr_hi_ref[...], preferred_element_type=F32)
                       + jnp.dot(s["h2_lo"], wr_hi_ref[...], preferred_element_type=F32)
                       + jnp.dot(s["h2_hi"], wr_lo_ref[...], preferred_element_type=F32)
                       + br_ref[...])

    def choose_experts(s):
        logits = s["logits"]
        g_logits = jnp.where(lane < N_GROUPS, logits, -jnp.inf)
        g_best, g_idx = _first_lane_of_max(g_logits, lane)
        g_w = 1.0 / jnp.sum(jnp.exp(g_logits - g_best), axis=1, keepdims=True)
        e_lo = EXPERT_LANE0 + EXPERTS_PER_GROUP * g_idx
        e_logits = jnp.where((lane >= e_lo) & (lane < e_lo + EXPERTS_PER_GROUP), logits, -jnp.inf)
        v1, i1 = _first_lane_of_max(e_logits, lane)
        v2, i2 = _first_lane_of_max(jnp.where(lane == i1, -jnp.inf, e_logits), lane)
        e21 = jnp.exp(v2 - v1)
        s["w1"] = g_w / (1.0 + e21)
        s["w2"] = g_w * e21 / (1.0 + e21)
        s["hit1"] = lane == i1
        s["hit2"] = lane == i2

    def sort_by_expert(s):
        hit1, hit2 = s["hit1"], s["hit2"]
        member = jnp.where(hit1, 1.0, jnp.where(hit2, 1.0, 0.0))
        r_tok = lax.broadcasted_iota(jnp.int32, (tm, tm), 0)
        c_tok = lax.broadcasted_iota(jnp.int32, (tm, tm), 1)
        earlier = jnp.where(c_tok < r_tok, 1.0, 0.0).astype(BF16)
        rank = jnp.dot(earlier, member.astype(BF16), preferred_element_type=F32)
        count = jnp.sum(member, axis=0, keepdims=True)
        chunks = jnp.floor((count + (MOE_CHUNK - 1)) * (1.0 / MOE_CHUNK))
        r_l = lax.broadcasted_iota(jnp.int32, (ROUTE_LANES, ROUTE_LANES), 0)
        c_l = lax.broadcasted_iota(jnp.int32, (ROUTE_LANES, ROUTE_LANES), 1)
        lower_lanes = jnp.where(r_l < c_l, 1.0, 0.0).astype(BF16)
        start = jnp.dot(jnp.broadcast_to(chunks, (8, ROUTE_LANES)).astype(BF16), lower_lanes,
                        preferred_element_type=F32)[0:1] * MOE_CHUNK
        pos = start + rank
        pos1 = jnp.sum(jnp.where(hit1, pos, 0.0), axis=1, keepdims=True)
        pos2 = jnp.sum(jnp.where(hit2, pos, 0.0), axis=1, keepdims=True)
        s["route"] = jnp.where(lane == 0, pos1, jnp.where(lane == 1, pos2, 0.0))
        route_ref[s["rows"], :] = s["route"]
        cnt_ref[s["index"]] = chunks

    def weight_pieces(w):
        pieces = jnp.zeros(lane.shape, F32)
        rest = w
        for n in range(MOE_WEIGHT_TERMS):
            piece = rest.astype(BF16).astype(F32)
            pieces = jnp.where(lane == n, piece, pieces)
            rest = rest - piece
        return pieces.astype(BF16)

    def compact(s):
        route_t = s["route"].T
        slot = lax.broadcasted_iota(jnp.int32, (cap, tm), 0)
        first = jnp.where(slot == route_t[0:1, :].astype(jnp.int32), 1.0, 0.0).astype(BF16)
        second = jnp.where(slot == route_t[1:2, :].astype(jnp.int32), 1.0, 0.0).astype(BF16)
        hs_ref[s["index"], :, 0:d_model] = jnp.dot(
            first + second, s["h2_hi"], preferred_element_type=F32).astype(BF16)
        slot_w = (jnp.dot(first, weight_pieces(s["w1"]), preferred_element_type=F32)
                  + jnp.dot(second, weight_pieces(s["w2"]), preferred_element_type=F32))
        hs_ref[s["index"], :, d_model:] = slot_w.astype(BF16)

    for stage in (normalise, router_logits, choose_experts, sort_by_expert, compact):
        for s in tiles:
            stage(s)


def _moe_cap(tm):
    worst = 2 * tm + N_EXPERTS * (MOE_CHUNK - 1)
    return -(-worst // 128) * 128


def _route_sort(x1, g, wr_hi, wr_lo, br, tm=MOE_TOKEN_TILE):
    S, D = x1.shape
    nT = S // tm
    cap = _moe_cap(tm)
    full = lambda i: (0, 0)
    row = lambda i: (i, 0)
    per_step = ROUTE_TILES_PER_STEP
    return pl.pallas_call(
        _route_sort_kernel,
        grid=(nT // per_step,),
        in_specs=[
            pl.BlockSpec((per_step * tm, D), row),
            pl.BlockSpec((1, D), full),
            pl.BlockSpec((D, ROUTE_LANES), full),
            pl.BlockSpec((D, ROUTE_LANES), full),
            pl.BlockSpec((1, ROUTE_LANES), full),
        ],
        out_specs=[
            pl.BlockSpec((per_step, cap, D + ROUTE_LANES), lambda i: (i, 0, 0)),
            pl.BlockSpec((per_step * tm, ROUTE_LANES), row),
            pl.BlockSpec((per_step, 1, ROUTE_LANES), lambda i: (i, 0, 0)),
        ],
        out_shape=[
            jax.ShapeDtypeStruct((nT, cap, D + ROUTE_LANES), BF16),
            jax.ShapeDtypeStruct((S, ROUTE_LANES), F32),
            jax.ShapeDtypeStruct((nT, 1, ROUTE_LANES), F32),
        ],
        compiler_params=_params(("parallel",)),
        name="route_sort",
    )(x1, g, wr_hi, wr_lo, br)


def _moe_plan(chunk_counts, n_row_tiles, n_chunk_slots):
    nT, E = chunk_counts.shape
    per_expert = chunk_counts.T
    seg_start = (jnp.cumsum(chunk_counts, axis=1) - chunk_counts).T
    seg_end = jnp.cumsum(per_expert, axis=1)
    n_chunks = seg_end[:, -1]
    padded = -(-n_chunks // MOE_CHUNKS_PER_TILE) * MOE_CHUNKS_PER_TILE
    e_end = jnp.cumsum(padded)
    n_used = e_end[-1] // MOE_CHUNKS_PER_TILE
    c = jnp.arange(n_chunk_slots, dtype=jnp.int32)
    e_of_c = jnp.minimum((e_end[None, :] <= c[:, None]).sum(axis=1), E - 1)
    is_e = (e_of_c[:, None] == jnp.arange(E)[None, :]).astype(jnp.int32)
    local = c - (is_e * (e_end - padded)[None, :]).sum(axis=1)
    real = local < (is_e * n_chunks[None, :]).sum(axis=1)
    pick_e = lambda table: (is_e[:, :, None] * table[None, :, :]).sum(axis=1)
    seg_end_c = pick_e(seg_end)
    t_of_c = jnp.minimum((seg_end_c <= local[:, None]).sum(axis=1), nT - 1)
    is_t = (t_of_c[:, None] == jnp.arange(nT)[None, :]).astype(jnp.int32)
    pick_t = lambda rows: (rows * is_t).sum(axis=1)
    within = local - (pick_t(seg_end_c) - pick_t(pick_e(per_expert)))
    src_tile = jnp.where(real, t_of_c, 0).astype(jnp.int32)
    src_row = jnp.where(real, (pick_t(pick_e(seg_start)) + within) * MOE_CHUNK, 0).astype(jnp.int32)
    tile = jnp.arange(n_row_tiles, dtype=jnp.int32)
    first = jnp.minimum(tile, n_used - 1) * MOE_CHUNKS_PER_TILE
    tile_expert = jnp.minimum((e_end[None, :] <= first[:, None]).sum(axis=1), E - 1)
    tile_real = real.reshape(n_row_tiles, MOE_CHUNKS_PER_TILE).sum(axis=1)
    experts = jnp.arange(E)
    nonempty = padded > 0
    buffer_of_e = jnp.cumsum(nonempty) - nonempty
    later = (experts[None, :] > experts[:, None]) & nonempty[None, :]
    next_of_e = jnp.min(jnp.where(later, experts[None, :], E), axis=1)
    next_of_e = jnp.where(next_of_e < E, next_of_e, -1)
    is_next = (next_of_e[:, None] == experts[None, :]).astype(jnp.int32)
    after_of_e = jnp.where(next_of_e >= 0, (is_next * next_of_e[None, :]).sum(axis=1), -1)
    is_te = (tile_expert[:, None] == experts[None, :]).astype(jnp.int32)
    at = lambda table: (is_te * table[None, :]).sum(axis=1)
    run_len = jnp.maximum(at(padded) // MOE_CHUNKS_PER_TILE, 1)
    run_pos = tile - at(e_end - padded) // MOE_CHUNKS_PER_TILE
    next_expert = at(next_of_e)
    streams = (tile < n_used) & (next_expert >= 0)
    piece_lo = jnp.where(streams, (MOE_WEIGHT_PIECES * run_pos) // run_len, 0)
    piece_hi = jnp.where(streams, (MOE_WEIGHT_PIECES * (run_pos + 1)) // run_len, 0)
    as_i32 = lambda a: a.astype(jnp.int32)
    return (as_i32(tile_expert), as_i32(tile_real), src_tile, src_row, as_i32(n_used.reshape(1)),
            as_i32(at(buffer_of_e) % 2), as_i32(next_expert), as_i32(at(after_of_e)),
            as_i32(piece_lo), as_i32(piece_hi), as_i32(chunk_counts.sum(axis=1)))


def _moe_ffn_kernel(texp_ref, treal_ref, ctile_ref, crow_ref, nused_ref,
                    wbuf_ref, wnext_ref, wafter_ref, plo_ref, phi_ref, tused_ref,
                    hs_hbm, wg_hbm, wu_hbm, wd_hbm, ys_hbm,
                    xbuf, ybuf, zbuf, wg_buf, wu_buf, wd_buf, stage_g, stage_u, stage_d,
                    gather_sem, scatter_sem, weight_sem, zero_sem):
    i = pl.program_id(0)
    n_used = nused_ref[0]
    slot = lax.rem(i, 2)
    rows_gu = stage_g.shape[1]
    rows_d = stage_d.shape[1]

    def piece_copies(e, p, s):
        gu_rows = pl.ds(pl.multiple_of(p * rows_gu, rows_gu), rows_gu)
        d_rows = pl.ds(pl.multiple_of(p * rows_d, rows_d), rows_d)
        return (pltpu.make_async_copy(wg_hbm.at[e, gu_rows, :], stage_g.at[s], weight_sem.at[s]),
                pltpu.make_async_copy(wu_hbm.at[e, gu_rows, :], stage_u.at[s], weight_sem.at[s]),
                pltpu.make_async_copy(wd_hbm.at[e, d_rows, :], stage_d.at[s], weight_sem.at[s]))

    def start_piece(e, p):
        for copy in piece_copies(e, p, lax.rem(p, MOE_WEIGHT_STAGES)):
            copy.start()

    def finish_piece(e, p, side):
        s = lax.rem(p, MOE_WEIGHT_STAGES)
        for copy in piece_copies(e, p, s):
            copy.wait()
        gu_rows = pl.ds(pl.multiple_of(p * rows_gu, rows_gu), rows_gu)
        d_rows = pl.ds(pl.multiple_of(p * rows_d, rows_d), rows_d)
        wg_buf[side, gu_rows, :] = stage_g[s].astype(BF16)
        wu_buf[side, gu_rows, :] = stage_u[s].astype(BF16)
        wd_buf[side, d_rows, :] = stage_d[s].astype(BF16)

    ahead = MOE_WEIGHT_STAGES - 1

    def start_first_pieces(e):
        for p in range(ahead):
            start_piece(e, p)

    def stream_pieces(e, lo, hi, side, following):
        @pl.when(hi > lo)
        def _():
            def body(p, carry):
                @pl.when(p + ahead < MOE_WEIGHT_PIECES)
                def _():
                    start_piece(e, p + ahead)
                finish_piece(e, p, side)
                return carry
            lax.fori_loop(lo, hi, body, 0)

            @pl.when((hi == MOE_WEIGHT_PIECES) & (following >= 0))
            def _():
                start_first_pieces(following)

    def chunk_rows(c):
        return pl.ds(pl.multiple_of(c * MOE_CHUNK, MOE_CHUNK), MOE_CHUNK)

    def gather_copy(tile, c, buf):
        g = tile * MOE_CHUNKS_PER_TILE + c
        src = hs_hbm.at[ctile_ref[g], pl.ds(pl.multiple_of(crow_ref[g], MOE_CHUNK), MOE_CHUNK), :]
        return pltpu.make_async_copy(src, xbuf.at[buf, chunk_rows(c), :], gather_sem.at[buf])

    def scatter_copy(tile, c, buf):
        g = tile * MOE_CHUNKS_PER_TILE + c
        dst = ys_hbm.at[ctile_ref[g], pl.ds(pl.multiple_of(crow_ref[g], MOE_CHUNK), MOE_CHUNK), :]
        return pltpu.make_async_copy(ybuf.at[buf, chunk_rows(c), :], dst, scatter_sem.at[buf])

    def for_real_chunks(tile, fn, also=True):
        n_real = treal_ref[tile]
        for c in range(MOE_CHUNKS_PER_TILE):
            pl.when((c < n_real) & also)(functools.partial(fn, c))

    def for_tail_chunks(fn):
        n_token_tiles, cap = ys_hbm.shape[0], ys_hbm.shape[1]

        def per_tile(t, carry):
            def per_chunk(c, carry):
                fn(pltpu.make_async_copy(zbuf, ys_hbm.at[t, chunk_rows(c), :], zero_sem.at[0]))
                return carry
            return lax.fori_loop(tused_ref[t], cap // MOE_CHUNK, per_chunk, carry)
        lax.fori_loop(0, n_token_tiles, per_tile, 0)

    @pl.when(i == 0)
    def _():
        zbuf[...] = jnp.zeros(zbuf.shape, zbuf.dtype)
        for_tail_chunks(lambda copy: copy.start())
        xbuf[...] = jnp.zeros(xbuf.shape, xbuf.dtype)
        for_real_chunks(0, lambda c: gather_copy(0, c, 0).start())
        start_first_pieces(texp_ref[0])
        stream_pieces(texp_ref[0], 0, MOE_WEIGHT_PIECES, wbuf_ref[0], wnext_ref[0])

    @pl.when(i + 1 < n_used)
    def _():
        for_real_chunks(i + 1, lambda c: gather_copy(i + 1, c, 1 - slot).start())

    side = wbuf_ref[i]
    stream_pieces(wnext_ref[i], plo_ref[i], phi_ref[i], 1 - side, wafter_ref[i])

    def tile_work(n_rows):
        for_real_chunks(i, lambda c: gather_copy(i, c, slot).wait())
        before = jnp.maximum(i - 2, 0)
        for_real_chunks(before, lambda c: scatter_copy(before, c, slot).wait(), also=i >= 2)
        d_model = wg_buf.shape[1]
        x = xbuf[slot, 0:n_rows, 0:d_model]
        w = jnp.sum(xbuf[slot, 0:n_rows, d_model:].astype(F32), axis=1, keepdims=True)
        a = jnp.dot(x, wg_buf[side], preferred_element_type=F32)
        u = jnp.dot(x, wu_buf[side], preferred_element_type=F32)
        hid = (jax.nn.silu(a) * u * w).astype(BF16)
        ybuf[slot, 0:n_rows, :] = jnp.dot(hid, wd_buf[side],
                                          preferred_element_type=F32).astype(BF16)
        for_real_chunks(i, lambda c: scatter_copy(i, c, slot).start())

    half = MOE_ROW_TILE // 2
    fits_half = treal_ref[i] * MOE_CHUNK <= half
    pl.when((i < n_used) & jnp.logical_not(fits_half))(functools.partial(tile_work, MOE_ROW_TILE))
    pl.when((i < n_used) & fits_half)(functools.partial(tile_work, half))

    @pl.when(i == n_used - 1)
    def _():
        @pl.when(i >= 1)
        def _():
            for_real_chunks(i - 1, lambda c: scatter_copy(i - 1, c, 1 - slot).wait())
        for_real_chunks(i, lambda c: scatter_copy(i, c, slot).wait())
        for_tail_chunks(lambda copy: copy.wait())


def _moe_ffn(hs, plan, wg, wu, wd):
    nT, cap, row_width = hs.shape
    E, D, F = wg.shape
    n_row_tiles = plan[0].shape[0]
    assert D % MOE_WEIGHT_PIECES == 0 and F % MOE_WEIGHT_PIECES == 0
    grid_spec = pltpu.PrefetchScalarGridSpec(
        num_scalar_prefetch=len(plan),
        grid=(n_row_tiles,),
        in_specs=[pl.BlockSpec(memory_space=pl.ANY)] * 4,
        out_specs=pl.BlockSpec(memory_space=pl.ANY),
        scratch_shapes=[
            pltpu.VMEM((2, MOE_ROW_TILE, row_width), BF16),
            pltpu.VMEM((2, MOE_ROW_TILE, D), BF16),
            pltpu.VMEM((MOE_CHUNK, D), BF16),
            pltpu.VMEM((2, D, F), BF16),
            pltpu.VMEM((2, D, F), BF16),
            pltpu.VMEM((2, F, D), BF16),
            pltpu.VMEM((MOE_WEIGHT_STAGES, D // MOE_WEIGHT_PIECES, F), F32),
            pltpu.VMEM((MOE_WEIGHT_STAGES, D // MOE_WEIGHT_PIECES, F), F32),
            pltpu.VMEM((MOE_WEIGHT_STAGES, F // MOE_WEIGHT_PIECES, D), F32),
            pltpu.SemaphoreType.DMA((2,)),
            pltpu.SemaphoreType.DMA((2,)),
            pltpu.SemaphoreType.DMA((MOE_WEIGHT_STAGES,)),
            pltpu.SemaphoreType.DMA((1,)),
        ],
    )
    return pl.pallas_call(
        _moe_ffn_kernel,
        grid_spec=grid_spec,
        out_shape=jax.ShapeDtypeStruct((nT, cap, D), BF16),
        compiler_params=_params(("arbitrary",)),
        name="moe_ffn",
    )(*plan, hs, wg, wu, wd)


def _moe_combine_kernel(ys_ref, route_ref, x1_ref, g_ref, o_ref):
    tm = x1_ref.shape[0]
    cap = ys_ref.shape[0]
    route = route_ref[...]
    slot = lax.broadcasted_iota(jnp.int32, (tm, cap), 1)
    pick = jnp.where(slot == route[:, 0:1].astype(jnp.int32), 1.0,
                     jnp.where(slot == route[:, 1:2].astype(jnp.int32), 1.0, 0.0)).astype(BF16)
    y = x1_ref[...] + jnp.dot(pick, ys_ref[...], preferred_element_type=F32)
    ms = jnp.mean(y * y, axis=-1, keepdims=True)
    o_ref[...] = y * lax.rsqrt(ms + RMS_EPS) * g_ref[...]


def _moe_combine(ys, route, x1, g, tm=MOE_TOKEN_TILE):
    S, D = x1.shape
    cap = ys.shape[1]
    row = lambda i: (i, 0)
    return pl.pallas_call(
        _moe_combine_kernel,
        grid=(S // tm,),
        in_specs=[
            pl.BlockSpec((None, cap, D), lambda i: (i, 0, 0)),
            pl.BlockSpec((tm, ROUTE_LANES), row),
            pl.BlockSpec((tm, D), row),
            pl.BlockSpec((1, D), lambda i: (0, 0)),
        ],
        out_specs=pl.BlockSpec((tm, D), row),
        out_shape=jax.ShapeDtypeStruct((S, D), F32),
        compiler_params=_params(("parallel",)),
        name="moe_combine",
    )(ys, route, x1, g)


def _router_weights(w_r_group, b_r_group, w_r_expert, b_r_expert):
    D = w_r_group.shape[0]
    w = jnp.concatenate(
        [w_r_group, jnp.transpose(w_r_expert, (1, 0, 2)).reshape(D, N_EXPERTS)], axis=1)
    b = jnp.concatenate([b_r_group, b_r_expert.reshape(N_EXPERTS)])
    pad = ROUTE_LANES - w.shape[1]
    w = jnp.pad(w, ((0, 0), (0, pad)))
    b = jnp.pad(b, (0, pad)).reshape(1, ROUTE_LANES)
    w_hi = w.astype(BF16)
    w_lo = (w - w_hi.astype(F32)).astype(BF16)
    return w_hi, w_lo, b


def kernel(x, norm_mix, w_in, w_pool, pool_scale, w_branch_pool, w_branch_attn, w_out, norm_ffn,
           w_r_group, b_r_group, w_r_expert, b_r_expert, w_gate, w_up, w_down, norm_final):
    B, S, D = x.shape
    depth = w_in.shape[0]
    assert depth == 1, "the final rms_norm is fused into the expert kernel of a single layer"
    slopes = jnp.exp2(-8.0 * jnp.arange(1, ATTN_HEADS + 1, dtype=F32) / ATTN_HEADS)
    slopes = jnp.broadcast_to(slopes[:, None, None], (ATTN_HEADS, 1, 128))
    outs = []
    for b in range(B):
        xb = x[b]
        for l in range(depth):
            proj = _inproj(_rms_norm(xb, norm_mix[l].reshape(1, D)), w_in[l])
            y_pool = _pool_mixer(proj, w_pool[l].astype(BF16), pool_scale[l].reshape(1, POOL_WIDTH))
            qT, k_aug, vT, sel = _moba_gate(proj, slopes)
            y_attn = _moba_attention(k_aug, qT, vT, sel, slopes)
            wr_hi, wr_lo, br = _router_weights(w_r_group[l], b_r_group[l], w_r_expert[l], b_r_expert[l])
            x1 = _merge_out(y_pool, y_attn, proj, xb, w_branch_pool[l].astype(BF16),
                            w_branch_attn[l].astype(BF16), w_out[l].astype(BF16))
            hs, route, cnt = _route_sort(x1, norm_ffn[l].reshape(1, D), wr_hi, wr_lo, br)
            nT = hs.shape[0]
            chunk_counts = cnt[:, 0, EXPERT_LANE0:EXPERT_LANE0 + N_EXPERTS].astype(jnp.int32)
            max_chunks = nT * ((2 * MOE_TOKEN_TILE + N_EXPERTS * (MOE_CHUNK - 1)) // MOE_CHUNK)
            n_row_tiles = -(-(max_chunks + N_EXPERTS * (MOE_CHUNKS_PER_TILE - 1))
                            // MOE_CHUNKS_PER_TILE)
            plan = _moe_plan(chunk_counts, n_row_tiles, n_row_tiles * MOE_CHUNKS_PER_TILE)
            ys = _moe_ffn(hs, plan, w_gate[l], w_up[l], w_down[l])
            xb = _moe_combine(ys, route, x1, norm_final.reshape(1, D))
        outs.append(xb)
    return jnp.stack(outs, axis=0)
```

```python
import functools

import jax
import jax.numpy as jnp
from jax import lax
from jax.experimental import pallas as pl
from jax.experimental.pallas import tpu as pltpu

F32 = jnp.float32
BF16 = jnp.bfloat16

POOL_WINDOWS = (2, 4, 8, 16)
MAX_WINDOW = 16
POOL_WIDTH = 1024
POOL_GROUP = 256
HEAD_DIM = 128
ATTN_HEADS = 8
ATTN_WIDTH = 1024
MOBA_BLOCK = 256
MOBA_TOPK = 3
N_GROUPS = 4
EXPERTS_PER_GROUP = 4
N_EXPERTS = 16
ROUTE_LANES = 128
EXPERT_LANE0 = N_GROUPS
RMS_EPS = 1e-6
NEG_INF = -1e30
LOG2_E = 1.4426950408889634
QUERY_SCALE = (HEAD_DIM ** -0.5) * LOG2_E
KEY_AUG = 2 * HEAD_DIM
ALIBI_TERMS = 3
VALUE_AUG = HEAD_DIM + 16
ATTN_KV_UNROLL = 3
ATTN_TILES_PER_STEP = 4
MOE_CHUNK = 16
MOE_TOKEN_TILE = 256
ROUTE_TILES_PER_STEP = 4
MOE_ROW_TILE = 256
MOE_CHUNKS_PER_TILE = MOE_ROW_TILE // MOE_CHUNK
MOE_WEIGHT_TERMS = 3
MOE_WEIGHT_PIECES = 8
MOE_WEIGHT_STAGES = 4
assert MOE_WEIGHT_STAGES - 1 <= MOE_WEIGHT_PIECES

V7X_VMEM_LIMIT_BYTES = 56 * 1024 * 1024


def _params(semantics, vmem=V7X_VMEM_LIMIT_BYTES, flags=None):
    return pltpu.CompilerParams(dimension_semantics=semantics, vmem_limit_bytes=vmem, flags=flags)


def _rms_norm_kernel(x_ref, g_ref, o_ref):
    x = x_ref[...]
    ms = jnp.mean(x * x, axis=-1, keepdims=True)
    o_ref[...] = (x * lax.rsqrt(ms + RMS_EPS) * g_ref[...]).astype(o_ref.dtype)


def _rms_norm(x, g, tm=512):
    S, D = x.shape
    return pl.pallas_call(
        _rms_norm_kernel,
        grid=(S // tm,),
        in_specs=[pl.BlockSpec((tm, D), lambda i: (i, 0)), pl.BlockSpec((1, D), lambda i: (0, 0))],
        out_specs=pl.BlockSpec((tm, D), lambda i: (i, 0)),
        out_shape=jax.ShapeDtypeStruct((S, D), BF16),
        compiler_params=_params(("parallel",)),
        name="rms_norm",
    )(x, g)


def _inproj_kernel(h_ref, w_ref, o_ref, wb_ref):
    @pl.when(pl.program_id(1) == 0)
    def _():
        wb_ref[...] = w_ref[...].astype(BF16)

    col0 = pl.program_id(0) * o_ref.shape[1]
    is_q = (col0 >= POOL_WIDTH) & (col0 < POOL_WIDTH + ATTN_WIDTH)
    factor = jnp.where(is_q, QUERY_SCALE, 1.0).astype(F32)
    o_ref[...] = (jnp.dot(h_ref[...], wb_ref[...], preferred_element_type=F32)
                  * factor).astype(o_ref.dtype)


def _inproj(h, w, tm=2048, tn=1024):
    S, D = h.shape
    N = w.shape[1]
    tm = min(tm, S)
    assert POOL_WIDTH % tn == 0 and ATTN_WIDTH % tn == 0
    return pl.pallas_call(
        _inproj_kernel,
        grid=(N // tn, S // tm),
        in_specs=[
            pl.BlockSpec((tm, D), lambda j, i: (i, 0)),
            pl.BlockSpec((D, tn), lambda j, i: (0, j)),
        ],
        out_specs=pl.BlockSpec((tm, tn), lambda j, i: (i, j)),
        out_shape=jax.ShapeDtypeStruct((S, N), BF16),
        scratch_shapes=[pltpu.VMEM((D, tn), BF16)],
        compiler_params=_params(("parallel", "arbitrary")),
        name="inproj",
    )(h, w)


def _pool_kernel(cur_ref, prev_ref, w_ref, scale_ref, o_ref, ext_ref):
    i = pl.program_id(0)
    tm = cur_ref.shape[0]
    u = cur_ref[...].astype(F32)
    halo = jnp.where(i > 0, prev_ref[...].astype(F32), 0.0)
    ext_ref[0:MAX_WINDOW, :] = halo
    ext_ref[MAX_WINDOW:MAX_WINDOW + tm, :] = u
    t = i * tm + lax.broadcasted_iota(jnp.int32, (tm, 1), 0)
    for g, w in enumerate(POOL_WINDOWS):
        cols = slice(g * POOL_GROUP, (g + 1) * POOL_GROUP)
        ug = u[:, cols]
        wsum = ug
        for s in range(1, w):
            wsum = wsum + ext_ref[MAX_WINDOW - s:MAX_WINDOW - s + tm, cols]
        cnt = jnp.minimum(t + 1, w).astype(F32)
        mixed = (wsum / cnt - ug).astype(BF16)
        y = jnp.dot(mixed, w_ref[g], preferred_element_type=F32)
        o_ref[:, cols] = (y * scale_ref[:, cols]).astype(o_ref.dtype)


def _pool_mixer(proj, w_pool, pool_scale, tm=512):
    S = proj.shape[0]
    halo_blocks = tm // MAX_WINDOW
    return pl.pallas_call(
        _pool_kernel,
        grid=(S // tm,),
        in_specs=[
            pl.BlockSpec((tm, POOL_WIDTH), lambda i: (i, 0)),
            pl.BlockSpec((MAX_WINDOW, POOL_WIDTH),
                         lambda i: (jnp.maximum(i * halo_blocks - 1, 0), 0)),
            pl.BlockSpec((len(POOL_WINDOWS), POOL_GROUP, POOL_GROUP), lambda i: (0, 0, 0)),
            pl.BlockSpec((1, POOL_WIDTH), lambda i: (0, 0)),
        ],
        out_specs=pl.BlockSpec((tm, POOL_WIDTH), lambda i: (i, 0)),
        out_shape=jax.ShapeDtypeStruct((S, POOL_WIDTH), BF16),
        scratch_shapes=[pltpu.VMEM((tm + MAX_WINDOW, POOL_WIDTH), F32)],
        compiler_params=_params(("parallel",)),
        name="pool_mixer",
    )(proj, proj, w_pool, pool_scale)


def _alibi_query_rows(slope_ref):
    slope2 = slope_ref[...][:, 0:1] * LOG2_E
    q_extra = jnp.zeros((KEY_AUG - HEAD_DIM, MOBA_BLOCK), F32)
    row = lax.broadcasted_iota(jnp.int32, q_extra.shape, 0)
    rest = slope2
    for n in range(ALIBI_TERMS):
        piece = rest.astype(BF16).astype(F32)
        q_extra = jnp.where(row == n, piece, q_extra)
        rest = rest - piece
    return q_extra.astype(BF16)


def _moba_gate_kernel(q_ref, k_ref, v_ref, qT_ref, vT_ref, sel_ref):
    S = q_ref.shape[0]
    nb = S // MOBA_BLOCK
    topk = min(MOBA_TOPK, nb)
    kf = k_ref[...].astype(F32).reshape(nb, MOBA_BLOCK, HEAD_DIM)
    kmean = jnp.sum(kf, axis=1) * (1.0 / MOBA_BLOCK)
    km_hi = kmean.astype(BF16)
    km_lo = (kmean - km_hi.astype(F32)).astype(BF16)
    blk = lax.broadcasted_iota(jnp.int32, (nb, MOBA_BLOCK), 0)
    v_row = lax.broadcasted_iota(jnp.int32, (VALUE_AUG - HEAD_DIM, MOBA_BLOCK), 0)
    v_extra = jnp.where(v_row == 0, 1.0, 0.0).astype(BF16)

    def body(i, carry):
        rows = pl.ds(pl.multiple_of(i * MOBA_BLOCK, MOBA_BLOCK), MOBA_BLOCK)
        qT = q_ref[rows, :].astype(F32).T.astype(BF16)
        qT_ref[i] = qT
        vT_ref[i, 0:HEAD_DIM, :] = v_ref[rows, :].astype(F32).T.astype(BF16)
        vT_ref[i, HEAD_DIM:VALUE_AUG, :] = v_extra
        gate = (jnp.dot(km_hi, qT, preferred_element_type=F32)
                + jnp.dot(km_lo, qT, preferred_element_type=F32)) * (1.0 / QUERY_SCALE)
        gate = jnp.where(blk < i, gate, NEG_INF)
        sel = jnp.zeros((nb, MOBA_BLOCK), F32)
        for _ in range(topk):
            best = jnp.max(gate, axis=0, keepdims=True)
            idx = jnp.min(jnp.where(gate == best, blk, nb), axis=0, keepdims=True)
            hit = blk == idx
            sel = jnp.where(hit & (blk < i), 1.0, sel)
            gate = jnp.where(hit, -jnp.inf, gate)
        sel_ref[i] = sel
        return carry

    lax.fori_loop(0, nb, body, 0, unroll=4 if nb % 4 == 0 else 1)


def _moba_gate(proj):
    S = proj.shape[0]
    nb = S // MOBA_BLOCK
    H = ATTN_HEADS
    q0 = POOL_WIDTH // HEAD_DIM
    k0 = q0 + H
    v0 = k0 + H
    blocked = lambda h: (h, 0, 0, 0)
    return pl.pallas_call(
        _moba_gate_kernel,
        grid=(H,),
        in_specs=[
            pl.BlockSpec((S, HEAD_DIM), lambda h: (0, q0 + h)),
            pl.BlockSpec((S, HEAD_DIM), lambda h: (0, k0 + h)),
            pl.BlockSpec((S, HEAD_DIM), lambda h: (0, v0 + h)),
        ],
        out_specs=[
            pl.BlockSpec((None, nb, HEAD_DIM, MOBA_BLOCK), blocked),
            pl.BlockSpec((None, nb, VALUE_AUG, MOBA_BLOCK), blocked),
            pl.BlockSpec((None, nb, nb, MOBA_BLOCK), blocked),
        ],
        out_shape=[
            jax.ShapeDtypeStruct((H, nb, HEAD_DIM, MOBA_BLOCK), BF16),
            jax.ShapeDtypeStruct((H, nb, VALUE_AUG, MOBA_BLOCK), BF16),
            jax.ShapeDtypeStruct((H, nb, nb, MOBA_BLOCK), F32),
        ],
        compiler_params=_params(("parallel",)),
        name="moba_gate",
    )(proj, proj, proj)


def _moba_attn_kernel(slope_ref, qT_ref, qT_next_ref, sel_ref, k_ref, vT_ref, o_ref,
                      ka_ref, *scratch):
    first = pl.program_id(1) * ATTN_TILES_PER_STEP
    nb = sel_ref.shape[1]

    @pl.when(first == 0)
    def _():
        pos = lax.broadcasted_iota(jnp.int32, (MOBA_BLOCK, KEY_AUG - HEAD_DIM), 0).astype(F32)
        col = lax.broadcasted_iota(jnp.int32, (MOBA_BLOCK, KEY_AUG - HEAD_DIM), 1)
        k_extra = jnp.where(col < ALIBI_TERMS, pos, 0.0).astype(BF16)

        def per_block(j, carry):
            rows = pl.ds(pl.multiple_of(j * MOBA_BLOCK, MOBA_BLOCK), MOBA_BLOCK)
            ka_ref[rows, 0:HEAD_DIM] = k_ref[rows, :]
            ka_ref[rows, HEAD_DIM:KEY_AUG] = k_extra
            return carry
        lax.fori_loop(0, nb, per_block, 0)

    q_extra = _alibi_query_rows(slope_ref)
    augment = lambda q: jnp.concatenate([q, q_extra], axis=0)
    for t in range(ATTN_TILES_PER_STEP):
        last = t + 1 == ATTN_TILES_PER_STEP
        _moba_attn_tile(first + t, t == 0, slope_ref, augment(qT_ref[t]),
                        augment(qT_next_ref[...] if last else qT_ref[t + 1]), sel_ref.at[t],
                        ka_ref, vT_ref, o_ref.at[t * MOBA_BLOCK:(t + 1) * MOBA_BLOCK, :], *scratch)


def _moba_attn_tile(i, may_start_head, slope_ref, qT, q_next, sel_ref, k_ref, vT_ref, o_ref,
                    s0_ref, s1_ref, p0_ref, p1_ref, s_own_ref):
    nb = sel_ref.shape[0]
    slope2 = slope_ref[...][:, 0:1] * LOG2_E

    def block_of(t, u):
        return jnp.clip(t * ATTN_KV_UNROLL + u, 0, nb - 1)

    def key_rows(j):
        return pl.ds(pl.multiple_of(j * MOBA_BLOCK, MOBA_BLOCK), MOBA_BLOCK)

    def issue_scores(t, s_ref, queries=qT):
        for u in range(ATTN_KV_UNROLL):
            s_ref[u] = jnp.dot(k_ref[key_rows(block_of(t, u)), :], queries,
                               preferred_element_type=F32)

    def apply_probs(t, p_ref, alpha, acc):
        acc = alpha * acc
        for u in range(ATTN_KV_UNROLL):
            acc = acc + jnp.dot(vT_ref[block_of(t, u)], p_ref[u], preferred_element_type=F32)
        return acc

    def softmax_group(t, s_ref, p_ref, m):
        m_new = m
        shifts = []
        for u in range(ATTN_KV_UNROLL):
            j = t * ATTN_KV_UNROLL + u
            valid = jnp.where(j < i, sel_ref[pl.ds(block_of(t, u), 1), :], 0.0) > 0.0
            gap = slope2 * ((i - j) * MOBA_BLOCK).astype(F32)
            top = jnp.max(s_ref[u], axis=0, keepdims=True) - gap
            m_new = jnp.maximum(m_new, jnp.where(valid, top, NEG_INF))
            shifts.append((valid, gap))
        alpha = jnp.exp2(m - m_new)
        for u, (valid, gap) in enumerate(shifts):
            p = jnp.exp2(s_ref[u] - jnp.where(valid, m_new + gap, jnp.inf))
            p_ref[u] = p.astype(BF16)
        return m_new, alpha

    def body(r, carry):
        m, acc, alpha = carry
        acc = apply_probs(2 * r - 1, p1_ref, alpha, acc)
        m, alpha = softmax_group(2 * r, s0_ref, p0_ref, m)
        issue_scores(2 * r + 1, s1_ref)
        acc = apply_probs(2 * r, p0_ref, alpha, acc)
        m, alpha = softmax_group(2 * r + 1, s1_ref, p1_ref, m)
        issue_scores(2 * r + 2, s0_ref)
        return m, acc, alpha

    n_groups = (i + ATTN_KV_UNROLL - 1) // ATTN_KV_UNROLL
    n_pairs = n_groups // 2
    if may_start_head:
        @pl.when(i == 0)
        def _():
            s0_ref[...] = jnp.zeros(s0_ref.shape, F32)
            s_own_ref[...] = jnp.dot(k_ref[key_rows(0), :], qT, preferred_element_type=F32)

    p1_ref[...] = jnp.zeros(p1_ref.shape, BF16)
    kpos = lax.broadcasted_iota(jnp.int32, (MOBA_BLOCK, MOBA_BLOCK), 0)
    qpos = lax.broadcasted_iota(jnp.int32, (MOBA_BLOCK, MOBA_BLOCK), 1)
    s_own = jnp.where(qpos >= kpos, s_own_ref[...], NEG_INF)
    top_own = jnp.max(s_own, axis=0, keepdims=True)
    row = jnp.zeros((1, MOBA_BLOCK), F32)
    init = (row + NEG_INF, jnp.zeros((VALUE_AUG, MOBA_BLOCK), F32), row + 1.0)
    m, acc, alpha = lax.fori_loop(0, n_pairs, body, init)

    def finish(has_last_group):
        m_, acc_ = m, apply_probs(2 * n_pairs - 1, p1_ref, alpha, acc)
        if has_last_group:
            m_, alpha_ = softmax_group(2 * n_pairs, s0_ref, p0_ref, m_)
            acc_ = apply_probs(2 * n_pairs, p0_ref, alpha_, acc_)
        m_new = jnp.maximum(m_, top_own)
        p = jnp.exp2(s_own - m_new)
        acc_ = (jnp.exp2(m_ - m_new) * acc_
                + jnp.dot(vT_ref[i], p.astype(BF16), preferred_element_type=F32))
        out = acc_[0:HEAD_DIM] / acc_[HEAD_DIM:HEAD_DIM + 1]
        o_ref[...] = out.T.astype(o_ref.dtype)
        issue_scores(0, s0_ref, q_next)
        s_own_ref[...] = jnp.dot(k_ref[key_rows(jnp.minimum(i + 1, nb - 1)), :], q_next,
                                 preferred_element_type=F32)

    pl.when(n_groups > 2 * n_pairs)(functools.partial(finish, True))
    pl.when(n_groups == 2 * n_pairs)(functools.partial(finish, False))


def _moba_attention(proj, qT, vT, sel, slopes):
    S = proj.shape[0]
    H = ATTN_HEADS
    nb = S // MOBA_BLOCK
    k0 = POOL_WIDTH // HEAD_DIM + H
    per_step = ATTN_TILES_PER_STEP
    assert nb % per_step == 0
    return pl.pallas_call(
        _moba_attn_kernel,
        grid=(H, nb // per_step),
        in_specs=[
            pl.BlockSpec((None, 1, 128), lambda h, g: (h, 0, 0)),
            pl.BlockSpec((None, per_step, HEAD_DIM, MOBA_BLOCK), lambda h, g: (h, g, 0, 0)),
            pl.BlockSpec((None, None, HEAD_DIM, MOBA_BLOCK),
                         lambda h, g: (h, jnp.minimum((g + 1) * per_step, nb - 1), 0, 0)),
            pl.BlockSpec((None, per_step, nb, MOBA_BLOCK), lambda h, g: (h, g, 0, 0)),
            pl.BlockSpec((S, HEAD_DIM), lambda h, g: (0, k0 + h)),
            pl.BlockSpec((None, nb, VALUE_AUG, MOBA_BLOCK), lambda h, g: (h, 0, 0, 0)),
        ],
        out_specs=pl.BlockSpec((per_step * MOBA_BLOCK, HEAD_DIM), lambda h, g: (g, h)),
        out_shape=jax.ShapeDtypeStruct((S, ATTN_WIDTH), BF16),
        scratch_shapes=[
            pltpu.VMEM((S, KEY_AUG), BF16),
            pltpu.VMEM((ATTN_KV_UNROLL, MOBA_BLOCK, MOBA_BLOCK), F32),
            pltpu.VMEM((ATTN_KV_UNROLL, MOBA_BLOCK, MOBA_BLOCK), F32),
            pltpu.VMEM((ATTN_KV_UNROLL, MOBA_BLOCK, MOBA_BLOCK), BF16),
            pltpu.VMEM((ATTN_KV_UNROLL, MOBA_BLOCK, MOBA_BLOCK), BF16),
            pltpu.VMEM((MOBA_BLOCK, MOBA_BLOCK), F32),
        ],
        compiler_params=_params(("arbitrary", "arbitrary")),
        name="moba_attention",
    )(slopes, qT, qT, sel, proj, vT)


def _first_lane_of_max(vals, lane):
    best = jnp.max(vals, axis=1, keepdims=True)
    idx = jnp.min(jnp.where(vals == best, lane, ROUTE_LANES), axis=1, keepdims=True)
    return best, idx


def _merge_out_kernel(yp_ref, ya_ref, glp_ref, gla_ref, x_ref, wbp_ref, wba_ref, wout_ref, x1_ref):
    bp = jnp.dot(yp_ref[...], wbp_ref[...], preferred_element_type=F32)
    ba = jnp.dot(ya_ref[...], wba_ref[...], preferred_element_type=F32)
    merged = (jax.nn.sigmoid(glp_ref[...].astype(F32)) * bp
              + jax.nn.sigmoid(gla_ref[...].astype(F32)) * ba)
    x1_ref[...] = x_ref[...] + jnp.dot(merged.astype(BF16), wout_ref[...],
                                       preferred_element_type=F32)


def _merge_out(yp, ya, proj, x, wbp, wba, wout, tm=512):
    S, D = x.shape
    row = lambda i: (i, 0)
    glp_blk = (POOL_WIDTH + 3 * ATTN_WIDTH) // D
    resident = lambda shape: pl.BlockSpec(shape, lambda i: (0, 0), pipeline_mode=pl.Buffered(1))
    return pl.pallas_call(
        _merge_out_kernel,
        grid=(S // tm,),
        in_specs=[
            pl.BlockSpec((tm, POOL_WIDTH), row),
            pl.BlockSpec((tm, ATTN_WIDTH), row),
            pl.BlockSpec((tm, D), lambda i: (i, glp_blk)),
            pl.BlockSpec((tm, D), lambda i: (i, glp_blk + 1)),
            pl.BlockSpec((tm, D), row),
            resident((POOL_WIDTH, D)),
            resident((ATTN_WIDTH, D)),
            resident((D, D)),
        ],
        out_specs=pl.BlockSpec((tm, D), row),
        out_shape=jax.ShapeDtypeStruct((S, D), F32),
        compiler_params=_params(("parallel",)),
        name="merge_out",
    )(yp, ya, proj, proj, x, wbp, wba, wout)


def _route_sort_kernel(x1_ref, g_ref, wr_hi_ref, wr_lo_ref, br_ref, hs_ref, route_ref, cnt_ref):
    n_tiles = ROUTE_TILES_PER_STEP
    tm = x1_ref.shape[0] // n_tiles
    cap = hs_ref.shape[1]
    d_model = x1_ref.shape[1]
    lane = lax.broadcasted_iota(jnp.int32, (tm, ROUTE_LANES), 1)
    tiles = [{"index": t, "rows": slice(t * tm, (t + 1) * tm)} for t in range(n_tiles)]

    def normalise(s):
        x1 = x1_ref[s["rows"], :]
        ms = jnp.mean(x1 * x1, axis=-1, keepdims=True)
        h2 = x1 * lax.rsqrt(ms + RMS_EPS) * g_ref[...]
        s["h2_hi"] = h2.astype(BF16)
        s["h2_lo"] = (h2 - s["h2_hi"].astype(F32)).astype(BF16)

    def router_logits(s):
        s["logits"] = (jnp.dot(s["h2_hi"], wr_hi_ref[...], preferred_element_type=F32)
                       + jnp.dot(s["h2_lo"], wr_hi_ref[...], preferred_element_type=F32)
                       + jnp.dot(s["h2_hi"], wr_lo_ref[...], preferred_element_type=F32)
                       + br_ref[...])

    def choose_experts(s):
        logits = s["logits"]
        g_logits = jnp.where(lane < N_GROUPS, logits, -jnp.inf)
        g_best, g_idx = _first_lane_of_max(g_logits, lane)
        g_w = 1.0 / jnp.sum(jnp.exp(g_logits - g_best), axis=1, keepdims=True)
        e_lo = EXPERT_LANE0 + EXPERTS_PER_GROUP * g_idx
        e_logits = jnp.where((lane >= e_lo) & (lane < e_lo + EXPERTS_PER_GROUP), logits, -jnp.inf)
        v1, i1 = _first_lane_of_max(e_logits, lane)
        v2, i2 = _first_lane_of_max(jnp.where(lane == i1, -jnp.inf, e_logits), lane)
        e21 = jnp.exp(v2 - v1)
        s["w1"] = g_w / (1.0 + e21)
        s["w2"] = g_w * e21 / (1.0 + e21)
        s["hit1"] = lane == i1
        s["hit2"] = lane == i2

    def sort_by_expert(s):
        hit1, hit2 = s["hit1"], s["hit2"]
        member = jnp.where(hit1, 1.0, jnp.where(hit2, 1.0, 0.0))
        r_tok = lax.broadcasted_iota(jnp.int32, (tm, tm), 0)
        c_tok = lax.broadcasted_iota(jnp.int32, (tm, tm), 1)
        earlier = jnp.where(c_tok < r_tok, 1.0, 0.0).astype(BF16)
        rank = jnp.dot(earlier, member.astype(BF16), preferred_element_type=F32)
        count = jnp.sum(member, axis=0, keepdims=True)
        chunks = jnp.floor((count + (MOE_CHUNK - 1)) * (1.0 / MOE_CHUNK))
        r_l = lax.broadcasted_iota(jnp.int32, (ROUTE_LANES, ROUTE_LANES), 0)
        c_l = lax.broadcasted_iota(jnp.int32, (ROUTE_LANES, ROUTE_LANES), 1)
        lower_lanes = jnp.where(r_l < c_l, 1.0, 0.0).astype(BF16)
        start = jnp.dot(jnp.broadcast_to(chunks, (8, ROUTE_LANES)).astype(BF16), lower_lanes,
                        preferred_element_type=F32)[0:1] * MOE_CHUNK
        pos = start + rank
        pos1 = jnp.sum(jnp.where(hit1, pos, 0.0), axis=1, keepdims=True)
        pos2 = jnp.sum(jnp.where(hit2, pos, 0.0), axis=1, keepdims=True)
        s["route"] = jnp.where(lane == 0, pos1, jnp.where(lane == 1, pos2, 0.0))
        route_ref[s["rows"], :] = s["route"]
        cnt_ref[s["index"]] = chunks

    def weight_pieces(w):
        pieces = jnp.zeros(lane.shape, F32)
        rest = w
        for n in range(MOE_WEIGHT_TERMS):
            piece = rest.astype(BF16).astype(F32)
            pieces = jnp.where(lane == n, piece, pieces)
            rest = rest - piece
        return pieces.astype(BF16)

    def compact(s):
        route_t = s["route"].T
        slot = lax.broadcasted_iota(jnp.int32, (cap, tm), 0)
        first = jnp.where(slot == route_t[0:1, :].astype(jnp.int32), 1.0, 0.0).astype(BF16)
        second = jnp.where(slot == route_t[1:2, :].astype(jnp.int32), 1.0, 0.0).astype(BF16)
        hs_ref[s["index"], :, 0:d_model] = jnp.dot(
            first + second, s["h2_hi"], preferred_element_type=F32).astype(BF16)
        slot_w = (jnp.dot(first, weight_pieces(s["w1"]), preferred_element_type=F32)
                  + jnp.dot(second, weight_pieces(s["w2"]), preferred_element_type=F32))
        hs_ref[s["index"], :, d_model:] = slot_w.astype(BF16)

    for stage in (normalise, router_logits, choose_experts, sort_by_expert, compact):
        for s in tiles:
            stage(s)


def _moe_cap(tm):
    worst = 2 * tm + N_EXPERTS * (MOE_CHUNK - 1)
    return -(-worst // 128) * 128


def _route_sort(x1, g, wr_hi, wr_lo, br, tm=MOE_TOKEN_TILE):
    S, D = x1.shape
    nT = S // tm
    cap = _moe_cap(tm)
    full = lambda i: (0, 0)
    row = lambda i: (i, 0)
    per_step = ROUTE_TILES_PER_STEP
    return pl.pallas_call(
        _route_sort_kernel,
        grid=(nT // per_step,),
        in_specs=[
            pl.BlockSpec((per_step * tm, D), row),
            pl.BlockSpec((1, D), full),
            pl.BlockSpec((D, ROUTE_LANES), full),
            pl.BlockSpec((D, ROUTE_LANES), full),
            pl.BlockSpec((1, ROUTE_LANES), full),
        ],
        out_specs=[
            pl.BlockSpec((per_step, cap, D + ROUTE_LANES), lambda i: (i, 0, 0)),
            pl.BlockSpec((per_step * tm, ROUTE_LANES), row),
            pl.BlockSpec((per_step, 1, ROUTE_LANES), lambda i: (i, 0, 0)),
        ],
        out_shape=[
            jax.ShapeDtypeStruct((nT, cap, D + ROUTE_LANES), BF16),
            jax.ShapeDtypeStruct((S, ROUTE_LANES), F32),
            jax.ShapeDtypeStruct((nT, 1, ROUTE_LANES), F32),
        ],
        compiler_params=_params(("parallel",)),
        name="route_sort",
    )(x1, g, wr_hi, wr_lo, br)


def _moe_plan(chunk_counts, n_row_tiles, n_chunk_slots):
    nT, E = chunk_counts.shape
    per_expert = chunk_counts.T
    seg_start = (jnp.cumsum(chunk_counts, axis=1) - chunk_counts).T
    seg_end = jnp.cumsum(per_expert, axis=1)
    n_chunks = seg_end[:, -1]
    padded = -(-n_chunks // MOE_CHUNKS_PER_TILE) * MOE_CHUNKS_PER_TILE
    e_end = jnp.cumsum(padded)
    n_used = e_end[-1] // MOE_CHUNKS_PER_TILE
    c = jnp.arange(n_chunk_slots, dtype=jnp.int32)
    e_of_c = jnp.minimum((e_end[None, :] <= c[:, None]).sum(axis=1), E - 1)
    is_e = (e_of_c[:, None] == jnp.arange(E)[None, :]).astype(jnp.int32)
    local = c - (is_e * (e_end - padded)[None, :]).sum(axis=1)
    real = local < (is_e * n_chunks[None, :]).sum(axis=1)
    pick_e = lambda table: (is_e[:, :, None] * table[None, :, :]).sum(axis=1)
    seg_end_c = pick_e(seg_end)
    t_of_c = jnp.minimum((seg_end_c <= local[:, None]).sum(axis=1), nT - 1)
    is_t = (t_of_c[:, None] == jnp.arange(nT)[None, :]).astype(jnp.int32)
    pick_t = lambda rows: (rows * is_t).sum(axis=1)
    within = local - (pick_t(seg_end_c) - pick_t(pick_e(per_expert)))
    src_tile = jnp.where(real, t_of_c, 0).astype(jnp.int32)
    src_row = jnp.where(real, (pick_t(pick_e(seg_start)) + within) * MOE_CHUNK, 0).astype(jnp.int32)
    tile = jnp.arange(n_row_tiles, dtype=jnp.int32)
    first = jnp.minimum(tile, n_used - 1) * MOE_CHUNKS_PER_TILE
    tile_expert = jnp.minimum((e_end[None, :] <= first[:, None]).sum(axis=1), E - 1)
    tile_real = real.reshape(n_row_tiles, MOE_CHUNKS_PER_TILE).sum(axis=1)
    experts = jnp.arange(E)
    nonempty = padded > 0
    buffer_of_e = jnp.cumsum(nonempty) - nonempty
    later = (experts[None, :] > experts[:, None]) & nonempty[None, :]
    next_of_e = jnp.min(jnp.where(later, experts[None, :], E), axis=1)
    next_of_e = jnp.where(next_of_e < E, next_of_e, -1)
    is_next = (next_of_e[:, None] == experts[None, :]).astype(jnp.int32)
    after_of_e = jnp.where(next_of_e >= 0, (is_next * next_of_e[None, :]).sum(axis=1), -1)
    is_te = (tile_expert[:, None] == experts[None, :]).astype(jnp.int32)
    at = lambda table: (is_te * table[None, :]).sum(axis=1)
    run_len = jnp.maximum(at(padded) // MOE_CHUNKS_PER_TILE, 1)
    run_pos = tile - at(e_end - padded) // MOE_CHUNKS_PER_TILE
    next_expert = at(next_of_e)
    streams = (tile < n_used) & (next_expert >= 0)
    piece_lo = jnp.where(streams, (MOE_WEIGHT_PIECES * run_pos) // run_len, 0)
    piece_hi = jnp.where(streams, (MOE_WEIGHT_PIECES * (run_pos + 1)) // run_len, 0)
    as_i32 = lambda a: a.astype(jnp.int32)
    return (as_i32(tile_expert), as_i32(tile_real), src_tile, src_row, as_i32(n_used.reshape(1)),
            as_i32(at(buffer_of_e) % 2), as_i32(next_expert), as_i32(at(after_of_e)),
            as_i32(piece_lo), as_i32(piece_hi), as_i32(chunk_counts.sum(axis=1)))


def _moe_ffn_kernel(texp_ref, treal_ref, ctile_ref, crow_ref, nused_ref,
                    wbuf_ref, wnext_ref, wafter_ref, plo_ref, phi_ref, tused_ref,
                    hs_hbm, wg_hbm, wu_hbm, wd_hbm, ys_hbm,
                    xbuf, ybuf, zbuf, wg_buf, wu_buf, wd_buf, stage_g, stage_u, stage_d,
                    gather_sem, scatter_sem, weight_sem, zero_sem):
    i = pl.program_id(0)
    n_used = nused_ref[0]
    slot = lax.rem(i, 2)
    rows_gu = stage_g.shape[1]
    rows_d = stage_d.shape[1]

    def piece_copies(e, p, s):
        gu_rows = pl.ds(pl.multiple_of(p * rows_gu, rows_gu), rows_gu)
        d_rows = pl.ds(pl.multiple_of(p * rows_d, rows_d), rows_d)
        return (pltpu.make_async_copy(wg_hbm.at[e, gu_rows, :], stage_g.at[s], weight_sem.at[s]),
                pltpu.make_async_copy(wu_hbm.at[e, gu_rows, :], stage_u.at[s], weight_sem.at[s]),
                pltpu.make_async_copy(wd_hbm.at[e, d_rows, :], stage_d.at[s], weight_sem.at[s]))

    def start_piece(e, p):
        for copy in piece_copies(e, p, lax.rem(p, MOE_WEIGHT_STAGES)):
            copy.start()

    def finish_piece(e, p, side):
        s = lax.rem(p, MOE_WEIGHT_STAGES)
        for copy in piece_copies(e, p, s):
            copy.wait()
        gu_rows = pl.ds(pl.multiple_of(p * rows_gu, rows_gu), rows_gu)
        d_rows = pl.ds(pl.multiple_of(p * rows_d, rows_d), rows_d)
        wg_buf[side, gu_rows, :] = stage_g[s].astype(BF16)
        wu_buf[side, gu_rows, :] = stage_u[s].astype(BF16)
        wd_buf[side, d_rows, :] = stage_d[s].astype(BF16)

    ahead = MOE_WEIGHT_STAGES - 1

    def start_first_pieces(e):
        for p in range(ahead):
            start_piece(e, p)

    def stream_pieces(e, lo, hi, side, following):
        @pl.when(hi > lo)
        def _():
            def body(p, carry):
                @pl.when(p + ahead < MOE_WEIGHT_PIECES)
                def _():
                    start_piece(e, p + ahead)
                finish_piece(e, p, side)
                return carry
            lax.fori_loop(lo, hi, body, 0)

            @pl.when((hi == MOE_WEIGHT_PIECES) & (following >= 0))
            def _():
                start_first_pieces(following)

    def chunk_rows(c):
        return pl.ds(pl.multiple_of(c * MOE_CHUNK, MOE_CHUNK), MOE_CHUNK)

    def gather_copy(tile, c, buf):
        g = tile * MOE_CHUNKS_PER_TILE + c
        src = hs_hbm.at[ctile_ref[g], pl.ds(pl.multiple_of(crow_ref[g], MOE_CHUNK), MOE_CHUNK), :]
        return pltpu.make_async_copy(src, xbuf.at[buf, chunk_rows(c), :], gather_sem.at[buf])

    def scatter_copy(tile, c, buf):
        g = tile * MOE_CHUNKS_PER_TILE + c
        dst = ys_hbm.at[ctile_ref[g], pl.ds(pl.multiple_of(crow_ref[g], MOE_CHUNK), MOE_CHUNK), :]
        return pltpu.make_async_copy(ybuf.at[buf, chunk_rows(c), :], dst, scatter_sem.at[buf])

    def for_real_chunks(tile, fn, also=True):
        n_real = treal_ref[tile]
        for c in range(MOE_CHUNKS_PER_TILE):
            pl.when((c < n_real) & also)(functools.partial(fn, c))

    def for_tail_chunks(fn):
        n_token_tiles, cap = ys_hbm.shape[0], ys_hbm.shape[1]

        def per_tile(t, carry):
            def per_chunk(c, carry):
                fn(pltpu.make_async_copy(zbuf, ys_hbm.at[t, chunk_rows(c), :], zero_sem.at[0]))
                return carry
            return lax.fori_loop(tused_ref[t], cap // MOE_CHUNK, per_chunk, carry)
        lax.fori_loop(0, n_token_tiles, per_tile, 0)

    @pl.when(i == 0)
    def _():
        zbuf[...] = jnp.zeros(zbuf.shape, zbuf.dtype)
        for_tail_chunks(lambda copy: copy.start())
        xbuf[...] = jnp.zeros(xbuf.shape, xbuf.dtype)
        for_real_chunks(0, lambda c: gather_copy(0, c, 0).start())
        start_first_pieces(texp_ref[0])
        stream_pieces(texp_ref[0], 0, MOE_WEIGHT_PIECES, wbuf_ref[0], wnext_ref[0])

    @pl.when(i + 1 < n_used)
    def _():
        for_real_chunks(i + 1, lambda c: gather_copy(i + 1, c, 1 - slot).start())

    side = wbuf_ref[i]
    stream_pieces(wnext_ref[i], plo_ref[i], phi_ref[i], 1 - side, wafter_ref[i])

    def tile_work(n_rows):
        for_real_chunks(i, lambda c: gather_copy(i, c, slot).wait())
        before = jnp.maximum(i - 2, 0)
        for_real_chunks(before, lambda c: scatter_copy(before, c, slot).wait(), also=i >= 2)
        d_model = wg_buf.shape[1]
        x = xbuf[slot, 0:n_rows, 0:d_model]
        w = jnp.sum(xbuf[slot, 0:n_rows, d_model:].astype(F32), axis=1, keepdims=True)
        a = jnp.dot(x, wg_buf[side], preferred_element_type=F32)
        u = jnp.dot(x, wu_buf[side], preferred_element_type=F32)
        hid = (jax.nn.silu(a) * u * w).astype(BF16)
        ybuf[slot, 0:n_rows, :] = jnp.dot(hid, wd_buf[side],
                                          preferred_element_type=F32).astype(BF16)
        for_real_chunks(i, lambda c: scatter_copy(i, c, slot).start())

    half = MOE_ROW_TILE // 2
    fits_half = treal_ref[i] * MOE_CHUNK <= half
    pl.when((i < n_used) & jnp.logical_not(fits_half))(functools.partial(tile_work, MOE_ROW_TILE))
    pl.when((i < n_used) & fits_half)(functools.partial(tile_work, half))

    @pl.when(i == n_used - 1)
    def _():
        @pl.when(i >= 1)
        def _():
            for_real_chunks(i - 1, lambda c: scatter_copy(i - 1, c, 1 - slot).wait())
        for_real_chunks(i, lambda c: scatter_copy(i, c, slot).wait())
        for_tail_chunks(lambda copy: copy.wait())


def _moe_ffn(hs, plan, wg, wu, wd):
    nT, cap, row_width = hs.shape
    E, D, F = wg.shape
    n_row_tiles = plan[0].shape[0]
    assert D % MOE_WEIGHT_PIECES == 0 and F % MOE_WEIGHT_PIECES == 0
    grid_spec = pltpu.PrefetchScalarGridSpec(
        num_scalar_prefetch=len(plan),
        grid=(n_row_tiles,),
        in_specs=[pl.BlockSpec(memory_space=pl.ANY)] * 4,
        out_specs=pl.BlockSpec(memory_space=pl.ANY),
        scratch_shapes=[
            pltpu.VMEM((2, MOE_ROW_TILE, row_width), BF16),
            pltpu.VMEM((2, MOE_ROW_TILE, D), BF16),
            pltpu.VMEM((MOE_CHUNK, D), BF16),
            pltpu.VMEM((2, D, F), BF16),
            pltpu.VMEM((2, D, F), BF16),
            pltpu.VMEM((2, F, D), BF16),
            pltpu.VMEM((MOE_WEIGHT_STAGES, D // MOE_WEIGHT_PIECES, F), F32),
            pltpu.VMEM((MOE_WEIGHT_STAGES, D // MOE_WEIGHT_PIECES, F), F32),
            pltpu.VMEM((MOE_WEIGHT_STAGES, F // MOE_WEIGHT_PIECES, D), F32),
            pltpu.SemaphoreType.DMA((2,)),
            pltpu.SemaphoreType.DMA((2,)),
            pltpu.SemaphoreType.DMA((MOE_WEIGHT_STAGES,)),
            pltpu.SemaphoreType.DMA((1,)),
        ],
    )
    return pl.pallas_call(
        _moe_ffn_kernel,
        grid_spec=grid_spec,
        out_shape=jax.ShapeDtypeStruct((nT, cap, D), BF16),
        compiler_params=_params(("arbitrary",)),
        name="moe_ffn",
    )(*plan, hs, wg, wu, wd)


def _moe_combine_kernel(ys_ref, route_ref, x1_ref, g_ref, o_ref):
    tm = x1_ref.shape[0]
    cap = ys_ref.shape[0]
    route = route_ref[...]
    slot = lax.broadcasted_iota(jnp.int32, (tm, cap), 1)
    pick = jnp.where(slot == route[:, 0:1].astype(jnp.int32), 1.0,
                     jnp.where(slot == route[:, 1:2].astype(jnp.int32), 1.0, 0.0)).astype(BF16)
    y = x1_ref[...] + jnp.dot(pick, ys_ref[...], preferred_element_type=F32)
    ms = jnp.mean(y * y, axis=-1, keepdims=True)
    o_ref[...] = y * lax.rsqrt(ms + RMS_EPS) * g_ref[...]


def _moe_combine(ys, route, x1, g, tm=MOE_TOKEN_TILE):
    S, D = x1.shape
    cap = ys.shape[1]
    row = lambda i: (i, 0)
    return pl.pallas_call(
        _moe_combine_kernel,
        grid=(S // tm,),
        in_specs=[
            pl.BlockSpec((None, cap, D), lambda i: (i, 0, 0)),
            pl.BlockSpec((tm, ROUTE_LANES), row),
            pl.BlockSpec((tm, D), row),
            pl.BlockSpec((1, D), lambda i: (0, 0)),
        ],
        out_specs=pl.BlockSpec((tm, D), row),
        out_shape=jax.ShapeDtypeStruct((S, D), F32),
        compiler_params=_params(("parallel",)),
        name="moe_combine",
    )(ys, route, x1, g)


def _router_weights(w_r_group, b_r_group, w_r_expert, b_r_expert):
    D = w_r_group.shape[0]
    w = jnp.concatenate(
        [w_r_group, jnp.transpose(w_r_expert, (1, 0, 2)).reshape(D, N_EXPERTS)], axis=1)
    b = jnp.concatenate([b_r_group, b_r_expert.reshape(N_EXPERTS)])
    pad = ROUTE_LANES - w.shape[1]
    w = jnp.pad(w, ((0, 0), (0, pad)))
    b = jnp.pad(b, (0, pad)).reshape(1, ROUTE_LANES)
    w_hi = w.astype(BF16)
    w_lo = (w - w_hi.astype(F32)).astype(BF16)
    return w_hi, w_lo, b


def kernel(x, norm_mix, w_in, w_pool, pool_scale, w_branch_pool, w_branch_attn, w_out, norm_ffn,
           w_r_group, b_r_group, w_r_expert, b_r_expert, w_gate, w_up, w_down, norm_final):
    B, S, D = x.shape
    depth = w_in.shape[0]
    assert depth == 1, "the final rms_norm is fused into the expert kernel of a single layer"
    slopes = jnp.exp2(-8.0 * jnp.arange(1, ATTN_HEADS + 1, dtype=F32) / ATTN_HEADS)
    slopes = jnp.broadcast_to(slopes[:, None, None], (ATTN_HEADS, 1, 128))
    outs = []
    for b in range(B):
        xb = x[b]
        for l in range(depth):
            proj = _inproj(_rms_norm(xb, norm_mix[l].reshape(1, D)), w_in[l])
            y_pool = _pool_mixer(proj, w_pool[l].astype(BF16), pool_scale[l].reshape(1, POOL_WIDTH))
            qT, vT, sel = _moba_gate(proj)
            y_attn = _moba_attention(proj, qT, vT, sel, slopes)
            wr_hi, wr_lo, br = _router_weights(w_r_group[l], b_r_group[l], w_r_expert[l], b_r_expert[l])
            x1 = _merge_out(y_pool, y_attn, proj, xb, w_branch_pool[l].astype(BF16),
                            w_branch_attn[l].astype(BF16), w_out[l].astype(BF16))
            hs, route, cnt = _route_sort(x1, norm_ffn[l].reshape(1, D), wr_hi, wr_lo, br)
            nT = hs.shape[0]
            chunk_counts = cnt[:, 0, EXPERT_LANE0:EXPERT_LANE0 + N_EXPERTS].astype(jnp.int32)
            max_chunks = nT * ((2 * MOE_TOKEN_TILE + N_EXPERTS * (MOE_CHUNK - 1)) // MOE_CHUNK)
            n_row_tiles = -(-(max_chunks + N_EXPERTS * (MOE_CHUNKS_PER_TILE - 1))
                            // MOE_CHUNKS_PER_TILE)
            plan = _moe_plan(chunk_counts, n_row_tiles, n_row_tiles * MOE_CHUNKS_PER_TILE)
            ys = _moe_ffn(hs, plan, w_gate[l], w_up[l], w_down[l])
            xb = _moe_combine(ys, route, x1, norm_final.reshape(1, D))
        outs.append(xb)
    return jnp.stack(outs, axis=0)
```

```python
import functools

import jax
import jax.numpy as jnp
from jax import lax
from jax.experimental import pallas as pl
from jax.experimental.pallas import tpu as pltpu

F32 = jnp.float32
BF16 = jnp.bfloat16

POOL_WINDOWS = (2, 4, 8, 16)
MAX_WINDOW = 16
POOL_WIDTH = 1024
POOL_GROUP = 256
HEAD_DIM = 128
ATTN_HEADS = 8
ATTN_WIDTH = 1024
MOBA_BLOCK = 256
MOBA_TOPK = 3
N_GROUPS = 4
EXPERTS_PER_GROUP = 4
N_EXPERTS = 16
ROUTE_LANES = 128
EXPERT_LANE0 = N_GROUPS
RMS_EPS = 1e-6
NEG_INF = -1e30
LOG2_E = 1.4426950408889634
QUERY_SCALE = (HEAD_DIM ** -0.5) * LOG2_E
KEY_AUG = 2 * HEAD_DIM
ALIBI_TERMS = 3
VALUE_AUG = HEAD_DIM + 16
ATTN_KV_UNROLL = 3
ATTN_TILES_PER_STEP = 4
MOE_CHUNK = 16
MOE_TOKEN_TILE = 256
ROUTE_TILES_PER_STEP = 4
MOE_ROW_TILE = 256
MOE_CHUNKS_PER_TILE = MOE_ROW_TILE // MOE_CHUNK
MOE_WEIGHT_TERMS = 3
MOE_WEIGHT_PIECES = 8
MOE_WEIGHT_STAGES = 4
assert MOE_WEIGHT_STAGES - 1 <= MOE_WEIGHT_PIECES

V7X_VMEM_LIMIT_BYTES = 56 * 1024 * 1024


def _params(semantics, vmem=V7X_VMEM_LIMIT_BYTES, flags=None):
    return pltpu.CompilerParams(dimension_semantics=semantics, vmem_limit_bytes=vmem, flags=flags)


def _rms_norm_kernel(x_ref, g_ref, o_ref):
    x = x_ref[...]
    ms = jnp.mean(x * x, axis=-1, keepdims=True)
    o_ref[...] = (x * lax.rsqrt(ms + RMS_EPS) * g_ref[...]).astype(o_ref.dtype)


def _rms_norm(x, g, tm=512):
    S, D = x.shape
    return pl.pallas_call(
        _rms_norm_kernel,
        grid=(S // tm,),
        in_specs=[pl.BlockSpec((tm, D), lambda i: (i, 0)), pl.BlockSpec((1, D), lambda i: (0, 0))],
        out_specs=pl.BlockSpec((tm, D), lambda i: (i, 0)),
        out_shape=jax.ShapeDtypeStruct((S, D), BF16),
        compiler_params=_params(("parallel",)),
        name="rms_norm",
    )(x, g)


def _inproj_kernel(h_ref, w_ref, o_ref, wb_ref):
    @pl.when(pl.program_id(1) == 0)
    def _():
        wb_ref[...] = w_ref[...].astype(BF16)

    col0 = pl.program_id(0) * o_ref.shape[1]
    is_q = (col0 >= POOL_WIDTH) & (col0 < POOL_WIDTH + ATTN_WIDTH)
    factor = jnp.where(is_q, QUERY_SCALE, 1.0).astype(F32)
    o_ref[...] = (jnp.dot(h_ref[...], wb_ref[...], preferred_element_type=F32)
                  * factor).astype(o_ref.dtype)


def _inproj(h, w, tm=2048, tn=1024):
    S, D = h.shape
    N = w.shape[1]
    tm = min(tm, S)
    assert POOL_WIDTH % tn == 0 and ATTN_WIDTH % tn == 0
    return pl.pallas_call(
        _inproj_kernel,
        grid=(N // tn, S // tm),
        in_specs=[
            pl.BlockSpec((tm, D), lambda j, i: (i, 0)),
            pl.BlockSpec((D, tn), lambda j, i: (0, j)),
        ],
        out_specs=pl.BlockSpec((tm, tn), lambda j, i: (i, j)),
        out_shape=jax.ShapeDtypeStruct((S, N), BF16),
        scratch_shapes=[pltpu.VMEM((D, tn), BF16)],
        compiler_params=_params(("parallel", "arbitrary")),
        name="inproj",
    )(h, w)


def _pool_kernel(cur_ref, prev_ref, w_ref, scale_ref, o_ref):
    i = pl.program_id(0)
    tm = cur_ref.shape[0]
    u = cur_ref[...].astype(F32)
    halo = jnp.where(i > 0, prev_ref[...].astype(F32), 0.0)
    ext = jnp.concatenate([halo, u], axis=0)
    t = i * tm + lax.broadcasted_iota(jnp.int32, (tm, 1), 0)
    for g, w in enumerate(POOL_WINDOWS):
        cols = slice(g * POOL_GROUP, (g + 1) * POOL_GROUP)
        ug = u[:, cols]
        run = ext[:, cols]
        s = 1
        while s < w:
            run = run + pltpu.roll(run, s, axis=0)
            s *= 2
        wsum = run[MAX_WINDOW:, :]
        cnt = jnp.minimum(t + 1, w).astype(F32)
        mixed = (wsum / cnt - ug).astype(BF16)
        y = jnp.dot(mixed, w_ref[g], preferred_element_type=F32)
        o_ref[:, cols] = (y * scale_ref[:, cols]).astype(o_ref.dtype)


def _pool_mixer(proj, w_pool, pool_scale, tm=512):
    S = proj.shape[0]
    halo_blocks = tm // MAX_WINDOW
    return pl.pallas_call(
        _pool_kernel,
        grid=(S // tm,),
        in_specs=[
            pl.BlockSpec((tm, POOL_WIDTH), lambda i: (i, 0)),
            pl.BlockSpec((MAX_WINDOW, POOL_WIDTH),
                         lambda i: (jnp.maximum(i * halo_blocks - 1, 0), 0)),
            pl.BlockSpec((len(POOL_WINDOWS), POOL_GROUP, POOL_GROUP), lambda i: (0, 0, 0)),
            pl.BlockSpec((1, POOL_WIDTH), lambda i: (0, 0)),
        ],
        out_specs=pl.BlockSpec((tm, POOL_WIDTH), lambda i: (i, 0)),
        out_shape=jax.ShapeDtypeStruct((S, POOL_WIDTH), BF16),
        compiler_params=_params(("parallel",)),
        name="pool_mixer",
    )(proj, proj, w_pool, pool_scale)


def _alibi_query_rows(slope_ref):
    slope2 = slope_ref[...][:, 0:1] * LOG2_E
    q_extra = jnp.zeros((KEY_AUG - HEAD_DIM, MOBA_BLOCK), F32)
    row = lax.broadcasted_iota(jnp.int32, q_extra.shape, 0)
    rest = slope2
    for n in range(ALIBI_TERMS):
        piece = rest.astype(BF16).astype(F32)
        q_extra = jnp.where(row == n, piece, q_extra)
        rest = rest - piece
    return q_extra.astype(BF16)


def _moba_gate_kernel(q_ref, k_ref, v_ref, qT_ref, vT_ref, sel_ref):
    S = q_ref.shape[0]
    nb = S // MOBA_BLOCK
    topk = min(MOBA_TOPK, nb)
    kf = k_ref[...].astype(F32).reshape(nb, MOBA_BLOCK, HEAD_DIM)
    kmean = jnp.sum(kf, axis=1) * (1.0 / MOBA_BLOCK)
    km_hi = kmean.astype(BF16)
    km_lo = (kmean - km_hi.astype(F32)).astype(BF16)
    blk = lax.broadcasted_iota(jnp.int32, (nb, MOBA_BLOCK), 0)
    v_row = lax.broadcasted_iota(jnp.int32, (VALUE_AUG - HEAD_DIM, MOBA_BLOCK), 0)
    v_extra = jnp.where(v_row == 0, 1.0, 0.0).astype(BF16)

    def body(i, carry):
        rows = pl.ds(pl.multiple_of(i * MOBA_BLOCK, MOBA_BLOCK), MOBA_BLOCK)
        qT = q_ref[rows, :].astype(F32).T.astype(BF16)
        qT_ref[i] = qT
        vT_ref[i, 0:HEAD_DIM, :] = v_ref[rows, :].astype(F32).T.astype(BF16)
        vT_ref[i, HEAD_DIM:VALUE_AUG, :] = v_extra
        gate = (jnp.dot(km_hi, qT, preferred_element_type=F32)
                + jnp.dot(km_lo, qT, preferred_element_type=F32)) * (1.0 / QUERY_SCALE)
        gate = jnp.where(blk < i, gate, NEG_INF)
        sel = jnp.zeros((nb, MOBA_BLOCK), F32)
        for _ in range(topk):
            best = jnp.max(gate, axis=0, keepdims=True)
            idx = jnp.min(jnp.where(gate == best, blk, nb), axis=0, keepdims=True)
            hit = blk == idx
            sel = jnp.where(hit & (blk < i), 1.0, sel)
            gate = jnp.where(hit, -jnp.inf, gate)
        sel_ref[i] = sel
        return carry

    lax.fori_loop(0, nb, body, 0, unroll=4 if nb % 4 == 0 else 1)


def _moba_gate(proj):
    S = proj.shape[0]
    nb = S // MOBA_BLOCK
    H = ATTN_HEADS
    q0 = POOL_WIDTH // HEAD_DIM
    k0 = q0 + H
    v0 = k0 + H
    blocked = lambda h: (h, 0, 0, 0)
    return pl.pallas_call(
        _moba_gate_kernel,
        grid=(H,),
        in_specs=[
            pl.BlockSpec((S, HEAD_DIM), lambda h: (0, q0 + h)),
            pl.BlockSpec((S, HEAD_DIM), lambda h: (0, k0 + h)),
            pl.BlockSpec((S, HEAD_DIM), lambda h: (0, v0 + h)),
        ],
        out_specs=[
            pl.BlockSpec((None, nb, HEAD_DIM, MOBA_BLOCK), blocked),
            pl.BlockSpec((None, nb, VALUE_AUG, MOBA_BLOCK), blocked),
            pl.BlockSpec((None, nb, nb, MOBA_BLOCK), blocked),
        ],
        out_shape=[
            jax.ShapeDtypeStruct((H, nb, HEAD_DIM, MOBA_BLOCK), BF16),
            jax.ShapeDtypeStruct((H, nb, VALUE_AUG, MOBA_BLOCK), BF16),
            jax.ShapeDtypeStruct((H, nb, nb, MOBA_BLOCK), F32),
        ],
        compiler_params=_params(("parallel",)),
        name="moba_gate",
    )(proj, proj, proj)


def _moba_attn_kernel(slope_ref, qT_ref, qT_next_ref, sel_ref, k_ref, vT_ref, o_ref,
                      ka_ref, *scratch):
    first = pl.program_id(1) * ATTN_TILES_PER_STEP
    nb = sel_ref.shape[1]

    @pl.when(first == 0)
    def _():
        pos = lax.broadcasted_iota(jnp.int32, (MOBA_BLOCK, KEY_AUG - HEAD_DIM), 0).astype(F32)
        col = lax.broadcasted_iota(jnp.int32, (MOBA_BLOCK, KEY_AUG - HEAD_DIM), 1)
        k_extra = jnp.where(col < ALIBI_TERMS, pos, 0.0).astype(BF16)

        def per_block(j, carry):
            rows = pl.ds(pl.multiple_of(j * MOBA_BLOCK, MOBA_BLOCK), MOBA_BLOCK)
            ka_ref[rows, 0:HEAD_DIM] = k_ref[rows, :]
            ka_ref[rows, HEAD_DIM:KEY_AUG] = k_extra
            return carry
        lax.fori_loop(0, nb, per_block, 0)

    q_extra = _alibi_query_rows(slope_ref)
    augment = lambda q: jnp.concatenate([q, q_extra], axis=0)
    for t in range(ATTN_TILES_PER_STEP):
        last = t + 1 == ATTN_TILES_PER_STEP
        _moba_attn_tile(first + t, t == 0, slope_ref, augment(qT_ref[t]),
                        augment(qT_next_ref[...] if last else qT_ref[t + 1]), sel_ref.at[t],
                        ka_ref, vT_ref, o_ref.at[t * MOBA_BLOCK:(t + 1) * MOBA_BLOCK, :], *scratch)


def _moba_attn_tile(i, may_start_head, slope_ref, qT, q_next, sel_ref, k_ref, vT_ref, o_ref,
                    s0_ref, s1_ref, p0_ref, p1_ref, s_own_ref):
    nb = sel_ref.shape[0]
    slope2 = slope_ref[...][:, 0:1] * LOG2_E

    def block_of(t, u):
        return jnp.clip(t * ATTN_KV_UNROLL + u, 0, nb - 1)

    def key_rows(j):
        return pl.ds(pl.multiple_of(j * MOBA_BLOCK, MOBA_BLOCK), MOBA_BLOCK)

    def issue_scores(t, s_ref, queries=qT):
        for u in range(ATTN_KV_UNROLL):
            s_ref[u] = jnp.dot(k_ref[key_rows(block_of(t, u)), :], queries,
                               preferred_element_type=F32)

    def apply_probs(t, p_ref, alpha, acc):
        acc = alpha * acc
        for u in range(ATTN_KV_UNROLL):
            acc = acc + jnp.dot(vT_ref[block_of(t, u)], p_ref[u], preferred_element_type=F32)
        return acc

    def softmax_group(t, s_ref, p_ref, m):
        m_new = m
        shifts = []
        for u in range(ATTN_KV_UNROLL):
            j = t * ATTN_KV_UNROLL + u
            valid = jnp.where(j < i, sel_ref[pl.ds(block_of(t, u), 1), :], 0.0) > 0.0
            gap = slope2 * ((i - j) * MOBA_BLOCK).astype(F32)
            top = jnp.max(s_ref[u], axis=0, keepdims=True) - gap
            m_new = jnp.maximum(m_new, jnp.where(valid, top, NEG_INF))
            shifts.append((valid, gap))
        alpha = jnp.exp2(m - m_new)
        for u, (valid, gap) in enumerate(shifts):
            p = jnp.exp2(s_ref[u] - jnp.where(valid, m_new + gap, jnp.inf))
            p_ref[u] = p.astype(BF16)
        return m_new, alpha

    def body(r, carry):
        m, acc, alpha = carry
        acc = apply_probs(2 * r - 1, p1_ref, alpha, acc)
        m, alpha = softmax_group(2 * r, s0_ref, p0_ref, m)
        issue_scores(2 * r + 1, s1_ref)
        acc = apply_probs(2 * r, p0_ref, alpha, acc)
        m, alpha = softmax_group(2 * r + 1, s1_ref, p1_ref, m)
        issue_scores(2 * r + 2, s0_ref)
        return m, acc, alpha

    n_groups = (i + ATTN_KV_UNROLL - 1) // ATTN_KV_UNROLL
    n_pairs = n_groups // 2
    if may_start_head:
        @pl.when(i == 0)
        def _():
            s0_ref[...] = jnp.zeros(s0_ref.shape, F32)
            s_own_ref[...] = jnp.dot(k_ref[key_rows(0), :], qT, preferred_element_type=F32)

    p1_ref[...] = jnp.zeros(p1_ref.shape, BF16)
    kpos = lax.broadcasted_iota(jnp.int32, (MOBA_BLOCK, MOBA_BLOCK), 0)
    qpos = lax.broadcasted_iota(jnp.int32, (MOBA_BLOCK, MOBA_BLOCK), 1)
    s_own = jnp.where(qpos >= kpos, s_own_ref[...], NEG_INF)
    top_own = jnp.max(s_own, axis=0, keepdims=True)
    row = jnp.zeros((1, MOBA_BLOCK), F32)
    init = (row + NEG_INF, jnp.zeros((VALUE_AUG, MOBA_BLOCK), F32), row + 1.0)
    m, acc, alpha = lax.fori_loop(0, n_pairs, body, init)

    def finish(has_last_group):
        m_, acc_ = m, apply_probs(2 * n_pairs - 1, p1_ref, alpha, acc)
        if has_last_group:
            m_, alpha_ = softmax_group(2 * n_pairs, s0_ref, p0_ref, m_)
            acc_ = apply_probs(2 * n_pairs, p0_ref, alpha_, acc_)
        m_new = jnp.maximum(m_, top_own)
        p = jnp.exp2(s_own - m_new)
        acc_ = (jnp.exp2(m_ - m_new) * acc_
                + jnp.dot(vT_ref[i], p.astype(BF16), preferred_element_type=F32))
        out = acc_[0:HEAD_DIM] / acc_[HEAD_DIM:HEAD_DIM + 1]
        o_ref[...] = out.T.astype(o_ref.dtype)
        issue_scores(0, s0_ref, q_next)
        s_own_ref[...] = jnp.dot(k_ref[key_rows(jnp.minimum(i + 1, nb - 1)), :], q_next,
                                 preferred_element_type=F32)

    pl.when(n_groups > 2 * n_pairs)(functools.partial(finish, True))
    pl.when(n_groups == 2 * n_pairs)(functools.partial(finish, False))


def _moba_attention(proj, qT, vT, sel, slopes):
    S = proj.shape[0]
    H = ATTN_HEADS
    nb = S // MOBA_BLOCK
    k0 = POOL_WIDTH // HEAD_DIM + H
    per_step = ATTN_TILES_PER_STEP
    assert nb % per_step == 0
    return pl.pallas_call(
        _moba_attn_kernel,
        grid=(H, nb // per_step),
        in_specs=[
            pl.BlockSpec((None, 1, 128), lambda h, g: (h, 0, 0)),
            pl.BlockSpec((None, per_step, HEAD_DIM, MOBA_BLOCK), lambda h, g: (h, g, 0, 0)),
            pl.BlockSpec((None, None, HEAD_DIM, MOBA_BLOCK),
                         lambda h, g: (h, jnp.minimum((g + 1) * per_step, nb - 1), 0, 0)),
            pl.BlockSpec((None, per_step, nb, MOBA_BLOCK), lambda h, g: (h, g, 0, 0)),
            pl.BlockSpec((S, HEAD_DIM), lambda h, g: (0, k0 + h)),
            pl.BlockSpec((None, nb, VALUE_AUG, MOBA_BLOCK), lambda h, g: (h, 0, 0, 0)),
        ],
        out_specs=pl.BlockSpec((per_step * MOBA_BLOCK, HEAD_DIM), lambda h, g: (g, h)),
        out_shape=jax.ShapeDtypeStruct((S, ATTN_WIDTH), BF16),
        scratch_shapes=[
            pltpu.VMEM((S, KEY_AUG), BF16),
            pltpu.VMEM((ATTN_KV_UNROLL, MOBA_BLOCK, MOBA_BLOCK), F32),
            pltpu.VMEM((ATTN_KV_UNROLL, MOBA_BLOCK, MOBA_BLOCK), F32),
            pltpu.VMEM((ATTN_KV_UNROLL, MOBA_BLOCK, MOBA_BLOCK), BF16),
            pltpu.VMEM((ATTN_KV_UNROLL, MOBA_BLOCK, MOBA_BLOCK), BF16),
            pltpu.VMEM((MOBA_BLOCK, MOBA_BLOCK), F32),
        ],
        compiler_params=_params(("arbitrary", "arbitrary")),
        name="moba_attention",
    )(slopes, qT, qT, sel, proj, vT)


def _first_lane_of_max(vals, lane):
    best = jnp.max(vals, axis=1, keepdims=True)
    idx = jnp.min(jnp.where(vals == best, lane, ROUTE_LANES), axis=1, keepdims=True)
    return best, idx


def _merge_out_kernel(yp_ref, ya_ref, glp_ref, gla_ref, x_ref, wbp_ref, wba_ref, wout_ref, x1_ref):
    bp = jnp.dot(yp_ref[...], wbp_ref[...], preferred_element_type=F32)
    ba = jnp.dot(ya_ref[...], wba_ref[...], preferred_element_type=F32)
    merged = (jax.nn.sigmoid(glp_ref[...].astype(F32)) * bp
              + jax.nn.sigmoid(gla_ref[...].astype(F32)) * ba)
    x1_ref[...] = x_ref[...] + jnp.dot(merged.astype(BF16), wout_ref[...],
                                       preferred_element_type=F32)


def _merge_out(yp, ya, proj, x, wbp, wba, wout, tm=512):
    S, D = x.shape
    row = lambda i: (i, 0)
    glp_blk = (POOL_WIDTH + 3 * ATTN_WIDTH) // D
    resident = lambda shape: pl.BlockSpec(shape, lambda i: (0, 0), pipeline_mode=pl.Buffered(1))
    return pl.pallas_call(
        _merge_out_kernel,
        grid=(S // tm,),
        in_specs=[
            pl.BlockSpec((tm, POOL_WIDTH), row),
            pl.BlockSpec((tm, ATTN_WIDTH), row),
            pl.BlockSpec((tm, D), lambda i: (i, glp_blk)),
            pl.BlockSpec((tm, D), lambda i: (i, glp_blk + 1)),
            pl.BlockSpec((tm, D), row),
            resident((POOL_WIDTH, D)),
            resident((ATTN_WIDTH, D)),
            resident((D, D)),
        ],
        out_specs=pl.BlockSpec((tm, D), row),
        out_shape=jax.ShapeDtypeStruct((S, D), F32),
        compiler_params=_params(("parallel",)),
        name="merge_out",
    )(yp, ya, proj, proj, x, wbp, wba, wout)


def _route_sort_kernel(x1_ref, g_ref, wr_hi_ref, wr_lo_ref, br_ref, hs_ref, route_ref, cnt_ref):
    n_tiles = ROUTE_TILES_PER_STEP
    tm = x1_ref.shape[0] // n_tiles
    cap = hs_ref.shape[1]
    d_model = x1_ref.shape[1]
    lane = lax.broadcasted_iota(jnp.int32, (tm, ROUTE_LANES), 1)
    tiles = [{"index": t, "rows": slice(t * tm, (t + 1) * tm)} for t in range(n_tiles)]

    def normalise(s):
        x1 = x1_ref[s["rows"], :]
        ms = jnp.mean(x1 * x1, axis=-1, keepdims=True)
        h2 = x1 * lax.rsqrt(ms + RMS_EPS) * g_ref[...]
        s["h2_hi"] = h2.astype(BF16)
        s["h2_lo"] = (h2 - s["h2_hi"].astype(F32)).astype(BF16)

    def router_logits(s):
        s["logits"] = (jnp.dot(s["h2_hi"], wr_hi_ref[...], preferred_element_type=F32)
                       + jnp.dot(s["h2_lo"], wr_hi_ref[...], preferred_element_type=F32)
                       + jnp.dot(s["h2_hi"], wr_lo_ref[...], preferred_element_type=F32)
                       + br_ref[...])

    def choose_experts(s):
        logits = s["logits"]
        g_logits = jnp.where(lane < N_GROUPS, logits, -jnp.inf)
        g_best, g_idx = _first_lane_of_max(g_logits, lane)
        g_w = 1.0 / jnp.sum(jnp.exp(g_logits - g_best), axis=1, keepdims=True)
        e_lo = EXPERT_LANE0 + EXPERTS_PER_GROUP * g_idx
        e_logits = jnp.where((lane >= e_lo) & (lane < e_lo + EXPERTS_PER_GROUP), logits, -jnp.inf)
        v1, i1 = _first_lane_of_max(e_logits, lane)
        v2, i2 = _first_lane_of_max(jnp.where(lane == i1, -jnp.inf, e_logits), lane)
        e21 = jnp.exp(v2 - v1)
        s["w1"] = g_w / (1.0 + e21)
        s["w2"] = g_w * e21 / (1.0 + e21)
        s["hit1"] = lane == i1
        s["hit2"] = lane == i2

    def sort_by_expert(s):
        hit1, hit2 = s["hit1"], s["hit2"]
        member = jnp.where(hit1, 1.0, jnp.where(hit2, 1.0, 0.0))
        r_tok = lax.broadcasted_iota(jnp.int32, (tm, tm), 0)
        c_tok = lax.broadcasted_iota(jnp.int32, (tm, tm), 1)
        earlier = jnp.where(c_tok < r_tok, 1.0, 0.0).astype(BF16)
        rank = jnp.dot(earlier, member.astype(BF16), preferred_element_type=F32)
        count = jnp.sum(member, axis=0, keepdims=True)
        chunks = jnp.floor((count + (MOE_CHUNK - 1)) * (1.0 / MOE_CHUNK))
        r_l = lax.broadcasted_iota(jnp.int32, (ROUTE_LANES, ROUTE_LANES), 0)
        c_l = lax.broadcasted_iota(jnp.int32, (ROUTE_LANES, ROUTE_LANES), 1)
        lower_lanes = jnp.where(r_l < c_l, 1.0, 0.0).astype(BF16)
        start = jnp.dot(jnp.broadcast_to(chunks, (8, ROUTE_LANES)).astype(BF16), lower_lanes,
                        preferred_element_type=F32)[0:1] * MOE_CHUNK
        pos = start + rank
        pos1 = jnp.sum(jnp.where(hit1, pos, 0.0), axis=1, keepdims=True)
        pos2 = jnp.sum(jnp.where(hit2, pos, 0.0), axis=1, keepdims=True)
        s["route"] = jnp.where(lane == 0, pos1, jnp.where(lane == 1, pos2, 0.0))
        route_ref[s["rows"], :] = s["route"]
        cnt_ref[s["index"]] = chunks

    def weight_pieces(w):
        pieces = jnp.zeros(lane.shape, F32)
        rest = w
        for n in range(MOE_WEIGHT_TERMS):
            piece = rest.astype(BF16).astype(F32)
            pieces = jnp.where(lane == n, piece, pieces)
            rest = rest - piece
        return pieces.astype(BF16)

    def compact(s):
        route_t = s["route"].T
        slot = lax.broadcasted_iota(jnp.int32, (cap, tm), 0)
        first = jnp.where(slot == route_t[0:1, :].astype(jnp.int32), 1.0, 0.0).astype(BF16)
        second = jnp.where(slot == route_t[1:2, :].astype(jnp.int32), 1.0, 0.0).astype(BF16)
        hs_ref[s["index"], :, 0:d_model] = jnp.dot(
            first + second, s["h2_hi"], preferred_element_type=F32).astype(BF16)
        slot_w = (jnp.dot(first, weight_pieces(s["w1"]), preferred_element_type=F32)
                  + jnp.dot(second, weight_pieces(s["w2"]), preferred_element_type=F32))
        hs_ref[s["index"], :, d_model:] = slot_w.astype(BF16)

    for stage in (normalise, router_logits, choose_experts, sort_by_expert, compact):
        for s in tiles:
            stage(s)


def _moe_cap(tm):
    worst = 2 * tm + N_EXPERTS * (MOE_CHUNK - 1)
    return -(-worst // 128) * 128


def _route_sort(x1, g, wr_hi, wr_lo, br, tm=MOE_TOKEN_TILE):
    S, D = x1.shape
    nT = S // tm
    cap = _moe_cap(tm)
    full = lambda i: (0, 0)
    row = lambda i: (i, 0)
    per_step = ROUTE_TILES_PER_STEP
    return pl.pallas_call(
        _route_sort_kernel,
        grid=(nT // per_step,),
        in_specs=[
            pl.BlockSpec((per_step * tm, D), row),
            pl.BlockSpec((1, D), full),
            pl.BlockSpec((D, ROUTE_LANES), full),
            pl.BlockSpec((D, ROUTE_LANES), full),
            pl.BlockSpec((1, ROUTE_LANES), full),
        ],
        out_specs=[
            pl.BlockSpec((per_step, cap, D + ROUTE_LANES), lambda i: (i, 0, 0)),
            pl.BlockSpec((per_step * tm, ROUTE_LANES), row),
            pl.BlockSpec((per_step, 1, ROUTE_LANES), lambda i: (i, 0, 0)),
        ],
        out_shape=[
            jax.ShapeDtypeStruct((nT, cap, D + ROUTE_LANES), BF16),
            jax.ShapeDtypeStruct((S, ROUTE_LANES), F32),
            jax.ShapeDtypeStruct((nT, 1, ROUTE_LANES), F32),
        ],
        compiler_params=_params(("parallel",)),
        name="route_sort",
    )(x1, g, wr_hi, wr_lo, br)


def _moe_plan(chunk_counts, n_row_tiles, n_chunk_slots):
    nT, E = chunk_counts.shape
    per_expert = chunk_counts.T
    seg_start = (jnp.cumsum(chunk_counts, axis=1) - chunk_counts).T
    seg_end = jnp.cumsum(per_expert, axis=1)
    n_chunks = seg_end[:, -1]
    padded = -(-n_chunks // MOE_CHUNKS_PER_TILE) * MOE_CHUNKS_PER_TILE
    e_end = jnp.cumsum(padded)
    n_used = e_end[-1] // MOE_CHUNKS_PER_TILE
    c = jnp.arange(n_chunk_slots, dtype=jnp.int32)
    e_of_c = jnp.minimum((e_end[None, :] <= c[:, None]).sum(axis=1), E - 1)
    is_e = (e_of_c[:, None] == jnp.arange(E)[None, :]).astype(jnp.int32)
    local = c - (is_e * (e_end - padded)[None, :]).sum(axis=1)
    real = local < (is_e * n_chunks[None, :]).sum(axis=1)
    pick_e = lambda table: (is_e[:, :, None] * table[None, :, :]).sum(axis=1)
    seg_end_c = pick_e(seg_end)
    t_of_c = jnp.minimum((seg_end_c <= local[:, None]).sum(axis=1), nT - 1)
    is_t = (t_of_c[:, None] == jnp.arange(nT)[None, :]).astype(jnp.int32)
    pick_t = lambda rows: (rows * is_t).sum(axis=1)
    within = local - (pick_t(seg_end_c) - pick_t(pick_e(per_expert)))
    src_tile = jnp.where(real, t_of_c, 0).astype(jnp.int32)
    src_row = jnp.where(real, (pick_t(pick_e(seg_start)) + within) * MOE_CHUNK, 0).astype(jnp.int32)
    tile = jnp.arange(n_row_tiles, dtype=jnp.int32)
    first = jnp.minimum(tile, n_used - 1) * MOE_CHUNKS_PER_TILE
    tile_expert = jnp.minimum((e_end[None, :] <= first[:, None]).sum(axis=1), E - 1)
    tile_real = real.reshape(n_row_tiles, MOE_CHUNKS_PER_TILE).sum(axis=1)
    experts = jnp.arange(E)
    nonempty = padded > 0
    buffer_of_e = jnp.cumsum(nonempty) - nonempty
    later = (experts[None, :] > experts[:, None]) & nonempty[None, :]
    next_of_e = jnp.min(jnp.where(later, experts[None, :], E), axis=1)
    next_of_e = jnp.where(next_of_e < E, next_of_e, -1)
    is_next = (next_of_e[:, None] == experts[None, :]).astype(jnp.int32)
    after_of_e = jnp.where(next_of_e >= 0, (is_next * next_of_e[None, :]).sum(axis=1), -1)
    is_te = (tile_expert[:, None] == experts[None, :]).astype(jnp.int32)
    at = lambda table: (is_te * table[None, :]).sum(axis=1)
    run_len = jnp.maximum(at(padded) // MOE_CHUNKS_PER_TILE, 1)
    run_pos = tile - at(e_end - padded) // MOE_CHUNKS_PER_TILE
    next_expert = at(next_of_e)
    streams = (tile < n_used) & (next_expert >= 0)
    piece_lo = jnp.where(streams, (MOE_WEIGHT_PIECES * run_pos) // run_len, 0)
    piece_hi = jnp.where(streams, (MOE_WEIGHT_PIECES * (run_pos + 1)) // run_len, 0)
    as_i32 = lambda a: a.astype(jnp.int32)
    return (as_i32(tile_expert), as_i32(tile_real), src_tile, src_row, as_i32(n_used.reshape(1)),
            as_i32(at(buffer_of_e) % 2), as_i32(next_expert), as_i32(at(after_of_e)),
            as_i32(piece_lo), as_i32(piece_hi), as_i32(chunk_counts.sum(axis=1)))


def _moe_ffn_kernel(texp_ref, treal_ref, ctile_ref, crow_ref, nused_ref,
                    wbuf_ref, wnext_ref, wafter_ref, plo_ref, phi_ref, tused_ref,
                    hs_hbm, wg_hbm, wu_hbm, wd_hbm, ys_hbm,
                    xbuf, ybuf, zbuf, wg_buf, wu_buf, wd_buf, stage_g, stage_u, stage_d,
                    gather_sem, scatter_sem, weight_sem, zero_sem):
    i = pl.program_id(0)
    n_used = nused_ref[0]
    slot = lax.rem(i, 2)
    rows_gu = stage_g.shape[1]
    rows_d = stage_d.shape[1]

    def piece_copies(e, p, s):
        gu_rows = pl.ds(pl.multiple_of(p * rows_gu, rows_gu), rows_gu)
        d_rows = pl.ds(pl.multiple_of(p * rows_d, rows_d), rows_d)
        return (pltpu.make_async_copy(wg_hbm.at[e, gu_rows, :], stage_g.at[s], weight_sem.at[s]),
                pltpu.make_async_copy(wu_hbm.at[e, gu_rows, :], stage_u.at[s], weight_sem.at[s]),
                pltpu.make_async_copy(wd_hbm.at[e, d_rows, :], stage_d.at[s], weight_sem.at[s]))

    def start_piece(e, p):
        for copy in piece_copies(e, p, lax.rem(p, MOE_WEIGHT_STAGES)):
            copy.start()

    def finish_piece(e, p, side):
        s = lax.rem(p, MOE_WEIGHT_STAGES)
        for copy in piece_copies(e, p, s):
            copy.wait()
        gu_rows = pl.ds(pl.multiple_of(p * rows_gu, rows_gu), rows_gu)
        d_rows = pl.ds(pl.multiple_of(p * rows_d, rows_d), rows_d)
        wg_buf[side, gu_rows, :] = stage_g[s].astype(BF16)
        wu_buf[side, gu_rows, :] = stage_u[s].astype(BF16)
        wd_buf[side, d_rows, :] = stage_d[s].astype(BF16)

    ahead = MOE_WEIGHT_STAGES - 1

    def start_first_pieces(e):
        for p in range(ahead):
            start_piece(e, p)

    def stream_pieces(e, lo, hi, side, following):
        @pl.when(hi > lo)
        def _():
            def body(p, carry):
                @pl.when(p + ahead < MOE_WEIGHT_PIECES)
                def _():
                    start_piece(e, p + ahead)
                finish_piece(e, p, side)
                return carry
            lax.fori_loop(lo, hi, body, 0)

            @pl.when((hi == MOE_WEIGHT_PIECES) & (following >= 0))
            def _():
                start_first_pieces(following)

    def chunk_rows(c):
        return pl.ds(pl.multiple_of(c * MOE_CHUNK, MOE_CHUNK), MOE_CHUNK)

    def gather_copy(tile, c, buf):
        g = tile * MOE_CHUNKS_PER_TILE + c
        src = hs_hbm.at[ctile_ref[g], pl.ds(pl.multiple_of(crow_ref[g], MOE_CHUNK), MOE_CHUNK), :]
        return pltpu.make_async_copy(src, xbuf.at[buf, chunk_rows(c), :], gather_sem.at[buf])

    def scatter_copy(tile, c, buf):
        g = tile * MOE_CHUNKS_PER_TILE + c
        dst = ys_hbm.at[ctile_ref[g], pl.ds(pl.multiple_of(crow_ref[g], MOE_CHUNK), MOE_CHUNK), :]
        return pltpu.make_async_copy(ybuf.at[buf, chunk_rows(c), :], dst, scatter_sem.at[buf])

    def for_real_chunks(tile, fn, also=True):
        n_real = treal_ref[tile]
        for c in range(MOE_CHUNKS_PER_TILE):
            pl.when((c < n_real) & also)(functools.partial(fn, c))

    def for_tail_chunks(fn):
        n_token_tiles, cap = ys_hbm.shape[0], ys_hbm.shape[1]

        def per_tile(t, carry):
            def per_chunk(c, carry):
                fn(pltpu.make_async_copy(zbuf, ys_hbm.at[t, chunk_rows(c), :], zero_sem.at[0]))
                return carry
            return lax.fori_loop(tused_ref[t], cap // MOE_CHUNK, per_chunk, carry)
        lax.fori_loop(0, n_token_tiles, per_tile, 0)

    @pl.when(i == 0)
    def _():
        zbuf[...] = jnp.zeros(zbuf.shape, zbuf.dtype)
        for_tail_chunks(lambda copy: copy.start())
        xbuf[...] = jnp.zeros(xbuf.shape, xbuf.dtype)
        for_real_chunks(0, lambda c: gather_copy(0, c, 0).start())
        start_first_pieces(texp_ref[0])
        stream_pieces(texp_ref[0], 0, MOE_WEIGHT_PIECES, wbuf_ref[0], wnext_ref[0])

    @pl.when(i + 1 < n_used)
    def _():
        for_real_chunks(i + 1, lambda c: gather_copy(i + 1, c, 1 - slot).start())

    side = wbuf_ref[i]
    stream_pieces(wnext_ref[i], plo_ref[i], phi_ref[i], 1 - side, wafter_ref[i])

    def tile_work(n_rows):
        for_real_chunks(i, lambda c: gather_copy(i, c, slot).wait())
        before = jnp.maximum(i - 2, 0)
        for_real_chunks(before, lambda c: scatter_copy(before, c, slot).wait(), also=i >= 2)
        d_model = wg_buf.shape[1]
        x = xbuf[slot, 0:n_rows, 0:d_model]
        w = jnp.sum(xbuf[slot, 0:n_rows, d_model:].astype(F32), axis=1, keepdims=True)
        a = jnp.dot(x, wg_buf[side], preferred_element_type=F32)
        u = jnp.dot(x, wu_buf[side], preferred_element_type=F32)
        hid = (jax.nn.silu(a) * u * w).astype(BF16)
        ybuf[slot, 0:n_rows, :] = jnp.dot(hid, wd_buf[side],
                                          preferred_element_type=F32).astype(BF16)
        for_real_chunks(i, lambda c: scatter_copy(i, c, slot).start())

    half = MOE_ROW_TILE // 2
    fits_half = treal_ref[i] * MOE_CHUNK <= half
    pl.when((i < n_used) & jnp.logical_not(fits_half))(functools.partial(tile_work, MOE_ROW_TILE))
    pl.when((i < n_used) & fits_half)(functools.partial(tile_work, half))

    @pl.when(i == n_used - 1)
    def _():
        @pl.when(i >= 1)
        def _():
            for_real_chunks(i - 1, lambda c: scatter_copy(i - 1, c, 1 - slot).wait())
        for_real_chunks(i, lambda c: scatter_copy(i, c, slot).wait())
        for_tail_chunks(lambda copy: copy.wait())


def _moe_ffn(hs, plan, wg, wu, wd):
    nT, cap, row_width = hs.shape
    E, D, F = wg.shape
    n_row_tiles = plan[0].shape[0]
    assert D % MOE_WEIGHT_PIECES == 0 and F % MOE_WEIGHT_PIECES == 0
    grid_spec = pltpu.PrefetchScalarGridSpec(
        num_scalar_prefetch=len(plan),
        grid=(n_row_tiles,),
        in_specs=[pl.BlockSpec(memory_space=pl.ANY)] * 4,
        out_specs=pl.BlockSpec(memory_space=pl.ANY),
        scratch_shapes=[
            pltpu.VMEM((2, MOE_ROW_TILE, row_width), BF16),
            pltpu.VMEM((2, MOE_ROW_TILE, D), BF16),
            pltpu.VMEM((MOE_CHUNK, D), BF16),
            pltpu.VMEM((2, D, F), BF16),
            pltpu.VMEM((2, D, F), BF16),
            pltpu.VMEM((2, F, D), BF16),
            pltpu.VMEM((MOE_WEIGHT_STAGES, D // MOE_WEIGHT_PIECES, F), F32),
            pltpu.VMEM((MOE_WEIGHT_STAGES, D // MOE_WEIGHT_PIECES, F), F32),
            pltpu.VMEM((MOE_WEIGHT_STAGES, F // MOE_WEIGHT_PIECES, D), F32),
            pltpu.SemaphoreType.DMA((2,)),
            pltpu.SemaphoreType.DMA((2,)),
            pltpu.SemaphoreType.DMA((MOE_WEIGHT_STAGES,)),
            pltpu.SemaphoreType.DMA((1,)),
        ],
    )
    return pl.pallas_call(
        _moe_ffn_kernel,
        grid_spec=grid_spec,
        out_shape=jax.ShapeDtypeStruct((nT, cap, D), BF16),
        compiler_params=_params(("arbitrary",)),
        name="moe_ffn",
    )(*plan, hs, wg, wu, wd)


def _moe_combine_kernel(ys_ref, route_ref, x1_ref, g_ref, o_ref):
    tm = x1_ref.shape[0]
    cap = ys_ref.shape[0]
    route = route_ref[...]
    slot = lax.broadcasted_iota(jnp.int32, (tm, cap), 1)
    pick = jnp.where(slot == route[:, 0:1].astype(jnp.int32), 1.0,
                     jnp.where(slot == route[:, 1:2].astype(jnp.int32), 1.0, 0.0)).astype(BF16)
    y = x1_ref[...] + jnp.dot(pick, ys_ref[...], preferred_element_type=F32)
    ms = jnp.mean(y * y, axis=-1, keepdims=True)
    o_ref[...] = y * lax.rsqrt(ms + RMS_EPS) * g_ref[...]


def _moe_combine(ys, route, x1, g, tm=MOE_TOKEN_TILE):
    S, D = x1.shape
    cap = ys.shape[1]
    row = lambda i: (i, 0)
    return pl.pallas_call(
        _moe_combine_kernel,
        grid=(S // tm,),
        in_specs=[
            pl.BlockSpec((None, cap, D), lambda i: (i, 0, 0)),
            pl.BlockSpec((tm, ROUTE_LANES), row),
            pl.BlockSpec((tm, D), row),
            pl.BlockSpec((1, D), lambda i: (0, 0)),
        ],
        out_specs=pl.BlockSpec((tm, D), row),
        out_shape=jax.ShapeDtypeStruct((S, D), F32),
        compiler_params=_params(("parallel",)),
        name="moe_combine",
    )(ys, route, x1, g)


def _router_weights(w_r_group, b_r_group, w_r_expert, b_r_expert):
    D = w_r_group.shape[0]
    w = jnp.concatenate(
        [w_r_group, jnp.transpose(w_r_expert, (1, 0, 2)).reshape(D, N_EXPERTS)], axis=1)
    b = jnp.concatenate([b_r_group, b_r_expert.reshape(N_EXPERTS)])
    pad = ROUTE_LANES - w.shape[1]
    w = jnp.pad(w, ((0, 0), (0, pad)))
    b = jnp.pad(b, (0, pad)).reshape(1, ROUTE_LANES)
    w_hi = w.astype(BF16)
    w_lo = (w - w_hi.astype(F32)).astype(BF16)
    return w_hi, w_lo, b


def kernel(x, norm_mix, w_in, w_pool, pool_scale, w_branch_pool, w_branch_attn, w_out, norm_ffn,
           w_r_group, b_r_group, w_r_expert, b_r_expert, w_gate, w_up, w_down, norm_final):
    B, S, D = x.shape
    depth = w_in.shape[0]
    assert depth == 1, "the final rms_norm is fused into the expert kernel of a single layer"
    slopes = jnp.exp2(-8.0 * jnp.arange(1, ATTN_HEADS + 1, dtype=F32) / ATTN_HEADS)
    slopes = jnp.broadcast_to(slopes[:, None, None], (ATTN_HEADS, 1, 128))
    outs = []
    for b in range(B):
        xb = x[b]
        for l in range(depth):
            proj = _inproj(_rms_norm(xb, norm_mix[l].reshape(1, D)), w_in[l])
            y_pool = _pool_mixer(proj, w_pool[l].astype(BF16), pool_scale[l].reshape(1, POOL_WIDTH))
            qT, vT, sel = _moba_gate(proj)
            y_attn = _moba_attention(proj, qT, vT, sel, slopes)
            wr_hi, wr_lo, br = _router_weights(w_r_group[l], b_r_group[l], w_r_expert[l], b_r_expert[l])
            x1 = _merge_out(y_pool, y_attn, proj, xb, w_branch_pool[l].astype(BF16),
                            w_branch_attn[l].astype(BF16), w_out[l].astype(BF16))
            hs, route, cnt = _route_sort(x1, norm_ffn[l].reshape(1, D), wr_hi, wr_lo, br)
            nT = hs.shape[0]
            chunk_counts = cnt[:, 0, EXPERT_LANE0:EXPERT_LANE0 + N_EXPERTS].astype(jnp.int32)
            max_chunks = nT * ((2 * MOE_TOKEN_TILE + N_EXPERTS * (MOE_CHUNK - 1)) // MOE_CHUNK)
            n_row_tiles = -(-(max_chunks + N_EXPERTS * (MOE_CHUNKS_PER_TILE - 1))
                            // MOE_CHUNKS_PER_TILE)
            plan = _moe_plan(chunk_counts, n_row_tiles, n_row_tiles * MOE_CHUNKS_PER_TILE)
            ys = _moe_ffn(hs, plan, w_gate[l], w_up[l], w_down[l])
            xb = _moe_combine(ys, route, x1, norm_final.reshape(1, D))
        outs.append(xb)
    return jnp.stack(outs, axis=0)
```

```python
import functools

import jax
import jax.numpy as jnp
from jax import lax
from jax.experimental import pallas as pl
from jax.experimental.pallas import tpu as pltpu

F32 = jnp.float32
BF16 = jnp.bfloat16

POOL_WINDOWS = (2, 4, 8, 16)
MAX_WINDOW = 16
POOL_WIDTH = 1024
POOL_GROUP = 256
HEAD_DIM = 128
ATTN_HEADS = 8
ATTN_WIDTH = 1024
MOBA_BLOCK = 256
MOBA_TOPK = 3
N_GROUPS = 4
EXPERTS_PER_GROUP = 4
N_EXPERTS = 16
ROUTE_LANES = 128
EXPERT_LANE0 = N_GROUPS
RMS_EPS = 1e-6
NEG_INF = -1e30
LOG2_E = 1.4426950408889634
QUERY_SCALE = (HEAD_DIM ** -0.5) * LOG2_E
KEY_AUG = 2 * HEAD_DIM
ALIBI_TERMS = 3
VALUE_AUG = HEAD_DIM + 16
ATTN_KV_UNROLL = 3
ATTN_TILES_PER_STEP = 4
MOE_CHUNK = 16
MOE_TOKEN_TILE = 256
ROUTE_TILES_PER_STEP = 4
MOE_ROW_TILE = 256
MOE_CHUNKS_PER_TILE = MOE_ROW_TILE // MOE_CHUNK
MOE_WEIGHT_TERMS = 3
MOE_WEIGHT_PIECES = 8
MOE_WEIGHT_STAGES = 4
assert MOE_WEIGHT_STAGES - 1 <= MOE_WEIGHT_PIECES

V7X_VMEM_LIMIT_BYTES = 56 * 1024 * 1024


def _params(semantics, vmem=V7X_VMEM_LIMIT_BYTES, flags=None):
    return pltpu.CompilerParams(dimension_semantics=semantics, vmem_limit_bytes=vmem, flags=flags)


def _rms_norm_kernel(x_ref, g_ref, o_ref):
    x = x_ref[...]
    ms = jnp.mean(x * x, axis=-1, keepdims=True)
    o_ref[...] = (x * lax.rsqrt(ms + RMS_EPS) * g_ref[...]).astype(o_ref.dtype)


def _rms_norm(x, g, tm=1024):
    S, D = x.shape
    return pl.pallas_call(
        _rms_norm_kernel,
        grid=(S // tm,),
        in_specs=[pl.BlockSpec((tm, D), lambda i: (i, 0)), pl.BlockSpec((1, D), lambda i: (0, 0))],
        out_specs=pl.BlockSpec((tm, D), lambda i: (i, 0)),
        out_shape=jax.ShapeDtypeStruct((S, D), BF16),
        compiler_params=_params(("parallel",)),
        name="rms_norm",
    )(x, g)


def _inproj_kernel(h_ref, w_ref, o_ref, wb_ref):
    @pl.when(pl.program_id(1) == 0)
    def _():
        wb_ref[...] = w_ref[...].astype(BF16)

    col0 = pl.program_id(0) * o_ref.shape[1]
    is_q = (col0 >= POOL_WIDTH) & (col0 < POOL_WIDTH + ATTN_WIDTH)
    factor = jnp.where(is_q, QUERY_SCALE, 1.0).astype(F32)
    o_ref[...] = (jnp.dot(h_ref[...], wb_ref[...], preferred_element_type=F32)
                  * factor).astype(o_ref.dtype)


def _inproj(h, w, tm=2048, tn=1024):
    S, D = h.shape
    N = w.shape[1]
    tm = min(tm, S)
    assert POOL_WIDTH % tn == 0 and ATTN_WIDTH % tn == 0
    return pl.pallas_call(
        _inproj_kernel,
        grid=(N // tn, S // tm),
        in_specs=[
            pl.BlockSpec((tm, D), lambda j, i: (i, 0)),
            pl.BlockSpec((D, tn), lambda j, i: (0, j)),
        ],
        out_specs=pl.BlockSpec((tm, tn), lambda j, i: (i, j)),
        out_shape=jax.ShapeDtypeStruct((S, N), BF16),
        scratch_shapes=[pltpu.VMEM((D, tn), BF16)],
        compiler_params=_params(("parallel", "arbitrary")),
        name="inproj",
    )(h, w)


def _pool_kernel(cur_ref, prev_ref, w_ref, scale_ref, o_ref):
    i = pl.program_id(0)
    tm = cur_ref.shape[0]
    u = cur_ref[...].astype(F32)
    halo = jnp.where(i > 0, prev_ref[...].astype(F32), 0.0)
    ext = jnp.concatenate([halo, u], axis=0)
    t = i * tm + lax.broadcasted_iota(jnp.int32, (tm, 1), 0)
    for g, w in enumerate(POOL_WINDOWS):
        cols = slice(g * POOL_GROUP, (g + 1) * POOL_GROUP)
        ug = u[:, cols]
        run = ext[:, cols]
        s = 1
        while s < w:
            run = run + pltpu.roll(run, s, axis=0)
            s *= 2
        wsum = run[MAX_WINDOW:, :]
        cnt = jnp.minimum(t + 1, w).astype(F32)
        mixed = (wsum / cnt - ug).astype(BF16)
        y = jnp.dot(mixed, w_ref[g], preferred_element_type=F32)
        o_ref[:, cols] = (y * scale_ref[:, cols]).astype(o_ref.dtype)


def _pool_mixer(proj, w_pool, pool_scale, tm=512):
    S = proj.shape[0]
    halo_blocks = tm // MAX_WINDOW
    return pl.pallas_call(
        _pool_kernel,
        grid=(S // tm,),
        in_specs=[
            pl.BlockSpec((tm, POOL_WIDTH), lambda i: (i, 0)),
            pl.BlockSpec((MAX_WINDOW, POOL_WIDTH),
                         lambda i: (jnp.maximum(i * halo_blocks - 1, 0), 0)),
            pl.BlockSpec((len(POOL_WINDOWS), POOL_GROUP, POOL_GROUP), lambda i: (0, 0, 0)),
            pl.BlockSpec((1, POOL_WIDTH), lambda i: (0, 0)),
        ],
        out_specs=pl.BlockSpec((tm, POOL_WIDTH), lambda i: (i, 0)),
        out_shape=jax.ShapeDtypeStruct((S, POOL_WIDTH), BF16),
        compiler_params=_params(("parallel",)),
        name="pool_mixer",
    )(proj, proj, w_pool, pool_scale)


def _alibi_query_rows(slope_ref):
    slope2 = slope_ref[...][:, 0:1] * LOG2_E
    q_extra = jnp.zeros((KEY_AUG - HEAD_DIM, MOBA_BLOCK), F32)
    row = lax.broadcasted_iota(jnp.int32, q_extra.shape, 0)
    rest = slope2
    for n in range(ALIBI_TERMS):
        piece = rest.astype(BF16).astype(F32)
        q_extra = jnp.where(row == n, piece, q_extra)
        rest = rest - piece
    return q_extra.astype(BF16)


def _moba_gate_kernel(q_ref, k_ref, v_ref, qT_ref, vT_ref, sel_ref):
    S = q_ref.shape[0]
    nb = S // MOBA_BLOCK
    topk = min(MOBA_TOPK, nb)
    kf = k_ref[...].astype(F32).reshape(nb, MOBA_BLOCK, HEAD_DIM)
    kmean = jnp.sum(kf, axis=1) * (1.0 / MOBA_BLOCK)
    km_hi = kmean.astype(BF16)
    km_lo = (kmean - km_hi.astype(F32)).astype(BF16)
    blk = lax.broadcasted_iota(jnp.int32, (nb, MOBA_BLOCK), 0)
    v_row = lax.broadcasted_iota(jnp.int32, (VALUE_AUG - HEAD_DIM, MOBA_BLOCK), 0)
    v_extra = jnp.where(v_row == 0, 1.0, 0.0).astype(BF16)

    def body(i, carry):
        rows = pl.ds(pl.multiple_of(i * MOBA_BLOCK, MOBA_BLOCK), MOBA_BLOCK)
        qT = q_ref[rows, :].astype(F32).T.astype(BF16)
        qT_ref[i] = qT
        vT_ref[i, 0:HEAD_DIM, :] = v_ref[rows, :].astype(F32).T.astype(BF16)
        vT_ref[i, HEAD_DIM:VALUE_AUG, :] = v_extra
        gate = (jnp.dot(km_hi, qT, preferred_element_type=F32)
                + jnp.dot(km_lo, qT, preferred_element_type=F32)) * (1.0 / QUERY_SCALE)
        gate = jnp.where(blk < i, gate, NEG_INF)
        sel = jnp.zeros((nb, MOBA_BLOCK), F32)
        for _ in range(topk):
            best = jnp.max(gate, axis=0, keepdims=True)
            idx = jnp.min(jnp.where(gate == best, blk, nb), axis=0, keepdims=True)
            hit = blk == idx
            sel = jnp.where(hit & (blk < i), 1.0, sel)
            gate = jnp.where(hit, -jnp.inf, gate)
        sel_ref[i] = sel
        return carry

    lax.fori_loop(0, nb, body, 0, unroll=4 if nb % 4 == 0 else 1)


def _moba_gate(proj):
    S = proj.shape[0]
    nb = S // MOBA_BLOCK
    H = ATTN_HEADS
    q0 = POOL_WIDTH // HEAD_DIM
    k0 = q0 + H
    v0 = k0 + H
    blocked = lambda h: (h, 0, 0, 0)
    return pl.pallas_call(
        _moba_gate_kernel,
        grid=(H,),
        in_specs=[
            pl.BlockSpec((S, HEAD_DIM), lambda h: (0, q0 + h)),
            pl.BlockSpec((S, HEAD_DIM), lambda h: (0, k0 + h)),
            pl.BlockSpec((S, HEAD_DIM), lambda h: (0, v0 + h)),
        ],
        out_specs=[
            pl.BlockSpec((None, nb, HEAD_DIM, MOBA_BLOCK), blocked),
            pl.BlockSpec((None, nb, VALUE_AUG, MOBA_BLOCK), blocked),
            pl.BlockSpec((None, nb, nb, MOBA_BLOCK), blocked),
        ],
        out_shape=[
            jax.ShapeDtypeStruct((H, nb, HEAD_DIM, MOBA_BLOCK), BF16),
            jax.ShapeDtypeStruct((H, nb, VALUE_AUG, MOBA_BLOCK), BF16),
            jax.ShapeDtypeStruct((H, nb, nb, MOBA_BLOCK), F32),
        ],
        compiler_params=_params(("parallel",)),
        name="moba_gate",
    )(proj, proj, proj)


def _moba_attn_kernel(slope_ref, qT_ref, qT_next_ref, sel_ref, k_ref, vT_ref, o_ref,
                      ka_ref, *scratch):
    first = pl.program_id(1) * ATTN_TILES_PER_STEP
    nb = sel_ref.shape[1]

    @pl.when(first == 0)
    def _():
        pos = lax.broadcasted_iota(jnp.int32, (MOBA_BLOCK, KEY_AUG - HEAD_DIM), 0).astype(F32)
        col = lax.broadcasted_iota(jnp.int32, (MOBA_BLOCK, KEY_AUG - HEAD_DIM), 1)
        k_extra = jnp.where(col < ALIBI_TERMS, pos, 0.0).astype(BF16)

        def per_block(j, carry):
            rows = pl.ds(pl.multiple_of(j * MOBA_BLOCK, MOBA_BLOCK), MOBA_BLOCK)
            ka_ref[rows, 0:HEAD_DIM] = k_ref[rows, :]
            ka_ref[rows, HEAD_DIM:KEY_AUG] = k_extra
            return carry
        lax.fori_loop(0, nb, per_block, 0)

    q_extra = _alibi_query_rows(slope_ref)
    augment = lambda q: jnp.concatenate([q, q_extra], axis=0)
    for t in range(ATTN_TILES_PER_STEP):
        last = t + 1 == ATTN_TILES_PER_STEP
        _moba_attn_tile(first + t, t == 0, slope_ref, augment(qT_ref[t]),
                        augment(qT_next_ref[...] if last else qT_ref[t + 1]), sel_ref.at[t],
                        ka_ref, vT_ref, o_ref.at[t * MOBA_BLOCK:(t + 1) * MOBA_BLOCK, :], *scratch)


def _moba_attn_tile(i, may_start_head, slope_ref, qT, q_next, sel_ref, k_ref, vT_ref, o_ref,
                    s0_ref, s1_ref, p0_ref, p1_ref, s_own_ref):
    nb = sel_ref.shape[0]
    slope2 = slope_ref[...][:, 0:1] * LOG2_E

    def block_of(t, u):
        return jnp.clip(t * ATTN_KV_UNROLL + u, 0, nb - 1)

    def key_rows(j):
        return pl.ds(pl.multiple_of(j * MOBA_BLOCK, MOBA_BLOCK), MOBA_BLOCK)

    def issue_scores(t, s_ref, queries=qT):
        for u in range(ATTN_KV_UNROLL):
            s_ref[u] = jnp.dot(k_ref[key_rows(block_of(t, u)), :], queries,
                               preferred_element_type=F32)

    def apply_probs(t, p_ref, alpha, acc):
        acc = alpha * acc
        for u in range(ATTN_KV_UNROLL):
            acc = acc + jnp.dot(vT_ref[block_of(t, u)], p_ref[u], preferred_element_type=F32)
        return acc

    def softmax_group(t, s_ref, p_ref, m):
        m_new = m
        shifts = []
        for u in range(ATTN_KV_UNROLL):
            j = t * ATTN_KV_UNROLL + u
            valid = jnp.where(j < i, sel_ref[pl.ds(block_of(t, u), 1), :], 0.0) > 0.0
            gap = slope2 * ((i - j) * MOBA_BLOCK).astype(F32)
            top = jnp.max(s_ref[u], axis=0, keepdims=True) - gap
            m_new = jnp.maximum(m_new, jnp.where(valid, top, NEG_INF))
            shifts.append((valid, gap))
        alpha = jnp.exp2(m - m_new)
        for u, (valid, gap) in enumerate(shifts):
            p = jnp.exp2(s_ref[u] - jnp.where(valid, m_new + gap, jnp.inf))
            p_ref[u] = p.astype(BF16)
        return m_new, alpha

    def body(r, carry):
        m, acc, alpha = carry
        acc = apply_probs(2 * r - 1, p1_ref, alpha, acc)
        m, alpha = softmax_group(2 * r, s0_ref, p0_ref, m)
        issue_scores(2 * r + 1, s1_ref)
        acc = apply_probs(2 * r, p0_ref, alpha, acc)
        m, alpha = softmax_group(2 * r + 1, s1_ref, p1_ref, m)
        issue_scores(2 * r + 2, s0_ref)
        return m, acc, alpha

    n_groups = (i + ATTN_KV_UNROLL - 1) // ATTN_KV_UNROLL
    n_pairs = n_groups // 2
    if may_start_head:
        @pl.when(i == 0)
        def _():
            s0_ref[...] = jnp.zeros(s0_ref.shape, F32)
            s_own_ref[...] = jnp.dot(k_ref[key_rows(0), :], qT, preferred_element_type=F32)

    p1_ref[...] = jnp.zeros(p1_ref.shape, BF16)
    kpos = lax.broadcasted_iota(jnp.int32, (MOBA_BLOCK, MOBA_BLOCK), 0)
    qpos = lax.broadcasted_iota(jnp.int32, (MOBA_BLOCK, MOBA_BLOCK), 1)
    s_own = jnp.where(qpos >= kpos, s_own_ref[...], NEG_INF)
    top_own = jnp.max(s_own, axis=0, keepdims=True)
    row = jnp.zeros((1, MOBA_BLOCK), F32)
    init = (row + NEG_INF, jnp.zeros((VALUE_AUG, MOBA_BLOCK), F32), row + 1.0)
    m, acc, alpha = lax.fori_loop(0, n_pairs, body, init)

    def finish(has_last_group):
        m_, acc_ = m, apply_probs(2 * n_pairs - 1, p1_ref, alpha, acc)
        if has_last_group:
            m_, alpha_ = softmax_group(2 * n_pairs, s0_ref, p0_ref, m_)
            acc_ = apply_probs(2 * n_pairs, p0_ref, alpha_, acc_)
        m_new = jnp.maximum(m_, top_own)
        p = jnp.exp2(s_own - m_new)
        acc_ = (jnp.exp2(m_ - m_new) * acc_
                + jnp.dot(vT_ref[i], p.astype(BF16), preferred_element_type=F32))
        out = acc_[0:HEAD_DIM] / acc_[HEAD_DIM:HEAD_DIM + 1]
        o_ref[...] = out.T.astype(o_ref.dtype)
        issue_scores(0, s0_ref, q_next)
        s_own_ref[...] = jnp.dot(k_ref[key_rows(jnp.minimum(i + 1, nb - 1)), :], q_next,
                                 preferred_element_type=F32)

    pl.when(n_groups > 2 * n_pairs)(functools.partial(finish, True))
    pl.when(n_groups == 2 * n_pairs)(functools.partial(finish, False))


def _moba_attention(proj, qT, vT, sel, slopes):
    S = proj.shape[0]
    H = ATTN_HEADS
    nb = S // MOBA_BLOCK
    k0 = POOL_WIDTH // HEAD_DIM + H
    per_step = ATTN_TILES_PER_STEP
    assert nb % per_step == 0
    return pl.pallas_call(
        _moba_attn_kernel,
        grid=(H, nb // per_step),
        in_specs=[
            pl.BlockSpec((None, 1, 128), lambda h, g: (h, 0, 0)),
            pl.BlockSpec((None, per_step, HEAD_DIM, MOBA_BLOCK), lambda h, g: (h, g, 0, 0)),
            pl.BlockSpec((None, None, HEAD_DIM, MOBA_BLOCK),
                         lambda h, g: (h, jnp.minimum((g + 1) * per_step, nb - 1), 0, 0)),
            pl.BlockSpec((None, per_step, nb, MOBA_BLOCK), lambda h, g: (h, g, 0, 0)),
            pl.BlockSpec((S, HEAD_DIM), lambda h, g: (0, k0 + h)),
            pl.BlockSpec((None, nb, VALUE_AUG, MOBA_BLOCK), lambda h, g: (h, 0, 0, 0)),
        ],
        out_specs=pl.BlockSpec((per_step * MOBA_BLOCK, HEAD_DIM), lambda h, g: (g, h)),
        out_shape=jax.ShapeDtypeStruct((S, ATTN_WIDTH), BF16),
        scratch_shapes=[
            pltpu.VMEM((S, KEY_AUG), BF16),
            pltpu.VMEM((ATTN_KV_UNROLL, MOBA_BLOCK, MOBA_BLOCK), F32),
            pltpu.VMEM((ATTN_KV_UNROLL, MOBA_BLOCK, MOBA_BLOCK), F32),
            pltpu.VMEM((ATTN_KV_UNROLL, MOBA_BLOCK, MOBA_BLOCK), BF16),
            pltpu.VMEM((ATTN_KV_UNROLL, MOBA_BLOCK, MOBA_BLOCK), BF16),
            pltpu.VMEM((MOBA_BLOCK, MOBA_BLOCK), F32),
        ],
        compiler_params=_params(("arbitrary", "arbitrary")),
        name="moba_attention",
    )(slopes, qT, qT, sel, proj, vT)


def _first_lane_of_max(vals, lane):
    best = jnp.max(vals, axis=1, keepdims=True)
    idx = jnp.min(jnp.where(vals == best, lane, ROUTE_LANES), axis=1, keepdims=True)
    return best, idx


def _merge_out_kernel(yp_ref, ya_ref, glp_ref, gla_ref, x_ref, wbp_ref, wba_ref, wout_ref, x1_ref):
    bp = jnp.dot(yp_ref[...], wbp_ref[...], preferred_element_type=F32)
    ba = jnp.dot(ya_ref[...], wba_ref[...], preferred_element_type=F32)
    merged = (jax.nn.sigmoid(glp_ref[...].astype(F32)) * bp
              + jax.nn.sigmoid(gla_ref[...].astype(F32)) * ba)
    x1_ref[...] = x_ref[...] + jnp.dot(merged.astype(BF16), wout_ref[...],
                                       preferred_element_type=F32)


def _merge_out(yp, ya, proj, x, wbp, wba, wout, tm=512):
    S, D = x.shape
    row = lambda i: (i, 0)
    glp_blk = (POOL_WIDTH + 3 * ATTN_WIDTH) // D
    resident = lambda shape: pl.BlockSpec(shape, lambda i: (0, 0), pipeline_mode=pl.Buffered(1))
    return pl.pallas_call(
        _merge_out_kernel,
        grid=(S // tm,),
        in_specs=[
            pl.BlockSpec((tm, POOL_WIDTH), row),
            pl.BlockSpec((tm, ATTN_WIDTH), row),
            pl.BlockSpec((tm, D), lambda i: (i, glp_blk)),
            pl.BlockSpec((tm, D), lambda i: (i, glp_blk + 1)),
            pl.BlockSpec((tm, D), row),
            resident((POOL_WIDTH, D)),
            resident((ATTN_WIDTH, D)),
            resident((D, D)),
        ],
        out_specs=pl.BlockSpec((tm, D), row),
        out_shape=jax.ShapeDtypeStruct((S, D), F32),
        compiler_params=_params(("parallel",)),
        name="merge_out",
    )(yp, ya, proj, proj, x, wbp, wba, wout)


def _route_sort_kernel(x1_ref, g_ref, wr_hi_ref, wr_lo_ref, br_ref, hs_ref, route_ref, cnt_ref):
    n_tiles = ROUTE_TILES_PER_STEP
    tm = x1_ref.shape[0] // n_tiles
    cap = hs_ref.shape[1]
    d_model = x1_ref.shape[1]
    lane = lax.broadcasted_iota(jnp.int32, (tm, ROUTE_LANES), 1)
    tiles = [{"index": t, "rows": slice(t * tm, (t + 1) * tm)} for t in range(n_tiles)]

    def normalise(s):
        x1 = x1_ref[s["rows"], :]
        ms = jnp.mean(x1 * x1, axis=-1, keepdims=True)
        h2 = x1 * lax.rsqrt(ms + RMS_EPS) * g_ref[...]
        s["h2_hi"] = h2.astype(BF16)
        s["h2_lo"] = (h2 - s["h2_hi"].astype(F32)).astype(BF16)

    def router_logits(s):
        s["logits"] = (jnp.dot(s["h2_hi"], wr_hi_ref[...], preferred_element_type=F32)
                       + jnp.dot(s["h2_lo"], wr_hi_ref[...], preferred_element_type=F32)
                       + jnp.dot(s["h2_hi"], wr_lo_ref[...], preferred_element_type=F32)
                       + br_ref[...])

    def choose_experts(s):
        logits = s["logits"]
        g_logits = jnp.where(lane < N_GROUPS, logits, -jnp.inf)
        g_best, g_idx = _first_lane_of_max(g_logits, lane)
        g_w = 1.0 / jnp.sum(jnp.exp(g_logits - g_best), axis=1, keepdims=True)
        e_lo = EXPERT_LANE0 + EXPERTS_PER_GROUP * g_idx
        e_logits = jnp.where((lane >= e_lo) & (lane < e_lo + EXPERTS_PER_GROUP), logits, -jnp.inf)
        v1, i1 = _first_lane_of_max(e_logits, lane)
        v2, i2 = _first_lane_of_max(jnp.where(lane == i1, -jnp.inf, e_logits), lane)
        e21 = jnp.exp(v2 - v1)
        s["w1"] = g_w / (1.0 + e21)
        s["w2"] = g_w * e21 / (1.0 + e21)
        s["hit1"] = lane == i1
        s["hit2"] = lane == i2

    def sort_by_expert(s):
        hit1, hit2 = s["hit1"], s["hit2"]
        member = jnp.where(hit1, 1.0, jnp.where(hit2, 1.0, 0.0))
        r_tok = lax.broadcasted_iota(jnp.int32, (tm, tm), 0)
        c_tok = lax.broadcasted_iota(jnp.int32, (tm, tm), 1)
        earlier = jnp.where(c_tok < r_tok, 1.0, 0.0).astype(BF16)
        rank = jnp.dot(earlier, member.astype(BF16), preferred_element_type=F32)
        count = jnp.sum(member, axis=0, keepdims=True)
        chunks = jnp.floor((count + (MOE_CHUNK - 1)) * (1.0 / MOE_CHUNK))
        r_l = lax.broadcasted_iota(jnp.int32, (ROUTE_LANES, ROUTE_LANES), 0)
        c_l = lax.broadcasted_iota(jnp.int32, (ROUTE_LANES, ROUTE_LANES), 1)
        lower_lanes = jnp.where(r_l < c_l, 1.0, 0.0).astype(BF16)
        start = jnp.dot(jnp.broadcast_to(chunks, (8, ROUTE_LANES)).astype(BF16), lower_lanes,
                        preferred_element_type=F32)[0:1] * MOE_CHUNK
        pos = start + rank
        pos1 = jnp.sum(jnp.where(hit1, pos, 0.0), axis=1, keepdims=True)
        pos2 = jnp.sum(jnp.where(hit2, pos, 0.0), axis=1, keepdims=True)
        s["route"] = jnp.where(lane == 0, pos1, jnp.where(lane == 1, pos2, 0.0))
        route_ref[s["rows"], :] = s["route"]
        cnt_ref[s["index"]] = chunks

    def weight_pieces(w):
        pieces = jnp.zeros(lane.shape, F32)
        rest = w
        for n in range(MOE_WEIGHT_TERMS):
            piece = rest.astype(BF16).astype(F32)
            pieces = jnp.where(lane == n, piece, pieces)
            rest = rest - piece
        return pieces.astype(BF16)

    def compact(s):
        route_t = s["route"].T
        slot = lax.broadcasted_iota(jnp.int32, (cap, tm), 0)
        first = jnp.where(slot == route_t[0:1, :].astype(jnp.int32), 1.0, 0.0).astype(BF16)
        second = jnp.where(slot == route_t[1:2, :].astype(jnp.int32), 1.0, 0.0).astype(BF16)
        hs_ref[s["index"], :, 0:d_model] = jnp.dot(
            first + second, s["h2_hi"], preferred_element_type=F32).astype(BF16)
        slot_w = (jnp.dot(first, weight_pieces(s["w1"]), preferred_element_type=F32)
                  + jnp.dot(second, weight_pieces(s["w2"]), preferred_element_type=F32))
        hs_ref[s["index"], :, d_model:] = slot_w.astype(BF16)

    for stage in (normalise, router_logits, choose_experts, sort_by_expert, compact):
        for s in tiles:
            stage(s)


def _moe_cap(tm):
    worst = 2 * tm + N_EXPERTS * (MOE_CHUNK - 1)
    return -(-worst // 128) * 128


def _route_sort(x1, g, wr_hi, wr_lo, br, tm=MOE_TOKEN_TILE):
    S, D = x1.shape
    nT = S // tm
    cap = _moe_cap(tm)
    full = lambda i: (0, 0)
    row = lambda i: (i, 0)
    per_step = ROUTE_TILES_PER_STEP
    return pl.pallas_call(
        _route_sort_kernel,
        grid=(nT // per_step,),
        in_specs=[
            pl.BlockSpec((per_step * tm, D), row),
            pl.BlockSpec((1, D), full),
            pl.BlockSpec((D, ROUTE_LANES), full),
            pl.BlockSpec((D, ROUTE_LANES), full),
            pl.BlockSpec((1, ROUTE_LANES), full),
        ],
        out_specs=[
            pl.BlockSpec((per_step, cap, D + ROUTE_LANES), lambda i: (i, 0, 0)),
            pl.BlockSpec((per_step * tm, ROUTE_LANES), row),
            pl.BlockSpec((per_step, 1, ROUTE_LANES), lambda i: (i, 0, 0)),
        ],
        out_shape=[
            jax.ShapeDtypeStruct((nT, cap, D + ROUTE_LANES), BF16),
            jax.ShapeDtypeStruct((S, ROUTE_LANES), F32),
            jax.ShapeDtypeStruct((nT, 1, ROUTE_LANES), F32),
        ],
        compiler_params=_params(("parallel",)),
        name="route_sort",
    )(x1, g, wr_hi, wr_lo, br)


def _moe_plan(chunk_counts, n_row_tiles, n_chunk_slots):
    nT, E = chunk_counts.shape
    per_expert = chunk_counts.T
    seg_start = (jnp.cumsum(chunk_counts, axis=1) - chunk_counts).T
    seg_end = jnp.cumsum(per_expert, axis=1)
    n_chunks = seg_end[:, -1]
    padded = -(-n_chunks // MOE_CHUNKS_PER_TILE) * MOE_CHUNKS_PER_TILE
    e_end = jnp.cumsum(padded)
    n_used = e_end[-1] // MOE_CHUNKS_PER_TILE
    c = jnp.arange(n_chunk_slots, dtype=jnp.int32)
    e_of_c = jnp.minimum((e_end[None, :] <= c[:, None]).sum(axis=1), E - 1)
    is_e = (e_of_c[:, None] == jnp.arange(E)[None, :]).astype(jnp.int32)
    local = c - (is_e * (e_end - padded)[None, :]).sum(axis=1)
    real = local < (is_e * n_chunks[None, :]).sum(axis=1)
    pick_e = lambda table: (is_e[:, :, None] * table[None, :, :]).sum(axis=1)
    seg_end_c = pick_e(seg_end)
    t_of_c = jnp.minimum((seg_end_c <= local[:, None]).sum(axis=1), nT - 1)
    is_t = (t_of_c[:, None] == jnp.arange(nT)[None, :]).astype(jnp.int32)
    pick_t = lambda rows: (rows * is_t).sum(axis=1)
    within = local - (pick_t(seg_end_c) - pick_t(pick_e(per_expert)))
    src_tile = jnp.where(real, t_of_c, 0).astype(jnp.int32)
    src_row = jnp.where(real, (pick_t(pick_e(seg_start)) + within) * MOE_CHUNK, 0).astype(jnp.int32)
    tile = jnp.arange(n_row_tiles, dtype=jnp.int32)
    first = jnp.minimum(tile, n_used - 1) * MOE_CHUNKS_PER_TILE
    tile_expert = jnp.minimum((e_end[None, :] <= first[:, None]).sum(axis=1), E - 1)
    tile_real = real.reshape(n_row_tiles, MOE_CHUNKS_PER_TILE).sum(axis=1)
    experts = jnp.arange(E)
    nonempty = padded > 0
    buffer_of_e = jnp.cumsum(nonempty) - nonempty
    later = (experts[None, :] > experts[:, None]) & nonempty[None, :]
    next_of_e = jnp.min(jnp.where(later, experts[None, :], E), axis=1)
    next_of_e = jnp.where(next_of_e < E, next_of_e, -1)
    is_next = (next_of_e[:, None] == experts[None, :]).astype(jnp.int32)
    after_of_e = jnp.where(next_of_e >= 0, (is_next * next_of_e[None, :]).sum(axis=1), -1)
    is_te = (tile_expert[:, None] == experts[None, :]).astype(jnp.int32)
    at = lambda table: (is_te * table[None, :]).sum(axis=1)
    run_len = jnp.maximum(at(padded) // MOE_CHUNKS_PER_TILE, 1)
    run_pos = tile - at(e_end - padded) // MOE_CHUNKS_PER_TILE
    next_expert = at(next_of_e)
    streams = (tile < n_used) & (next_expert >= 0)
    piece_lo = jnp.where(streams, (MOE_WEIGHT_PIECES * run_pos) // run_len, 0)
    piece_hi = jnp.where(streams, (MOE_WEIGHT_PIECES * (run_pos + 1)) // run_len, 0)
    as_i32 = lambda a: a.astype(jnp.int32)
    return (as_i32(tile_expert), as_i32(tile_real), src_tile, src_row, as_i32(n_used.reshape(1)),
            as_i32(at(buffer_of_e) % 2), as_i32(next_expert), as_i32(at(after_of_e)),
            as_i32(piece_lo), as_i32(piece_hi), as_i32(chunk_counts.sum(axis=1)))


def _moe_ffn_kernel(texp_ref, treal_ref, ctile_ref, crow_ref, nused_ref,
                    wbuf_ref, wnext_ref, wafter_ref, plo_ref, phi_ref, tused_ref,
                    hs_hbm, wg_hbm, wu_hbm, wd_hbm, ys_hbm,
                    xbuf, ybuf, zbuf, wg_buf, wu_buf, wd_buf, stage_g, stage_u, stage_d,
                    gather_sem, scatter_sem, weight_sem, zero_sem):
    i = pl.program_id(0)
    n_used = nused_ref[0]
    slot = lax.rem(i, 2)
    rows_gu = stage_g.shape[1]
    rows_d = stage_d.shape[1]

    def piece_copies(e, p, s):
        gu_rows = pl.ds(pl.multiple_of(p * rows_gu, rows_gu), rows_gu)
        d_rows = pl.ds(pl.multiple_of(p * rows_d, rows_d), rows_d)
        return (pltpu.make_async_copy(wg_hbm.at[e, gu_rows, :], stage_g.at[s], weight_sem.at[s]),
                pltpu.make_async_copy(wu_hbm.at[e, gu_rows, :], stage_u.at[s], weight_sem.at[s]),
                pltpu.make_async_copy(wd_hbm.at[e, d_rows, :], stage_d.at[s], weight_sem.at[s]))

    def start_piece(e, p):
        for copy in piece_copies(e, p, lax.rem(p, MOE_WEIGHT_STAGES)):
            copy.start()

    def finish_piece(e, p, side):
        s = lax.rem(p, MOE_WEIGHT_STAGES)
        for copy in piece_copies(e, p, s):
            copy.wait()
        gu_rows = pl.ds(pl.multiple_of(p * rows_gu, rows_gu), rows_gu)
        d_rows = pl.ds(pl.multiple_of(p * rows_d, rows_d), rows_d)
        wg_buf[side, gu_rows, :] = stage_g[s].astype(BF16)
        wu_buf[side, gu_rows, :] = stage_u[s].astype(BF16)
        wd_buf[side, d_rows, :] = stage_d[s].astype(BF16)

    ahead = MOE_WEIGHT_STAGES - 1

    def start_first_pieces(e):
        for p in range(ahead):
            start_piece(e, p)

    def stream_pieces(e, lo, hi, side, following):
        @pl.when(hi > lo)
        def _():
            def body(p, carry):
                @pl.when(p + ahead < MOE_WEIGHT_PIECES)
                def _():
                    start_piece(e, p + ahead)
                finish_piece(e, p, side)
                return carry
            lax.fori_loop(lo, hi, body, 0)

            @pl.when((hi == MOE_WEIGHT_PIECES) & (following >= 0))
            def _():
                start_first_pieces(following)

    def chunk_rows(c):
        return pl.ds(pl.multiple_of(c * MOE_CHUNK, MOE_CHUNK), MOE_CHUNK)

    def gather_copy(tile, c, buf):
        g = tile * MOE_CHUNKS_PER_TILE + c
        src = hs_hbm.at[ctile_ref[g], pl.ds(pl.multiple_of(crow_ref[g], MOE_CHUNK), MOE_CHUNK), :]
        return pltpu.make_async_copy(src, xbuf.at[buf, chunk_rows(c), :], gather_sem.at[buf])

    def scatter_copy(tile, c, buf):
        g = tile * MOE_CHUNKS_PER_TILE + c
        dst = ys_hbm.at[ctile_ref[g], pl.ds(pl.multiple_of(crow_ref[g], MOE_CHUNK), MOE_CHUNK), :]
        return pltpu.make_async_copy(ybuf.at[buf, chunk_rows(c), :], dst, scatter_sem.at[buf])

    def for_real_chunks(tile, fn, also=True):
        n_real = treal_ref[tile]
        for c in range(MOE_CHUNKS_PER_TILE):
            pl.when((c < n_real) & also)(functools.partial(fn, c))

    def for_tail_chunks(fn):
        n_token_tiles, cap = ys_hbm.shape[0], ys_hbm.shape[1]

        def per_tile(t, carry):
            def per_chunk(c, carry):
                fn(pltpu.make_async_copy(zbuf, ys_hbm.at[t, chunk_rows(c), :], zero_sem.at[0]))
                return carry
            return lax.fori_loop(tused_ref[t], cap // MOE_CHUNK, per_chunk, carry)
        lax.fori_loop(0, n_token_tiles, per_tile, 0)

    @pl.when(i == 0)
    def _():
        zbuf[...] = jnp.zeros(zbuf.shape, zbuf.dtype)
        for_tail_chunks(lambda copy: copy.start())
        xbuf[...] = jnp.zeros(xbuf.shape, xbuf.dtype)
        for_real_chunks(0, lambda c: gather_copy(0, c, 0).start())
        start_first_pieces(texp_ref[0])
        stream_pieces(texp_ref[0], 0, MOE_WEIGHT_PIECES, wbuf_ref[0], wnext_ref[0])

    @pl.when(i + 1 < n_used)
    def _():
        for_real_chunks(i + 1, lambda c: gather_copy(i + 1, c, 1 - slot).start())

    side = wbuf_ref[i]
    stream_pieces(wnext_ref[i], plo_ref[i], phi_ref[i], 1 - side, wafter_ref[i])

    def tile_work(n_rows):
        for_real_chunks(i, lambda c: gather_copy(i, c, slot).wait())
        before = jnp.maximum(i - 2, 0)
        for_real_chunks(before, lambda c: scatter_copy(before, c, slot).wait(), also=i >= 2)
        d_model = wg_buf.shape[1]
        x = xbuf[slot, 0:n_rows, 0:d_model]
        w = jnp.sum(xbuf[slot, 0:n_rows, d_model:].astype(F32), axis=1, keepdims=True)
        a = jnp.dot(x, wg_buf[side], preferred_element_type=F32)
        u = jnp.dot(x, wu_buf[side], preferred_element_type=F32)
        hid = (jax.nn.silu(a) * u * w).astype(BF16)
        ybuf[slot, 0:n_rows, :] = jnp.dot(hid, wd_buf[side],
                                          preferred_element_type=F32).astype(BF16)
        for_real_chunks(i, lambda c: scatter_copy(i, c, slot).start())

    half = MOE_ROW_TILE // 2
    fits_half = treal_ref[i] * MOE_CHUNK <= half
    pl.when((i < n_used) & jnp.logical_not(fits_half))(functools.partial(tile_work, MOE_ROW_TILE))
    pl.when((i < n_used) & fits_half)(functools.partial(tile_work, half))

    @pl.when(i == n_used - 1)
    def _():
        @pl.when(i >= 1)
        def _():
            for_real_chunks(i - 1, lambda c: scatter_copy(i - 1, c, 1 - slot).wait())
        for_real_chunks(i, lambda c: scatter_copy(i, c, slot).wait())
        for_tail_chunks(lambda copy: copy.wait())


def _moe_ffn(hs, plan, wg, wu, wd):
    nT, cap, row_width = hs.shape
    E, D, F = wg.shape
    n_row_tiles = plan[0].shape[0]
    assert D % MOE_WEIGHT_PIECES == 0 and F % MOE_WEIGHT_PIECES == 0
    grid_spec = pltpu.PrefetchScalarGridSpec(
        num_scalar_prefetch=len(plan),
        grid=(n_row_tiles,),
        in_specs=[pl.BlockSpec(memory_space=pl.ANY)] * 4,
        out_specs=pl.BlockSpec(memory_space=pl.ANY),
        scratch_shapes=[
            pltpu.VMEM((2, MOE_ROW_TILE, row_width), BF16),
            pltpu.VMEM((2, MOE_ROW_TILE, D), BF16),
            pltpu.VMEM((MOE_CHUNK, D), BF16),
            pltpu.VMEM((2, D, F), BF16),
            pltpu.VMEM((2, D, F), BF16),
            pltpu.VMEM((2, F, D), BF16),
            pltpu.VMEM((MOE_WEIGHT_STAGES, D // MOE_WEIGHT_PIECES, F), F32),
            pltpu.VMEM((MOE_WEIGHT_STAGES, D // MOE_WEIGHT_PIECES, F), F32),
            pltpu.VMEM((MOE_WEIGHT_STAGES, F // MOE_WEIGHT_PIECES, D), F32),
            pltpu.SemaphoreType.DMA((2,)),
            pltpu.SemaphoreType.DMA((2,)),
            pltpu.SemaphoreType.DMA((MOE_WEIGHT_STAGES,)),
            pltpu.SemaphoreType.DMA((1,)),
        ],
    )
    return pl.pallas_call(
        _moe_ffn_kernel,
        grid_spec=grid_spec,
        out_shape=jax.ShapeDtypeStruct((nT, cap, D), BF16),
        compiler_params=_params(("arbitrary",)),
        name="moe_ffn",
    )(*plan, hs, wg, wu, wd)


def _moe_combine_kernel(ys_ref, route_ref, x1_ref, g_ref, o_ref):
    tm = x1_ref.shape[0]
    cap = ys_ref.shape[0]
    route = route_ref[...]
    slot = lax.broadcasted_iota(jnp.int32, (tm, cap), 1)
    pick = jnp.where(slot == route[:, 0:1].astype(jnp.int32), 1.0,
                     jnp.where(slot == route[:, 1:2].astype(jnp.int32), 1.0, 0.0)).astype(BF16)
    y = x1_ref[...] + jnp.dot(pick, ys_ref[...], preferred_element_type=F32)
    ms = jnp.mean(y * y, axis=-1, keepdims=True)
    o_ref[...] = y * lax.rsqrt(ms + RMS_EPS) * g_ref[...]


def _moe_combine(ys, route, x1, g, tm=MOE_TOKEN_TILE):
    S, D = x1.shape
    cap = ys.shape[1]
    row = lambda i: (i, 0)
    return pl.pallas_call(
        _moe_combine_kernel,
        grid=(S // tm,),
        in_specs=[
            pl.BlockSpec((None, cap, D), lambda i: (i, 0, 0)),
            pl.BlockSpec((tm, ROUTE_LANES), row),
            pl.BlockSpec((tm, D), row),
            pl.BlockSpec((1, D), lambda i: (0, 0)),
        ],
        out_specs=pl.BlockSpec((tm, D), row),
        out_shape=jax.ShapeDtypeStruct((S, D), F32),
        compiler_params=_params(("parallel",)),
        name="moe_combine",
    )(ys, route, x1, g)


def _router_weights(w_r_group, b_r_group, w_r_expert, b_r_expert):
    D = w_r_group.shape[0]
    w = jnp.concatenate(
        [w_r_group, jnp.transpose(w_r_expert, (1, 0, 2)).reshape(D, N_EXPERTS)], axis=1)
    b = jnp.concatenate([b_r_group, b_r_expert.reshape(N_EXPERTS)])
    pad = ROUTE_LANES - w.shape[1]
    w = jnp.pad(w, ((0, 0), (0, pad)))
    b = jnp.pad(b, (0, pad)).reshape(1, ROUTE_LANES)
    w_hi = w.astype(BF16)
    w_lo = (w - w_hi.astype(F32)).astype(BF16)
    return w_hi, w_lo, b


def kernel(x, norm_mix, w_in, w_pool, pool_scale, w_branch_pool, w_branch_attn, w_out, norm_ffn,
           w_r_group, b_r_group, w_r_expert, b_r_expert, w_gate, w_up, w_down, norm_final):
    B, S, D = x.shape
    depth = w_in.shape[0]
    assert depth == 1, "the final rms_norm is fused into the expert kernel of a single layer"
    slopes = jnp.exp2(-8.0 * jnp.arange(1, ATTN_HEADS + 1, dtype=F32) / ATTN_HEADS)
    slopes = jnp.broadcast_to(slopes[:, None, None], (ATTN_HEADS, 1, 128))
    outs = []
    for b in range(B):
        xb = x[b]
        for l in range(depth):
            proj = _inproj(_rms_norm(xb, norm_mix[l].reshape(1, D)), w_in[l])
            y_pool = _pool_mixer(proj, w_pool[l].astype(BF16), pool_scale[l].reshape(1, POOL_WIDTH))
            qT, vT, sel = _moba_gate(proj)
            y_attn = _moba_attention(proj, qT, vT, sel, slopes)
            wr_hi, wr_lo, br = _router_weights(w_r_group[l], b_r_group[l], w_r_expert[l], b_r_expert[l])
            x1 = _merge_out(y_pool, y_attn, proj, xb, w_branch_pool[l].astype(BF16),
                            w_branch_attn[l].astype(BF16), w_out[l].astype(BF16))
            hs, route, cnt = _route_sort(x1, norm_ffn[l].reshape(1, D), wr_hi, wr_lo, br)
            nT = hs.shape[0]
            chunk_counts = cnt[:, 0, EXPERT_LANE0:EXPERT_LANE0 + N_EXPERTS].astype(jnp.int32)
            max_chunks = nT * ((2 * MOE_TOKEN_TILE + N_EXPERTS * (MOE_CHUNK - 1)) // MOE_CHUNK)
            n_row_tiles = -(-(max_chunks + N_EXPERTS * (MOE_CHUNKS_PER_TILE - 1))
                            // MOE_CHUNKS_PER_TILE)
            plan = _moe_plan(chunk_counts, n_row_tiles, n_row_tiles * MOE_CHUNKS_PER_TILE)
            ys = _moe_ffn(hs, plan, w_gate[l], w_up[l], w_down[l])
            xb = _moe_combine(ys, route, x1, norm_final.reshape(1, D))
        outs.append(xb)
    return jnp.stack(outs, axis=0)
```

```python
import functools

import jax
import jax.numpy as jnp
from jax import lax
from jax.experimental import pallas as pl
from jax.experimental.pallas import tpu as pltpu

F32 = jnp.float32
BF16 = jnp.bfloat16

POOL_WINDOWS = (2, 4, 8, 16)
MAX_WINDOW = 16
POOL_WIDTH = 1024
POOL_GROUP = 256
HEAD_DIM = 128
ATTN_HEADS = 8
ATTN_WIDTH = 1024
MOBA_BLOCK = 256
MOBA_TOPK = 3
N_GROUPS = 4
EXPERTS_PER_GROUP = 4
N_EXPERTS = 16
ROUTE_LANES = 128
EXPERT_LANE0 = N_GROUPS
RMS_EPS = 1e-6
NEG_INF = -1e30
LOG2_E = 1.4426950408889634
QUERY_SCALE = (HEAD_DIM ** -0.5) * LOG2_E
KEY_AUG = 2 * HEAD_DIM
ALIBI_TERMS = 3
VALUE_AUG = HEAD_DIM + 16
ATTN_KV_UNROLL = 3
ATTN_TILES_PER_STEP = 4
MOE_CHUNK = 16
MOE_TOKEN_TILE = 256
MERGE_WEIGHT_ROWS = 128
ROUTE_TILES_PER_STEP = 4
MOE_ROW_TILE = 256
MOE_CHUNKS_PER_TILE = MOE_ROW_TILE // MOE_CHUNK
MOE_WEIGHT_TERMS = 3
MOE_WEIGHT_PIECES = 8
MOE_WEIGHT_STAGES = 4
assert MOE_WEIGHT_STAGES - 1 <= MOE_WEIGHT_PIECES

V7X_VMEM_LIMIT_BYTES = 56 * 1024 * 1024


def _params(semantics, vmem=V7X_VMEM_LIMIT_BYTES, flags=None):
    return pltpu.CompilerParams(dimension_semantics=semantics, vmem_limit_bytes=vmem, flags=flags)


def _rms_norm_kernel(x_ref, g_ref, o_ref):
    x = x_ref[...]
    ms = jnp.mean(x * x, axis=-1, keepdims=True)
    o_ref[...] = (x * lax.rsqrt(ms + RMS_EPS) * g_ref[...]).astype(o_ref.dtype)


def _rms_norm(x, g, tm=1024):
    S, D = x.shape
    return pl.pallas_call(
        _rms_norm_kernel,
        grid=(S // tm,),
        in_specs=[pl.BlockSpec((tm, D), lambda i: (i, 0)), pl.BlockSpec((1, D), lambda i: (0, 0))],
        out_specs=pl.BlockSpec((tm, D), lambda i: (i, 0)),
        out_shape=jax.ShapeDtypeStruct((S, D), BF16),
        compiler_params=_params(("parallel",)),
        name="rms_norm",
    )(x, g)


def _inproj_kernel(h_ref, w_ref, o_ref, wb_ref):
    @pl.when(pl.program_id(1) == 0)
    def _():
        wb_ref[...] = w_ref[...].astype(BF16)

    col0 = pl.program_id(0) * o_ref.shape[1]
    is_q = (col0 >= POOL_WIDTH) & (col0 < POOL_WIDTH + ATTN_WIDTH)
    factor = jnp.where(is_q, QUERY_SCALE, 1.0).astype(F32)
    o_ref[...] = (jnp.dot(h_ref[...], wb_ref[...], preferred_element_type=F32)
                  * factor).astype(o_ref.dtype)


def _inproj(h, w, tm=2048, tn=1024):
    S, D = h.shape
    N = w.shape[1]
    tm = min(tm, S)
    assert POOL_WIDTH % tn == 0 and ATTN_WIDTH % tn == 0
    return pl.pallas_call(
        _inproj_kernel,
        grid=(N // tn, S // tm),
        in_specs=[
            pl.BlockSpec((tm, D), lambda j, i: (i, 0)),
            pl.BlockSpec((D, tn), lambda j, i: (0, j)),
        ],
        out_specs=pl.BlockSpec((tm, tn), lambda j, i: (i, j)),
        out_shape=jax.ShapeDtypeStruct((S, N), BF16),
        scratch_shapes=[pltpu.VMEM((D, tn), BF16)],
        compiler_params=_params(("parallel", "arbitrary")),
        name="inproj",
    )(h, w)


def _pool_kernel(cur_ref, prev_ref, w_ref, scale_ref, o_ref):
    i = pl.program_id(0)
    tm = cur_ref.shape[0]
    u = cur_ref[...].astype(F32)
    halo = jnp.where(i > 0, prev_ref[...].astype(F32), 0.0)
    ext = jnp.concatenate([halo, u], axis=0)
    t = i * tm + lax.broadcasted_iota(jnp.int32, (tm, 1), 0)
    for g, w in enumerate(POOL_WINDOWS):
        cols = slice(g * POOL_GROUP, (g + 1) * POOL_GROUP)
        ug = u[:, cols]
        run = ext[:, cols]
        s = 1
        while s < w:
            run = run + pltpu.roll(run, s, axis=0)
            s *= 2
        wsum = run[MAX_WINDOW:, :]
        cnt = jnp.minimum(t + 1, w).astype(F32)
        mixed = (wsum / cnt - ug).astype(BF16)
        y = jnp.dot(mixed, w_ref[g], preferred_element_type=F32)
        o_ref[:, cols] = (y * scale_ref[:, cols]).astype(o_ref.dtype)


def _pool_mixer(proj, w_pool, pool_scale, tm=512):
    S = proj.shape[0]
    halo_blocks = tm // MAX_WINDOW
    return pl.pallas_call(
        _pool_kernel,
        grid=(S // tm,),
        in_specs=[
            pl.BlockSpec((tm, POOL_WIDTH), lambda i: (i, 0)),
            pl.BlockSpec((MAX_WINDOW, POOL_WIDTH),
                         lambda i: (jnp.maximum(i * halo_blocks - 1, 0), 0)),
            pl.BlockSpec((len(POOL_WINDOWS), POOL_GROUP, POOL_GROUP), lambda i: (0, 0, 0)),
            pl.BlockSpec((1, POOL_WIDTH), lambda i: (0, 0)),
        ],
        out_specs=pl.BlockSpec((tm, POOL_WIDTH), lambda i: (i, 0)),
        out_shape=jax.ShapeDtypeStruct((S, POOL_WIDTH), BF16),
        compiler_params=_params(("parallel",)),
        name="pool_mixer",
    )(proj, proj, w_pool, pool_scale)


def _alibi_query_rows(slope_ref):
    slope2 = slope_ref[...][:, 0:1] * LOG2_E
    q_extra = jnp.zeros((KEY_AUG - HEAD_DIM, MOBA_BLOCK), F32)
    row = lax.broadcasted_iota(jnp.int32, q_extra.shape, 0)
    rest = slope2
    for n in range(ALIBI_TERMS):
        piece = rest.astype(BF16).astype(F32)
        q_extra = jnp.where(row == n, piece, q_extra)
        rest = rest - piece
    return q_extra.astype(BF16)


def _moba_gate_kernel(q_ref, k_ref, v_ref, qT_ref, vT_ref, sel_ref):
    S = q_ref.shape[0]
    nb = S // MOBA_BLOCK
    topk = min(MOBA_TOPK, nb)
    kf = k_ref[...].astype(F32).reshape(nb, MOBA_BLOCK, HEAD_DIM)
    kmean = jnp.sum(kf, axis=1) * (1.0 / MOBA_BLOCK)
    km_hi = kmean.astype(BF16)
    km_lo = (kmean - km_hi.astype(F32)).astype(BF16)
    blk = lax.broadcasted_iota(jnp.int32, (nb, MOBA_BLOCK), 0)
    v_row = lax.broadcasted_iota(jnp.int32, (VALUE_AUG - HEAD_DIM, MOBA_BLOCK), 0)
    v_extra = jnp.where(v_row == 0, 1.0, 0.0).astype(BF16)

    def body(i, carry):
        rows = pl.ds(pl.multiple_of(i * MOBA_BLOCK, MOBA_BLOCK), MOBA_BLOCK)
        qT = q_ref[rows, :].astype(F32).T.astype(BF16)
        qT_ref[i] = qT
        vT_ref[i, 0:HEAD_DIM, :] = v_ref[rows, :].astype(F32).T.astype(BF16)
        vT_ref[i, HEAD_DIM:VALUE_AUG, :] = v_extra
        gate = (jnp.dot(km_hi, qT, preferred_element_type=F32)
                + jnp.dot(km_lo, qT, preferred_element_type=F32)) * (1.0 / QUERY_SCALE)
        gate = jnp.where(blk < i, gate, NEG_INF)
        sel = jnp.zeros((nb, MOBA_BLOCK), F32)
        for _ in range(topk):
            best = jnp.max(gate, axis=0, keepdims=True)
            idx = jnp.min(jnp.where(gate == best, blk, nb), axis=0, keepdims=True)
            hit = blk == idx
            sel = jnp.where(hit & (blk < i), 1.0, sel)
            gate = jnp.where(hit, -jnp.inf, gate)
        sel_ref[i] = sel
        return carry

    lax.fori_loop(0, nb, body, 0, unroll=4 if nb % 4 == 0 else 1)


def _moba_gate(proj):
    S = proj.shape[0]
    nb = S // MOBA_BLOCK
    H = ATTN_HEADS
    q0 = POOL_WIDTH // HEAD_DIM
    k0 = q0 + H
    v0 = k0 + H
    blocked = lambda h: (h, 0, 0, 0)
    return pl.pallas_call(
        _moba_gate_kernel,
        grid=(H,),
        in_specs=[
            pl.BlockSpec((S, HEAD_DIM), lambda h: (0, q0 + h)),
            pl.BlockSpec((S, HEAD_DIM), lambda h: (0, k0 + h)),
            pl.BlockSpec((S, HEAD_DIM), lambda h: (0, v0 + h)),
        ],
        out_specs=[
            pl.BlockSpec((None, nb, HEAD_DIM, MOBA_BLOCK), blocked),
            pl.BlockSpec((None, nb, VALUE_AUG, MOBA_BLOCK), blocked),
            pl.BlockSpec((None, nb, nb, MOBA_BLOCK), blocked),
        ],
        out_shape=[
            jax.ShapeDtypeStruct((H, nb, HEAD_DIM, MOBA_BLOCK), BF16),
            jax.ShapeDtypeStruct((H, nb, VALUE_AUG, MOBA_BLOCK), BF16),
            jax.ShapeDtypeStruct((H, nb, nb, MOBA_BLOCK), F32),
        ],
        compiler_params=_params(("parallel",)),
        name="moba_gate",
    )(proj, proj, proj)


def _moba_attn_kernel(slope_ref, qT_ref, qT_next_ref, sel_ref, k_ref, vT_ref, o_ref,
                      ka_ref, *scratch):
    first = pl.program_id(1) * ATTN_TILES_PER_STEP
    nb = sel_ref.shape[1]

    @pl.when(first == 0)
    def _():
        pos = lax.broadcasted_iota(jnp.int32, (MOBA_BLOCK, KEY_AUG - HEAD_DIM), 0).astype(F32)
        col = lax.broadcasted_iota(jnp.int32, (MOBA_BLOCK, KEY_AUG - HEAD_DIM), 1)
        k_extra = jnp.where(col < ALIBI_TERMS, pos, 0.0).astype(BF16)

        def per_block(j, carry):
            rows = pl.ds(pl.multiple_of(j * MOBA_BLOCK, MOBA_BLOCK), MOBA_BLOCK)
            ka_ref[rows, 0:HEAD_DIM] = k_ref[rows, :]
            ka_ref[rows, HEAD_DIM:KEY_AUG] = k_extra
            return carry
        lax.fori_loop(0, nb, per_block, 0)

    q_extra = _alibi_query_rows(slope_ref)
    augment = lambda q: jnp.concatenate([q, q_extra], axis=0)
    for t in range(ATTN_TILES_PER_STEP):
        last = t + 1 == ATTN_TILES_PER_STEP
        _moba_attn_tile(first + t, t == 0, slope_ref, augment(qT_ref[t]),
                        augment(qT_next_ref[...] if last else qT_ref[t + 1]), sel_ref.at[t],
                        ka_ref, vT_ref, o_ref.at[t * MOBA_BLOCK:(t + 1) * MOBA_BLOCK, :], *scratch)


def _moba_attn_tile(i, may_start_head, slope_ref, qT, q_next, sel_ref, k_ref, vT_ref, o_ref,
                    s0_ref, s1_ref, p0_ref, p1_ref, s_own_ref):
    nb = sel_ref.shape[0]
    slope2 = slope_ref[...][:, 0:1] * LOG2_E

    def block_of(t, u):
        return jnp.clip(t * ATTN_KV_UNROLL + u, 0, nb - 1)

    def key_rows(j):
        return pl.ds(pl.multiple_of(j * MOBA_BLOCK, MOBA_BLOCK), MOBA_BLOCK)

    def issue_scores(t, s_ref, queries=qT):
        for u in range(ATTN_KV_UNROLL):
            s_ref[u] = jnp.dot(k_ref[key_rows(block_of(t, u)), :], queries,
                               preferred_element_type=F32)

    def apply_probs(t, p_ref, alpha, acc):
        acc = alpha * acc
        for u in range(ATTN_KV_UNROLL):
            acc = acc + jnp.dot(vT_ref[block_of(t, u)], p_ref[u], preferred_element_type=F32)
        return acc

    def softmax_group(t, s_ref, p_ref, m):
        m_new = m
        shifts = []
        for u in range(ATTN_KV_UNROLL):
            j = t * ATTN_KV_UNROLL + u
            valid = jnp.where(j < i, sel_ref[pl.ds(block_of(t, u), 1), :], 0.0) > 0.0
            gap = slope2 * ((i - j) * MOBA_BLOCK).astype(F32)
            top = jnp.max(s_ref[u], axis=0, keepdims=True) - gap
            m_new = jnp.maximum(m_new, jnp.where(valid, top, NEG_INF))
            shifts.append((valid, gap))
        alpha = jnp.exp2(m - m_new)
        for u, (valid, gap) in enumerate(shifts):
            p = jnp.exp2(s_ref[u] - jnp.where(valid, m_new + gap, jnp.inf))
            p_ref[u] = p.astype(BF16)
        return m_new, alpha

    def body(r, carry):
        m, acc, alpha = carry
        acc = apply_probs(2 * r - 1, p1_ref, alpha, acc)
        m, alpha = softmax_group(2 * r, s0_ref, p0_ref, m)
        issue_scores(2 * r + 1, s1_ref)
        acc = apply_probs(2 * r, p0_ref, alpha, acc)
        m, alpha = softmax_group(2 * r + 1, s1_ref, p1_ref, m)
        issue_scores(2 * r + 2, s0_ref)
        return m, acc, alpha

    n_groups = (i + ATTN_KV_UNROLL - 1) // ATTN_KV_UNROLL
    n_pairs = n_groups // 2
    if may_start_head:
        @pl.when(i == 0)
        def _():
            s0_ref[...] = jnp.zeros(s0_ref.shape, F32)
            s_own_ref[...] = jnp.dot(k_ref[key_rows(0), :], qT, preferred_element_type=F32)

    p1_ref[...] = jnp.zeros(p1_ref.shape, BF16)
    kpos = lax.broadcasted_iota(jnp.int32, (MOBA_BLOCK, MOBA_BLOCK), 0)
    qpos = lax.broadcasted_iota(jnp.int32, (MOBA_BLOCK, MOBA_BLOCK), 1)
    s_own = jnp.where(qpos >= kpos, s_own_ref[...], NEG_INF)
    top_own = jnp.max(s_own, axis=0, keepdims=True)
    row = jnp.zeros((1, MOBA_BLOCK), F32)
    init = (row + NEG_INF, jnp.zeros((VALUE_AUG, MOBA_BLOCK), F32), row + 1.0)
    m, acc, alpha = lax.fori_loop(0, n_pairs, body, init)

    def finish(has_last_group):
        m_, acc_ = m, apply_probs(2 * n_pairs - 1, p1_ref, alpha, acc)
        if has_last_group:
            m_, alpha_ = softmax_group(2 * n_pairs, s0_ref, p0_ref, m_)
            acc_ = apply_probs(2 * n_pairs, p0_ref, alpha_, acc_)
        m_new = jnp.maximum(m_, top_own)
        p = jnp.exp2(s_own - m_new)
        acc_ = (jnp.exp2(m_ - m_new) * acc_
                + jnp.dot(vT_ref[i], p.astype(BF16), preferred_element_type=F32))
        out = acc_[0:HEAD_DIM] / acc_[HEAD_DIM:HEAD_DIM + 1]
        o_ref[...] = out.T.astype(o_ref.dtype)
        issue_scores(0, s0_ref, q_next)
        s_own_ref[...] = jnp.dot(k_ref[key_rows(jnp.minimum(i + 1, nb - 1)), :], q_next,
                                 preferred_element_type=F32)

    pl.when(n_groups > 2 * n_pairs)(functools.partial(finish, True))
    pl.when(n_groups == 2 * n_pairs)(functools.partial(finish, False))


def _moba_attention(proj, qT, vT, sel, slopes):
    S = proj.shape[0]
    H = ATTN_HEADS
    nb = S // MOBA_BLOCK
    k0 = POOL_WIDTH // HEAD_DIM + H
    per_step = ATTN_TILES_PER_STEP
    assert nb % per_step == 0
    return pl.pallas_call(
        _moba_attn_kernel,
        grid=(H, nb // per_step),
        in_specs=[
            pl.BlockSpec((None, 1, 128), lambda h, g: (h, 0, 0)),
            pl.BlockSpec((None, per_step, HEAD_DIM, MOBA_BLOCK), lambda h, g: (h, g, 0, 0)),
            pl.BlockSpec((None, None, HEAD_DIM, MOBA_BLOCK),
                         lambda h, g: (h, jnp.minimum((g + 1) * per_step, nb - 1), 0, 0)),
            pl.BlockSpec((None, per_step, nb, MOBA_BLOCK), lambda h, g: (h, g, 0, 0)),
            pl.BlockSpec((S, HEAD_DIM), lambda h, g: (0, k0 + h)),
            pl.BlockSpec((None, nb, VALUE_AUG, MOBA_BLOCK), lambda h, g: (h, 0, 0, 0)),
        ],
        out_specs=pl.BlockSpec((per_step * MOBA_BLOCK, HEAD_DIM), lambda h, g: (g, h)),
        out_shape=jax.ShapeDtypeStruct((S, ATTN_WIDTH), BF16),
        scratch_shapes=[
            pltpu.VMEM((S, KEY_AUG), BF16),
            pltpu.VMEM((ATTN_KV_UNROLL, MOBA_BLOCK, MOBA_BLOCK), F32),
            pltpu.VMEM((ATTN_KV_UNROLL, MOBA_BLOCK, MOBA_BLOCK), F32),
            pltpu.VMEM((ATTN_KV_UNROLL, MOBA_BLOCK, MOBA_BLOCK), BF16),
            pltpu.VMEM((ATTN_KV_UNROLL, MOBA_BLOCK, MOBA_BLOCK), BF16),
            pltpu.VMEM((MOBA_BLOCK, MOBA_BLOCK), F32),
        ],
        compiler_params=_params(("arbitrary", "arbitrary")),
        name="moba_attention",
    )(slopes, qT, qT, sel, proj, vT)


def _first_lane_of_max(vals, lane):
    best = jnp.max(vals, axis=1, keepdims=True)
    idx = jnp.min(jnp.where(vals == best, lane, ROUTE_LANES), axis=1, keepdims=True)
    return best, idx


def _merge_out_kernel(yp_ref, ya_ref, glp_ref, gla_ref, x_ref, wbp_hbm, wba_hbm, wout_hbm, x1_ref,
                      wbp_ref, wba_ref, wout_ref, stage_ref, stage_sem):
    @pl.when(pl.program_id(0) == 0)
    def _():
        rows = stage_ref.shape[1]
        pieces = [(src, dst, r) for src, dst in ((wbp_hbm, wbp_ref), (wba_hbm, wba_ref),
                                                 (wout_hbm, wout_ref))
                  for r in range(0, src.shape[0], rows)]
        copies = [pltpu.make_async_copy(src.at[r:r + rows, :], stage_ref.at[n % 2],
                                        stage_sem.at[n % 2])
                  for n, (src, _, r) in enumerate(pieces)]
        copies[0].start()
        for n, (_, dst, r) in enumerate(pieces):
            if n + 1 < len(pieces):
                copies[n + 1].start()
            copies[n].wait()
            dst[r:r + rows, :] = stage_ref[n % 2].astype(BF16)

    bp = jnp.dot(yp_ref[...], wbp_ref[...], preferred_element_type=F32)
    ba = jnp.dot(ya_ref[...], wba_ref[...], preferred_element_type=F32)
    merged = (jax.nn.sigmoid(glp_ref[...].astype(F32)) * bp
              + jax.nn.sigmoid(gla_ref[...].astype(F32)) * ba)
    x1_ref[...] = x_ref[...] + jnp.dot(merged.astype(BF16), wout_ref[...],
                                       preferred_element_type=F32)


def _merge_out(yp, ya, proj, x, wbp, wba, wout, tm=512):
    S, D = x.shape
    row = lambda i: (i, 0)
    glp_blk = (POOL_WIDTH + 3 * ATTN_WIDTH) // D
    hbm = pl.BlockSpec(memory_space=pl.ANY)
    return pl.pallas_call(
        _merge_out_kernel,
        grid=(S // tm,),
        in_specs=[
            pl.BlockSpec((tm, POOL_WIDTH), row),
            pl.BlockSpec((tm, ATTN_WIDTH), row),
            pl.BlockSpec((tm, D), lambda i: (i, glp_blk)),
            pl.BlockSpec((tm, D), lambda i: (i, glp_blk + 1)),
            pl.BlockSpec((tm, D), row),
            hbm, hbm, hbm,
        ],
        out_specs=pl.BlockSpec((tm, D), row),
        out_shape=jax.ShapeDtypeStruct((S, D), F32),
        scratch_shapes=[
            pltpu.VMEM((POOL_WIDTH, D), BF16),
            pltpu.VMEM((ATTN_WIDTH, D), BF16),
            pltpu.VMEM((D, D), BF16),
            pltpu.VMEM((2, MERGE_WEIGHT_ROWS, D), F32),
            pltpu.SemaphoreType.DMA((2,)),
        ],
        compiler_params=_params(("arbitrary",)),
        name="merge_out",
    )(yp, ya, proj, proj, x, wbp, wba, wout)


def _route_sort_kernel(x1_ref, g_ref, wr_hi_ref, wr_lo_ref, br_ref, hs_ref, route_ref, cnt_ref):
    n_tiles = ROUTE_TILES_PER_STEP
    tm = x1_ref.shape[0] // n_tiles
    cap = hs_ref.shape[1]
    d_model = x1_ref.shape[1]
    lane = lax.broadcasted_iota(jnp.int32, (tm, ROUTE_LANES), 1)
    tiles = [{"index": t, "rows": slice(t * tm, (t + 1) * tm)} for t in range(n_tiles)]

    def normalise(s):
        x1 = x1_ref[s["rows"], :]
        ms = jnp.mean(x1 * x1, axis=-1, keepdims=True)
        h2 = x1 * lax.rsqrt(ms + RMS_EPS) * g_ref[...]
        s["h2_hi"] = h2.astype(BF16)
        s["h2_lo"] = (h2 - s["h2_hi"].astype(F32)).astype(BF16)

    def router_logits(s):
        s["logits"] = (jnp.dot(s["h2_hi"], wr_hi_ref[...], preferred_element_type=F32)
                       + jnp.dot(s["h2_lo"], wr_hi_ref[...], preferred_element_type=F32)
                       + jnp.dot(s["h2_hi"], wr_lo_ref[...], preferred_element_type=F32)
                       + br_ref[...])

    def choose_experts(s):
        logits = s["logits"]
        g_logits = jnp.where(lane < N_GROUPS, logits, -jnp.inf)
        g_best, g_idx = _first_lane_of_max(g_logits, lane)
        g_w = 1.0 / jnp.sum(jnp.exp(g_logits - g_best), axis=1, keepdims=True)
        e_lo = EXPERT_LANE0 + EXPERTS_PER_GROUP * g_idx
        e_logits = jnp.where((lane >= e_lo) & (lane < e_lo + EXPERTS_PER_GROUP), logits, -jnp.inf)
        v1, i1 = _first_lane_of_max(e_logits, lane)
        v2, i2 = _first_lane_of_max(jnp.where(lane == i1, -jnp.inf, e_logits), lane)
        e21 = jnp.exp(v2 - v1)
        s["w1"] = g_w / (1.0 + e21)
        s["w2"] = g_w * e21 / (1.0 + e21)
        s["hit1"] = lane == i1
        s["hit2"] = lane == i2

    def sort_by_expert(s):
        hit1, hit2 = s["hit1"], s["hit2"]
        member = jnp.where(hit1, 1.0, jnp.where(hit2, 1.0, 0.0))
        r_tok = lax.broadcasted_iota(jnp.int32, (tm, tm), 0)
        c_tok = lax.broadcasted_iota(jnp.int32, (tm, tm), 1)
        earlier = jnp.where(c_tok < r_tok, 1.0, 0.0).astype(BF16)
        rank = jnp.dot(earlier, member.astype(BF16), preferred_element_type=F32)
        count = jnp.sum(member, axis=0, keepdims=True)
        chunks = jnp.floor((count + (MOE_CHUNK - 1)) * (1.0 / MOE_CHUNK))
        r_l = lax.broadcasted_iota(jnp.int32, (ROUTE_LANES, ROUTE_LANES), 0)
        c_l = lax.broadcasted_iota(jnp.int32, (ROUTE_LANES, ROUTE_LANES), 1)
        lower_lanes = jnp.where(r_l < c_l, 1.0, 0.0).astype(BF16)
        start = jnp.dot(jnp.broadcast_to(chunks, (8, ROUTE_LANES)).astype(BF16), lower_lanes,
                        preferred_element_type=F32)[0:1] * MOE_CHUNK
        pos = start + rank
        pos1 = jnp.sum(jnp.where(hit1, pos, 0.0), axis=1, keepdims=True)
        pos2 = jnp.sum(jnp.where(hit2, pos, 0.0), axis=1, keepdims=True)
        s["route"] = jnp.where(lane == 0, pos1, jnp.where(lane == 1, pos2, 0.0))
        route_ref[s["rows"], :] = s["route"]
        cnt_ref[s["index"]] = chunks

    def weight_pieces(w):
        pieces = jnp.zeros(lane.shape, F32)
        rest = w
        for n in range(MOE_WEIGHT_TERMS):
            piece = rest.astype(BF16).astype(F32)
            pieces = jnp.where(lane == n, piece, pieces)
            rest = rest - piece
        return pieces.astype(BF16)

    def compact(s):
        route_t = s["route"].T
        slot = lax.broadcasted_iota(jnp.int32, (cap, tm), 0)
        first = jnp.where(slot == route_t[0:1, :].astype(jnp.int32), 1.0, 0.0).astype(BF16)
        second = jnp.where(slot == route_t[1:2, :].astype(jnp.int32), 1.0, 0.0).astype(BF16)
        hs_ref[s["index"], :, 0:d_model] = jnp.dot(
            first + second, s["h2_hi"], preferred_element_type=F32).astype(BF16)
        slot_w = (jnp.dot(first, weight_pieces(s["w1"]), preferred_element_type=F32)
                  + jnp.dot(second, weight_pieces(s["w2"]), preferred_element_type=F32))
        hs_ref[s["index"], :, d_model:] = slot_w.astype(BF16)

    for stage in (normalise, router_logits, choose_experts, sort_by_expert, compact):
        for s in tiles:
            stage(s)


def _moe_cap(tm):
    worst = 2 * tm + N_EXPERTS * (MOE_CHUNK - 1)
    return -(-worst // 128) * 128


def _route_sort(x1, g, wr_hi, wr_lo, br, tm=MOE_TOKEN_TILE):
    S, D = x1.shape
    nT = S // tm
    cap = _moe_cap(tm)
    full = lambda i: (0, 0)
    row = lambda i: (i, 0)
    per_step = ROUTE_TILES_PER_STEP
    return pl.pallas_call(
        _route_sort_kernel,
        grid=(nT // per_step,),
        in_specs=[
            pl.BlockSpec((per_step * tm, D), row),
            pl.BlockSpec((1, D), full),
            pl.BlockSpec((D, ROUTE_LANES), full),
            pl.BlockSpec((D, ROUTE_LANES), full),
            pl.BlockSpec((1, ROUTE_LANES), full),
        ],
        out_specs=[
            pl.BlockSpec((per_step, cap, D + ROUTE_LANES), lambda i: (i, 0, 0)),
            pl.BlockSpec((per_step * tm, ROUTE_LANES), row),
            pl.BlockSpec((per_step, 1, ROUTE_LANES), lambda i: (i, 0, 0)),
        ],
        out_shape=[
            jax.ShapeDtypeStruct((nT, cap, D + ROUTE_LANES), BF16),
            jax.ShapeDtypeStruct((S, ROUTE_LANES), F32),
            jax.ShapeDtypeStruct((nT, 1, ROUTE_LANES), F32),
        ],
        compiler_params=_params(("parallel",)),
        name="route_sort",
    )(x1, g, wr_hi, wr_lo, br)


def _moe_plan(chunk_counts, n_row_tiles, n_chunk_slots):
    nT, E = chunk_counts.shape
    per_expert = chunk_counts.T
    seg_start = (jnp.cumsum(chunk_counts, axis=1) - chunk_counts).T
    seg_end = jnp.cumsum(per_expert, axis=1)
    n_chunks = seg_end[:, -1]
    padded = -(-n_chunks // MOE_CHUNKS_PER_TILE) * MOE_CHUNKS_PER_TILE
    e_end = jnp.cumsum(padded)
    n_used = e_end[-1] // MOE_CHUNKS_PER_TILE
    c = jnp.arange(n_chunk_slots, dtype=jnp.int32)
    e_of_c = jnp.minimum((e_end[None, :] <= c[:, None]).sum(axis=1), E - 1)
    is_e = (e_of_c[:, None] == jnp.arange(E)[None, :]).astype(jnp.int32)
    local = c - (is_e * (e_end - padded)[None, :]).sum(axis=1)
    real = local < (is_e * n_chunks[None, :]).sum(axis=1)
    pick_e = lambda table: (is_e[:, :, None] * table[None, :, :]).sum(axis=1)
    seg_end_c = pick_e(seg_end)
    t_of_c = jnp.minimum((seg_end_c <= local[:, None]).sum(axis=1), nT - 1)
    is_t = (t_of_c[:, None] == jnp.arange(nT)[None, :]).astype(jnp.int32)
    pick_t = lambda rows: (rows * is_t).sum(axis=1)
    within = local - (pick_t(seg_end_c) - pick_t(pick_e(per_expert)))
    src_tile = jnp.where(real, t_of_c, 0).astype(jnp.int32)
    src_row = jnp.where(real, (pick_t(pick_e(seg_start)) + within) * MOE_CHUNK, 0).astype(jnp.int32)
    tile = jnp.arange(n_row_tiles, dtype=jnp.int32)
    first = jnp.minimum(tile, n_used - 1) * MOE_CHUNKS_PER_TILE
    tile_expert = jnp.minimum((e_end[None, :] <= first[:, None]).sum(axis=1), E - 1)
    tile_real = real.reshape(n_row_tiles, MOE_CHUNKS_PER_TILE).sum(axis=1)
    experts = jnp.arange(E)
    nonempty = padded > 0
    buffer_of_e = jnp.cumsum(nonempty) - nonempty
    later = (experts[None, :] > experts[:, None]) & nonempty[None, :]
    next_of_e = jnp.min(jnp.where(later, experts[None, :], E), axis=1)
    next_of_e = jnp.where(next_of_e < E, next_of_e, -1)
    is_next = (next_of_e[:, None] == experts[None, :]).astype(jnp.int32)
    after_of_e = jnp.where(next_of_e >= 0, (is_next * next_of_e[None, :]).sum(axis=1), -1)
    is_te = (tile_expert[:, None] == experts[None, :]).astype(jnp.int32)
    at = lambda table: (is_te * table[None, :]).sum(axis=1)
    run_len = jnp.maximum(at(padded) // MOE_CHUNKS_PER_TILE, 1)
    run_pos = tile - at(e_end - padded) // MOE_CHUNKS_PER_TILE
    next_expert = at(next_of_e)
    streams = (tile < n_used) & (next_expert >= 0)
    piece_lo = jnp.where(streams, (MOE_WEIGHT_PIECES * run_pos) // run_len, 0)
    piece_hi = jnp.where(streams, (MOE_WEIGHT_PIECES * (run_pos + 1)) // run_len, 0)
    as_i32 = lambda a: a.astype(jnp.int32)
    return (as_i32(tile_expert), as_i32(tile_real), src_tile, src_row, as_i32(n_used.reshape(1)),
            as_i32(at(buffer_of_e) % 2), as_i32(next_expert), as_i32(at(after_of_e)),
            as_i32(piece_lo), as_i32(piece_hi), as_i32(chunk_counts.sum(axis=1)))


def _moe_ffn_kernel(texp_ref, treal_ref, ctile_ref, crow_ref, nused_ref,
                    wbuf_ref, wnext_ref, wafter_ref, plo_ref, phi_ref, tused_ref,
                    hs_hbm, wg_hbm, wu_hbm, wd_hbm, ys_hbm,
                    xbuf, ybuf, zbuf, wg_buf, wu_buf, wd_buf, stage_g, stage_u, stage_d,
                    gather_sem, scatter_sem, weight_sem, zero_sem):
    i = pl.program_id(0)
    n_used = nused_ref[0]
    slot = lax.rem(i, 2)
    rows_gu = stage_g.shape[1]
    rows_d = stage_d.shape[1]

    def piece_copies(e, p, s):
        gu_rows = pl.ds(pl.multiple_of(p * rows_gu, rows_gu), rows_gu)
        d_rows = pl.ds(pl.multiple_of(p * rows_d, rows_d), rows_d)
        return (pltpu.make_async_copy(wg_hbm.at[e, gu_rows, :], stage_g.at[s], weight_sem.at[s]),
                pltpu.make_async_copy(wu_hbm.at[e, gu_rows, :], stage_u.at[s], weight_sem.at[s]),
                pltpu.make_async_copy(wd_hbm.at[e, d_rows, :], stage_d.at[s], weight_sem.at[s]))

    def start_piece(e, p):
        for copy in piece_copies(e, p, lax.rem(p, MOE_WEIGHT_STAGES)):
            copy.start()

    def finish_piece(e, p, side):
        s = lax.rem(p, MOE_WEIGHT_STAGES)
        for copy in piece_copies(e, p, s):
            copy.wait()
        gu_rows = pl.ds(pl.multiple_of(p * rows_gu, rows_gu), rows_gu)
        d_rows = pl.ds(pl.multiple_of(p * rows_d, rows_d), rows_d)
        wg_buf[side, gu_rows, :] = stage_g[s].astype(BF16)
        wu_buf[side, gu_rows, :] = stage_u[s].astype(BF16)
        wd_buf[side, d_rows, :] = stage_d[s].astype(BF16)

    ahead = MOE_WEIGHT_STAGES - 1

    def start_first_pieces(e):
        for p in range(ahead):
            start_piece(e, p)

    def stream_pieces(e, lo, hi, side, following):
        @pl.when(hi > lo)
        def _():
            def body(p, carry):
                @pl.when(p + ahead < MOE_WEIGHT_PIECES)
                def _():
                    start_piece(e, p + ahead)
                finish_piece(e, p, side)
                return carry
            lax.fori_loop(lo, hi, body, 0)

            @pl.when((hi == MOE_WEIGHT_PIECES) & (following >= 0))
            def _():
                start_first_pieces(following)

    def chunk_rows(c):
        return pl.ds(pl.multiple_of(c * MOE_CHUNK, MOE_CHUNK), MOE_CHUNK)

    def gather_copy(tile, c, buf):
        g = tile * MOE_CHUNKS_PER_TILE + c
        src = hs_hbm.at[ctile_ref[g], pl.ds(pl.multiple_of(crow_ref[g], MOE_CHUNK), MOE_CHUNK), :]
        return pltpu.make_async_copy(src, xbuf.at[buf, chunk_rows(c), :], gather_sem.at[buf])

    def scatter_copy(tile, c, buf):
        g = tile * MOE_CHUNKS_PER_TILE + c
        dst = ys_hbm.at[ctile_ref[g], pl.ds(pl.multiple_of(crow_ref[g], MOE_CHUNK), MOE_CHUNK), :]
        return pltpu.make_async_copy(ybuf.at[buf, chunk_rows(c), :], dst, scatter_sem.at[buf])

    def for_real_chunks(tile, fn, also=True):
        n_real = treal_ref[tile]
        for c in range(MOE_CHUNKS_PER_TILE):
            pl.when((c < n_real) & also)(functools.partial(fn, c))

    def for_tail_chunks(fn):
        n_token_tiles, cap = ys_hbm.shape[0], ys_hbm.shape[1]

        def per_tile(t, carry):
            def per_chunk(c, carry):
                fn(pltpu.make_async_copy(zbuf, ys_hbm.at[t, chunk_rows(c), :], zero_sem.at[0]))
                return carry
            return lax.fori_loop(tused_ref[t], cap // MOE_CHUNK, per_chunk, carry)
        lax.fori_loop(0, n_token_tiles, per_tile, 0)

    @pl.when(i == 0)
    def _():
        zbuf[...] = jnp.zeros(zbuf.shape, zbuf.dtype)
        for_tail_chunks(lambda copy: copy.start())
        xbuf[...] = jnp.zeros(xbuf.shape, xbuf.dtype)
        for_real_chunks(0, lambda c: gather_copy(0, c, 0).start())
        start_first_pieces(texp_ref[0])
        stream_pieces(texp_ref[0], 0, MOE_WEIGHT_PIECES, wbuf_ref[0], wnext_ref[0])

    @pl.when(i + 1 < n_used)
    def _():
        for_real_chunks(i + 1, lambda c: gather_copy(i + 1, c, 1 - slot).start())

    side = wbuf_ref[i]
    stream_pieces(wnext_ref[i], plo_ref[i], phi_ref[i], 1 - side, wafter_ref[i])

    def tile_work(n_rows):
        for_real_chunks(i, lambda c: gather_copy(i, c, slot).wait())
        before = jnp.maximum(i - 2, 0)
        for_real_chunks(before, lambda c: scatter_copy(before, c, slot).wait(), also=i >= 2)
        d_model = wg_buf.shape[1]
        x = xbuf[slot, 0:n_rows, 0:d_model]
        w = jnp.sum(xbuf[slot, 0:n_rows, d_model:].astype(F32), axis=1, keepdims=True)
        a = jnp.dot(x, wg_buf[side], preferred_element_type=F32)
        u = jnp.dot(x, wu_buf[side], preferred_element_type=F32)
        hid = (jax.nn.silu(a) * u * w).astype(BF16)
        ybuf[slot, 0:n_rows, :] = jnp.dot(hid, wd_buf[side],
                                          preferred_element_type=F32).astype(BF16)
        for_real_chunks(i, lambda c: scatter_copy(i, c, slot).start())

    half = MOE_ROW_TILE // 2
    fits_half = treal_ref[i] * MOE_CHUNK <= half
    pl.when((i < n_used) & jnp.logical_not(fits_half))(functools.partial(tile_work, MOE_ROW_TILE))
    pl.when((i < n_used) & fits_half)(functools.partial(tile_work, half))

    @pl.when(i == n_used - 1)
    def _():
        @pl.when(i >= 1)
        def _():
            for_real_chunks(i - 1, lambda c: scatter_copy(i - 1, c, 1 - slot).wait())
        for_real_chunks(i, lambda c: scatter_copy(i, c, slot).wait())
        for_tail_chunks(lambda copy: copy.wait())


def _moe_ffn(hs, plan, wg, wu, wd):
    nT, cap, row_width = hs.shape
    E, D, F = wg.shape
    n_row_tiles = plan[0].shape[0]
    assert D % MOE_WEIGHT_PIECES == 0 and F % MOE_WEIGHT_PIECES == 0
    grid_spec = pltpu.PrefetchScalarGridSpec(
        num_scalar_prefetch=len(plan),
        grid=(n_row_tiles,),
        in_specs=[pl.BlockSpec(memory_space=pl.ANY)] * 4,
        out_specs=pl.BlockSpec(memory_space=pl.ANY),
        scratch_shapes=[
            pltpu.VMEM((2, MOE_ROW_TILE, row_width), BF16),
            pltpu.VMEM((2, MOE_ROW_TILE, D), BF16),
            pltpu.VMEM((MOE_CHUNK, D), BF16),
            pltpu.VMEM((2, D, F), BF16),
            pltpu.VMEM((2, D, F), BF16),
            pltpu.VMEM((2, F, D), BF16),
            pltpu.VMEM((MOE_WEIGHT_STAGES, D // MOE_WEIGHT_PIECES, F), F32),
            pltpu.VMEM((MOE_WEIGHT_STAGES, D // MOE_WEIGHT_PIECES, F), F32),
            pltpu.VMEM((MOE_WEIGHT_STAGES, F // MOE_WEIGHT_PIECES, D), F32),
            pltpu.SemaphoreType.DMA((2,)),
            pltpu.SemaphoreType.DMA((2,)),
            pltpu.SemaphoreType.DMA((MOE_WEIGHT_STAGES,)),
            pltpu.SemaphoreType.DMA((1,)),
        ],
    )
    return pl.pallas_call(
        _moe_ffn_kernel,
        grid_spec=grid_spec,
        out_shape=jax.ShapeDtypeStruct((nT, cap, D), BF16),
        compiler_params=_params(("arbitrary",)),
        name="moe_ffn",
    )(*plan, hs, wg, wu, wd)


def _moe_combine_kernel(ys_ref, route_ref, x1_ref, g_ref, o_ref):
    tm = x1_ref.shape[0]
    cap = ys_ref.shape[0]
    route = route_ref[...]
    slot = lax.broadcasted_iota(jnp.int32, (tm, cap), 1)
    pick = jnp.where(slot == route[:, 0:1].astype(jnp.int32), 1.0,
                     jnp.where(slot == route[:, 1:2].astype(jnp.int32), 1.0, 0.0)).astype(BF16)
    y = x1_ref[...] + jnp.dot(pick, ys_ref[...], preferred_element_type=F32)
    ms = jnp.mean(y * y, axis=-1, keepdims=True)
    o_ref[...] = y * lax.rsqrt(ms + RMS_EPS) * g_ref[...]


def _moe_combine(ys, route, x1, g, tm=MOE_TOKEN_TILE):
    S, D = x1.shape
    cap = ys.shape[1]
    row = lambda i: (i, 0)
    return pl.pallas_call(
        _moe_combine_kernel,
        grid=(S // tm,),
        in_specs=[
            pl.BlockSpec((None, cap, D), lambda i: (i, 0, 0)),
            pl.BlockSpec((tm, ROUTE_LANES), row),
            pl.BlockSpec((tm, D), row),
            pl.BlockSpec((1, D), lambda i: (0, 0)),
        ],
        out_specs=pl.BlockSpec((tm, D), row),
        out_shape=jax.ShapeDtypeStruct((S, D), F32),
        compiler_params=_params(("parallel",)),
        name="moe_combine",
    )(ys, route, x1, g)


def _router_weights(w_r_group, b_r_group, w_r_expert, b_r_expert):
    D = w_r_group.shape[0]
    w = jnp.concatenate(
        [w_r_group, jnp.transpose(w_r_expert, (1, 0, 2)).reshape(D, N_EXPERTS)], axis=1)
    b = jnp.concatenate([b_r_group, b_r_expert.reshape(N_EXPERTS)])
    pad = ROUTE_LANES - w.shape[1]
    w = jnp.pad(w, ((0, 0), (0, pad)))
    b = jnp.pad(b, (0, pad)).reshape(1, ROUTE_LANES)
    w_hi = w.astype(BF16)
    w_lo = (w - w_hi.astype(F32)).astype(BF16)
    return w_hi, w_lo, b


def kernel(x, norm_mix, w_in, w_pool, pool_scale, w_branch_pool, w_branch_attn, w_out, norm_ffn,
           w_r_group, b_r_group, w_r_expert, b_r_expert, w_gate, w_up, w_down, norm_final):
    B, S, D = x.shape
    depth = w_in.shape[0]
    assert depth == 1, "the final rms_norm is fused into the expert kernel of a single layer"
    slopes = jnp.exp2(-8.0 * jnp.arange(1, ATTN_HEADS + 1, dtype=F32) / ATTN_HEADS)
    slopes = jnp.broadcast_to(slopes[:, None, None], (ATTN_HEADS, 1, 128))
    outs = []
    for b in range(B):
        xb = x[b]
        for l in range(depth):
            proj = _inproj(_rms_norm(xb, norm_mix[l].reshape(1, D)), w_in[l])
            y_pool = _pool_mixer(proj, w_pool[l].astype(BF16), pool_scale[l].reshape(1, POOL_WIDTH))
            qT, vT, sel = _moba_gate(proj)
            y_attn = _moba_attention(proj, qT, vT, sel, slopes)
            wr_hi, wr_lo, br = _router_weights(w_r_group[l], b_r_group[l], w_r_expert[l], b_r_expert[l])
            x1 = _merge_out(y_pool, y_attn, proj, xb, w_branch_pool[l], w_branch_attn[l],
                            w_out[l])
            hs, route, cnt = _route_sort(x1, norm_ffn[l].reshape(1, D), wr_hi, wr_lo, br)
            nT = hs.shape[0]
            chunk_counts = cnt[:, 0, EXPERT_LANE0:EXPERT_LANE0 + N_EXPERTS].astype(jnp.int32)
            max_chunks = nT * ((2 * MOE_TOKEN_TILE + N_EXPERTS * (MOE_CHUNK - 1)) // MOE_CHUNK)
            n_row_tiles = -(-(max_chunks + N_EXPERTS * (MOE_CHUNKS_PER_TILE - 1))
                            // MOE_CHUNKS_PER_TILE)
            plan = _moe_plan(chunk_counts, n_row_tiles, n_row_tiles * MOE_CHUNKS_PER_TILE)
            ys = _moe_ffn(hs, plan, w_gate[l], w_up[l], w_down[l])
            xb = _moe_combine(ys, route, x1, norm_final.reshape(1, D))
        outs.append(xb)
    return jnp.stack(outs, axis=0)
```

```python
import functools

import jax
import jax.numpy as jnp
from jax import lax
from jax.experimental import pallas as pl
from jax.experimental.pallas import tpu as pltpu

F32 = jnp.float32
BF16 = jnp.bfloat16

POOL_WINDOWS = (2, 4, 8, 16)
MAX_WINDOW = 16
POOL_WIDTH = 1024
POOL_GROUP = 256
HEAD_DIM = 128
ATTN_HEADS = 8
ATTN_WIDTH = 1024
MOBA_BLOCK = 256
MOBA_TOPK = 3
N_GROUPS = 4
EXPERTS_PER_GROUP = 4
N_EXPERTS = 16
ROUTE_LANES = 128
EXPERT_LANE0 = N_GROUPS
RMS_EPS = 1e-6
NEG_INF = -1e30
LOG2_E = 1.4426950408889634
QUERY_SCALE = (HEAD_DIM ** -0.5) * LOG2_E
KEY_AUG = 2 * HEAD_DIM
ALIBI_TERMS = 3
VALUE_AUG = HEAD_DIM + 16
ATTN_KV_UNROLL = 3
ATTN_TILES_PER_STEP = 4
MOE_CHUNK = 16
MOE_TOKEN_TILE = 256
MERGE_WEIGHT_ROWS = 256
ROUTE_TILES_PER_STEP = 4
MOE_ROW_TILE = 256
MOE_CHUNKS_PER_TILE = MOE_ROW_TILE // MOE_CHUNK
MOE_WEIGHT_TERMS = 3
MOE_WEIGHT_PIECES = 8
MOE_WEIGHT_STAGES = 4
assert MOE_WEIGHT_STAGES - 1 <= MOE_WEIGHT_PIECES

V7X_VMEM_LIMIT_BYTES = 56 * 1024 * 1024


def _params(semantics, vmem=V7X_VMEM_LIMIT_BYTES, flags=None):
    return pltpu.CompilerParams(dimension_semantics=semantics, vmem_limit_bytes=vmem, flags=flags)


def _rms_norm_kernel(x_ref, g_ref, o_ref):
    x = x_ref[...]
    ms = jnp.mean(x * x, axis=-1, keepdims=True)
    o_ref[...] = (x * lax.rsqrt(ms + RMS_EPS) * g_ref[...]).astype(o_ref.dtype)


def _rms_norm(x, g, tm=1024):
    S, D = x.shape
    return pl.pallas_call(
        _rms_norm_kernel,
        grid=(S // tm,),
        in_specs=[pl.BlockSpec((tm, D), lambda i: (i, 0)), pl.BlockSpec((1, D), lambda i: (0, 0))],
        out_specs=pl.BlockSpec((tm, D), lambda i: (i, 0)),
        out_shape=jax.ShapeDtypeStruct((S, D), BF16),
        compiler_params=_params(("parallel",)),
        name="rms_norm",
    )(x, g)


def _inproj_kernel(h_ref, w_ref, o_ref, wb_ref):
    @pl.when(pl.program_id(1) == 0)
    def _():
        wb_ref[...] = w_ref[...].astype(BF16)

    col0 = pl.program_id(0) * o_ref.shape[1]
    is_q = (col0 >= POOL_WIDTH) & (col0 < POOL_WIDTH + ATTN_WIDTH)
    factor = jnp.where(is_q, QUERY_SCALE, 1.0).astype(F32)
    o_ref[...] = (jnp.dot(h_ref[...], wb_ref[...], preferred_element_type=F32)
                  * factor).astype(o_ref.dtype)


def _inproj(h, w, tm=2048, tn=1024):
    S, D = h.shape
    N = w.shape[1]
    tm = min(tm, S)
    assert POOL_WIDTH % tn == 0 and ATTN_WIDTH % tn == 0
    return pl.pallas_call(
        _inproj_kernel,
        grid=(N // tn, S // tm),
        in_specs=[
            pl.BlockSpec((tm, D), lambda j, i: (i, 0)),
            pl.BlockSpec((D, tn), lambda j, i: (0, j)),
        ],
        out_specs=pl.BlockSpec((tm, tn), lambda j, i: (i, j)),
        out_shape=jax.ShapeDtypeStruct((S, N), BF16),
        scratch_shapes=[pltpu.VMEM((D, tn), BF16)],
        compiler_params=_params(("parallel", "arbitrary")),
        name="inproj",
    )(h, w)


def _pool_kernel(cur_ref, prev_ref, w_ref, scale_ref, o_ref):
    i = pl.program_id(0)
    tm = cur_ref.shape[0]
    u = cur_ref[...].astype(F32)
    halo = jnp.where(i > 0, prev_ref[...].astype(F32), 0.0)
    ext = jnp.concatenate([halo, u], axis=0)
    t = i * tm + lax.broadcasted_iota(jnp.int32, (tm, 1), 0)
    for g, w in enumerate(POOL_WINDOWS):
        cols = slice(g * POOL_GROUP, (g + 1) * POOL_GROUP)
        ug = u[:, cols]
        run = ext[:, cols]
        s = 1
        while s < w:
            run = run + pltpu.roll(run, s, axis=0)
            s *= 2
        wsum = run[MAX_WINDOW:, :]
        cnt = jnp.minimum(t + 1, w).astype(F32)
        mixed = (wsum / cnt - ug).astype(BF16)
        y = jnp.dot(mixed, w_ref[g], preferred_element_type=F32)
        o_ref[:, cols] = (y * scale_ref[:, cols]).astype(o_ref.dtype)


def _pool_mixer(proj, w_pool, pool_scale, tm=512):
    S = proj.shape[0]
    halo_blocks = tm // MAX_WINDOW
    return pl.pallas_call(
        _pool_kernel,
        grid=(S // tm,),
        in_specs=[
            pl.BlockSpec((tm, POOL_WIDTH), lambda i: (i, 0)),
            pl.BlockSpec((MAX_WINDOW, POOL_WIDTH),
                         lambda i: (jnp.maximum(i * halo_blocks - 1, 0), 0)),
            pl.BlockSpec((len(POOL_WINDOWS), POOL_GROUP, POOL_GROUP), lambda i: (0, 0, 0)),
            pl.BlockSpec((1, POOL_WIDTH), lambda i: (0, 0)),
        ],
        out_specs=pl.BlockSpec((tm, POOL_WIDTH), lambda i: (i, 0)),
        out_shape=jax.ShapeDtypeStruct((S, POOL_WIDTH), BF16),
        compiler_params=_params(("parallel",)),
        name="pool_mixer",
    )(proj, proj, w_pool, pool_scale)


def _alibi_query_rows(slope_ref):
    slope2 = slope_ref[...][:, 0:1] * LOG2_E
    q_extra = jnp.zeros((KEY_AUG - HEAD_DIM, MOBA_BLOCK), F32)
    row = lax.broadcasted_iota(jnp.int32, q_extra.shape, 0)
    rest = slope2
    for n in range(ALIBI_TERMS):
        piece = rest.astype(BF16).astype(F32)
        q_extra = jnp.where(row == n, piece, q_extra)
        rest = rest - piece
    return q_extra.astype(BF16)


def _moba_gate_kernel(q_ref, k_ref, v_ref, qT_ref, vT_ref, sel_ref):
    S = q_ref.shape[0]
    nb = S // MOBA_BLOCK
    topk = min(MOBA_TOPK, nb)
    kf = k_ref[...].astype(F32).reshape(nb, MOBA_BLOCK, HEAD_DIM)
    kmean = jnp.sum(kf, axis=1) * (1.0 / MOBA_BLOCK)
    km_hi = kmean.astype(BF16)
    km_lo = (kmean - km_hi.astype(F32)).astype(BF16)
    blk = lax.broadcasted_iota(jnp.int32, (nb, MOBA_BLOCK), 0)
    v_row = lax.broadcasted_iota(jnp.int32, (VALUE_AUG - HEAD_DIM, MOBA_BLOCK), 0)
    v_extra = jnp.where(v_row == 0, 1.0, 0.0).astype(BF16)

    def body(i, carry):
        rows = pl.ds(pl.multiple_of(i * MOBA_BLOCK, MOBA_BLOCK), MOBA_BLOCK)
        qT = q_ref[rows, :].astype(F32).T.astype(BF16)
        qT_ref[i] = qT
        vT_ref[i, 0:HEAD_DIM, :] = v_ref[rows, :].astype(F32).T.astype(BF16)
        vT_ref[i, HEAD_DIM:VALUE_AUG, :] = v_extra
        gate = (jnp.dot(km_hi, qT, preferred_element_type=F32)
                + jnp.dot(km_lo, qT, preferred_element_type=F32)) * (1.0 / QUERY_SCALE)
        gate = jnp.where(blk < i, gate, NEG_INF)
        sel = jnp.zeros((nb, MOBA_BLOCK), F32)
        for _ in range(topk):
            best = jnp.max(gate, axis=0, keepdims=True)
            idx = jnp.min(jnp.where(gate == best, blk, nb), axis=0, keepdims=True)
            hit = blk == idx
            sel = jnp.where(hit & (blk < i), 1.0, sel)
            gate = jnp.where(hit, -jnp.inf, gate)
        sel_ref[i] = sel
        return carry

    lax.fori_loop(0, nb, body, 0, unroll=4 if nb % 4 == 0 else 1)


def _moba_gate(proj):
    S = proj.shape[0]
    nb = S // MOBA_BLOCK
    H = ATTN_HEADS
    q0 = POOL_WIDTH // HEAD_DIM
    k0 = q0 + H
    v0 = k0 + H
    blocked = lambda h: (h, 0, 0, 0)
    return pl.pallas_call(
        _moba_gate_kernel,
        grid=(H,),
        in_specs=[
            pl.BlockSpec((S, HEAD_DIM), lambda h: (0, q0 + h)),
            pl.BlockSpec((S, HEAD_DIM), lambda h: (0, k0 + h)),
            pl.BlockSpec((S, HEAD_DIM), lambda h: (0, v0 + h)),
        ],
        out_specs=[
            pl.BlockSpec((None, nb, HEAD_DIM, MOBA_BLOCK), blocked),
            pl.BlockSpec((None, nb, VALUE_AUG, MOBA_BLOCK), blocked),
            pl.BlockSpec((None, nb, nb, MOBA_BLOCK), blocked),
        ],
        out_shape=[
            jax.ShapeDtypeStruct((H, nb, HEAD_DIM, MOBA_BLOCK), BF16),
            jax.ShapeDtypeStruct((H, nb, VALUE_AUG, MOBA_BLOCK), BF16),
            jax.ShapeDtypeStruct((H, nb, nb, MOBA_BLOCK), F32),
        ],
        compiler_params=_params(("parallel",)),
        name="moba_gate",
    )(proj, proj, proj)


def _moba_attn_kernel(slope_ref, qT_ref, qT_next_ref, sel_ref, k_ref, vT_ref, o_ref,
                      ka_ref, *scratch):
    first = pl.program_id(1) * ATTN_TILES_PER_STEP
    nb = sel_ref.shape[1]

    @pl.when(first == 0)
    def _():
        pos = lax.broadcasted_iota(jnp.int32, (MOBA_BLOCK, KEY_AUG - HEAD_DIM), 0).astype(F32)
        col = lax.broadcasted_iota(jnp.int32, (MOBA_BLOCK, KEY_AUG - HEAD_DIM), 1)
        k_extra = jnp.where(col < ALIBI_TERMS, pos, 0.0).astype(BF16)

        def per_block(j, carry):
            rows = pl.ds(pl.multiple_of(j * MOBA_BLOCK, MOBA_BLOCK), MOBA_BLOCK)
            ka_ref[rows, 0:HEAD_DIM] = k_ref[rows, :]
            ka_ref[rows, HEAD_DIM:KEY_AUG] = k_extra
            return carry
        lax.fori_loop(0, nb, per_block, 0)

    q_extra = _alibi_query_rows(slope_ref)
    augment = lambda q: jnp.concatenate([q, q_extra], axis=0)
    for t in range(ATTN_TILES_PER_STEP):
        last = t + 1 == ATTN_TILES_PER_STEP
        _moba_attn_tile(first + t, t == 0, slope_ref, augment(qT_ref[t]),
                        augment(qT_next_ref[...] if last else qT_ref[t + 1]), sel_ref.at[t],
                        ka_ref, vT_ref, o_ref.at[t * MOBA_BLOCK:(t + 1) * MOBA_BLOCK, :], *scratch)


def _moba_attn_tile(i, may_start_head, slope_ref, qT, q_next, sel_ref, k_ref, vT_ref, o_ref,
                    s0_ref, s1_ref, p0_ref, p1_ref, s_own_ref):
    nb = sel_ref.shape[0]
    slope2 = slope_ref[...][:, 0:1] * LOG2_E

    def block_of(t, u):
        return jnp.clip(t * ATTN_KV_UNROLL + u, 0, nb - 1)

    def key_rows(j):
        return pl.ds(pl.multiple_of(j * MOBA_BLOCK, MOBA_BLOCK), MOBA_BLOCK)

    def issue_scores(t, s_ref, queries=qT):
        for u in range(ATTN_KV_UNROLL):
            s_ref[u] = jnp.dot(k_ref[key_rows(block_of(t, u)), :], queries,
                               preferred_element_type=F32)

    def apply_probs(t, p_ref, alpha, acc):
        acc = alpha * acc
        for u in range(ATTN_KV_UNROLL):
            acc = acc + jnp.dot(vT_ref[block_of(t, u)], p_ref[u], preferred_element_type=F32)
        return acc

    def softmax_group(t, s_ref, p_ref, m):
        m_new = m
        shifts = []
        for u in range(ATTN_KV_UNROLL):
            j = t * ATTN_KV_UNROLL + u
            valid = jnp.where(j < i, sel_ref[pl.ds(block_of(t, u), 1), :], 0.0) > 0.0
            gap = slope2 * ((i - j) * MOBA_BLOCK).astype(F32)
            top = jnp.max(s_ref[u], axis=0, keepdims=True) - gap
            m_new = jnp.maximum(m_new, jnp.where(valid, top, NEG_INF))
            shifts.append((valid, gap))
        alpha = jnp.exp2(m - m_new)
        for u, (valid, gap) in enumerate(shifts):
            p = jnp.exp2(s_ref[u] - jnp.where(valid, m_new + gap, jnp.inf))
            p_ref[u] = p.astype(BF16)
        return m_new, alpha

    def body(r, carry):
        m, acc, alpha = carry
        acc = apply_probs(2 * r - 1, p1_ref, alpha, acc)
        m, alpha = softmax_group(2 * r, s0_ref, p0_ref, m)
        issue_scores(2 * r + 1, s1_ref)
        acc = apply_probs(2 * r, p0_ref, alpha, acc)
        m, alpha = softmax_group(2 * r + 1, s1_ref, p1_ref, m)
        issue_scores(2 * r + 2, s0_ref)
        return m, acc, alpha

    n_groups = (i + ATTN_KV_UNROLL - 1) // ATTN_KV_UNROLL
    n_pairs = n_groups // 2
    if may_start_head:
        @pl.when(i == 0)
        def _():
            s0_ref[...] = jnp.zeros(s0_ref.shape, F32)
            s_own_ref[...] = jnp.dot(k_ref[key_rows(0), :], qT, preferred_element_type=F32)

    p1_ref[...] = jnp.zeros(p1_ref.shape, BF16)
    kpos = lax.broadcasted_iota(jnp.int32, (MOBA_BLOCK, MOBA_BLOCK), 0)
    qpos = lax.broadcasted_iota(jnp.int32, (MOBA_BLOCK, MOBA_BLOCK), 1)
    s_own = jnp.where(qpos >= kpos, s_own_ref[...], NEG_INF)
    top_own = jnp.max(s_own, axis=0, keepdims=True)
    row = jnp.zeros((1, MOBA_BLOCK), F32)
    init = (row + NEG_INF, jnp.zeros((VALUE_AUG, MOBA_BLOCK), F32), row + 1.0)
    m, acc, alpha = lax.fori_loop(0, n_pairs, body, init)

    def finish(has_last_group):
        m_, acc_ = m, apply_probs(2 * n_pairs - 1, p1_ref, alpha, acc)
        if has_last_group:
            m_, alpha_ = softmax_group(2 * n_pairs, s0_ref, p0_ref, m_)
            acc_ = apply_probs(2 * n_pairs, p0_ref, alpha_, acc_)
        m_new = jnp.maximum(m_, top_own)
        p = jnp.exp2(s_own - m_new)
        acc_ = (jnp.exp2(m_ - m_new) * acc_
                + jnp.dot(vT_ref[i], p.astype(BF16), preferred_element_type=F32))
        out = acc_[0:HEAD_DIM] / acc_[HEAD_DIM:HEAD_DIM + 1]
        o_ref[...] = out.T.astype(o_ref.dtype)
        issue_scores(0, s0_ref, q_next)
        s_own_ref[...] = jnp.dot(k_ref[key_rows(jnp.minimum(i + 1, nb - 1)), :], q_next,
                                 preferred_element_type=F32)

    pl.when(n_groups > 2 * n_pairs)(functools.partial(finish, True))
    pl.when(n_groups == 2 * n_pairs)(functools.partial(finish, False))


def _moba_attention(proj, qT, vT, sel, slopes):
    S = proj.shape[0]
    H = ATTN_HEADS
    nb = S // MOBA_BLOCK
    k0 = POOL_WIDTH // HEAD_DIM + H
    per_step = ATTN_TILES_PER_STEP
    assert nb % per_step == 0
    return pl.pallas_call(
        _moba_attn_kernel,
        grid=(H, nb // per_step),
        in_specs=[
            pl.BlockSpec((None, 1, 128), lambda h, g: (h, 0, 0)),
            pl.BlockSpec((None, per_step, HEAD_DIM, MOBA_BLOCK), lambda h, g: (h, g, 0, 0)),
            pl.BlockSpec((None, None, HEAD_DIM, MOBA_BLOCK),
                         lambda h, g: (h, jnp.minimum((g + 1) * per_step, nb - 1), 0, 0)),
            pl.BlockSpec((None, per_step, nb, MOBA_BLOCK), lambda h, g: (h, g, 0, 0)),
            pl.BlockSpec((S, HEAD_DIM), lambda h, g: (0, k0 + h)),
            pl.BlockSpec((None, nb, VALUE_AUG, MOBA_BLOCK), lambda h, g: (h, 0, 0, 0)),
        ],
        out_specs=pl.BlockSpec((per_step * MOBA_BLOCK, HEAD_DIM), lambda h, g: (g, h)),
        out_shape=jax.ShapeDtypeStruct((S, ATTN_WIDTH), BF16),
        scratch_shapes=[
            pltpu.VMEM((S, KEY_AUG), BF16),
            pltpu.VMEM((ATTN_KV_UNROLL, MOBA_BLOCK, MOBA_BLOCK), F32),
            pltpu.VMEM((ATTN_KV_UNROLL, MOBA_BLOCK, MOBA_BLOCK), F32),
            pltpu.VMEM((ATTN_KV_UNROLL, MOBA_BLOCK, MOBA_BLOCK), BF16),
            pltpu.VMEM((ATTN_KV_UNROLL, MOBA_BLOCK, MOBA_BLOCK), BF16),
            pltpu.VMEM((MOBA_BLOCK, MOBA_BLOCK), F32),
        ],
        compiler_params=_params(("arbitrary", "arbitrary")),
        name="moba_attention",
    )(slopes, qT, qT, sel, proj, vT)


def _first_lane_of_max(vals, lane):
    best = jnp.max(vals, axis=1, keepdims=True)
    idx = jnp.min(jnp.where(vals == best, lane, ROUTE_LANES), axis=1, keepdims=True)
    return best, idx


def _merge_out_kernel(yp_ref, ya_ref, glp_ref, gla_ref, x_ref, wbp_hbm, wba_hbm, wout_hbm, x1_ref,
                      wbp_ref, wba_ref, wout_ref, stage_ref, stage_sem):
    @pl.when(pl.program_id(0) == 0)
    def _():
        rows = stage_ref.shape[1]
        pieces = [(src, dst, r) for src, dst in ((wbp_hbm, wbp_ref), (wba_hbm, wba_ref),
                                                 (wout_hbm, wout_ref))
                  for r in range(0, src.shape[0], rows)]
        copies = [pltpu.make_async_copy(src.at[r:r + rows, :], stage_ref.at[n % 2],
                                        stage_sem.at[n % 2])
                  for n, (src, _, r) in enumerate(pieces)]
        copies[0].start()
        for n, (_, dst, r) in enumerate(pieces):
            if n + 1 < len(pieces):
                copies[n + 1].start()
            copies[n].wait()
            dst[r:r + rows, :] = stage_ref[n % 2].astype(BF16)

    bp = jnp.dot(yp_ref[...], wbp_ref[...], preferred_element_type=F32)
    ba = jnp.dot(ya_ref[...], wba_ref[...], preferred_element_type=F32)
    merged = (jax.nn.sigmoid(glp_ref[...].astype(F32)) * bp
              + jax.nn.sigmoid(gla_ref[...].astype(F32)) * ba)
    x1_ref[...] = x_ref[...] + jnp.dot(merged.astype(BF16), wout_ref[...],
                                       preferred_element_type=F32)


def _merge_out(yp, ya, proj, x, wbp, wba, wout, tm=512):
    S, D = x.shape
    row = lambda i: (i, 0)
    glp_blk = (POOL_WIDTH + 3 * ATTN_WIDTH) // D
    hbm = pl.BlockSpec(memory_space=pl.ANY)
    return pl.pallas_call(
        _merge_out_kernel,
        grid=(S // tm,),
        in_specs=[
            pl.BlockSpec((tm, POOL_WIDTH), row),
            pl.BlockSpec((tm, ATTN_WIDTH), row),
            pl.BlockSpec((tm, D), lambda i: (i, glp_blk)),
            pl.BlockSpec((tm, D), lambda i: (i, glp_blk + 1)),
            pl.BlockSpec((tm, D), row),
            hbm, hbm, hbm,
        ],
        out_specs=pl.BlockSpec((tm, D), row),
        out_shape=jax.ShapeDtypeStruct((S, D), F32),
        scratch_shapes=[
            pltpu.VMEM((POOL_WIDTH, D), BF16),
            pltpu.VMEM((ATTN_WIDTH, D), BF16),
            pltpu.VMEM((D, D), BF16),
            pltpu.VMEM((2, MERGE_WEIGHT_ROWS, D), F32),
            pltpu.SemaphoreType.DMA((2,)),
        ],
        compiler_params=_params(("arbitrary",)),
        name="merge_out",
    )(yp, ya, proj, proj, x, wbp, wba, wout)


def _route_sort_kernel(x1_ref, g_ref, wr_hi_ref, wr_lo_ref, br_ref, hs_ref, route_ref, cnt_ref):
    n_tiles = ROUTE_TILES_PER_STEP
    tm = x1_ref.shape[0] // n_tiles
    cap = hs_ref.shape[1]
    d_model = x1_ref.shape[1]
    lane = lax.broadcasted_iota(jnp.int32, (tm, ROUTE_LANES), 1)
    tiles = [{"index": t, "rows": slice(t * tm, (t + 1) * tm)} for t in range(n_tiles)]

    def normalise(s):
        x1 = x1_ref[s["rows"], :]
        ms = jnp.mean(x1 * x1, axis=-1, keepdims=True)
        h2 = x1 * lax.rsqrt(ms + RMS_EPS) * g_ref[...]
        s["h2_hi"] = h2.astype(BF16)
        s["h2_lo"] = (h2 - s["h2_hi"].astype(F32)).astype(BF16)

    def router_logits(s):
        s["logits"] = (jnp.dot(s["h2_hi"], wr_hi_ref[...], preferred_element_type=F32)
                       + jnp.dot(s["h2_lo"], wr_hi_ref[...], preferred_element_type=F32)
                       + jnp.dot(s["h2_hi"], wr_lo_ref[...], preferred_element_type=F32)
                       + br_ref[...])

    def choose_experts(s):
        logits = s["logits"]
        g_logits = jnp.where(lane < N_GROUPS, logits, -jnp.inf)
        g_best, g_idx = _first_lane_of_max(g_logits, lane)
        g_w = 1.0 / jnp.sum(jnp.exp(g_logits - g_best), axis=1, keepdims=True)
        e_lo = EXPERT_LANE0 + EXPERTS_PER_GROUP * g_idx
        e_logits = jnp.where((lane >= e_lo) & (lane < e_lo + EXPERTS_PER_GROUP), logits, -jnp.inf)
        v1, i1 = _first_lane_of_max(e_logits, lane)
        v2, i2 = _first_lane_of_max(jnp.where(lane == i1, -jnp.inf, e_logits), lane)
        e21 = jnp.exp(v2 - v1)
        s["w1"] = g_w / (1.0 + e21)
        s["w2"] = g_w * e21 / (1.0 + e21)
        s["hit1"] = lane == i1
        s["hit2"] = lane == i2

    def sort_by_expert(s):
        hit1, hit2 = s["hit1"], s["hit2"]
        member = jnp.where(hit1, 1.0, jnp.where(hit2, 1.0, 0.0))
        r_tok = lax.broadcasted_iota(jnp.int32, (tm, tm), 0)
        c_tok = lax.broadcasted_iota(jnp.int32, (tm, tm), 1)
        earlier = jnp.where(c_tok < r_tok, 1.0, 0.0).astype(BF16)
        rank = jnp.dot(earlier, member.astype(BF16), preferred_element_type=F32)
        count = jnp.sum(member, axis=0, keepdims=True)
        chunks = jnp.floor((count + (MOE_CHUNK - 1)) * (1.0 / MOE_CHUNK))
        r_l = lax.broadcasted_iota(jnp.int32, (ROUTE_LANES, ROUTE_LANES), 0)
        c_l = lax.broadcasted_iota(jnp.int32, (ROUTE_LANES, ROUTE_LANES), 1)
        lower_lanes = jnp.where(r_l < c_l, 1.0, 0.0).astype(BF16)
        start = jnp.dot(jnp.broadcast_to(chunks, (8, ROUTE_LANES)).astype(BF16), lower_lanes,
                        preferred_element_type=F32)[0:1] * MOE_CHUNK
        pos = start + rank
        pos1 = jnp.sum(jnp.where(hit1, pos, 0.0), axis=1, keepdims=True)
        pos2 = jnp.sum(jnp.where(hit2, pos, 0.0), axis=1, keepdims=True)
        s["route"] = jnp.where(lane == 0, pos1, jnp.where(lane == 1, pos2, 0.0))
        route_ref[s["rows"], :] = s["route"]
        cnt_ref[s["index"]] = chunks

    def weight_pieces(w):
        pieces = jnp.zeros(lane.shape, F32)
        rest = w
        for n in range(MOE_WEIGHT_TERMS):
            piece = rest.astype(BF16).astype(F32)
            pieces = jnp.where(lane == n, piece, pieces)
            rest = rest - piece
        return pieces.astype(BF16)

    def compact(s):
        route_t = s["route"].T
        slot = lax.broadcasted_iota(jnp.int32, (cap, tm), 0)
        first = jnp.where(slot == route_t[0:1, :].astype(jnp.int32), 1.0, 0.0).astype(BF16)
        second = jnp.where(slot == route_t[1:2, :].astype(jnp.int32), 1.0, 0.0).astype(BF16)
        hs_ref[s["index"], :, 0:d_model] = jnp.dot(
            first + second, s["h2_hi"], preferred_element_type=F32).astype(BF16)
        slot_w = (jnp.dot(first, weight_pieces(s["w1"]), preferred_element_type=F32)
                  + jnp.dot(second, weight_pieces(s["w2"]), preferred_element_type=F32))
        hs_ref[s["index"], :, d_model:] = slot_w.astype(BF16)

    for stage in (normalise, router_logits, choose_experts, sort_by_expert, compact):
        for s in tiles:
            stage(s)


def _moe_cap(tm):
    worst = 2 * tm + N_EXPERTS * (MOE_CHUNK - 1)
    return -(-worst // 128) * 128


def _route_sort(x1, g, wr_hi, wr_lo, br, tm=MOE_TOKEN_TILE):
    S, D = x1.shape
    nT = S // tm
    cap = _moe_cap(tm)
    full = lambda i: (0, 0)
    row = lambda i: (i, 0)
    per_step = ROUTE_TILES_PER_STEP
    return pl.pallas_call(
        _route_sort_kernel,
        grid=(nT // per_step,),
        in_specs=[
            pl.BlockSpec((per_step * tm, D), row),
            pl.BlockSpec((1, D), full),
            pl.BlockSpec((D, ROUTE_LANES), full),
            pl.BlockSpec((D, ROUTE_LANES), full),
            pl.BlockSpec((1, ROUTE_LANES), full),
        ],
        out_specs=[
            pl.BlockSpec((per_step, cap, D + ROUTE_LANES), lambda i: (i, 0, 0)),
            pl.BlockSpec((per_step * tm, ROUTE_LANES), row),
            pl.BlockSpec((per_step, 1, ROUTE_LANES), lambda i: (i, 0, 0)),
        ],
        out_shape=[
            jax.ShapeDtypeStruct((nT, cap, D + ROUTE_LANES), BF16),
            jax.ShapeDtypeStruct((S, ROUTE_LANES), F32),
            jax.ShapeDtypeStruct((nT, 1, ROUTE_LANES), F32),
        ],
        compiler_params=_params(("parallel",)),
        name="route_sort",
    )(x1, g, wr_hi, wr_lo, br)


def _moe_plan(chunk_counts, n_row_tiles, n_chunk_slots):
    nT, E = chunk_counts.shape
    per_expert = chunk_counts.T
    seg_start = (jnp.cumsum(chunk_counts, axis=1) - chunk_counts).T
    seg_end = jnp.cumsum(per_expert, axis=1)
    n_chunks = seg_end[:, -1]
    padded = -(-n_chunks // MOE_CHUNKS_PER_TILE) * MOE_CHUNKS_PER_TILE
    e_end = jnp.cumsum(padded)
    n_used = e_end[-1] // MOE_CHUNKS_PER_TILE
    c = jnp.arange(n_chunk_slots, dtype=jnp.int32)
    e_of_c = jnp.minimum((e_end[None, :] <= c[:, None]).sum(axis=1), E - 1)
    is_e = (e_of_c[:, None] == jnp.arange(E)[None, :]).astype(jnp.int32)
    local = c - (is_e * (e_end - padded)[None, :]).sum(axis=1)
    real = local < (is_e * n_chunks[None, :]).sum(axis=1)
    pick_e = lambda table: (is_e[:, :, None] * table[None, :, :]).sum(axis=1)
    seg_end_c = pick_e(seg_end)
    t_of_c = jnp.minimum((seg_end_c <= local[:, None]).sum(axis=1), nT - 1)
    is_t = (t_of_c[:, None] == jnp.arange(nT)[None, :]).astype(jnp.int32)
    pick_t = lambda rows: (rows * is_t).sum(axis=1)
    within = local - (pick_t(seg_end_c) - pick_t(pick_e(per_expert)))
    src_tile = jnp.where(real, t_of_c, 0).astype(jnp.int32)
    src_row = jnp.where(real, (pick_t(pick_e(seg_start)) + within) * MOE_CHUNK, 0).astype(jnp.int32)
    tile = jnp.arange(n_row_tiles, dtype=jnp.int32)
    first = jnp.minimum(tile, n_used - 1) * MOE_CHUNKS_PER_TILE
    tile_expert = jnp.minimum((e_end[None, :] <= first[:, None]).sum(axis=1), E - 1)
    tile_real = real.reshape(n_row_tiles, MOE_CHUNKS_PER_TILE).sum(axis=1)
    experts = jnp.arange(E)
    nonempty = padded > 0
    buffer_of_e = jnp.cumsum(nonempty) - nonempty
    later = (experts[None, :] > experts[:, None]) & nonempty[None, :]
    next_of_e = jnp.min(jnp.where(later, experts[None, :], E), axis=1)
    next_of_e = jnp.where(next_of_e < E, next_of_e, -1)
    is_next = (next_of_e[:, None] == experts[None, :]).astype(jnp.int32)
    after_of_e = jnp.where(next_of_e >= 0, (is_next * next_of_e[None, :]).sum(axis=1), -1)
    is_te = (tile_expert[:, None] == experts[None, :]).astype(jnp.int32)
    at = lambda table: (is_te * table[None, :]).sum(axis=1)
    run_len = jnp.maximum(at(padded) // MOE_CHUNKS_PER_TILE, 1)
    run_pos = tile - at(e_end - padded) // MOE_CHUNKS_PER_TILE
    next_expert = at(next_of_e)
    streams = (tile < n_used) & (next_expert >= 0)
    piece_lo = jnp.where(streams, (MOE_WEIGHT_PIECES * run_pos) // run_len, 0)
    piece_hi = jnp.where(streams, (MOE_WEIGHT_PIECES * (run_pos + 1)) // run_len, 0)
    as_i32 = lambda a: a.astype(jnp.int32)
    return (as_i32(tile_expert), as_i32(tile_real), src_tile, src_row, as_i32(n_used.reshape(1)),
            as_i32(at(buffer_of_e) % 2), as_i32(next_expert), as_i32(at(after_of_e)),
            as_i32(piece_lo), as_i32(piece_hi), as_i32(chunk_counts.sum(axis=1)))


def _moe_ffn_kernel(texp_ref, treal_ref, ctile_ref, crow_ref, nused_ref,
                    wbuf_ref, wnext_ref, wafter_ref, plo_ref, phi_ref, tused_ref,
                    hs_hbm, wg_hbm, wu_hbm, wd_hbm, ys_hbm,
                    xbuf, ybuf, zbuf, wg_buf, wu_buf, wd_buf, stage_g, stage_u, stage_d,
                    gather_sem, scatter_sem, weight_sem, zero_sem):
    i = pl.program_id(0)
    n_used = nused_ref[0]
    slot = lax.rem(i, 2)
    rows_gu = stage_g.shape[1]
    rows_d = stage_d.shape[1]

    def piece_copies(e, p, s):
        gu_rows = pl.ds(pl.multiple_of(p * rows_gu, rows_gu), rows_gu)
        d_rows = pl.ds(pl.multiple_of(p * rows_d, rows_d), rows_d)
        return (pltpu.make_async_copy(wg_hbm.at[e, gu_rows, :], stage_g.at[s], weight_sem.at[s]),
                pltpu.make_async_copy(wu_hbm.at[e, gu_rows, :], stage_u.at[s], weight_sem.at[s]),
                pltpu.make_async_copy(wd_hbm.at[e, d_rows, :], stage_d.at[s], weight_sem.at[s]))

    def start_piece(e, p):
        for copy in piece_copies(e, p, lax.rem(p, MOE_WEIGHT_STAGES)):
            copy.start()

    def finish_piece(e, p, side):
        s = lax.rem(p, MOE_WEIGHT_STAGES)
        for copy in piece_copies(e, p, s):
            copy.wait()
        gu_rows = pl.ds(pl.multiple_of(p * rows_gu, rows_gu), rows_gu)
        d_rows = pl.ds(pl.multiple_of(p * rows_d, rows_d), rows_d)
        wg_buf[side, gu_rows, :] = stage_g[s].astype(BF16)
        wu_buf[side, gu_rows, :] = stage_u[s].astype(BF16)
        wd_buf[side, d_rows, :] = stage_d[s].astype(BF16)

    ahead = MOE_WEIGHT_STAGES - 1

    def start_first_pieces(e):
        for p in range(ahead):
            start_piece(e, p)

    def stream_pieces(e, lo, hi, side, following):
        @pl.when(hi > lo)
        def _():
            def body(p, carry):
                @pl.when(p + ahead < MOE_WEIGHT_PIECES)
                def _():
                    start_piece(e, p + ahead)
                finish_piece(e, p, side)
                return carry
            lax.fori_loop(lo, hi, body, 0)

            @pl.when((hi == MOE_WEIGHT_PIECES) & (following >= 0))
            def _():
                start_first_pieces(following)

    def chunk_rows(c):
        return pl.ds(pl.multiple_of(c * MOE_CHUNK, MOE_CHUNK), MOE_CHUNK)

    def gather_copy(tile, c, buf):
        g = tile * MOE_CHUNKS_PER_TILE + c
        src = hs_hbm.at[ctile_ref[g], pl.ds(pl.multiple_of(crow_ref[g], MOE_CHUNK), MOE_CHUNK), :]
        return pltpu.make_async_copy(src, xbuf.at[buf, chunk_rows(c), :], gather_sem.at[buf])

    def scatter_copy(tile, c, buf):
        g = tile * MOE_CHUNKS_PER_TILE + c
        dst = ys_hbm.at[ctile_ref[g], pl.ds(pl.multiple_of(crow_ref[g], MOE_CHUNK), MOE_CHUNK), :]
        return pltpu.make_async_copy(ybuf.at[buf, chunk_rows(c), :], dst, scatter_sem.at[buf])

    def for_real_chunks(tile, fn, also=True):
        n_real = treal_ref[tile]
        for c in range(MOE_CHUNKS_PER_TILE):
            pl.when((c < n_real) & also)(functools.partial(fn, c))

    def for_tail_chunks(fn):
        n_token_tiles, cap = ys_hbm.shape[0], ys_hbm.shape[1]

        def per_tile(t, carry):
            def per_chunk(c, carry):
                fn(pltpu.make_async_copy(zbuf, ys_hbm.at[t, chunk_rows(c), :], zero_sem.at[0]))
                return carry
            return lax.fori_loop(tused_ref[t], cap // MOE_CHUNK, per_chunk, carry)
        lax.fori_loop(0, n_token_tiles, per_tile, 0)

    @pl.when(i == 0)
    def _():
        zbuf[...] = jnp.zeros(zbuf.shape, zbuf.dtype)
        for_tail_chunks(lambda copy: copy.start())
        xbuf[...] = jnp.zeros(xbuf.shape, xbuf.dtype)
        for_real_chunks(0, lambda c: gather_copy(0, c, 0).start())
        start_first_pieces(texp_ref[0])
        stream_pieces(texp_ref[0], 0, MOE_WEIGHT_PIECES, wbuf_ref[0], wnext_ref[0])

    @pl.when(i + 1 < n_used)
    def _():
        for_real_chunks(i + 1, lambda c: gather_copy(i + 1, c, 1 - slot).start())

    side = wbuf_ref[i]
    stream_pieces(wnext_ref[i], plo_ref[i], phi_ref[i], 1 - side, wafter_ref[i])

    def tile_work(n_rows):
        for_real_chunks(i, lambda c: gather_copy(i, c, slot).wait())
        before = jnp.maximum(i - 2, 0)
        for_real_chunks(before, lambda c: scatter_copy(before, c, slot).wait(), also=i >= 2)
        d_model = wg_buf.shape[1]
        x = xbuf[slot, 0:n_rows, 0:d_model]
        w = jnp.sum(xbuf[slot, 0:n_rows, d_model:].astype(F32), axis=1, keepdims=True)
        a = jnp.dot(x, wg_buf[side], preferred_element_type=F32)
        u = jnp.dot(x, wu_buf[side], preferred_element_type=F32)
        hid = (jax.nn.silu(a) * u * w).astype(BF16)
        ybuf[slot, 0:n_rows, :] = jnp.dot(hid, wd_buf[side],
                                          preferred_element_type=F32).astype(BF16)
        for_real_chunks(i, lambda c: scatter_copy(i, c, slot).start())

    half = MOE_ROW_TILE // 2
    fits_half = treal_ref[i] * MOE_CHUNK <= half
    pl.when((i < n_used) & jnp.logical_not(fits_half))(functools.partial(tile_work, MOE_ROW_TILE))
    pl.when((i < n_used) & fits_half)(functools.partial(tile_work, half))

    @pl.when(i == n_used - 1)
    def _():
        @pl.when(i >= 1)
        def _():
            for_real_chunks(i - 1, lambda c: scatter_copy(i - 1, c, 1 - slot).wait())
        for_real_chunks(i, lambda c: scatter_copy(i, c, slot).wait())
        for_tail_chunks(lambda copy: copy.wait())


def _moe_ffn(hs, plan, wg, wu, wd):
    nT, cap, row_width = hs.shape
    E, D, F = wg.shape
    n_row_tiles = plan[0].shape[0]
    assert D % MOE_WEIGHT_PIECES == 0 and F % MOE_WEIGHT_PIECES == 0
    grid_spec = pltpu.PrefetchScalarGridSpec(
        num_scalar_prefetch=len(plan),
        grid=(n_row_tiles,),
        in_specs=[pl.BlockSpec(memory_space=pl.ANY)] * 4,
        out_specs=pl.BlockSpec(memory_space=pl.ANY),
        scratch_shapes=[
            pltpu.VMEM((2, MOE_ROW_TILE, row_width), BF16),
            pltpu.VMEM((2, MOE_ROW_TILE, D), BF16),
            pltpu.VMEM((MOE_CHUNK, D), BF16),
            pltpu.VMEM((2, D, F), BF16),
            pltpu.VMEM((2, D, F), BF16),
            pltpu.VMEM((2, F, D), BF16),
            pltpu.VMEM((MOE_WEIGHT_STAGES, D // MOE_WEIGHT_PIECES, F), F32),
            pltpu.VMEM((MOE_WEIGHT_STAGES, D // MOE_WEIGHT_PIECES, F), F32),
            pltpu.VMEM((MOE_WEIGHT_STAGES, F // MOE_WEIGHT_PIECES, D), F32),
            pltpu.SemaphoreType.DMA((2,)),
            pltpu.SemaphoreType.DMA((2,)),
            pltpu.SemaphoreType.DMA((MOE_WEIGHT_STAGES,)),
            pltpu.SemaphoreType.DMA((1,)),
        ],
    )
    return pl.pallas_call(
        _moe_ffn_kernel,
        grid_spec=grid_spec,
        out_shape=jax.ShapeDtypeStruct((nT, cap, D), BF16),
        compiler_params=_params(("arbitrary",)),
        name="moe_ffn",
    )(*plan, hs, wg, wu, wd)


def _moe_combine_kernel(ys_ref, route_ref, x1_ref, g_ref, o_ref):
    tm = x1_ref.shape[0]
    cap = ys_ref.shape[0]
    route = route_ref[...]
    slot = lax.broadcasted_iota(jnp.int32, (tm, cap), 1)
    pick = jnp.where(slot == route[:, 0:1].astype(jnp.int32), 1.0,
                     jnp.where(slot == route[:, 1:2].astype(jnp.int32), 1.0, 0.0)).astype(BF16)
    y = x1_ref[...] + jnp.dot(pick, ys_ref[...], preferred_element_type=F32)
    ms = jnp.mean(y * y, axis=-1, keepdims=True)
    o_ref[...] = y * lax.rsqrt(ms + RMS_EPS) * g_ref[...]


def _moe_combine(ys, route, x1, g, tm=MOE_TOKEN_TILE):
    S, D = x1.shape
    cap = ys.shape[1]
    row = lambda i: (i, 0)
    return pl.pallas_call(
        _moe_combine_kernel,
        grid=(S // tm,),
        in_specs=[
            pl.BlockSpec((None, cap, D), lambda i: (i, 0, 0)),
            pl.BlockSpec((tm, ROUTE_LANES), row),
            pl.BlockSpec((tm, D), row),
            pl.BlockSpec((1, D), lambda i: (0, 0)),
        ],
        out_specs=pl.BlockSpec((tm, D), row),
        out_shape=jax.ShapeDtypeStruct((S, D), F32),
        compiler_params=_params(("parallel",)),
        name="moe_combine",
    )(ys, route, x1, g)


def _router_weights(w_r_group, b_r_group, w_r_expert, b_r_expert):
    D = w_r_group.shape[0]
    w = jnp.concatenate(
        [w_r_group, jnp.transpose(w_r_expert, (1, 0, 2)).reshape(D, N_EXPERTS)], axis=1)
    b = jnp.concatenate([b_r_group, b_r_expert.reshape(N_EXPERTS)])
    pad = ROUTE_LANES - w.shape[1]
    w = jnp.pad(w, ((0, 0), (0, pad)))
    b = jnp.pad(b, (0, pad)).reshape(1, ROUTE_LANES)
    w_hi = w.astype(BF16)
    w_lo = (w - w_hi.astype(F32)).astype(BF16)
    return w_hi, w_lo, b


def kernel(x, norm_mix, w_in, w_pool, pool_scale, w_branch_pool, w_branch_attn, w_out, norm_ffn,
           w_r_group, b_r_group, w_r_expert, b_r_expert, w_gate, w_up, w_down, norm_final):
    B, S, D = x.shape
    depth = w_in.shape[0]
    assert depth == 1, "the final rms_norm is fused into the expert kernel of a single layer"
    slopes = jnp.exp2(-8.0 * jnp.arange(1, ATTN_HEADS + 1, dtype=F32) / ATTN_HEADS)
    slopes = jnp.broadcast_to(slopes[:, None, None], (ATTN_HEADS, 1, 128))
    outs = []
    for b in range(B):
        xb = x[b]
        for l in range(depth):
            proj = _inproj(_rms_norm(xb, norm_mix[l].reshape(1, D)), w_in[l])
            y_pool = _pool_mixer(proj, w_pool[l].astype(BF16), pool_scale[l].reshape(1, POOL_WIDTH))
            qT, vT, sel = _moba_gate(proj)
            y_attn = _moba_attention(proj, qT, vT, sel, slopes)
            wr_hi, wr_lo, br = _router_weights(w_r_group[l], b_r_group[l], w_r_expert[l], b_r_expert[l])
            x1 = _merge_out(y_pool, y_attn, proj, xb, w_branch_pool[l], w_branch_attn[l],
                            w_out[l])
            hs, route, cnt = _route_sort(x1, norm_ffn[l].reshape(1, D), wr_hi, wr_lo, br)
            nT = hs.shape[0]
            chunk_counts = cnt[:, 0, EXPERT_LANE0:EXPERT_LANE0 + N_EXPERTS].astype(jnp.int32)
            max_chunks = nT * ((2 * MOE_TOKEN_TILE + N_EXPERTS * (MOE_CHUNK - 1)) // MOE_CHUNK)
            n_row_tiles = -(-(max_chunks + N_EXPERTS * (MOE_CHUNKS_PER_TILE - 1))
                            // MOE_CHUNKS_PER_TILE)
            plan = _moe_plan(chunk_counts, n_row_tiles, n_row_tiles * MOE_CHUNKS_PER_TILE)
            ys = _moe_ffn(hs, plan, w_gate[l], w_up[l], w_down[l])
            xb = _moe_combine(ys, route, x1, norm_final.reshape(1, D))
        outs.append(xb)
    return jnp.stack(outs, axis=0)
```

```python
import functools

import jax
import jax.numpy as jnp
from jax import lax
from jax.experimental import pallas as pl
from jax.experimental.pallas import tpu as pltpu

F32 = jnp.float32
BF16 = jnp.bfloat16

POOL_WINDOWS = (2, 4, 8, 16)
MAX_WINDOW = 16
POOL_WIDTH = 1024
POOL_GROUP = 256
HEAD_DIM = 128
ATTN_HEADS = 8
ATTN_WIDTH = 1024
MOBA_BLOCK = 256
MOBA_TOPK = 3
N_GROUPS = 4
EXPERTS_PER_GROUP = 4
N_EXPERTS = 16
ROUTE_LANES = 128
EXPERT_LANE0 = N_GROUPS
RMS_EPS = 1e-6
NEG_INF = -1e30
LOG2_E = 1.4426950408889634
QUERY_SCALE = (HEAD_DIM ** -0.5) * LOG2_E
KEY_AUG = 2 * HEAD_DIM
ALIBI_TERMS = 3
VALUE_AUG = HEAD_DIM + 16
ATTN_KV_UNROLL = 3
ATTN_KEY_CHUNK = 64
ATTN_TILES_PER_STEP = 4
MOE_CHUNK = 16
MOE_TOKEN_TILE = 256
MERGE_WEIGHT_ROWS = 256
ROUTE_TILES_PER_STEP = 4
MOE_ROW_TILE = 256
MOE_CHUNKS_PER_TILE = MOE_ROW_TILE // MOE_CHUNK
MOE_WEIGHT_TERMS = 3
MOE_WEIGHT_PIECES = 8
MOE_WEIGHT_STAGES = 4
assert MOE_WEIGHT_STAGES - 1 <= MOE_WEIGHT_PIECES

V7X_VMEM_LIMIT_BYTES = 56 * 1024 * 1024


def _params(semantics, vmem=V7X_VMEM_LIMIT_BYTES, flags=None):
    return pltpu.CompilerParams(dimension_semantics=semantics, vmem_limit_bytes=vmem, flags=flags)


def _rms_norm_kernel(x_ref, g_ref, o_ref):
    x = x_ref[...]
    ms = jnp.mean(x * x, axis=-1, keepdims=True)
    o_ref[...] = (x * lax.rsqrt(ms + RMS_EPS) * g_ref[...]).astype(o_ref.dtype)


def _rms_norm(x, g, tm=1024):
    S, D = x.shape
    return pl.pallas_call(
        _rms_norm_kernel,
        grid=(S // tm,),
        in_specs=[pl.BlockSpec((tm, D), lambda i: (i, 0)), pl.BlockSpec((1, D), lambda i: (0, 0))],
        out_specs=pl.BlockSpec((tm, D), lambda i: (i, 0)),
        out_shape=jax.ShapeDtypeStruct((S, D), BF16),
        compiler_params=_params(("parallel",)),
        name="rms_norm",
    )(x, g)


def _inproj_kernel(h_ref, w_ref, o_ref, wb_ref):
    @pl.when(pl.program_id(1) == 0)
    def _():
        wb_ref[...] = w_ref[...].astype(BF16)

    col0 = pl.program_id(0) * o_ref.shape[1]
    is_q = (col0 >= POOL_WIDTH) & (col0 < POOL_WIDTH + ATTN_WIDTH)
    factor = jnp.where(is_q, QUERY_SCALE, 1.0).astype(F32)
    o_ref[...] = (jnp.dot(h_ref[...], wb_ref[...], preferred_element_type=F32)
                  * factor).astype(o_ref.dtype)


def _inproj(h, w, tm=2048, tn=1024):
    S, D = h.shape
    N = w.shape[1]
    tm = min(tm, S)
    assert POOL_WIDTH % tn == 0 and ATTN_WIDTH % tn == 0
    return pl.pallas_call(
        _inproj_kernel,
        grid=(N // tn, S // tm),
        in_specs=[
            pl.BlockSpec((tm, D), lambda j, i: (i, 0)),
            pl.BlockSpec((D, tn), lambda j, i: (0, j)),
        ],
        out_specs=pl.BlockSpec((tm, tn), lambda j, i: (i, j)),
        out_shape=jax.ShapeDtypeStruct((S, N), BF16),
        scratch_shapes=[pltpu.VMEM((D, tn), BF16)],
        compiler_params=_params(("parallel", "arbitrary")),
        name="inproj",
    )(h, w)


def _pool_kernel(cur_ref, prev_ref, w_ref, scale_ref, o_ref):
    i = pl.program_id(0)
    tm = cur_ref.shape[0]
    u = cur_ref[...].astype(F32)
    halo = jnp.where(i > 0, prev_ref[...].astype(F32), 0.0)
    ext = jnp.concatenate([halo, u], axis=0)
    t = i * tm + lax.broadcasted_iota(jnp.int32, (tm, 1), 0)
    for g, w in enumerate(POOL_WINDOWS):
        cols = slice(g * POOL_GROUP, (g + 1) * POOL_GROUP)
        ug = u[:, cols]
        run = ext[:, cols]
        s = 1
        while s < w:
            run = run + pltpu.roll(run, s, axis=0)
            s *= 2
        wsum = run[MAX_WINDOW:, :]
        cnt = jnp.minimum(t + 1, w).astype(F32)
        mixed = (wsum / cnt - ug).astype(BF16)
        y = jnp.dot(mixed, w_ref[g], preferred_element_type=F32)
        o_ref[:, cols] = (y * scale_ref[:, cols]).astype(o_ref.dtype)


def _pool_mixer(proj, w_pool, pool_scale, tm=512):
    S = proj.shape[0]
    halo_blocks = tm // MAX_WINDOW
    return pl.pallas_call(
        _pool_kernel,
        grid=(S // tm,),
        in_specs=[
            pl.BlockSpec((tm, POOL_WIDTH), lambda i: (i, 0)),
            pl.BlockSpec((MAX_WINDOW, POOL_WIDTH),
                         lambda i: (jnp.maximum(i * halo_blocks - 1, 0), 0)),
            pl.BlockSpec((len(POOL_WINDOWS), POOL_GROUP, POOL_GROUP), lambda i: (0, 0, 0)),
            pl.BlockSpec((1, POOL_WIDTH), lambda i: (0, 0)),
        ],
        out_specs=pl.BlockSpec((tm, POOL_WIDTH), lambda i: (i, 0)),
        out_shape=jax.ShapeDtypeStruct((S, POOL_WIDTH), BF16),
        compiler_params=_params(("parallel",)),
        name="pool_mixer",
    )(proj, proj, w_pool, pool_scale)


def _alibi_query_rows(slope_ref):
    slope2 = slope_ref[...][:, 0:1] * LOG2_E
    q_extra = jnp.zeros((KEY_AUG - HEAD_DIM, MOBA_BLOCK), F32)
    row = lax.broadcasted_iota(jnp.int32, q_extra.shape, 0)
    rest = slope2
    for n in range(ALIBI_TERMS):
        piece = rest.astype(BF16).astype(F32)
        q_extra = jnp.where(row == n, piece, q_extra)
        rest = rest - piece
    return q_extra.astype(BF16)


def _moba_gate_kernel(q_ref, k_ref, v_ref, qT_ref, vT_ref, sel_ref):
    S = q_ref.shape[0]
    nb = S // MOBA_BLOCK
    topk = min(MOBA_TOPK, nb)
    kf = k_ref[...].astype(F32).reshape(nb, MOBA_BLOCK, HEAD_DIM)
    kmean = jnp.sum(kf, axis=1) * (1.0 / MOBA_BLOCK)
    km_hi = kmean.astype(BF16)
    km_lo = (kmean - km_hi.astype(F32)).astype(BF16)
    blk = lax.broadcasted_iota(jnp.int32, (nb, MOBA_BLOCK), 0)
    v_row = lax.broadcasted_iota(jnp.int32, (VALUE_AUG - HEAD_DIM, MOBA_BLOCK), 0)
    v_extra = jnp.where(v_row == 0, 1.0, 0.0).astype(BF16)

    def body(i, carry):
        rows = pl.ds(pl.multiple_of(i * MOBA_BLOCK, MOBA_BLOCK), MOBA_BLOCK)
        qT = q_ref[rows, :].astype(F32).T.astype(BF16)
        qT_ref[i] = qT
        vT_ref[i, 0:HEAD_DIM, :] = v_ref[rows, :].astype(F32).T.astype(BF16)
        vT_ref[i, HEAD_DIM:VALUE_AUG, :] = v_extra
        gate = (jnp.dot(km_hi, qT, preferred_element_type=F32)
                + jnp.dot(km_lo, qT, preferred_element_type=F32)) * (1.0 / QUERY_SCALE)
        gate = jnp.where(blk < i, gate, NEG_INF)
        sel = jnp.zeros((nb, MOBA_BLOCK), F32)
        for _ in range(topk):
            best = jnp.max(gate, axis=0, keepdims=True)
            idx = jnp.min(jnp.where(gate == best, blk, nb), axis=0, keepdims=True)
            hit = blk == idx
            sel = jnp.where(hit & (blk < i), 1.0, sel)
            gate = jnp.where(hit, -jnp.inf, gate)
        sel_ref[i] = sel
        return carry

    lax.fori_loop(0, nb, body, 0, unroll=4 if nb % 4 == 0 else 1)


def _moba_gate(proj):
    S = proj.shape[0]
    nb = S // MOBA_BLOCK
    H = ATTN_HEADS
    q0 = POOL_WIDTH // HEAD_DIM
    k0 = q0 + H
    v0 = k0 + H
    blocked = lambda h: (h, 0, 0, 0)
    return pl.pallas_call(
        _moba_gate_kernel,
        grid=(H,),
        in_specs=[
            pl.BlockSpec((S, HEAD_DIM), lambda h: (0, q0 + h)),
            pl.BlockSpec((S, HEAD_DIM), lambda h: (0, k0 + h)),
            pl.BlockSpec((S, HEAD_DIM), lambda h: (0, v0 + h)),
        ],
        out_specs=[
            pl.BlockSpec((None, nb, HEAD_DIM, MOBA_BLOCK), blocked),
            pl.BlockSpec((None, nb, VALUE_AUG, MOBA_BLOCK), blocked),
            pl.BlockSpec((None, nb, nb, MOBA_BLOCK), blocked),
        ],
        out_shape=[
            jax.ShapeDtypeStruct((H, nb, HEAD_DIM, MOBA_BLOCK), BF16),
            jax.ShapeDtypeStruct((H, nb, VALUE_AUG, MOBA_BLOCK), BF16),
            jax.ShapeDtypeStruct((H, nb, nb, MOBA_BLOCK), F32),
        ],
        compiler_params=_params(("parallel",)),
        name="moba_gate",
    )(proj, proj, proj)


def _moba_attn_kernel(slope_ref, qT_ref, qT_next_ref, sel_ref, k_ref, vT_ref, o_ref,
                      ka_ref, *scratch):
    first = pl.program_id(1) * ATTN_TILES_PER_STEP
    nb = sel_ref.shape[1]

    @pl.when(first == 0)
    def _():
        pos = lax.broadcasted_iota(jnp.int32, (MOBA_BLOCK, KEY_AUG - HEAD_DIM), 0).astype(F32)
        col = lax.broadcasted_iota(jnp.int32, (MOBA_BLOCK, KEY_AUG - HEAD_DIM), 1)
        k_extra = jnp.where(col < ALIBI_TERMS, pos, 0.0).astype(BF16)

        def per_block(j, carry):
            rows = pl.ds(pl.multiple_of(j * MOBA_BLOCK, MOBA_BLOCK), MOBA_BLOCK)
            ka_ref[rows, 0:HEAD_DIM] = k_ref[rows, :]
            ka_ref[rows, HEAD_DIM:KEY_AUG] = k_extra
            return carry
        lax.fori_loop(0, nb, per_block, 0)

    q_extra = _alibi_query_rows(slope_ref)
    augment = lambda q: jnp.concatenate([q, q_extra], axis=0)
    for t in range(ATTN_TILES_PER_STEP):
        last = t + 1 == ATTN_TILES_PER_STEP
        _moba_attn_tile(first + t, t == 0, slope_ref, augment(qT_ref[t]),
                        augment(qT_next_ref[...] if last else qT_ref[t + 1]), sel_ref.at[t],
                        ka_ref, vT_ref, o_ref.at[t * MOBA_BLOCK:(t + 1) * MOBA_BLOCK, :], *scratch)


def _moba_attn_tile(i, may_start_head, slope_ref, qT, q_next, sel_ref, k_ref, vT_ref, o_ref,
                    s0_ref, s1_ref, p0_ref, p1_ref, s_own_ref):
    nb = sel_ref.shape[0]
    slope2 = slope_ref[...][:, 0:1] * LOG2_E

    key_chunks = [slice(c, c + ATTN_KEY_CHUNK) for c in range(0, MOBA_BLOCK, ATTN_KEY_CHUNK)]

    def block_of(t, u):
        return jnp.clip(t * ATTN_KV_UNROLL + u, 0, nb - 1)

    def key_rows(j):
        return pl.ds(pl.multiple_of(j * MOBA_BLOCK, MOBA_BLOCK), MOBA_BLOCK)

    def issue_scores(t, s_ref, queries=qT):
        for u in range(ATTN_KV_UNROLL):
            s_ref[u] = jnp.dot(k_ref[key_rows(block_of(t, u)), :], queries,
                               preferred_element_type=F32)

    def apply_probs(t, p_ref, alpha, acc):
        acc = alpha * acc
        for u in range(ATTN_KV_UNROLL):
            acc = acc + jnp.dot(vT_ref[block_of(t, u)], p_ref[u], preferred_element_type=F32)
        return acc

    def softmax_group(t, s_ref, p_ref, m):
        m_new = m
        shifts = []
        for u in range(ATTN_KV_UNROLL):
            j = t * ATTN_KV_UNROLL + u
            valid = jnp.where(j < i, sel_ref[pl.ds(block_of(t, u), 1), :], 0.0) > 0.0
            gap = slope2 * ((i - j) * MOBA_BLOCK).astype(F32)
            top = functools.reduce(jnp.maximum, [
                jnp.max(s_ref[u, rows, :], axis=0, keepdims=True) for rows in key_chunks]) - gap
            m_new = jnp.maximum(m_new, jnp.where(valid, top, NEG_INF))
            shifts.append((valid, gap))
        alpha = jnp.exp2(m - m_new)
        for u, (valid, gap) in enumerate(shifts):
            shift = jnp.where(valid, m_new + gap, jnp.inf)
            for rows in key_chunks:
                p_ref[u, rows, :] = jnp.exp2(s_ref[u, rows, :] - shift).astype(BF16)
        return m_new, alpha

    def body(r, carry):
        m, acc, alpha = carry
        acc = apply_probs(2 * r - 1, p1_ref, alpha, acc)
        m, alpha = softmax_group(2 * r, s0_ref, p0_ref, m)
        issue_scores(2 * r + 1, s1_ref)
        acc = apply_probs(2 * r, p0_ref, alpha, acc)
        m, alpha = softmax_group(2 * r + 1, s1_ref, p1_ref, m)
        issue_scores(2 * r + 2, s0_ref)
        return m, acc, alpha

    n_groups = (i + ATTN_KV_UNROLL - 1) // ATTN_KV_UNROLL
    n_pairs = n_groups // 2
    if may_start_head:
        @pl.when(i == 0)
        def _():
            s0_ref[...] = jnp.zeros(s0_ref.shape, F32)
            s_own_ref[...] = jnp.dot(k_ref[key_rows(0), :], qT, preferred_element_type=F32)

    p1_ref[...] = jnp.zeros(p1_ref.shape, BF16)
    row = jnp.zeros((1, MOBA_BLOCK), F32)
    init = (row + NEG_INF, jnp.zeros((VALUE_AUG, MOBA_BLOCK), F32), row + 1.0)
    m, acc, alpha = lax.fori_loop(0, n_pairs, body, init)

    def finish(has_last_group):
        m_, acc_ = m, apply_probs(2 * n_pairs - 1, p1_ref, alpha, acc)
        if has_last_group:
            m_, alpha_ = softmax_group(2 * n_pairs, s0_ref, p0_ref, m_)
            acc_ = apply_probs(2 * n_pairs, p0_ref, alpha_, acc_)
        kpos = lax.broadcasted_iota(jnp.int32, (MOBA_BLOCK, MOBA_BLOCK), 0)
        qpos = lax.broadcasted_iota(jnp.int32, (MOBA_BLOCK, MOBA_BLOCK), 1)
        s_own = jnp.where(qpos >= kpos, s_own_ref[...], NEG_INF)
        m_new = jnp.maximum(m_, jnp.max(s_own, axis=0, keepdims=True))
        p = jnp.exp2(s_own - m_new)
        acc_ = (jnp.exp2(m_ - m_new) * acc_
                + jnp.dot(vT_ref[i], p.astype(BF16), preferred_element_type=F32))
        out = acc_[0:HEAD_DIM] / acc_[HEAD_DIM:HEAD_DIM + 1]
        o_ref[...] = out.T.astype(o_ref.dtype)
        issue_scores(0, s0_ref, q_next)
        s_own_ref[...] = jnp.dot(k_ref[key_rows(jnp.minimum(i + 1, nb - 1)), :], q_next,
                                 preferred_element_type=F32)

    pl.when(n_groups > 2 * n_pairs)(functools.partial(finish, True))
    pl.when(n_groups == 2 * n_pairs)(functools.partial(finish, False))


def _moba_attention(proj, qT, vT, sel, slopes):
    S = proj.shape[0]
    H = ATTN_HEADS
    nb = S // MOBA_BLOCK
    k0 = POOL_WIDTH // HEAD_DIM + H
    per_step = ATTN_TILES_PER_STEP
    assert nb % per_step == 0
    return pl.pallas_call(
        _moba_attn_kernel,
        grid=(H, nb // per_step),
        in_specs=[
            pl.BlockSpec((None, 1, 128), lambda h, g: (h, 0, 0)),
            pl.BlockSpec((None, per_step, HEAD_DIM, MOBA_BLOCK), lambda h, g: (h, g, 0, 0)),
            pl.BlockSpec((None, None, HEAD_DIM, MOBA_BLOCK),
                         lambda h, g: (h, jnp.minimum((g + 1) * per_step, nb - 1), 0, 0)),
            pl.BlockSpec((None, per_step, nb, MOBA_BLOCK), lambda h, g: (h, g, 0, 0)),
            pl.BlockSpec((S, HEAD_DIM), lambda h, g: (0, k0 + h)),
            pl.BlockSpec((None, nb, VALUE_AUG, MOBA_BLOCK), lambda h, g: (h, 0, 0, 0)),
        ],
        out_specs=pl.BlockSpec((per_step * MOBA_BLOCK, HEAD_DIM), lambda h, g: (g, h)),
        out_shape=jax.ShapeDtypeStruct((S, ATTN_WIDTH), BF16),
        scratch_shapes=[
            pltpu.VMEM((S, KEY_AUG), BF16),
            pltpu.VMEM((ATTN_KV_UNROLL, MOBA_BLOCK, MOBA_BLOCK), F32),
            pltpu.VMEM((ATTN_KV_UNROLL, MOBA_BLOCK, MOBA_BLOCK), F32),
            pltpu.VMEM((ATTN_KV_UNROLL, MOBA_BLOCK, MOBA_BLOCK), BF16),
            pltpu.VMEM((ATTN_KV_UNROLL, MOBA_BLOCK, MOBA_BLOCK), BF16),
            pltpu.VMEM((MOBA_BLOCK, MOBA_BLOCK), F32),
        ],
        compiler_params=_params(("arbitrary", "arbitrary")),
        name="moba_attention",
    )(slopes, qT, qT, sel, proj, vT)


def _first_lane_of_max(vals, lane):
    best = jnp.max(vals, axis=1, keepdims=True)
    idx = jnp.min(jnp.where(vals == best, lane, ROUTE_LANES), axis=1, keepdims=True)
    return best, idx


def _merge_out_kernel(yp_ref, ya_ref, glp_ref, gla_ref, x_ref, wbp_hbm, wba_hbm, wout_hbm, x1_ref,
                      wbp_ref, wba_ref, wout_ref, stage_ref, stage_sem):
    @pl.when(pl.program_id(0) == 0)
    def _():
        rows = stage_ref.shape[1]
        pieces = [(src, dst, r) for src, dst in ((wbp_hbm, wbp_ref), (wba_hbm, wba_ref),
                                                 (wout_hbm, wout_ref))
                  for r in range(0, src.shape[0], rows)]
        copies = [pltpu.make_async_copy(src.at[r:r + rows, :], stage_ref.at[n % 2],
                                        stage_sem.at[n % 2])
                  for n, (src, _, r) in enumerate(pieces)]
        copies[0].start()
        for n, (_, dst, r) in enumerate(pieces):
            if n + 1 < len(pieces):
                copies[n + 1].start()
            copies[n].wait()
            dst[r:r + rows, :] = stage_ref[n % 2].astype(BF16)

    bp = jnp.dot(yp_ref[...], wbp_ref[...], preferred_element_type=F32)
    ba = jnp.dot(ya_ref[...], wba_ref[...], preferred_element_type=F32)
    merged = (jax.nn.sigmoid(glp_ref[...].astype(F32)) * bp
              + jax.nn.sigmoid(gla_ref[...].astype(F32)) * ba)
    x1_ref[...] = x_ref[...] + jnp.dot(merged.astype(BF16), wout_ref[...],
                                       preferred_element_type=F32)


def _merge_out(yp, ya, proj, x, wbp, wba, wout, tm=512):
    S, D = x.shape
    row = lambda i: (i, 0)
    glp_blk = (POOL_WIDTH + 3 * ATTN_WIDTH) // D
    hbm = pl.BlockSpec(memory_space=pl.ANY)
    return pl.pallas_call(
        _merge_out_kernel,
        grid=(S // tm,),
        in_specs=[
            pl.BlockSpec((tm, POOL_WIDTH), row),
            pl.BlockSpec((tm, ATTN_WIDTH), row),
            pl.BlockSpec((tm, D), lambda i: (i, glp_blk)),
            pl.BlockSpec((tm, D), lambda i: (i, glp_blk + 1)),
            pl.BlockSpec((tm, D), row),
            hbm, hbm, hbm,
        ],
        out_specs=pl.BlockSpec((tm, D), row),
        out_shape=jax.ShapeDtypeStruct((S, D), F32),
        scratch_shapes=[
            pltpu.VMEM((POOL_WIDTH, D), BF16),
            pltpu.VMEM((ATTN_WIDTH, D), BF16),
            pltpu.VMEM((D, D), BF16),
            pltpu.VMEM((2, MERGE_WEIGHT_ROWS, D), F32),
            pltpu.SemaphoreType.DMA((2,)),
        ],
        compiler_params=_params(("arbitrary",)),
        name="merge_out",
    )(yp, ya, proj, proj, x, wbp, wba, wout)


def _route_sort_kernel(x1_ref, g_ref, wr_hi_ref, wr_lo_ref, br_ref, hs_ref, route_ref, cnt_ref):
    n_tiles = ROUTE_TILES_PER_STEP
    tm = x1_ref.shape[0] // n_tiles
    cap = hs_ref.shape[1]
    d_model = x1_ref.shape[1]
    lane = lax.broadcasted_iota(jnp.int32, (tm, ROUTE_LANES), 1)
    tiles = [{"index": t, "rows": slice(t * tm, (t + 1) * tm)} for t in range(n_tiles)]

    def normalise(s):
        x1 = x1_ref[s["rows"], :]
        ms = jnp.mean(x1 * x1, axis=-1, keepdims=True)
        h2 = x1 * lax.rsqrt(ms + RMS_EPS) * g_ref[...]
        s["h2_hi"] = h2.astype(BF16)
        s["h2_lo"] = (h2 - s["h2_hi"].astype(F32)).astype(BF16)

    def router_logits(s):
        s["logits"] = (jnp.dot(s["h2_hi"], wr_hi_ref[...], preferred_element_type=F32)
                       + jnp.dot(s["h2_lo"], wr_hi_ref[...], preferred_element_type=F32)
                       + jnp.dot(s["h2_hi"], wr_lo_ref[...], preferred_element_type=F32)
                       + br_ref[...])

    def choose_experts(s):
        logits = s["logits"]
        g_logits = jnp.where(lane < N_GROUPS, logits, -jnp.inf)
        g_best, g_idx = _first_lane_of_max(g_logits, lane)
        g_w = 1.0 / jnp.sum(jnp.exp(g_logits - g_best), axis=1, keepdims=True)
        e_lo = EXPERT_LANE0 + EXPERTS_PER_GROUP * g_idx
        e_logits = jnp.where((lane >= e_lo) & (lane < e_lo + EXPERTS_PER_GROUP), logits, -jnp.inf)
        v1, i1 = _first_lane_of_max(e_logits, lane)
        v2, i2 = _first_lane_of_max(jnp.where(lane == i1, -jnp.inf, e_logits), lane)
        e21 = jnp.exp(v2 - v1)
        s["w1"] = g_w / (1.0 + e21)
        s["w2"] = g_w * e21 / (1.0 + e21)
        s["hit1"] = lane == i1
        s["hit2"] = lane == i2

    def sort_by_expert(s):
        hit1, hit2 = s["hit1"], s["hit2"]
        member = jnp.where(hit1, 1.0, jnp.where(hit2, 1.0, 0.0))
        r_tok = lax.broadcasted_iota(jnp.int32, (tm, tm), 0)
        c_tok = lax.broadcasted_iota(jnp.int32, (tm, tm), 1)
        earlier = jnp.where(c_tok < r_tok, 1.0, 0.0).astype(BF16)
        rank = jnp.dot(earlier, member.astype(BF16), preferred_element_type=F32)
        count = jnp.sum(member, axis=0, keepdims=True)
        chunks = jnp.floor((count + (MOE_CHUNK - 1)) * (1.0 / MOE_CHUNK))
        r_l = lax.broadcasted_iota(jnp.int32, (ROUTE_LANES, ROUTE_LANES), 0)
        c_l = lax.broadcasted_iota(jnp.int32, (ROUTE_LANES, ROUTE_LANES), 1)
        lower_lanes = jnp.where(r_l < c_l, 1.0, 0.0).astype(BF16)
        start = jnp.dot(jnp.broadcast_to(chunks, (8, ROUTE_LANES)).astype(BF16), lower_lanes,
                        preferred_element_type=F32)[0:1] * MOE_CHUNK
        pos = start + rank
        pos1 = jnp.sum(jnp.where(hit1, pos, 0.0), axis=1, keepdims=True)
        pos2 = jnp.sum(jnp.where(hit2, pos, 0.0), axis=1, keepdims=True)
        s["route"] = jnp.where(lane == 0, pos1, jnp.where(lane == 1, pos2, 0.0))
        route_ref[s["rows"], :] = s["route"]
        cnt_ref[s["index"]] = chunks

    def weight_pieces(w):
        pieces = jnp.zeros(lane.shape, F32)
        rest = w
        for n in range(MOE_WEIGHT_TERMS):
            piece = rest.astype(BF16).astype(F32)
            pieces = jnp.where(lane == n, piece, pieces)
            rest = rest - piece
        return pieces.astype(BF16)

    def compact(s):
        route_t = s["route"].T
        slot = lax.broadcasted_iota(jnp.int32, (cap, tm), 0)
        first = jnp.where(slot == route_t[0:1, :].astype(jnp.int32), 1.0, 0.0).astype(BF16)
        second = jnp.where(slot == route_t[1:2, :].astype(jnp.int32), 1.0, 0.0).astype(BF16)
        hs_ref[s["index"], :, 0:d_model] = jnp.dot(
            first + second, s["h2_hi"], preferred_element_type=F32).astype(BF16)
        slot_w = (jnp.dot(first, weight_pieces(s["w1"]), preferred_element_type=F32)
                  + jnp.dot(second, weight_pieces(s["w2"]), preferred_element_type=F32))
        hs_ref[s["index"], :, d_model:] = slot_w.astype(BF16)

    for stage in (normalise, router_logits, choose_experts, sort_by_expert, compact):
        for s in tiles:
            stage(s)


def _moe_cap(tm):
    worst = 2 * tm + N_EXPERTS * (MOE_CHUNK - 1)
    return -(-worst // 128) * 128


def _route_sort(x1, g, wr_hi, wr_lo, br, tm=MOE_TOKEN_TILE):
    S, D = x1.shape
    nT = S // tm
    cap = _moe_cap(tm)
    full = lambda i: (0, 0)
    row = lambda i: (i, 0)
    per_step = ROUTE_TILES_PER_STEP
    return pl.pallas_call(
        _route_sort_kernel,
        grid=(nT // per_step,),
        in_specs=[
            pl.BlockSpec((per_step * tm, D), row),
            pl.BlockSpec((1, D), full),
            pl.BlockSpec((D, ROUTE_LANES), full),
            pl.BlockSpec((D, ROUTE_LANES), full),
            pl.BlockSpec((1, ROUTE_LANES), full),
        ],
        out_specs=[
            pl.BlockSpec((per_step, cap, D + ROUTE_LANES), lambda i: (i, 0, 0)),
            pl.BlockSpec((per_step * tm, ROUTE_LANES), row),
            pl.BlockSpec((per_step, 1, ROUTE_LANES), lambda i: (i, 0, 0)),
        ],
        out_shape=[
            jax.ShapeDtypeStruct((nT, cap, D + ROUTE_LANES), BF16),
            jax.ShapeDtypeStruct((S, ROUTE_LANES), F32),
            jax.ShapeDtypeStruct((nT, 1, ROUTE_LANES), F32),
        ],
        compiler_params=_params(("parallel",)),
        name="route_sort",
    )(x1, g, wr_hi, wr_lo, br)


def _moe_plan(chunk_counts, n_row_tiles, n_chunk_slots):
    nT, E = chunk_counts.shape
    per_expert = chunk_counts.T
    seg_start = (jnp.cumsum(chunk_counts, axis=1) - chunk_counts).T
    seg_end = jnp.cumsum(per_expert, axis=1)
    n_chunks = seg_end[:, -1]
    padded = -(-n_chunks // MOE_CHUNKS_PER_TILE) * MOE_CHUNKS_PER_TILE
    e_end = jnp.cumsum(padded)
    n_used = e_end[-1] // MOE_CHUNKS_PER_TILE
    c = jnp.arange(n_chunk_slots, dtype=jnp.int32)
    e_of_c = jnp.minimum((e_end[None, :] <= c[:, None]).sum(axis=1), E - 1)
    is_e = (e_of_c[:, None] == jnp.arange(E)[None, :]).astype(jnp.int32)
    local = c - (is_e * (e_end - padded)[None, :]).sum(axis=1)
    real = local < (is_e * n_chunks[None, :]).sum(axis=1)
    pick_e = lambda table: (is_e[:, :, None] * table[None, :, :]).sum(axis=1)
    seg_end_c = pick_e(seg_end)
    t_of_c = jnp.minimum((seg_end_c <= local[:, None]).sum(axis=1), nT - 1)
    is_t = (t_of_c[:, None] == jnp.arange(nT)[None, :]).astype(jnp.int32)
    pick_t = lambda rows: (rows * is_t).sum(axis=1)
    within = local - (pick_t(seg_end_c) - pick_t(pick_e(per_expert)))
    src_tile = jnp.where(real, t_of_c, 0).astype(jnp.int32)
    src_row = jnp.where(real, (pick_t(pick_e(seg_start)) + within) * MOE_CHUNK, 0).astype(jnp.int32)
    tile = jnp.arange(n_row_tiles, dtype=jnp.int32)
    first = jnp.minimum(tile, n_used - 1) * MOE_CHUNKS_PER_TILE
    tile_expert = jnp.minimum((e_end[None, :] <= first[:, None]).sum(axis=1), E - 1)
    tile_real = real.reshape(n_row_tiles, MOE_CHUNKS_PER_TILE).sum(axis=1)
    experts = jnp.arange(E)
    nonempty = padded > 0
    buffer_of_e = jnp.cumsum(nonempty) - nonempty
    later = (experts[None, :] > experts[:, None]) & nonempty[None, :]
    next_of_e = jnp.min(jnp.where(later, experts[None, :], E), axis=1)
    next_of_e = jnp.where(next_of_e < E, next_of_e, -1)
    is_next = (next_of_e[:, None] == experts[None, :]).astype(jnp.int32)
    after_of_e = jnp.where(next_of_e >= 0, (is_next * next_of_e[None, :]).sum(axis=1), -1)
    is_te = (tile_expert[:, None] == experts[None, :]).astype(jnp.int32)
    at = lambda table: (is_te * table[None, :]).sum(axis=1)
    run_len = jnp.maximum(at(padded) // MOE_CHUNKS_PER_TILE, 1)
    run_pos = tile - at(e_end - padded) // MOE_CHUNKS_PER_TILE
    next_expert = at(next_of_e)
    streams = (tile < n_used) & (next_expert >= 0)
    piece_lo = jnp.where(streams, (MOE_WEIGHT_PIECES * run_pos) // run_len, 0)
    piece_hi = jnp.where(streams, (MOE_WEIGHT_PIECES * (run_pos + 1)) // run_len, 0)
    as_i32 = lambda a: a.astype(jnp.int32)
    return (as_i32(tile_expert), as_i32(tile_real), src_tile, src_row, as_i32(n_used.reshape(1)),
            as_i32(at(buffer_of_e) % 2), as_i32(next_expert), as_i32(at(after_of_e)),
            as_i32(piece_lo), as_i32(piece_hi), as_i32(chunk_counts.sum(axis=1)))


def _moe_ffn_kernel(texp_ref, treal_ref, ctile_ref, crow_ref, nused_ref,
                    wbuf_ref, wnext_ref, wafter_ref, plo_ref, phi_ref, tused_ref,
                    hs_hbm, wg_hbm, wu_hbm, wd_hbm, ys_hbm,
                    xbuf, ybuf, zbuf, wg_buf, wu_buf, wd_buf, stage_g, stage_u, stage_d,
                    gather_sem, scatter_sem, weight_sem, zero_sem):
    i = pl.program_id(0)
    n_used = nused_ref[0]
    slot = lax.rem(i, 2)
    rows_gu = stage_g.shape[1]
    rows_d = stage_d.shape[1]

    def piece_copies(e, p, s):
        gu_rows = pl.ds(pl.multiple_of(p * rows_gu, rows_gu), rows_gu)
        d_rows = pl.ds(pl.multiple_of(p * rows_d, rows_d), rows_d)
        return (pltpu.make_async_copy(wg_hbm.at[e, gu_rows, :], stage_g.at[s], weight_sem.at[s]),
                pltpu.make_async_copy(wu_hbm.at[e, gu_rows, :], stage_u.at[s], weight_sem.at[s]),
                pltpu.make_async_copy(wd_hbm.at[e, d_rows, :], stage_d.at[s], weight_sem.at[s]))

    def start_piece(e, p):
        for copy in piece_copies(e, p, lax.rem(p, MOE_WEIGHT_STAGES)):
            copy.start()

    def finish_piece(e, p, side):
        s = lax.rem(p, MOE_WEIGHT_STAGES)
        for copy in piece_copies(e, p, s):
            copy.wait()
        gu_rows = pl.ds(pl.multiple_of(p * rows_gu, rows_gu), rows_gu)
        d_rows = pl.ds(pl.multiple_of(p * rows_d, rows_d), rows_d)
        wg_buf[side, gu_rows, :] = stage_g[s].astype(BF16)
        wu_buf[side, gu_rows, :] = stage_u[s].astype(BF16)
        wd_buf[side, d_rows, :] = stage_d[s].astype(BF16)

    ahead = MOE_WEIGHT_STAGES - 1

    def start_first_pieces(e):
        for p in range(ahead):
            start_piece(e, p)

    def stream_pieces(e, lo, hi, side, following):
        @pl.when(hi > lo)
        def _():
            def body(p, carry):
                @pl.when(p + ahead < MOE_WEIGHT_PIECES)
                def _():
                    start_piece(e, p + ahead)
                finish_piece(e, p, side)
                return carry
            lax.fori_loop(lo, hi, body, 0)

            @pl.when((hi == MOE_WEIGHT_PIECES) & (following >= 0))
            def _():
                start_first_pieces(following)

    def chunk_rows(c):
        return pl.ds(pl.multiple_of(c * MOE_CHUNK, MOE_CHUNK), MOE_CHUNK)

    def gather_copy(tile, c, buf):
        g = tile * MOE_CHUNKS_PER_TILE + c
        src = hs_hbm.at[ctile_ref[g], pl.ds(pl.multiple_of(crow_ref[g], MOE_CHUNK), MOE_CHUNK), :]
        return pltpu.make_async_copy(src, xbuf.at[buf, chunk_rows(c), :], gather_sem.at[buf])

    def scatter_copy(tile, c, buf):
        g = tile * MOE_CHUNKS_PER_TILE + c
        dst = ys_hbm.at[ctile_ref[g], pl.ds(pl.multiple_of(crow_ref[g], MOE_CHUNK), MOE_CHUNK), :]
        return pltpu.make_async_copy(ybuf.at[buf, chunk_rows(c), :], dst, scatter_sem.at[buf])

    def for_real_chunks(tile, fn, also=True):
        n_real = treal_ref[tile]
        for c in range(MOE_CHUNKS_PER_TILE):
            pl.when((c < n_real) & also)(functools.partial(fn, c))

    def for_tail_chunks(fn):
        n_token_tiles, cap = ys_hbm.shape[0], ys_hbm.shape[1]

        def per_tile(t, carry):
            def per_chunk(c, carry):
                fn(pltpu.make_async_copy(zbuf, ys_hbm.at[t, chunk_rows(c), :], zero_sem.at[0]))
                return carry
            return lax.fori_loop(tused_ref[t], cap // MOE_CHUNK, per_chunk, carry)
        lax.fori_loop(0, n_token_tiles, per_tile, 0)

    @pl.when(i == 0)
    def _():
        zbuf[...] = jnp.zeros(zbuf.shape, zbuf.dtype)
        for_tail_chunks(lambda copy: copy.start())
        xbuf[...] = jnp.zeros(xbuf.shape, xbuf.dtype)
        for_real_chunks(0, lambda c: gather_copy(0, c, 0).start())
        start_first_pieces(texp_ref[0])
        stream_pieces(texp_ref[0], 0, MOE_WEIGHT_PIECES, wbuf_ref[0], wnext_ref[0])

    @pl.when(i + 1 < n_used)
    def _():
        for_real_chunks(i + 1, lambda c: gather_copy(i + 1, c, 1 - slot).start())

    side = wbuf_ref[i]
    stream_pieces(wnext_ref[i], plo_ref[i], phi_ref[i], 1 - side, wafter_ref[i])

    def tile_work(n_rows):
        for_real_chunks(i, lambda c: gather_copy(i, c, slot).wait())
        before = jnp.maximum(i - 2, 0)
        for_real_chunks(before, lambda c: scatter_copy(before, c, slot).wait(), also=i >= 2)
        d_model = wg_buf.shape[1]
        x = xbuf[slot, 0:n_rows, 0:d_model]
        w = jnp.sum(xbuf[slot, 0:n_rows, d_model:].astype(F32), axis=1, keepdims=True)
        a = jnp.dot(x, wg_buf[side], preferred_element_type=F32)
        u = jnp.dot(x, wu_buf[side], preferred_element_type=F32)
        hid = (jax.nn.silu(a) * u * w).astype(BF16)
        ybuf[slot, 0:n_rows, :] = jnp.dot(hid, wd_buf[side],
                                          preferred_element_type=F32).astype(BF16)
        for_real_chunks(i, lambda c: scatter_copy(i, c, slot).start())

    half = MOE_ROW_TILE // 2
    fits_half = treal_ref[i] * MOE_CHUNK <= half
    pl.when((i < n_used) & jnp.logical_not(fits_half))(functools.partial(tile_work, MOE_ROW_TILE))
    pl.when((i < n_used) & fits_half)(functools.partial(tile_work, half))

    @pl.when(i == n_used - 1)
    def _():
        @pl.when(i >= 1)
        def _():
            for_real_chunks(i - 1, lambda c: scatter_copy(i - 1, c, 1 - slot).wait())
        for_real_chunks(i, lambda c: scatter_copy(i, c, slot).wait())
        for_tail_chunks(lambda copy: copy.wait())


def _moe_ffn(hs, plan, wg, wu, wd):
    nT, cap, row_width = hs.shape
    E, D, F = wg.shape
    n_row_tiles = plan[0].shape[0]
    assert D % MOE_WEIGHT_PIECES == 0 and F % MOE_WEIGHT_PIECES == 0
    grid_spec = pltpu.PrefetchScalarGridSpec(
        num_scalar_prefetch=len(plan),
        grid=(n_row_tiles,),
        in_specs=[pl.BlockSpec(memory_space=pl.ANY)] * 4,
        out_specs=pl.BlockSpec(memory_space=pl.ANY),
        scratch_shapes=[
            pltpu.VMEM((2, MOE_ROW_TILE, row_width), BF16),
            pltpu.VMEM((2, MOE_ROW_TILE, D), BF16),
            pltpu.VMEM((MOE_CHUNK, D), BF16),
            pltpu.VMEM((2, D, F), BF16),
            pltpu.VMEM((2, D, F), BF16),
            pltpu.VMEM((2, F, D), BF16),
            pltpu.VMEM((MOE_WEIGHT_STAGES, D // MOE_WEIGHT_PIECES, F), F32),
            pltpu.VMEM((MOE_WEIGHT_STAGES, D // MOE_WEIGHT_PIECES, F), F32),
            pltpu.VMEM((MOE_WEIGHT_STAGES, F // MOE_WEIGHT_PIECES, D), F32),
            pltpu.SemaphoreType.DMA((2,)),
            pltpu.SemaphoreType.DMA((2,)),
            pltpu.SemaphoreType.DMA((MOE_WEIGHT_STAGES,)),
            pltpu.SemaphoreType.DMA((1,)),
        ],
    )
    return pl.pallas_call(
        _moe_ffn_kernel,
        grid_spec=grid_spec,
        out_shape=jax.ShapeDtypeStruct((nT, cap, D), BF16),
        compiler_params=_params(("arbitrary",)),
        name="moe_ffn",
    )(*plan, hs, wg, wu, wd)


def _moe_combine_kernel(ys_ref, route_ref, x1_ref, g_ref, o_ref):
    tm = x1_ref.shape[0]
    cap = ys_ref.shape[0]
    route = route_ref[...]
    slot = lax.broadcasted_iota(jnp.int32, (tm, cap), 1)
    pick = jnp.where(slot == route[:, 0:1].astype(jnp.int32), 1.0,
                     jnp.where(slot == route[:, 1:2].astype(jnp.int32), 1.0, 0.0)).astype(BF16)
    y = x1_ref[...] + jnp.dot(pick, ys_ref[...], preferred_element_type=F32)
    ms = jnp.mean(y * y, axis=-1, keepdims=True)
    o_ref[...] = y * lax.rsqrt(ms + RMS_EPS) * g_ref[...]


def _moe_combine(ys, route, x1, g, tm=MOE_TOKEN_TILE):
    S, D = x1.shape
    cap = ys.shape[1]
    row = lambda i: (i, 0)
    return pl.pallas_call(
        _moe_combine_kernel,
        grid=(S // tm,),
        in_specs=[
            pl.BlockSpec((None, cap, D), lambda i: (i, 0, 0)),
            pl.BlockSpec((tm, ROUTE_LANES), row),
            pl.BlockSpec((tm, D), row),
            pl.BlockSpec((1, D), lambda i: (0, 0)),
        ],
        out_specs=pl.BlockSpec((tm, D), row),
        out_shape=jax.ShapeDtypeStruct((S, D), F32),
        compiler_params=_params(("parallel",)),
        name="moe_combine",
    )(ys, route, x1, g)


def _router_weights(w_r_group, b_r_group, w_r_expert, b_r_expert):
    D = w_r_group.shape[0]
    w = jnp.concatenate(
        [w_r_group, jnp.transpose(w_r_expert, (1, 0, 2)).reshape(D, N_EXPERTS)], axis=1)
    b = jnp.concatenate([b_r_group, b_r_expert.reshape(N_EXPERTS)])
    pad = ROUTE_LANES - w.shape[1]
    w = jnp.pad(w, ((0, 0), (0, pad)))
    b = jnp.pad(b, (0, pad)).reshape(1, ROUTE_LANES)
    w_hi = w.astype(BF16)
    w_lo = (w - w_hi.astype(F32)).astype(BF16)
    return w_hi, w_lo, b


def kernel(x, norm_mix, w_in, w_pool, pool_scale, w_branch_pool, w_branch_attn, w_out, norm_ffn,
           w_r_group, b_r_group, w_r_expert, b_r_expert, w_gate, w_up, w_down, norm_final):
    B, S, D = x.shape
    depth = w_in.shape[0]
    assert depth == 1, "the final rms_norm is fused into the expert kernel of a single layer"
    slopes = jnp.exp2(-8.0 * jnp.arange(1, ATTN_HEADS + 1, dtype=F32) / ATTN_HEADS)
    slopes = jnp.broadcast_to(slopes[:, None, None], (ATTN_HEADS, 1, 128))
    outs = []
    for b in range(B):
        xb = x[b]
        for l in range(depth):
            proj = _inproj(_rms_norm(xb, norm_mix[l].reshape(1, D)), w_in[l])
            y_pool = _pool_mixer(proj, w_pool[l].astype(BF16), pool_scale[l].reshape(1, POOL_WIDTH))
            qT, vT, sel = _moba_gate(proj)
            y_attn = _moba_attention(proj, qT, vT, sel, slopes)
            wr_hi, wr_lo, br = _router_weights(w_r_group[l], b_r_group[l], w_r_expert[l], b_r_expert[l])
            x1 = _merge_out(y_pool, y_attn, proj, xb, w_branch_pool[l], w_branch_attn[l],
                            w_out[l])
            hs, route, cnt = _route_sort(x1, norm_ffn[l].reshape(1, D), wr_hi, wr_lo, br)
            nT = hs.shape[0]
            chunk_counts = cnt[:, 0, EXPERT_LANE0:EXPERT_LANE0 + N_EXPERTS].astype(jnp.int32)
            max_chunks = nT * ((2 * MOE_TOKEN_TILE + N_EXPERTS * (MOE_CHUNK - 1)) // MOE_CHUNK)
            n_row_tiles = -(-(max_chunks + N_EXPERTS * (MOE_CHUNKS_PER_TILE - 1))
                            // MOE_CHUNKS_PER_TILE)
            plan = _moe_plan(chunk_counts, n_row_tiles, n_row_tiles * MOE_CHUNKS_PER_TILE)
            ys = _moe_ffn(hs, plan, w_gate[l], w_up[l], w_down[l])
            xb = _moe_combine(ys, route, x1, norm_final.reshape(1, D))
        outs.append(xb)
    return jnp.stack(outs, axis=0)
```
